```python
import jax, jax.numpy as jnp
from jax import lax
import numpy as np

D_MODEL = 1024
BATCH = 32
SEQ = 2048
DEPTH = 4
DEC_BATCH = 32
DEC_SEQ = 32
PAST_LEN = 4096

CHUNK = 64
N_MIXERS = 3
N_POOL_LAYERS = (DEPTH + 2) // 3
N_ATTN_LAYERS = (DEPTH + 1) // 3
N_RWKV_LAYERS = DEPTH // 3
D_FF = 4 * D_MODEL
RMS_EPS = 1e-6
POOL_WINDOWS = (2, 4, 8, 16)
POOL_GROUPS = 4
POOL_GW = D_MODEL // POOL_GROUPS
POOL_HIST = max(POOL_WINDOWS) - 1
ATTN_HEADS = 8
ATTN_HEAD_DIM = D_MODEL // ATTN_HEADS
IDX_HEADS = 8
IDX_DIM = 64
TOPK_MAX = 256
ROPE_THETA = 500000.0
ROPE_FRACTION = 4
Q_BLOCK = 128
NEG_INF = -1e30
Q_W = ATTN_HEADS * ATTN_HEAD_DIM
OFF_K = Q_W
OFF_V = OFF_K + ATTN_HEAD_DIM
OFF_QI = OFF_V + ATTN_HEAD_DIM
OFF_KI = OFF_QI + IDX_HEADS * IDX_DIM
OFF_WI = OFF_KI + IDX_DIM
ATTN_IN_W = OFF_WI + IDX_HEADS
RWKV_HEAD = 64
RWKV_HEADS = D_MODEL // RWKV_HEAD
DECAY_LORA = 64
AAA_LORA = 64
GATE_LORA = 128
LNX_EPS = 64e-5

kernel_name = 'hybrid_pool_dsa_rwkv7_stream_step'


def rms_norm(x, g):
    xf = x.astype(jnp.float32)
    y = xf * lax.rsqrt(jnp.mean(xf * xf, axis=-1, keepdims=True) + RMS_EPS)
    return (y * g.astype(jnp.float32)).astype(x.dtype)


def squared_relu_mlp(h, w_up, w_down):
    return jnp.square(jax.nn.relu(h @ w_up)) @ w_down


def pool_mixer(h, hist, n_hist, w, scale):
    B, T, D = h.shape
    xp = jnp.concatenate([hist.astype(h.dtype), h], axis=1)
    csum = jnp.concatenate([jnp.zeros((B, 1, D), jnp.float32),
                            jnp.cumsum(xp.astype(jnp.float32), axis=1)], axis=1)
    t = jnp.arange(T)
    hi = csum[:, POOL_HIST + 1:POOL_HIST + 1 + T]
    groups = []
    for gi, win in enumerate(POOL_WINDOWS):
        cs = slice(gi * POOL_GW, (gi + 1) * POOL_GW)
        lo = csum[:, POOL_HIST + 1 - win:POOL_HIST + 1 - win + T, cs]
        cnt = jnp.minimum(t + 1 + n_hist, win).astype(jnp.float32)[None, :, None]
        groups.append((hi[..., cs] - lo) / cnt)
    pooled = jnp.concatenate(groups, axis=-1) - h.astype(jnp.float32)
    mixed = jnp.einsum('btgc,gcd->btgd', pooled.reshape(B, T, POOL_GROUPS, POOL_GW), w.astype(jnp.float32))
    out = mixed.reshape(B, T, D) * scale.astype(jnp.float32)
    return out.astype(h.dtype), xp[:, -POOL_HIST:]


def partial_rope(x, pos):
    d = x.shape[-1]
    rd = d // ROPE_FRACTION
    half = rd // 2
    inv = ROPE_THETA ** (-jnp.arange(half, dtype=jnp.float32) / half)
    ang = pos.astype(jnp.float32)[:, None] * inv[None, :]
    cos = jnp.cos(ang)[:, None, :]
    sin = jnp.sin(ang)[:, None, :]
    xf = x.astype(jnp.float32)
    x1, x2 = xf[..., :half], xf[..., half:rd]
    out = jnp.concatenate([x1 * cos - x2 * sin, x2 * cos + x1 * sin, xf[..., rd:]], axis=-1)
    return out.astype(x.dtype)


def dsa_project(h, w_in, pos):
    B, T, _ = h.shape
    p = h @ w_in
    q = partial_rope(p[..., :OFF_K].reshape(B, T, ATTN_HEADS, ATTN_HEAD_DIM), pos)
    k = partial_rope(p[..., OFF_K:OFF_V][:, :, None], pos)[:, :, 0]
    v = p[..., OFF_V:OFF_QI]
    qi = partial_rope(p[..., OFF_QI:OFF_KI].reshape(B, T, IDX_HEADS, IDX_DIM), pos)
    ki = partial_rope(p[..., OFF_KI:OFF_WI][:, :, None], pos)[:, :, 0]
    wi = p[..., OFF_WI:]
    return q, k, v, qi, ki, wi


def dsa_attend(q, qi, wi, q_pos, keys_k, keys_v, keys_i, k_pos, topk):
    admissible = (k_pos[None, :] // CHUNK) <= (q_pos[:, None] // CHUNK)
    dots = jnp.einsum('bqhd,bld->bhql', qi.astype(jnp.float32), keys_i.astype(jnp.float32)) * IDX_DIM ** -0.5
    score = jnp.einsum('bhql,bqh->bql', jax.nn.relu(dots), wi.astype(jnp.float32) * IDX_HEADS ** -0.5)
    score = jnp.where(admissible[None], score, NEG_INF)
    top_score, idx = lax.top_k(score, topk)
    valid = top_score > 0.5 * NEG_INF
    take = jax.vmap(lambda rows, ix: rows[ix])
    k_sel = take(keys_k, idx).astype(jnp.float32)
    v_sel = take(keys_v, idx).astype(jnp.float32)
    logits = jnp.einsum('bqhd,bqkd->bhqk', q.astype(jnp.float32), k_sel) * ATTN_HEAD_DIM ** -0.5
    logits = jnp.where(valid[:, None], logits, NEG_INF)
    probs = jax.nn.softmax(logits, axis=-1)
    return jnp.einsum('bhqk,bqkd->bqhd', probs, v_sel).astype(q.dtype)


def dsa_prompt(h, w_in, w_out):
    B, S, _ = h.shape
    pos = jnp.arange(S)
    q, k, v, qi, ki, wi = dsa_project(h, w_in, pos)
    topk = min(TOPK_MAX, S // 4)

    def block(b0):
        sl = lambda a: lax.dynamic_slice_in_dim(a, b0, Q_BLOCK, axis=1)
        return dsa_attend(sl(q), sl(qi), sl(wi), b0 + jnp.arange(Q_BLOCK), k, v, ki, pos, topk)

    o = lax.map(block, jnp.arange(0, S, Q_BLOCK))
    o = jnp.moveaxis(o, 0, 1).reshape(B, S, Q_W)
    return o @ w_out, k, v, ki


def dsa_sample(h, ck, cv, cki, w_in, w_out):
    B, T, _ = h.shape
    P = ck.shape[1]
    q_pos = P + jnp.arange(T)
    q, k, v, qi, ki, wi = dsa_project(h, w_in, q_pos)
    keys_k = jnp.concatenate([ck.astype(k.dtype), k], axis=1)
    keys_v = jnp.concatenate([cv.astype(v.dtype), v], axis=1)
    keys_i = jnp.concatenate([cki.astype(ki.dtype), ki], axis=1)
    topk = min(TOPK_MAX, (P + T) // 4)
    o = dsa_attend(q, qi, wi, q_pos, keys_k, keys_v, keys_i, jnp.arange(P + T), topk)
    return o.reshape(B, T, Q_W) @ w_out, k, v, ki


def rwkv7_mixer(h, shift_prev, wkv0, mix, w0, w1, w2, a0, a1, a2, g1, g2,
                k_k, k_a, r_k, w_r, w_k, w_v, w_o, lnx_g, lnx_b):
    B, T, D = h.shape
    H, N = RWKV_HEADS, RWKV_HEAD
    f32 = jnp.float32
    hf = h.astype(f32)
    xx = jnp.concatenate([shift_prev.astype(f32), hf[:, :-1]], axis=1) - hf
    mix = mix.astype(f32)
    lerp = lambda j: hf + xx * mix[j]
    r = lerp(0) @ w_r
    wl = w0 + jnp.tanh(lerp(1) @ w1) @ w2
    decay = jnp.exp(-jnp.exp(-jax.nn.softplus(-wl) - 0.5))
    k = lerp(2) @ w_k
    v = lerp(3) @ w_v
    a = jax.nn.sigmoid(a0 + (lerp(4) @ a1) @ a2)
    g = jax.nn.sigmoid(lerp(5) @ g1) @ g2
    kk = (k * k_k).reshape(B, T, H, N)
    kk = kk * lax.rsqrt(jnp.maximum(jnp.sum(kk * kk, axis=-1, keepdims=True), 1e-24))
    k = k * (1.0 + (a - 1.0) * k_a)
    r, decay, k, v, a = tuple(z.reshape(B, T, H, N).astype(f32) for z in (r, decay, k, v, a))
    neg_kk, b = -kk, kk * a

    def step(state, inp):
        r_t, w_t, k_t, v_t, a_t, b_t = inp
        sa = jnp.einsum('bhvk,bhk->bhv', state, a_t)
        state = (state * w_t[:, :, None, :] + sa[..., None] * b_t[:, :, None, :]
                 + v_t[..., None] * k_t[:, :, None, :])
        return state, jnp.einsum('bhvk,bhk->bhv', state, r_t)

    seq = tuple(jnp.moveaxis(z, 1, 0) for z in (r, decay, k, v, neg_kk, b))
    wkv_t, ys = lax.scan(step, wkv0.astype(f32), seq)
    y = jnp.moveaxis(ys, 0, 1)
    mu = jnp.mean(y, axis=-1, keepdims=True)
    var = jnp.mean(jnp.square(y - mu), axis=-1, keepdims=True)
    yn = ((y - mu) * lax.rsqrt(var + LNX_EPS)).reshape(B, T, D) * lnx_g + lnx_b
    bonus = (jnp.sum(r * k * r_k, axis=-1, keepdims=True) * v).reshape(B, T, D)
    out = ((yn + bonus) * g) @ w_o
    return out.astype(h.dtype), h[:, -1:], wkv_t


def setup_inputs(seed: int = 0) -> dict:
    key = jax.random.key(seed)
    ks = iter(jax.random.split(key, 48))
    nrm = lambda shape, s: jax.random.normal(next(ks), shape, jnp.float32) * s
    D = D_MODEL
    H, N = RWKV_HEADS, RWKV_HEAD
    return {
        'x_prompt': nrm((BATCH, SEQ, D), 1.0),
        'x_sample': nrm((DEC_BATCH, DEC_SEQ, D), 1.0),
        'state_pool': nrm((N_POOL_LAYERS, DEC_BATCH, POOL_HIST, D), 1.0),
        'cache_k': nrm((N_ATTN_LAYERS, DEC_BATCH, PAST_LEN, ATTN_HEAD_DIM), 1.0),
        'cache_v': nrm((N_ATTN_LAYERS, DEC_BATCH, PAST_LEN, ATTN_HEAD_DIM), 1.0),
        'cache_kidx': nrm((N_ATTN_LAYERS, DEC_BATCH, PAST_LEN, IDX_DIM), 1.0),
        'state_shift': nrm((N_RWKV_LAYERS, DEC_BATCH, 1, D), 1.0),
        'state_wkv': nrm((N_RWKV_LAYERS, DEC_BATCH, H, N, N), 0.3),
        'ln1_g': 1.0 + nrm((DEPTH, D), 0.05),
        'ln2_g': 1.0 + nrm((DEPTH, D), 0.05),
        'w_up': nrm((DEPTH, D, D_FF), D ** -0.5),
        'w_down': nrm((DEPTH, D_FF, D), 0.5 * D_FF ** -0.5),
        'ln_f_g': 1.0 + nrm((D,), 0.05),
        'pool_w': nrm((N_POOL_LAYERS, POOL_GROUPS, POOL_GW, POOL_GW), POOL_GW ** -0.5),
        'pool_scale': 0.5 + nrm((N_POOL_LAYERS, D), 0.1),
        'attn_w_in': nrm((N_ATTN_LAYERS, D, ATTN_IN_W), D ** -0.5),
        'attn_w_out': nrm((N_ATTN_LAYERS, Q_W, D), Q_W ** -0.5),
        'rwkv_mix': jax.random.uniform(next(ks), (N_RWKV_LAYERS, 6, D), jnp.float32),
        'rwkv_w0': nrm((N_RWKV_LAYERS, D), 1.5) - 1.0,
        'rwkv_w1': nrm((N_RWKV_LAYERS, D, DECAY_LORA), 0.5 * D ** -0.5),
        'rwkv_w2': nrm((N_RWKV_LAYERS, DECAY_LORA, D), 0.5 * DECAY_LORA ** -0.5),
        'rwkv_a0': nrm((N_RWKV_LAYERS, D), 0.3),
        'rwkv_a1': nrm((N_RWKV_LAYERS, D, AAA_LORA), 0.5 * D ** -0.5),
        'rwkv_a2': nrm((N_RWKV_LAYERS, AAA_LORA, D), 0.5 * AAA_LORA ** -0.5),
        'rwkv_g1': nrm((N_RWKV_LAYERS, D, GATE_LORA), D ** -0.5),
        'rwkv_g2': nrm((N_RWKV_LAYERS, GATE_LORA, D), GATE_LORA ** -0.5),
        'rwkv_k_k': 0.85 + nrm((N_RWKV_LAYERS, D), 0.05),
        'rwkv_k_a': 1.0 + nrm((N_RWKV_LAYERS, D), 0.05),
        'rwkv_r_k': nrm((N_RWKV_LAYERS, H, N), 0.3),
        'rwkv_w_r': nrm((N_RWKV_LAYERS, D, D), D ** -0.5),
        'rwkv_w_k': nrm((N_RWKV_LAYERS, D, D), D ** -0.5),
        'rwkv_w_v': nrm((N_RWKV_LAYERS, D, D), D ** -0.5),
        'rwkv_w_o': nrm((N_RWKV_LAYERS, D, D), D ** -0.5),
        'rwkv_lnx_g': 1.0 + nrm((N_RWKV_LAYERS, D), 0.05),
        'rwkv_lnx_b': nrm((N_RWKV_LAYERS, D), 0.01),
    }


def reference(x_prompt, x_sample, state_pool, cache_k, cache_v, cache_kidx, state_shift, state_wkv,
              ln1_g, ln2_g, w_up, w_down, ln_f_g, pool_w, pool_scale, attn_w_in, attn_w_out,
              rwkv_mix, rwkv_w0, rwkv_w1, rwkv_w2, rwkv_a0, rwkv_a1, rwkv_a2, rwkv_g1, rwkv_g2,
              rwkv_k_k, rwkv_k_a, rwkv_r_k, rwkv_w_r, rwkv_w_k, rwkv_w_v, rwkv_w_o, rwkv_lnx_g, rwkv_lnx_b):
    xp, xs = x_prompt, x_sample
    bp = xp.shape[0]
    (pool_pl, pool_sl, k_pl, k_sl, v_pl, v_sl, ki_pl, ki_sl,
     sh_pl, sh_sl, wkv_pl, wkv_sl) = ([] for _ in range(12))
    for i in range(DEPTH):
        j = i // N_MIXERS
        hp = rms_norm(xp, ln1_g[i])
        hs = rms_norm(xs, ln1_g[i])
        if i % N_MIXERS == 0:
            zero_hist = jnp.zeros((bp, POOL_HIST, D_MODEL), hp.dtype)
            mp, sp = pool_mixer(hp, zero_hist, 0, pool_w[j], pool_scale[j])
            ms, ss = pool_mixer(hs, state_pool[j], PAST_LEN, pool_w[j], pool_scale[j])
            pool_pl.append(sp)
            pool_sl.append(ss)
        elif i % N_MIXERS == 1:
            mp, kp, vp, kip = dsa_prompt(hp, attn_w_in[j], attn_w_out[j])
            ms, kn, vn, kin = dsa_sample(hs, cache_k[j], cache_v[j], cache_kidx[j], attn_w_in[j], attn_w_out[j])
            k_pl.append(kp); v_pl.append(vp); ki_pl.append(kip)
            k_sl.append(kn); v_sl.append(vn); ki_sl.append(kin)
        else:
            rw = (rwkv_mix[j], rwkv_w0[j], rwkv_w1[j], rwkv_w2[j], rwkv_a0[j], rwkv_a1[j], rwkv_a2[j],
                  rwkv_g1[j], rwkv_g2[j], rwkv_k_k[j], rwkv_k_a[j], rwkv_r_k[j], rwkv_w_r[j], rwkv_w_k[j],
                  rwkv_w_v[j], rwkv_w_o[j], rwkv_lnx_g[j], rwkv_lnx_b[j])
            zero_shift = jnp.zeros((bp, 1, D_MODEL), hp.dtype)
            zero_wkv = jnp.zeros((bp, RWKV_HEADS, RWKV_HEAD, RWKV_HEAD), jnp.float32)
            mp, shp, wp = rwkv7_mixer(hp, zero_shift, zero_wkv, *rw)
            ms, shs, wsn = rwkv7_mixer(hs, state_shift[j], state_wkv[j], *rw)
            sh_pl.append(shp); wkv_pl.append(wp)
            sh_sl.append(shs); wkv_sl.append(wsn)
        xp = xp + mp
        xs = xs + ms
        xp = xp + squared_relu_mlp(rms_norm(xp, ln2_g[i]), w_up[i], w_down[i])
        xs = xs + squared_relu_mlp(rms_norm(xs, ln2_g[i]), w_up[i], w_down[i])
    y_prompt = rms_norm(xp, ln_f_g)
    y_sample = rms_norm(xs, ln_f_g)
    pool_p = jnp.stack(pool_pl, 0)
    pool_s = jnp.stack(pool_sl, 0)
    k_p = jnp.stack(k_pl, 0)
    k_s = jnp.stack(k_sl, 0)
    v_p = jnp.stack(v_pl, 0)
    v_s = jnp.stack(v_sl, 0)
    kidx_p = jnp.stack(ki_pl, 0)
    kidx_s = jnp.stack(ki_sl, 0)
    shift_p = jnp.stack(sh_pl, 0)
    shift_s = jnp.stack(sh_sl, 0)
    wkv_p = jnp.stack(wkv_pl, 0)
    wkv_s = jnp.stack(wkv_sl, 0)
    return (y_prompt, y_sample, pool_p, pool_s, k_p, k_s, v_p, v_s, kidx_p, kidx_s, shift_p, shift_s, wkv_p, wkv_s)
```

```python
import functools
import math

import jax
import jax.numpy as jnp
import numpy as np
from jax import lax
from jax.experimental import pallas as pl
from jax.experimental.pallas import tpu as pltpu

F32 = jnp.float32
BF16 = jnp.bfloat16
SDS = jax.ShapeDtypeStruct

D_MODEL = 1024
DEPTH = 4
N_MIXERS = 3
CHUNK = 64
D_FF = 4 * D_MODEL
RMS_EPS = 1e-6
POOL_WINDOWS = (2, 4, 8, 16)
POOL_GROUPS = 4
POOL_GW = D_MODEL // POOL_GROUPS
POOL_HIST = max(POOL_WINDOWS) - 1
POOL_PAD = POOL_HIST + 1
ATTN_HEADS = 8
ATTN_HEAD_DIM = D_MODEL // ATTN_HEADS
IDX_HEADS = 8
IDX_DIM = 64
TOPK_MAX = 256
ROPE_THETA = 500000.0
ROPE_FRACTION = 4
NEG_INF = -1e30
Q_W = ATTN_HEADS * ATTN_HEAD_DIM
OFF_K = Q_W
OFF_V = OFF_K + ATTN_HEAD_DIM
OFF_QI = OFF_V + ATTN_HEAD_DIM
OFF_KI = OFF_QI + IDX_HEADS * IDX_DIM
OFF_WI = OFF_KI + IDX_DIM
ATTN_IN_W = OFF_WI + IDX_HEADS
RWKV_HEAD = 64
RWKV_HEADS = D_MODEL // RWKV_HEAD
LNX_EPS = 64e-5

LANES = 128
SUBLANES = 8
ATTN_IN_PAD = ((ATTN_IN_W + LANES - 1) // LANES) * LANES
VMEM_LIMIT = 56 * 1024 * 1024
INT_MIN = -2 ** 31
HALF_NEG_KEY = int(np.float32(0.5 * NEG_INF).view(np.int32)) ^ 0x7FFFFFFF
SCAN_CHUNK = 64
FF_CHUNK = 1024


def _cparams(sem):
    return pltpu.CompilerParams(dimension_semantics=sem, vmem_limit_bytes=VMEM_LIMIT)


def _const_spec(shape):
    nd = len(shape)
    return pl.BlockSpec(shape, lambda *_: (0,) * nd)


def _rms(x, g):
    ms = jnp.mean(x * x, axis=-1, keepdims=True)
    return x * lax.rsqrt(ms + RMS_EPS) * g


def _dot(a, b):
    return jnp.dot(a, b, preferred_element_type=F32)


def _dot_nt(a, b):
    return lax.dot_general(a, b, (((1,), (1,)), ((), ())), preferred_element_type=F32)


def _dot_tn(a, b):
    return lax.dot_general(a, b, (((0,), (0,)), ((), ())), preferred_element_type=F32)


def _dot_split(a, m):
    a1 = a.astype(BF16)
    r1 = a - a1.astype(F32)
    a2 = r1.astype(BF16)
    a3 = (r1 - a2.astype(F32)).astype(BF16)
    return _dot(a1, m) + _dot(a2, m) + _dot(a3, m)


def _sigmoid(x):
    return 1.0 / (1.0 + jnp.exp(-x))


def _pick_tile(n, pref):
    t = min(n, pref)
    assert n % t == 0, (n, t)
    return t


def _mlp_kernel(x_ref, g_ref, wu_ref, wd_ref, gf_ref, o_ref, *, final_norm):
    x = x_ref[...]
    h = _rms(x, g_ref[...]).astype(BF16)
    acc = x
    for j in range(D_FF // FF_CHUNK):
        u = _dot(h, wu_ref[:, j * FF_CHUNK:(j + 1) * FF_CHUNK])
        u = jnp.square(jnp.maximum(u, 0.0)).astype(BF16)
        acc = acc + _dot(u, wd_ref[j * FF_CHUNK:(j + 1) * FF_CHUNK, :])
    if final_norm:
        acc = _rms(acc, gf_ref[...])
    o_ref[...] = acc


def _mlp(x, g, wu, wd, gf, final_norm):
    M = x.shape[0]
    tm = _pick_tile(M, 512)
    return pl.pallas_call(
        functools.partial(_mlp_kernel, final_norm=final_norm),
        grid=(M // tm,),
        in_specs=[
            pl.BlockSpec((tm, D_MODEL), lambda i: (i, 0)),
            _const_spec((1, D_MODEL)),
            _const_spec((D_MODEL, D_FF)),
            _const_spec((D_FF, D_MODEL)),
            _const_spec((1, D_MODEL)),
        ],
        out_specs=pl.BlockSpec((tm, D_MODEL), lambda i: (i, 0)),
        out_shape=SDS((M, D_MODEL), F32),
        compiler_params=_cparams(("parallel",)),
        name="mlp",
    )(x, g, wu, wd, gf)


def _pool_kernel(x_ref, xp_ref, hist_ref, g_ref, w_ref, sc_ref, o_ref, hs_ref, *, tt, n_hist):
    i = pl.program_id(1)
    g = g_ref[...]
    x = x_ref[...]
    h = _rms(x, g)
    prev = jnp.where(i == 0, hist_ref[...], _rms(xp_ref[...], g))
    hs_ref[...] = h[tt - POOL_PAD:, :]
    full = jnp.concatenate([prev, h], axis=0)
    t1 = lax.broadcasted_iota(jnp.int32, (tt, POOL_GW), 0) + (i * tt + 1 + n_hist)
    for gi, win in enumerate(POOL_WINDOWS):
        cs = slice(gi * POOL_GW, (gi + 1) * POOL_GW)
        s = full[:, cs]
        d = 1
        while d < win:
            s = s + pltpu.roll(s, d, 0)
            d *= 2
        s = s[POOL_PAD:, :]
        cnt = jnp.minimum(t1, win).astype(F32)
        pooled = s / cnt - h[:, cs]
        mixed = _dot(pooled.astype(BF16), w_ref[gi]) * sc_ref[:, cs]
        o_ref[:, cs] = x[:, cs] + mixed


def _pool_layer(x, hist, n_hist, g, w_bf, scale):
    B, T, _ = x.shape
    tt = _pick_tile(T, 512)
    rb = tt // POOL_PAD
    return pl.pallas_call(
        functools.partial(_pool_kernel, tt=tt, n_hist=n_hist),
        grid=(B, T // tt),
        in_specs=[
            pl.BlockSpec((None, tt, D_MODEL), lambda b, i: (b, i, 0)),
            pl.BlockSpec((None, POOL_PAD, D_MODEL), lambda b, i: (b, jnp.maximum(i * rb - 1, 0), 0)),
            pl.BlockSpec((None, POOL_PAD, D_MODEL), lambda b, i: (b, 0, 0)),
            _const_spec((1, D_MODEL)),
            _const_spec((POOL_GROUPS, POOL_GW, POOL_GW)),
            _const_spec((1, D_MODEL)),
        ],
        out_specs=[
            pl.BlockSpec((None, tt, D_MODEL), lambda b, i: (b, i, 0)),
            pl.BlockSpec((None, POOL_PAD, D_MODEL), lambda b, i: (b, 0, 0)),
        ],
        out_shape=[SDS((B, T, D_MODEL), F32), SDS((B, POOL_PAD, D_MODEL), F32)],
        compiler_params=_cparams(("parallel", "arbitrary")),
        name="pool",
    )(x, x, hist, g, w_bf, scale)


def _rope_tables(pos):
    T = pos.shape[0]
    posf = pos.astype(F32)

    def head(d):
        rd = d // ROPE_FRACTION
        half = rd // 2
        inv = ROPE_THETA ** (-jnp.arange(half, dtype=F32) / half)
        ang = posf[:, None] * inv[None, :]
        cos, sin = jnp.cos(ang), jnp.sin(ang)
        z = lambda n: jnp.zeros((T, n), F32)
        c = jnp.concatenate([cos, cos, jnp.ones((T, d - rd), F32)], axis=1)
        sa = jnp.concatenate([-sin, z(d - half)], axis=1)
        sb = jnp.concatenate([z(half), sin, z(d - rd)], axis=1)
        return c, sa, sb

    qa = head(ATTN_HEAD_DIM)
    ia = head(IDX_DIM)
    i2 = tuple(jnp.concatenate([t, t], axis=1) for t in ia)
    ones, zeros = jnp.ones((T, IDX_DIM), F32), jnp.zeros((T, IDX_DIM), F32)
    ik = (jnp.concatenate([ia[0], ones], axis=1), jnp.concatenate([ia[1], zeros], axis=1),
          jnp.concatenate([ia[2], zeros], axis=1))
    return jnp.concatenate(list(qa) + list(i2) + list(ik), axis=1)


def _aproj_kernel(x_ref, g_ref, w_ref, tab_ref, q_ref, kv_ref, qi_ref, kis_ref):
    h = _rms(x_ref[...], g_ref[...]).astype(BF16)
    p = _dot(h, w_ref[...])

    def rope(x, kind, half):
        c = tab_ref[:, (3 * kind) * LANES:(3 * kind + 1) * LANES]
        sa = tab_ref[:, (3 * kind + 1) * LANES:(3 * kind + 2) * LANES]
        sb = tab_ref[:, (3 * kind + 2) * LANES:(3 * kind + 3) * LANES]
        return x * c + pltpu.roll(x, LANES - half, 1) * sa + pltpu.roll(x, half, 1) * sb

    slab = lambda off: p[:, off:off + LANES]
    qh = ATTN_HEAD_DIM // ROPE_FRACTION // 2
    ih = IDX_DIM // ROPE_FRACTION // 2
    qscale = ATTN_HEAD_DIM ** -0.5
    for s in range(ATTN_HEADS):
        q_ref[:, s * LANES:(s + 1) * LANES] = (rope(slab(s * LANES), 0, qh) * qscale).astype(BF16)
    kv_ref[:, 0:LANES] = rope(slab(OFF_K), 0, qh)
    kv_ref[:, LANES:2 * LANES] = slab(OFF_V)
    for s in range(IDX_HEADS * IDX_DIM // LANES):
        qi_ref[:, s * LANES:(s + 1) * LANES] = rope(slab(OFF_QI + s * LANES), 1, ih).astype(BF16)
    kis_ref[...] = rope(slab(OFF_KI), 2, ih)


def _attn_project(x, g, w_in_pad, tab):
    B, T, _ = x.shape
    tt = _pick_tile(T, 512)
    QI_W = IDX_HEADS * IDX_DIM
    return pl.pallas_call(
        _aproj_kernel,
        grid=(T // tt, B),
        in_specs=[
            pl.BlockSpec((None, tt, D_MODEL), lambda i, b: (b, i, 0)),
            _const_spec((1, D_MODEL)),
            _const_spec((D_MODEL, ATTN_IN_PAD)),
            pl.BlockSpec((tt, 9 * LANES), lambda i, b: (i, 0)),
        ],
        out_specs=[
            pl.BlockSpec((None, tt, Q_W), lambda i, b: (b, i, 0)),
            pl.BlockSpec((None, tt, 2 * LANES), lambda i, b: (b, i, 0)),
            pl.BlockSpec((None, tt, QI_W), lambda i, b: (b, i, 0)),
            pl.BlockSpec((None, tt, LANES), lambda i, b: (b, i, 0)),
        ],
        out_shape=[SDS((B, T, Q_W), BF16), SDS((B, T, 2 * LANES), F32),
                   SDS((B, T, QI_W), BF16), SDS((B, T, LANES), F32)],
        compiler_params=_cparams(("arbitrary", "arbitrary")),
        name="attn_proj",
    )(x, g, w_in_pad, tab)


def _attn_kernel(x_ref, q_ref, qi_ref, wi_ref, ki_ref, k_ref, v_ref, wo_ref, o_ref,
                 key_scr, bias_scr, jstar_scr, o_scr, *, tq, L, q_off, n_keys, topk):
    j = pl.program_id(1)
    qpos = lax.broadcasted_iota(jnp.int32, (tq, L), 0) + (q_off + j * tq)
    kpos = lax.broadcasted_iota(jnp.int32, (tq, L), 1)
    adm = (kpos < n_keys) & ((kpos // CHUNK) <= (qpos // CHUNK))

    qi = qi_ref[...]
    ki = ki_ref[...]
    wis = wi_ref[...] * (IDX_DIM ** -0.5 * IDX_HEADS ** -0.5)
    score = jnp.zeros((tq, L), F32)
    for h in range(IDX_HEADS):
        d = _dot_nt(qi[:, h * IDX_DIM:(h + 1) * IDX_DIM], ki)
        score = score + jnp.maximum(d, 0.0) * wis[:, IDX_DIM + h:IDX_DIM + h + 1]
    score = jnp.where(adm, score, NEG_INF)

    kb = pltpu.bitcast(score, jnp.int32)
    key_scr[...] = jnp.where(kb >= 0, kb, kb ^ jnp.int32(0x7FFFFFFF))

    def count(mask):
        return jnp.sum(jnp.where(mask, 1.0, 0.0), axis=1, keepdims=True)

    def vbody(it, lo_u):
        trial_u = lo_u | lax.shift_left(jnp.int32(1), 31 - it)
        c = count(key_scr[...] >= (trial_u ^ jnp.int32(INT_MIN)))
        return jnp.where(c >= topk, trial_u, lo_u)

    lo_u = lax.fori_loop(0, 32, vbody, jnp.zeros((tq, 1), jnp.int32))
    lo = lo_u ^ jnp.int32(INT_MIN)

    key = key_scr[...]
    gt = key > lo
    eq = key == lo
    need = topk - count(gt)
    lo_real = lo > jnp.int32(HALF_NEG_KEY)
    tie = lo_real & (count(eq) > need)
    jstar_scr[...] = jnp.full((tq, 1), L, jnp.int32)

    @pl.when(jnp.max(jnp.where(tie, 1.0, 0.0)) > 0.0)
    def _():
        nbits = max(1, (L - 1).bit_length())

        def ibody(it, lo_i):
            trial = lo_i + lax.shift_left(jnp.int32(1), nbits - 1 - it)
            c = count((kpos < trial) & (key_scr[...] == lo))
            return jnp.where(c < need, trial, lo_i)

        lo_i = lax.fori_loop(0, nbits, ibody, jnp.zeros((tq, 1), jnp.int32))
        jstar_scr[...] = jnp.where(tie, lo_i, L)

    keep = gt | (eq & (kpos <= jstar_scr[...]))
    bias_scr[...] = jnp.where(adm & keep, 0.0, NEG_INF)

    kk = k_ref[...]
    vv = v_ref[...]
    for h in range(ATTN_HEADS):
        hs = slice(h * ATTN_HEAD_DIM, (h + 1) * ATTN_HEAD_DIM)
        lg = _dot_nt(q_ref[:, hs], kk) + bias_scr[...]
        m = jnp.max(lg, axis=1, keepdims=True)
        p = jnp.exp(lg - m)
        den = jnp.sum(p, axis=1, keepdims=True)
        o_scr[:, hs] = (_dot(p.astype(BF16), vv) / den).astype(BF16)
    o_ref[...] = x_ref[...] + _dot(o_scr[...], wo_ref[...])


def _attn_layer(x, q, qi, kis, ki_all, k_all, v_all, wo_bf, q_off, n_keys, topk, tq):
    B, T, _ = x.shape
    L = k_all.shape[1]
    QI_W = IDX_HEADS * IDX_DIM
    return pl.pallas_call(
        functools.partial(_attn_kernel, tq=tq, L=L, q_off=q_off, n_keys=n_keys, topk=topk),
        grid=(B, T // tq),
        in_specs=[
            pl.BlockSpec((None, tq, D_MODEL), lambda b, j: (b, j, 0)),
            pl.BlockSpec((None, tq, Q_W), lambda b, j: (b, j, 0)),
            pl.BlockSpec((None, tq, QI_W), lambda b, j: (b, j, 0)),
            pl.BlockSpec((None, tq, LANES), lambda b, j: (b, j, 0)),
            pl.BlockSpec((None, L, IDX_DIM), lambda b, j: (b, 0, 0)),
            pl.BlockSpec((None, L, ATTN_HEAD_DIM), lambda b, j: (b, 0, 0)),
            pl.BlockSpec((None, L, ATTN_HEAD_DIM), lambda b, j: (b, 0, 0)),
            _const_spec((Q_W, D_MODEL)),
        ],
        out_specs=pl.BlockSpec((None, tq, D_MODEL), lambda b, j: (b, j, 0)),
        out_shape=SDS((B, T, D_MODEL), F32),
        scratch_shapes=[
            pltpu.VMEM((tq, L), jnp.int32),
            pltpu.VMEM((tq, L), F32),
            pltpu.VMEM((tq, 1), jnp.int32),
            pltpu.VMEM((tq, Q_W), BF16),
        ],
        compiler_params=_cparams(("parallel", "arbitrary")),
        name="attn",
    )(x, q, qi, kis, ki_all, k_all, v_all, wo_bf)


def _head_sum(z, e_ref, et_ref):
    return _dot_split(_dot_split(z, e_ref[...]), et_ref[...])


def _rproj_kernel(x_ref, xp_ref, sh_ref, g_ref, mix_ref, w0_ref, w1_ref, w2_ref, a0_ref, a1_ref, a2_ref,
                  g1_ref, g2_ref, kk_ref, ka_ref, wr_ref, wk_ref, wv_ref, e_ref, et_ref,
                  r_o, lw_o, k_o, v_o, na_o, b_o, g_o, hl_o, *, tt):
    i = pl.program_id(1)
    g = g_ref[...]
    h = _rms(x_ref[...], g)
    hl_o[...] = h[tt - SUBLANES:, :]
    prev = jnp.where(i == 0, sh_ref[...], _rms(xp_ref[SUBLANES - 1:SUBLANES, :], g))
    row = lax.broadcasted_iota(jnp.int32, (tt, D_MODEL), 0)
    xx = jnp.where(row == 0, prev, pltpu.roll(h, 1, 0)) - h
    lerp = lambda n: (h + xx * mix_ref[n:n + 1, :]).astype(BF16)
    r = _dot(lerp(0), wr_ref[...])
    wl = w0_ref[...] + _dot(jnp.tanh(_dot(lerp(1), w1_ref[...])).astype(BF16), w2_ref[...])
    lw = -_sigmoid(wl) * math.exp(-0.5)
    k = _dot(lerp(2), wk_ref[...])
    v = _dot(lerp(3), wv_ref[...])
    a = _sigmoid(a0_ref[...] + _dot(_dot(lerp(4), a1_ref[...]).astype(BF16), a2_ref[...]))
    gate = _dot(_sigmoid(_dot(lerp(5), g1_ref[...])).astype(BF16), g2_ref[...])
    kk = k * kk_ref[...]
    kk = kk * lax.rsqrt(jnp.maximum(_head_sum(kk * kk, e_ref, et_ref), 1e-24))
    r_o[...] = r
    lw_o[...] = lw
    k_o[...] = k * (1.0 + (a - 1.0) * ka_ref[...])
    v_o[...] = v
    na_o[...] = -kk
    b_o[...] = kk * a
    g_o[...] = gate


def _rwkv_project(x, shift_prev, g, rw):
    B, T, _ = x.shape
    tt = _pick_tile(T, 256)
    rb = tt // SUBLANES
    tok = pl.BlockSpec((None, tt, D_MODEL), lambda b, i: (b, i, 0))
    consts = [g, rw["mix"], rw["w0"], rw["w1"], rw["w2"], rw["a0"], rw["a1"], rw["a2"], rw["g1"], rw["g2"],
              rw["k_k"], rw["k_a"], rw["w_r"], rw["w_k"], rw["w_v"], rw["e"], rw["et"]]
    return pl.pallas_call(
        functools.partial(_rproj_kernel, tt=tt),
        grid=(B, T // tt),
        in_specs=[
            tok,
            pl.BlockSpec((None, SUBLANES, D_MODEL), lambda b, i: (b, jnp.maximum(i * rb - 1, 0), 0)),
            pl.BlockSpec((None, 1, D_MODEL), lambda b, i: (b, 0, 0)),
        ] + [_const_spec(c.shape) for c in consts],
        out_specs=[tok] * 7 + [pl.BlockSpec((None, SUBLANES, D_MODEL), lambda b, i: (b, 0, 0))],
        out_shape=[SDS((B, T, D_MODEL), F32)] * 7 + [SDS((B, SUBLANES, D_MODEL), F32)],
        compiler_params=_cparams(("parallel", "arbitrary")),
        name="rwkv_proj",
    )(x, x, shift_prev, *consts)


def _scan_kernel(r_ref, lw_ref, k_ref, v_ref, a_ref, b_ref, s0_ref, y_ref, st_ref, s_scr, *, tt, C):
    N = RWKV_HEAD

    @pl.when(pl.program_id(2) == 0)
    def _():
        s_scr[...] = s0_ref[...]

    row = lax.broadcasted_iota(jnp.int32, (C, LANES), 0)
    rr = lax.broadcasted_iota(jnp.int32, (C, C), 0)
    cc = lax.broadcasted_iota(jnp.int32, (C, C), 1)
    strict = rr > cc
    incl = rr >= cc

    def chunk(c, carry):
        sl = pl.ds(pl.multiple_of(c * C, C), C)
        lw = lw_ref[sl, :]
        cum = lw
        d = 1
        while d < C:
            cum = cum + jnp.where(row >= d, pltpu.roll(cum, d, 0), 0.0)
            d *= 2
        e_w = jnp.exp(cum)
        e_n = jnp.exp(-cum)
        at = a_ref[sl, :] * jnp.exp(cum - lw)
        rt = r_ref[sl, :] * e_w
        bt = b_ref[sl, :] * e_n
        kt = k_ref[sl, :] * e_n
        vv = v_ref[sl, :]
        w_c = e_w[C - 1:C, :]
        ys = []
        for hh in range(LANES // N):
            ls = slice(hh * N, (hh + 1) * N)
            s0 = s_scr[hh]
            m = jnp.concatenate([at[:, ls], rt[:, ls]], axis=0).astype(BF16)
            bk = jnp.concatenate([bt[:, ls], kt[:, ls]], axis=0).astype(BF16)
            v_h = vv[:, ls].astype(BF16)
            gm = _dot_nt(m, bk)
            hm = _dot_nt(m, s0.astype(BF16))
            l_ab = jnp.where(strict, gm[:C, :C], 0.0)
            l_ak = jnp.where(strict, gm[:C, C:], 0.0)
            g_rb = jnp.where(incl, gm[C:, :C], 0.0)
            g_rk = jnp.where(incl, gm[C:, C:], 0.0)
            u = hm[:C] + _dot(l_ak.astype(BF16), v_h)
            pw = l_ab
            n = 1
            while n < C:
                pb = pw.astype(BF16)
                u = u + _dot(pb, u.astype(BF16))
                n *= 2
                if n < C:
                    pw = _dot(pb, pb)
            ub = u.astype(BF16)
            ys.append(hm[C:] + _dot(g_rb.astype(BF16), ub) + _dot(g_rk.astype(BF16), v_h))
            s_new = s0 + _dot_tn(ub, bk[:C]) + _dot_tn(v_h, bk[C:])
            s_scr[hh] = s_new * w_c[:, ls]
        y_ref[sl, :] = jnp.concatenate(ys, axis=1)
        return carry

    lax.fori_loop(0, tt // C, chunk, 0)
    st_ref[...] = s_scr[...]


def _rwkv_scan(r, lw, k, v, na, b, wkv0):
    B, T, _ = r.shape
    tt = _pick_tile(T, 256)
    C = _pick_tile(tt, SCAN_CHUNK)
    hp = LANES // RWKV_HEAD
    tok = pl.BlockSpec((None, tt, LANES), lambda b_, p, i: (b_, i, p))
    st = pl.BlockSpec((None, hp, RWKV_HEAD, RWKV_HEAD), lambda b_, p, i: (b_, p, 0, 0))
    return pl.pallas_call(
        functools.partial(_scan_kernel, tt=tt, C=C),
        grid=(B, RWKV_HEADS // hp, T // tt),
        in_specs=[tok] * 6 + [st],
        out_specs=[tok, st],
        out_shape=[SDS((B, T, D_MODEL), F32), SDS((B, RWKV_HEADS, RWKV_HEAD, RWKV_HEAD), F32)],
        scratch_shapes=[pltpu.VMEM((hp, RWKV_HEAD, RWKV_HEAD), F32)],
        compiler_params=_cparams(("parallel", "parallel", "arbitrary")),
        name="rwkv_scan",
    )(r, lw, k, v, na, b, wkv0)


def _rout_kernel(x_ref, y_ref, r_ref, k_ref, v_ref, g_ref, rk_ref, lg_ref, lb_ref, wo_ref, e_ref, et_ref, o_ref):
    y = y_ref[...]
    inv_n = 1.0 / RWKV_HEAD
    mu = _head_sum(y, e_ref, et_ref) * inv_n
    yc = y - mu
    var = _head_sum(yc * yc, e_ref, et_ref) * inv_n
    yn = yc * lax.rsqrt(var + LNX_EPS) * lg_ref[...] + lb_ref[...]
    bonus = _head_sum(r_ref[...] * k_ref[...] * rk_ref[...], e_ref, et_ref) * v_ref[...]
    z = ((yn + bonus) * g_ref[...]).astype(BF16)
    o_ref[...] = x_ref[...] + _dot(z, wo_ref[...])


def _rwkv_out(x, y, r, k, v, gate, rw):
    M = x.shape[0]
    tm = _pick_tile(M, 512)
    tok = pl.BlockSpec((tm, D_MODEL), lambda i: (i, 0))
    consts = [rw["r_k"], rw["lnx_g"], rw["lnx_b"], rw["w_o"], rw["e"], rw["et"]]
    return pl.pallas_call(
        _rout_kernel,
        grid=(M // tm,),
        in_specs=[tok] * 6 + [_const_spec(c.shape) for c in consts],
        out_specs=tok,
        out_shape=SDS((M, D_MODEL), F32),
        compiler_params=_cparams(("parallel",)),
        name="rwkv_out",
    )(x, y, r, k, v, gate, *consts)


def _pool_block(x, state, n_hist, g, w_bf, scale):
    B = x.shape[0]
    if state is None:
        hist = jnp.zeros((B, POOL_PAD, D_MODEL), F32)
    else:
        hist = jnp.pad(state, ((0, 0), (POOL_PAD - POOL_HIST, 0), (0, 0)))
    out, hs = _pool_layer(x, hist, n_hist, g, w_bf, scale)
    return out, hs[:, POOL_PAD - POOL_HIST:]


def _attn_block(x, cache, g, w_in_pad, wo_bf, q_off, tq):
    B, T, _ = x.shape
    tab = _rope_tables(q_off + jnp.arange(T))
    q, kv, qi, kis = _attn_project(x, g, w_in_pad, tab)
    k_new, v_new, ki_new = kv[..., :LANES], kv[..., LANES:], kis[..., :IDX_DIM]
    if cache is None:
        keys = (ki_new, k_new, v_new)
    else:
        ck, cv, cki = cache
        keys = (jnp.concatenate([cki, ki_new], axis=1), jnp.concatenate([ck, k_new], axis=1),
                jnp.concatenate([cv, v_new], axis=1))
    n_keys = keys[0].shape[1]
    L = ((n_keys + LANES - 1) // LANES) * LANES
    keys = tuple(jnp.pad(a.astype(BF16), ((0, 0), (0, L - n_keys), (0, 0))) for a in keys)
    topk = min(TOPK_MAX, n_keys // 4)
    out = _attn_layer(x, q, qi, kis, keys[0], keys[1], keys[2], wo_bf, q_off, n_keys, topk, tq)
    return out, k_new, v_new, ki_new


def _rwkv_block(x, shift_prev, wkv0, g, rw):
    B, T, _ = x.shape
    r, lw, k, v, na, b, gate, hl = _rwkv_project(x, shift_prev, g, rw)
    y, s_t = _rwkv_scan(r, lw, k, v, na, b, wkv0)
    flat = lambda a: a.reshape(B * T, D_MODEL)
    out = _rwkv_out(flat(x), flat(y), flat(r), flat(k), flat(v), flat(gate), rw).reshape(B, T, D_MODEL)
    return out, hl[:, SUBLANES - 1:], s_t


def kernel(x_prompt, x_sample, state_pool, cache_k, cache_v, cache_kidx, state_shift, state_wkv, ln1_g, ln2_g, w_up, w_down, ln_f_g, pool_w, pool_scale, attn_w_in, attn_w_out, rwkv_mix, rwkv_w0, rwkv_w1, rwkv_w2, rwkv_a0, rwkv_a1, rwkv_a2, rwkv_g1, rwkv_g2, rwkv_k_k, rwkv_k_a, rwkv_r_k, rwkv_w_r, rwkv_w_k, rwkv_w_v, rwkv_w_o, rwkv_lnx_g, rwkv_lnx_b):
    xp, xs = x_prompt, x_sample
    bp, sp, _ = xp.shape
    bs, ss, _ = xs.shape
    past = cache_k.shape[2]
    row = lambda a: a.reshape(1, -1)
    bf = lambda a: a.astype(BF16)
    head_of = jnp.arange(D_MODEL) // RWKV_HEAD
    e_mat = (head_of[:, None] == jnp.arange(LANES)[None, :]).astype(BF16)
    outs = {n: [] for n in ("pool_p", "pool_s", "k_p", "k_s", "v_p", "v_s", "ki_p", "ki_s",
                            "sh_p", "sh_s", "wkv_p", "wkv_s")}
    for i in range(DEPTH):
        j = i // N_MIXERS
        g1 = row(ln1_g[i])
        if i % N_MIXERS == 0:
            w_bf = bf(pool_w[j])
            sc = row(pool_scale[j])
            xp, st_p = _pool_block(xp, None, 0, g1, w_bf, sc)
            xs, st_s = _pool_block(xs, state_pool[j], past, g1, w_bf, sc)
            outs["pool_p"].append(st_p)
            outs["pool_s"].append(st_s)
        elif i % N_MIXERS == 1:
            w_in_pad = jnp.pad(bf(attn_w_in[j]), ((0, 0), (0, ATTN_IN_PAD - ATTN_IN_W)))
            wo_bf = bf(attn_w_out[j])
            xp, kp, vp, kip = _attn_block(xp, None, g1, w_in_pad, wo_bf, 0, 128)
            xs, kn, vn, kin = _attn_block(xs, (cache_k[j], cache_v[j], cache_kidx[j]), g1, w_in_pad, wo_bf,
                                          past, ss)
            for n, a in (("k_p", kp), ("v_p", vp), ("ki_p", kip), ("k_s", kn), ("v_s", vn), ("ki_s", kin)):
                outs[n].append(a)
        else:
            rw = dict(mix=rwkv_mix[j], w0=row(rwkv_w0[j]), w1=bf(rwkv_w1[j]), w2=bf(rwkv_w2[j]),
                      a0=row(rwkv_a0[j]), a1=bf(rwkv_a1[j]), a2=bf(rwkv_a2[j]), g1=bf(rwkv_g1[j]),
                      g2=bf(rwkv_g2[j]), k_k=row(rwkv_k_k[j]), k_a=row(rwkv_k_a[j]), r_k=row(rwkv_r_k[j]),
                      w_r=bf(rwkv_w_r[j]), w_k=bf(rwkv_w_k[j]), w_v=bf(rwkv_w_v[j]), w_o=bf(rwkv_w_o[j]),
                      lnx_g=row(rwkv_lnx_g[j]), lnx_b=row(rwkv_lnx_b[j]), e=e_mat, et=e_mat.T)
            zero_shift = jnp.zeros((bp, 1, D_MODEL), F32)
            zero_wkv = jnp.zeros((bp, RWKV_HEADS, RWKV_HEAD, RWKV_HEAD), F32)
            xp, shp, wp = _rwkv_block(xp, zero_shift, zero_wkv, g1, rw)
            xs, shs, wsn = _rwkv_block(xs, state_shift[j], state_wkv[j], g1, rw)
            outs["sh_p"].append(shp)
            outs["sh_s"].append(shs)
            outs["wkv_p"].append(wp)
            outs["wkv_s"].append(wsn)
        last = i == DEPTH - 1
        g2 = row(ln2_g[i])
        gf = row(ln_f_g)
        wu, wd = bf(w_up[i]), bf(w_down[i])
        xp = _mlp(xp.reshape(bp * sp, D_MODEL), g2, wu, wd, gf, last).reshape(bp, sp, D_MODEL)
        xs = _mlp(xs.reshape(bs * ss, D_MODEL), g2, wu, wd, gf, last).reshape(bs, ss, D_MODEL)
    st = lambda n: jnp.stack(outs[n], 0)
    return (xp, xs, st("pool_p"), st("pool_s"), st("k_p"), st("k_s"), st("v_p"), st("v_s"),
            st("ki_p"), st("ki_s"), st("sh_p"), st("sh_s"), st("wkv_p"), st("wkv_s"))
```

```python
import functools
import math

import jax
import jax.numpy as jnp
import numpy as np
from jax import lax
from jax.experimental import pallas as pl
from jax.experimental.pallas import tpu as pltpu

F32 = jnp.float32
BF16 = jnp.bfloat16
SDS = jax.ShapeDtypeStruct

D_MODEL = 1024
DEPTH = 4
N_MIXERS = 3
CHUNK = 64
D_FF = 4 * D_MODEL
RMS_EPS = 1e-6
POOL_WINDOWS = (2, 4, 8, 16)
POOL_GROUPS = 4
POOL_GW = D_MODEL // POOL_GROUPS
POOL_HIST = max(POOL_WINDOWS) - 1
POOL_PAD = POOL_HIST + 1
ATTN_HEADS = 8
ATTN_HEAD_DIM = D_MODEL // ATTN_HEADS
IDX_HEADS = 8
IDX_DIM = 64
TOPK_MAX = 256
ROPE_THETA = 500000.0
ROPE_FRACTION = 4
NEG_INF = -1e30
Q_W = ATTN_HEADS * ATTN_HEAD_DIM
OFF_K = Q_W
OFF_V = OFF_K + ATTN_HEAD_DIM
OFF_QI = OFF_V + ATTN_HEAD_DIM
OFF_KI = OFF_QI + IDX_HEADS * IDX_DIM
OFF_WI = OFF_KI + IDX_DIM
ATTN_IN_W = OFF_WI + IDX_HEADS
RWKV_HEAD = 64
RWKV_HEADS = D_MODEL // RWKV_HEAD
LNX_EPS = 64e-5

LANES = 128
SUBLANES = 8
ATTN_IN_PAD = ((ATTN_IN_W + LANES - 1) // LANES) * LANES
MXU_DIM = 256
VMEM_LIMIT = 56 * 1024 * 1024
INT_MIN = -2 ** 31
HALF_NEG_KEY = int(np.float32(0.5 * NEG_INF).view(np.int32)) ^ 0x7FFFFFFF
ATTN_BUCKET_QBLOCKS = 1
SCAN_CHUNK = 64
SCAN_SEQS = 2
FF_CHUNK = 1024


def _cparams(sem):
    return pltpu.CompilerParams(dimension_semantics=sem, vmem_limit_bytes=VMEM_LIMIT)


def _const_spec(shape):
    nd = len(shape)
    return pl.BlockSpec(shape, lambda *_: (0,) * nd)


def _rms(x, g):
    ms = jnp.mean(x * x, axis=-1, keepdims=True)
    return x * lax.rsqrt(ms + RMS_EPS) * g


def _dot(a, b):
    return jnp.dot(a, b, preferred_element_type=F32)


def _dot_nt(a, b):
    return lax.dot_general(a, b, (((1,), (1,)), ((), ())), preferred_element_type=F32)


def _dot_tn(a, b):
    return lax.dot_general(a, b, (((0,), (0,)), ((), ())), preferred_element_type=F32)


def _head_sum(z, bd_ref):
    bd = bd_ref[...]
    hi = z.astype(BF16)
    lo = (z - hi.astype(F32)).astype(BF16)
    outs = []
    for c in range(D_MODEL // MXU_DIM):
        cs = slice(c * MXU_DIM, (c + 1) * MXU_DIM)
        outs.append(_dot(hi[:, cs], bd) + _dot(lo[:, cs], bd))
    return jnp.concatenate(outs, axis=1)


def _sigmoid(x):
    return 1.0 / (1.0 + jnp.exp(-x))


def _pick_tile(n, pref):
    t = min(n, pref)
    assert n % t == 0, (n, t)
    return t


def _mlp_kernel(x_ref, g_ref, wu_ref, wd_ref, gf_ref, o_ref, *, final_norm):
    x = x_ref[...]
    h = _rms(x, g_ref[...]).astype(BF16)
    acc = x
    for j in range(D_FF // FF_CHUNK):
        u = _dot(h, wu_ref[:, j * FF_CHUNK:(j + 1) * FF_CHUNK])
        u = jnp.square(jnp.maximum(u, 0.0)).astype(BF16)
        acc = acc + _dot(u, wd_ref[j * FF_CHUNK:(j + 1) * FF_CHUNK, :])
    if final_norm:
        acc = _rms(acc, gf_ref[...])
    o_ref[...] = acc


def _mlp(x, g, wu, wd, gf, final_norm):
    M = x.shape[0]
    tm = _pick_tile(M, 512)
    return pl.pallas_call(
        functools.partial(_mlp_kernel, final_norm=final_norm),
        grid=(M // tm,),
        in_specs=[
            pl.BlockSpec((tm, D_MODEL), lambda i: (i, 0)),
            _const_spec((1, D_MODEL)),
            _const_spec((D_MODEL, D_FF)),
            _const_spec((D_FF, D_MODEL)),
            _const_spec((1, D_MODEL)),
        ],
        out_specs=pl.BlockSpec((tm, D_MODEL), lambda i: (i, 0)),
        out_shape=SDS((M, D_MODEL), F32),
        compiler_params=_cparams(("parallel",)),
        name="mlp",
    )(x, g, wu, wd, gf)


def _pool_kernel(x_ref, xp_ref, hist_ref, g_ref, w_ref, sc_ref, o_ref, hs_ref, *, tt, n_hist):
    i = pl.program_id(1)
    g = g_ref[...]
    x = x_ref[...]
    h = _rms(x, g)
    prev = jnp.where(i == 0, hist_ref[...], _rms(xp_ref[...], g))
    hs_ref[...] = h[tt - POOL_PAD:, :]
    full = jnp.concatenate([prev, h], axis=0)
    t1 = lax.broadcasted_iota(jnp.int32, (tt, POOL_GW), 0) + (i * tt + 1 + n_hist)
    for gi, win in enumerate(POOL_WINDOWS):
        cs = slice(gi * POOL_GW, (gi + 1) * POOL_GW)
        s = full[:, cs]
        d = 1
        while d < win:
            s = s + pltpu.roll(s, d, 0)
            d *= 2
        s = s[POOL_PAD:, :]
        cnt = jnp.minimum(t1, win).astype(F32)
        pooled = s / cnt - h[:, cs]
        mixed = _dot(pooled.astype(BF16), w_ref[gi]) * sc_ref[:, cs]
        o_ref[:, cs] = x[:, cs] + mixed


def _pool_layer(x, hist, n_hist, g, w_bf, scale):
    B, T, _ = x.shape
    tt = _pick_tile(T, 512)
    rb = tt // POOL_PAD
    return pl.pallas_call(
        functools.partial(_pool_kernel, tt=tt, n_hist=n_hist),
        grid=(B, T // tt),
        in_specs=[
            pl.BlockSpec((None, tt, D_MODEL), lambda b, i: (b, i, 0)),
            pl.BlockSpec((None, POOL_PAD, D_MODEL), lambda b, i: (b, jnp.maximum(i * rb - 1, 0), 0)),
            pl.BlockSpec((None, POOL_PAD, D_MODEL), lambda b, i: (b, 0, 0)),
            _const_spec((1, D_MODEL)),
            _const_spec((POOL_GROUPS, POOL_GW, POOL_GW)),
            _const_spec((1, D_MODEL)),
        ],
        out_specs=[
            pl.BlockSpec((None, tt, D_MODEL), lambda b, i: (b, i, 0)),
            pl.BlockSpec((None, POOL_PAD, D_MODEL), lambda b, i: (b, 0, 0)),
        ],
        out_shape=[SDS((B, T, D_MODEL), F32), SDS((B, POOL_PAD, D_MODEL), F32)],
        compiler_params=_cparams(("parallel", "arbitrary")),
        name="pool",
    )(x, x, hist, g, w_bf, scale)


def _rope_tables(pos):
    T = pos.shape[0]
    posf = pos.astype(F32)

    def head(d):
        rd = d // ROPE_FRACTION
        half = rd // 2
        inv = ROPE_THETA ** (-jnp.arange(half, dtype=F32) / half)
        ang = posf[:, None] * inv[None, :]
        cos, sin = jnp.cos(ang), jnp.sin(ang)
        z = lambda n: jnp.zeros((T, n), F32)
        c = jnp.concatenate([cos, cos, jnp.ones((T, d - rd), F32)], axis=1)
        sa = jnp.concatenate([-sin, z(d - half)], axis=1)
        sb = jnp.concatenate([z(half), sin, z(d - rd)], axis=1)
        return c, sa, sb

    qa = head(ATTN_HEAD_DIM)
    ia = head(IDX_DIM)
    i2 = tuple(jnp.concatenate([t, t], axis=1) for t in ia)
    ones, zeros = jnp.ones((T, IDX_DIM), F32), jnp.zeros((T, IDX_DIM), F32)
    ik = (jnp.concatenate([ia[0], ones], axis=1), jnp.concatenate([ia[1], zeros], axis=1),
          jnp.concatenate([ia[2], zeros], axis=1))
    return jnp.concatenate(list(qa) + list(i2) + list(ik), axis=1)


def _aproj_kernel(x_ref, g_ref, w_ref, tab_ref, q_ref, kv_ref, qi_ref, kis_ref):
    h = _rms(x_ref[...], g_ref[...]).astype(BF16)
    p = _dot(h, w_ref[...])

    def rope(x, kind, half):
        c = tab_ref[:, (3 * kind) * LANES:(3 * kind + 1) * LANES]
        sa = tab_ref[:, (3 * kind + 1) * LANES:(3 * kind + 2) * LANES]
        sb = tab_ref[:, (3 * kind + 2) * LANES:(3 * kind + 3) * LANES]
        return x * c + pltpu.roll(x, LANES - half, 1) * sa + pltpu.roll(x, half, 1) * sb

    slab = lambda off: p[:, off:off + LANES]
    qh = ATTN_HEAD_DIM // ROPE_FRACTION // 2
    ih = IDX_DIM // ROPE_FRACTION // 2
    qscale = ATTN_HEAD_DIM ** -0.5 * math.log2(math.e)
    for s in range(ATTN_HEADS):
        q_ref[:, s * LANES:(s + 1) * LANES] = (rope(slab(s * LANES), 0, qh) * qscale).astype(BF16)
    kv_ref[:, 0:LANES] = rope(slab(OFF_K), 0, qh)
    kv_ref[:, LANES:2 * LANES] = slab(OFF_V)
    for s in range(IDX_HEADS * IDX_DIM // LANES):
        qi_ref[:, s * LANES:(s + 1) * LANES] = rope(slab(OFF_QI + s * LANES), 1, ih).astype(BF16)
    kis_ref[...] = rope(slab(OFF_KI), 2, ih)


def _attn_project(x, g, w_in_pad, tab):
    B, T, _ = x.shape
    tt = _pick_tile(T, 512)
    QI_W = IDX_HEADS * IDX_DIM
    return pl.pallas_call(
        _aproj_kernel,
        grid=(T // tt, B),
        in_specs=[
            pl.BlockSpec((None, tt, D_MODEL), lambda i, b: (b, i, 0)),
            _const_spec((1, D_MODEL)),
            _const_spec((D_MODEL, ATTN_IN_PAD)),
            pl.BlockSpec((tt, 9 * LANES), lambda i, b: (i, 0)),
        ],
        out_specs=[
            pl.BlockSpec((None, tt, Q_W), lambda i, b: (b, i, 0)),
            pl.BlockSpec((None, tt, 2 * LANES), lambda i, b: (b, i, 0)),
            pl.BlockSpec((None, tt, QI_W), lambda i, b: (b, i, 0)),
            pl.BlockSpec((None, tt, LANES), lambda i, b: (b, i, 0)),
        ],
        out_shape=[SDS((B, T, Q_W), BF16), SDS((B, T, 2 * LANES), F32),
                   SDS((B, T, QI_W), BF16), SDS((B, T, LANES), F32)],
        compiler_params=_cparams(("arbitrary", "arbitrary")),
        name="attn_proj",
    )(x, g, w_in_pad, tab)


def _attn_kernel(x_ref, q_ref, qi_ref, wi_ref, ki_ref, k_ref, v_ref, wo_ref, o_ref,
                 key_scr, bias_scr, jstar_scr, o_scr, *, tq, L, q_off, n_keys, topk):
    j = pl.program_id(1)
    qpos = lax.broadcasted_iota(jnp.int32, (tq, L), 0) + (q_off + j * tq)
    kpos = lax.broadcasted_iota(jnp.int32, (tq, L), 1)
    adm = (kpos < n_keys) & ((kpos // CHUNK) <= (qpos // CHUNK))

    qi = qi_ref[...]
    ki = ki_ref[...]
    wis = wi_ref[...] * (IDX_DIM ** -0.5 * IDX_HEADS ** -0.5)
    score = jnp.zeros((tq, L), F32)
    for h in range(IDX_HEADS):
        d = _dot_nt(qi[:, h * IDX_DIM:(h + 1) * IDX_DIM], ki)
        score = score + jnp.maximum(d, 0.0) * wis[:, IDX_DIM + h:IDX_DIM + h + 1]
    score = jnp.where(adm, score, NEG_INF)

    kb = pltpu.bitcast(score, jnp.int32)
    key_scr[...] = jnp.where(kb >= 0, kb, kb ^ jnp.int32(0x7FFFFFFF))

    def count(mask):
        return jnp.sum(jnp.where(mask, 1.0, 0.0), axis=1, keepdims=True)

    def vbody(it, lo_u):
        trial_u = lo_u | lax.shift_left(jnp.int32(1), 31 - it)
        c = count(key_scr[...] >= (trial_u ^ jnp.int32(INT_MIN)))
        return jnp.where(c >= topk, trial_u, lo_u)

    lo_u = lax.fori_loop(0, 32, vbody, jnp.zeros((tq, 1), jnp.int32))
    lo = lo_u ^ jnp.int32(INT_MIN)

    key = key_scr[...]
    gt = key > lo
    eq = key == lo
    need = topk - count(gt)
    lo_real = lo > jnp.int32(HALF_NEG_KEY)
    tie = lo_real & (count(eq) > need)
    jstar_scr[...] = jnp.full((tq, 1), L, jnp.int32)

    @pl.when(jnp.max(jnp.where(tie, 1.0, 0.0)) > 0.0)
    def _():
        nbits = max(1, (L - 1).bit_length())

        def ibody(it, lo_i):
            trial = lo_i + lax.shift_left(jnp.int32(1), nbits - 1 - it)
            c = count((kpos < trial) & (key_scr[...] == lo))
            return jnp.where(c < need, trial, lo_i)

        lo_i = lax.fori_loop(0, nbits, ibody, jnp.zeros((tq, 1), jnp.int32))
        jstar_scr[...] = jnp.where(tie, lo_i, L)

    keep = gt | (eq & (kpos <= jstar_scr[...]))
    bias_scr[...] = jnp.where(adm & keep, 0.0, NEG_INF)

    kk = k_ref[...]
    vv = v_ref[...]
    for h in range(ATTN_HEADS):
        hs = slice(h * ATTN_HEAD_DIM, (h + 1) * ATTN_HEAD_DIM)
        lg = _dot_nt(q_ref[:, hs], kk) + bias_scr[...]
        m = jnp.max(lg, axis=1, keepdims=True)
        pv = _dot(jnp.exp2(lg - m).astype(BF16), vv)
        o_scr[:, hs] = (pv[:, :ATTN_HEAD_DIM] / pv[:, ATTN_HEAD_DIM:ATTN_HEAD_DIM + 1]).astype(BF16)
    o_ref[...] = x_ref[...] + _dot(o_scr[...], wo_ref[...])


def _attn_layer(x, q, qi, kis, ki_all, k_all, v_all, wo_bf, q_off, n_keys, topk, tq, j0, nq, L):
    B, T, _ = x.shape
    QI_W = IDX_HEADS * IDX_DIM
    return pl.pallas_call(
        functools.partial(_attn_kernel, tq=tq, L=L, q_off=q_off + j0 * tq, n_keys=n_keys, topk=topk),
        grid=(B, nq),
        in_specs=[
            pl.BlockSpec((None, tq, D_MODEL), lambda b, j: (b, j0 + j, 0)),
            pl.BlockSpec((None, tq, Q_W), lambda b, j: (b, j0 + j, 0)),
            pl.BlockSpec((None, tq, QI_W), lambda b, j: (b, j0 + j, 0)),
            pl.BlockSpec((None, tq, LANES), lambda b, j: (b, j0 + j, 0)),
            pl.BlockSpec((None, L, IDX_DIM), lambda b, j: (b, 0, 0)),
            pl.BlockSpec((None, L, ATTN_HEAD_DIM), lambda b, j: (b, 0, 0)),
            pl.BlockSpec((None, L, 2 * ATTN_HEAD_DIM), lambda b, j: (b, 0, 0)),
            _const_spec((Q_W, D_MODEL)),
        ],
        out_specs=pl.BlockSpec((None, tq, D_MODEL), lambda b, j: (b, j0 + j, 0)),
        out_shape=SDS((B, T, D_MODEL), F32),
        input_output_aliases={0: 0},
        scratch_shapes=[
            pltpu.VMEM((tq, L), jnp.int32),
            pltpu.VMEM((tq, L), F32),
            pltpu.VMEM((tq, 1), jnp.int32),
            pltpu.VMEM((tq, Q_W), BF16),
        ],
        compiler_params=_cparams(("parallel", "arbitrary")),
        name="attn",
    )(x, q, qi, kis, ki_all, k_all, v_all, wo_bf)


def _rproj_kernel(x_ref, xp_ref, sh_ref, g_ref, mix_ref, w0_ref, w1_ref, w2_ref, a0_ref, a1_ref, a2_ref,
                  g1_ref, g2_ref, kk_ref, ka_ref, wr_ref, wk_ref, wv_ref, bd_ref,
                  r_o, lw_o, k_o, v_o, na_o, b_o, g_o, hl_o, *, tt):
    i = pl.program_id(1)
    g = g_ref[...]
    h = _rms(x_ref[...], g)
    hl_o[...] = h[tt - SUBLANES:, :]
    prev = jnp.where(i == 0, sh_ref[...], _rms(xp_ref[SUBLANES - 1:SUBLANES, :], g))
    row = lax.broadcasted_iota(jnp.int32, (tt, D_MODEL), 0)
    xx = jnp.where(row == 0, prev, pltpu.roll(h, 1, 0)) - h
    lerp = lambda n: (h + xx * mix_ref[n:n + 1, :]).astype(BF16)
    r = _dot(lerp(0), wr_ref[...])
    wl = w0_ref[...] + _dot(jnp.tanh(_dot(lerp(1), w1_ref[...])).astype(BF16), w2_ref[...])
    lw = -_sigmoid(wl) * math.exp(-0.5)
    k = _dot(lerp(2), wk_ref[...])
    v = _dot(lerp(3), wv_ref[...])
    a = _sigmoid(a0_ref[...] + _dot(_dot(lerp(4), a1_ref[...]).astype(BF16), a2_ref[...]))
    gate = _dot(_sigmoid(_dot(lerp(5), g1_ref[...])).astype(BF16), g2_ref[...])
    kk = k * kk_ref[...]
    kk = kk * lax.rsqrt(jnp.maximum(_head_sum(kk * kk, bd_ref), 1e-24))
    r_o[...] = r
    lw_o[...] = lw
    k_o[...] = k * (1.0 + (a - 1.0) * ka_ref[...])
    v_o[...] = v
    na_o[...] = -kk
    b_o[...] = kk * a
    g_o[...] = gate


def _rwkv_project(x, shift_prev, g, rw):
    B, T, _ = x.shape
    tt = _pick_tile(T, 256)
    rb = tt // SUBLANES
    tok = pl.BlockSpec((None, tt, D_MODEL), lambda b, i: (b, i, 0))
    consts = [g, rw["mix"], rw["w0"], rw["w1"], rw["w2"], rw["a0"], rw["a1"], rw["a2"], rw["g1"], rw["g2"],
              rw["k_k"], rw["k_a"], rw["w_r"], rw["w_k"], rw["w_v"], rw["bd"]]
    return pl.pallas_call(
        functools.partial(_rproj_kernel, tt=tt),
        grid=(B, T // tt),
        in_specs=[
            tok,
            pl.BlockSpec((None, SUBLANES, D_MODEL), lambda b, i: (b, jnp.maximum(i * rb - 1, 0), 0)),
            pl.BlockSpec((None, 1, D_MODEL), lambda b, i: (b, 0, 0)),
        ] + [_const_spec(c.shape) for c in consts],
        out_specs=[tok] * 7 + [pl.BlockSpec((None, SUBLANES, D_MODEL), lambda b, i: (b, 0, 0))],
        out_shape=[SDS((B, T, D_MODEL), F32)] * 7 + [SDS((B, SUBLANES, D_MODEL), F32)],
        compiler_params=_cparams(("parallel", "arbitrary")),
        name="rwkv_proj",
    )(x, x, shift_prev, *consts)


def _scan_kernel(r_ref, lw_ref, k_ref, v_ref, a_ref, b_ref, s0_ref, y_ref, st_ref, s_scr, *, nb, tt, C):
    N = RWKV_HEAD
    assert C == N and 2 * N == LANES

    @pl.when(pl.program_id(1) == 0)
    def _():
        s_scr[...] = s0_ref[...]

    row_w = lax.broadcasted_iota(jnp.int32, (C, D_MODEL), 0)
    lane = lax.broadcasted_iota(jnp.int32, (C, LANES), 1)
    h0 = lane < N
    r1 = lax.broadcasted_iota(jnp.int32, (C, 2 * C), 0)
    c1 = lax.broadcasted_iota(jnp.int32, (C, 2 * C), 1) % C
    strict = r1 > c1
    r2 = lax.broadcasted_iota(jnp.int32, (C, 4 * C), 0)
    c2 = lax.broadcasted_iota(jnp.int32, (C, 4 * C), 1) % C
    incl = r2 >= c2
    rs = lax.broadcasted_iota(jnp.int32, (LANES, LANES), 0)
    cs = lax.broadcasted_iota(jnp.int32, (LANES, LANES), 1)
    same_head = (rs < N) == (cs < N)
    eye = rs == cs

    def split(x):
        return jnp.concatenate([jnp.where(h0, x, 0.0), jnp.where(h0, 0.0, x)], axis=0)

    def chunk(c, carry):
        sl = pl.ds(pl.multiple_of(c * C, C), C)
        n_pairs = D_MODEL // LANES
        at, rt, bt, kt, vv, wc = [], [], [], [], [], []
        for s in range(nb):
            lw = lw_ref[s, sl, :]
            cum = lw
            d = 1
            while d < C:
                cum = cum + jnp.where(row_w >= d, pltpu.roll(cum, d, 0), 0.0)
                d *= 2
            e_w = jnp.exp(cum)
            e_n = jnp.exp(-cum)
            rows = (a_ref[s, sl, :] * jnp.exp(cum - lw), r_ref[s, sl, :] * e_w, b_ref[s, sl, :] * e_n,
                    k_ref[s, sl, :] * e_n, v_ref[s, sl, :], e_w[C - 1:C, :])
            for dst, src in zip((at, rt, bt, kt, vv, wc), rows):
                dst.extend(src[:, p * LANES:(p + 1) * LANES] for p in range(n_pairs))
        pairs = range(nb * n_pairs)
        st = [s_scr[p // n_pairs, p % n_pairs] for p in pairs]
        ar = [jnp.concatenate([at[p], rt[p]], axis=0).astype(BF16) for p in pairs]
        v2 = [split(vv[p]).astype(BF16) for p in pairs]
        g = [_dot_nt(ar[p], jnp.concatenate([split(bt[p]), split(kt[p])], axis=0).astype(BF16)) for p in pairs]
        hm = [_dot(ar[p], st[p].astype(BF16)) for p in pairs]
        pw = [jnp.where(strict, g[p][:C, :2 * C], 0.0) for p in pairs]
        u = [hm[p][:C] + _dot(jnp.where(strict, g[p][:C, 2 * C:], 0.0).astype(BF16), v2[p]) for p in pairs]
        n = 1
        while n < C:
            pb = [pw[p].astype(BF16) for p in pairs]
            u = [u[p] + _dot(pb[p], split(u[p]).astype(BF16)) for p in pairs]
            n *= 2
            if n < C:
                pw = [_dot(pb[p], split(pw[p]).astype(BF16)) for p in pairs]
        ys = [hm[p][C:] + _dot(jnp.where(incl, g[p][C:, :], 0.0).astype(BF16),
                               jnp.concatenate([split(u[p]).astype(BF16), v2[p]], axis=0)) for p in pairs]
        for s in range(nb):
            y_ref[s, sl, :] = jnp.concatenate(ys[s * n_pairs:(s + 1) * n_pairs], axis=1)
        for p in pairs:
            bk = jnp.concatenate([bt[p], kt[p]], axis=0).astype(BF16)
            uvp = jnp.concatenate([u[p], vv[p]], axis=0).astype(BF16)
            upd = jnp.where(same_head, _dot_tn(bk, uvp), 0.0)
            w_col = jnp.sum(jnp.where(eye, wc[p], 0.0), axis=1, keepdims=True)
            s_scr[p // n_pairs, p % n_pairs] = (st[p] + upd) * w_col
        return carry

    lax.fori_loop(0, tt // C, chunk, 0)
    st_ref[...] = s_scr[...]


def _state_to_pairs(wkv):
    B = wkv.shape[0]
    hp = LANES // RWKV_HEAD
    w = jnp.swapaxes(wkv.reshape(B, RWKV_HEADS // hp, hp, RWKV_HEAD, RWKV_HEAD), -1, -2)
    bd = w[:, :, :, :, None, :] * jnp.eye(hp, dtype=wkv.dtype)[None, None, :, None, :, None]
    return bd.reshape(B, RWKV_HEADS // hp, LANES, LANES)


def _state_from_pairs(bd):
    B = bd.shape[0]
    hp = LANES // RWKV_HEAD
    w = bd.reshape(B, RWKV_HEADS // hp, hp, RWKV_HEAD, hp, RWKV_HEAD)
    diag = jnp.stack([w[:, :, h, :, h, :] for h in range(hp)], axis=2)
    return jnp.swapaxes(diag, -1, -2).reshape(B, RWKV_HEADS, RWKV_HEAD, RWKV_HEAD)


def _rwkv_scan(r, lw, k, v, na, b, wkv0):
    B, T, _ = r.shape
    C = SCAN_CHUNK
    t_pad = ((T + C - 1) // C) * C
    seqs = (r, lw, k, v, na, b)
    if t_pad != T:
        seqs = tuple(jnp.pad(a, ((0, 0), (0, t_pad - T), (0, 0))) for a in seqs)
    tt = _pick_tile(t_pad, 128)
    nb = _pick_tile(B, SCAN_SEQS)
    n_pairs = D_MODEL // LANES
    tok = pl.BlockSpec((nb, tt, D_MODEL), lambda b_, i: (b_, i, 0))
    st = pl.BlockSpec((nb, n_pairs, LANES, LANES), lambda b_, i: (b_, 0, 0, 0))
    y, s_t = pl.pallas_call(
        functools.partial(_scan_kernel, nb=nb, tt=tt, C=C),
        grid=(B // nb, t_pad // tt),
        in_specs=[tok] * 6 + [st],
        out_specs=[tok, st],
        out_shape=[SDS((B, t_pad, D_MODEL), F32), SDS((B, n_pairs, LANES, LANES), F32)],
        scratch_shapes=[pltpu.VMEM((nb, n_pairs, LANES, LANES), F32)],
        compiler_params=_cparams(("parallel", "arbitrary")),
        name="rwkv_scan",
    )(*seqs, _state_to_pairs(wkv0))
    return y[:, :T], _state_from_pairs(s_t)


def _rout_kernel(x_ref, y_ref, r_ref, k_ref, v_ref, g_ref, rk_ref, lg_ref, lb_ref, wo_ref, bd_ref, o_ref):
    y = y_ref[...]
    inv_n = 1.0 / RWKV_HEAD
    mu = _head_sum(y, bd_ref) * inv_n
    yc = y - mu
    var = _head_sum(yc * yc, bd_ref) * inv_n
    yn = yc * lax.rsqrt(var + LNX_EPS) * lg_ref[...] + lb_ref[...]
    bonus = _head_sum(r_ref[...] * k_ref[...] * rk_ref[...], bd_ref) * v_ref[...]
    z = ((yn + bonus) * g_ref[...]).astype(BF16)
    o_ref[...] = x_ref[...] + _dot(z, wo_ref[...])


def _rwkv_out(x, y, r, k, v, gate, rw):
    M = x.shape[0]
    tm = _pick_tile(M, 512)
    tok = pl.BlockSpec((tm, D_MODEL), lambda i: (i, 0))
    consts = [rw["r_k"], rw["lnx_g"], rw["lnx_b"], rw["w_o"], rw["bd"]]
    return pl.pallas_call(
        _rout_kernel,
        grid=(M // tm,),
        in_specs=[tok] * 6 + [_const_spec(c.shape) for c in consts],
        out_specs=tok,
        out_shape=SDS((M, D_MODEL), F32),
        compiler_params=_cparams(("parallel",)),
        name="rwkv_out",
    )(x, y, r, k, v, gate, *consts)


def _pool_block(x, state, n_hist, g, w_bf, scale):
    B = x.shape[0]
    if state is None:
        hist = jnp.zeros((B, POOL_PAD, D_MODEL), F32)
    else:
        hist = jnp.pad(state, ((0, 0), (POOL_PAD - POOL_HIST, 0), (0, 0)))
    out, hs = _pool_layer(x, hist, n_hist, g, w_bf, scale)
    return out, hs[:, POOL_PAD - POOL_HIST:]


def _attn_block(x, cache, g, w_in_pad, wo_bf, q_off, tq):
    B, T, _ = x.shape
    tab = _rope_tables(q_off + jnp.arange(T))
    q, kv, qi, kis = _attn_project(x, g, w_in_pad, tab)
    k_new, v_new, ki_new = kv[..., :LANES], kv[..., LANES:], kis[..., :IDX_DIM]
    if cache is None:
        keys = (ki_new, k_new, v_new)
    else:
        ck, cv, cki = cache
        keys = (jnp.concatenate([cki, ki_new], axis=1), jnp.concatenate([ck, k_new], axis=1),
                jnp.concatenate([cv, v_new], axis=1))
    n_keys = keys[0].shape[1]
    L = ((n_keys + LANES - 1) // LANES) * LANES
    keys = tuple(jnp.pad(a.astype(BF16), ((0, 0), (0, L - n_keys), (0, 0))) for a in keys)
    ones_col = jnp.zeros((B, L, ATTN_HEAD_DIM), BF16).at[:, :, 0].set(1.0)
    keys = (keys[0], keys[1], jnp.concatenate([keys[2], ones_col], axis=2))
    topk = min(TOPK_MAX, n_keys // 4)
    nq = T // tq
    if cache is None and nq % ATTN_BUCKET_QBLOCKS == 0:
        groups = [(j0, ATTN_BUCKET_QBLOCKS, (j0 + ATTN_BUCKET_QBLOCKS) * tq)
                  for j0 in range(0, nq, ATTN_BUCKET_QBLOCKS)]
    else:
        groups = [(0, nq, L)]
    out = x
    for j0, n, l_g in groups:
        out = _attn_layer(out, q, qi, kis, keys[0], keys[1], keys[2], wo_bf, q_off, min(n_keys, l_g), topk, tq,
                          j0, n, l_g)
    return out, k_new, v_new, ki_new


def _rwkv_block(x, shift_prev, wkv0, g, rw):
    B, T, _ = x.shape
    r, lw, k, v, na, b, gate, hl = _rwkv_project(x, shift_prev, g, rw)
    y, s_t = _rwkv_scan(r, lw, k, v, na, b, wkv0)
    flat = lambda a: a.reshape(B * T, D_MODEL)
    out = _rwkv_out(flat(x), flat(y), flat(r), flat(k), flat(v), flat(gate), rw).reshape(B, T, D_MODEL)
    return out, hl[:, SUBLANES - 1:], s_t


def kernel(x_prompt, x_sample, state_pool, cache_k, cache_v, cache_kidx, state_shift, state_wkv, ln1_g, ln2_g, w_up, w_down, ln_f_g, pool_w, pool_scale, attn_w_in, attn_w_out, rwkv_mix, rwkv_w0, rwkv_w1, rwkv_w2, rwkv_a0, rwkv_a1, rwkv_a2, rwkv_g1, rwkv_g2, rwkv_k_k, rwkv_k_a, rwkv_r_k, rwkv_w_r, rwkv_w_k, rwkv_w_v, rwkv_w_o, rwkv_lnx_g, rwkv_lnx_b):
    xp, xs = x_prompt, x_sample
    bp, sp, _ = xp.shape
    bs, ss, _ = xs.shape
    past = cache_k.shape[2]
    row = lambda a: a.reshape(1, -1)
    bf = lambda a: a.astype(BF16)
    head_of = jnp.arange(MXU_DIM) // RWKV_HEAD
    bd_mat = (head_of[:, None] == head_of[None, :]).astype(BF16)
    outs = {n: [] for n in ("pool_p", "pool_s", "k_p", "k_s", "v_p", "v_s", "ki_p", "ki_s",
                            "sh_p", "sh_s", "wkv_p", "wkv_s")}
    for i in range(DEPTH):
        j = i // N_MIXERS
        g1 = row(ln1_g[i])
        if i % N_MIXERS == 0:
            w_bf = bf(pool_w[j])
            sc = row(pool_scale[j])
            xp, st_p = _pool_block(xp, None, 0, g1, w_bf, sc)
            xs, st_s = _pool_block(xs, state_pool[j], past, g1, w_bf, sc)
            outs["pool_p"].append(st_p)
            outs["pool_s"].append(st_s)
        elif i % N_MIXERS == 1:
            w_in_pad = jnp.pad(bf(attn_w_in[j]), ((0, 0), (0, ATTN_IN_PAD - ATTN_IN_W)))
            wo_bf = bf(attn_w_out[j])
            xp, kp, vp, kip = _attn_block(xp, None, g1, w_in_pad, wo_bf, 0, 256)
            xs, kn, vn, kin = _attn_block(xs, (cache_k[j], cache_v[j], cache_kidx[j]), g1, w_in_pad, wo_bf,
                                          past, ss)
            for n, a in (("k_p", kp), ("v_p", vp), ("ki_p", kip), ("k_s", kn), ("v_s", vn), ("ki_s", kin)):
                outs[n].append(a)
        else:
            rw = dict(mix=rwkv_mix[j], w0=row(rwkv_w0[j]), w1=bf(rwkv_w1[j]), w2=bf(rwkv_w2[j]),
                      a0=row(rwkv_a0[j]), a1=bf(rwkv_a1[j]), a2=bf(rwkv_a2[j]), g1=bf(rwkv_g1[j]),
                      g2=bf(rwkv_g2[j]), k_k=row(rwkv_k_k[j]), k_a=row(rwkv_k_a[j]), r_k=row(rwkv_r_k[j]),
                      w_r=bf(rwkv_w_r[j]), w_k=bf(rwkv_w_k[j]), w_v=bf(rwkv_w_v[j]), w_o=bf(rwkv_w_o[j]),
                      lnx_g=row(rwkv_lnx_g[j]), lnx_b=row(rwkv_lnx_b[j]), bd=bd_mat)
            zero_shift = jnp.zeros((bp, 1, D_MODEL), F32)
            zero_wkv = jnp.zeros((bp, RWKV_HEADS, RWKV_HEAD, RWKV_HEAD), F32)
            xp, shp, wp = _rwkv_block(xp, zero_shift, zero_wkv, g1, rw)
            xs, shs, wsn = _rwkv_block(xs, state_shift[j], state_wkv[j], g1, rw)
            outs["sh_p"].append(shp)
            outs["sh_s"].append(shs)
            outs["wkv_p"].append(wp)
            outs["wkv_s"].append(wsn)
        last = i == DEPTH - 1
        g2 = row(ln2_g[i])
        gf = row(ln_f_g)
        wu, wd = bf(w_up[i]), bf(w_down[i])
        xp = _mlp(xp.reshape(bp * sp, D_MODEL), g2, wu, wd, gf, last).reshape(bp, sp, D_MODEL)
        xs = _mlp(xs.reshape(bs * ss, D_MODEL), g2, wu, wd, gf, last).reshape(bs, ss, D_MODEL)
    st = lambda n: jnp.stack(outs[n], 0)
    return (xp, xs, st("pool_p"), st("pool_s"), st("k_p"), st("k_s"), st("v_p"), st("v_s"),
            st("ki_p"), st("ki_s"), st("sh_p"), st("sh_s"), st("wkv_p"), st("wkv_s"))
```

```python
import functools
import math

import jax
import jax.numpy as jnp
import numpy as np
from jax import lax
from jax.experimental import pallas as pl
from jax.experimental.pallas import tpu as pltpu

F32 = jnp.float32
BF16 = jnp.bfloat16
SDS = jax.ShapeDtypeStruct

D_MODEL = 1024
DEPTH = 4
N_MIXERS = 3
CHUNK = 64
D_FF = 4 * D_MODEL
RMS_EPS = 1e-6
POOL_WINDOWS = (2, 4, 8, 16)
POOL_GROUPS = 4
POOL_GW = D_MODEL // POOL_GROUPS
POOL_HIST = max(POOL_WINDOWS) - 1
POOL_PAD = POOL_HIST + 1
ATTN_HEADS = 8
ATTN_HEAD_DIM = D_MODEL // ATTN_HEADS
IDX_HEADS = 8
IDX_DIM = 64
TOPK_MAX = 256
ROPE_THETA = 500000.0
ROPE_FRACTION = 4
NEG_INF = -1e30
Q_W = ATTN_HEADS * ATTN_HEAD_DIM
OFF_K = Q_W
OFF_V = OFF_K + ATTN_HEAD_DIM
OFF_QI = OFF_V + ATTN_HEAD_DIM
OFF_KI = OFF_QI + IDX_HEADS * IDX_DIM
OFF_WI = OFF_KI + IDX_DIM
ATTN_IN_W = OFF_WI + IDX_HEADS
RWKV_HEAD = 64
RWKV_HEADS = D_MODEL // RWKV_HEAD
LNX_EPS = 64e-5

LANES = 128
SUBLANES = 8
ATTN_IN_PAD = ((ATTN_IN_W + LANES - 1) // LANES) * LANES
MXU_DIM = 256
VMEM_LIMIT = 56 * 1024 * 1024
HALF16 = 2 ** 15
HALF_NEG_KEY = int(np.float32(0.5 * NEG_INF).view(np.int32)) ^ 0x7FFFFFFF
POOL_MLP_MIN_ROWS = 512
POOL_MLP_SUBTILES = 2
ATTN_BUCKET_QBLOCKS = 1
SCAN_CHUNK = 64
SCAN_SEQS = 2
FF_CHUNK = 1024


def _cparams(sem):
    return pltpu.CompilerParams(dimension_semantics=sem, vmem_limit_bytes=VMEM_LIMIT)


def _const_spec(shape):
    nd = len(shape)
    return pl.BlockSpec(shape, lambda *_: (0,) * nd, pipeline_mode=pl.Buffered(1))


def _rms(x, g):
    ms = jnp.mean(x * x, axis=-1, keepdims=True)
    return x * lax.rsqrt(ms + RMS_EPS) * g


def _dot(a, b):
    return jnp.dot(a, b, preferred_element_type=F32)


def _dot_nt(a, b):
    return lax.dot_general(a, b, (((1,), (1,)), ((), ())), preferred_element_type=F32)


def _dot_tn(a, b):
    return lax.dot_general(a, b, (((0,), (0,)), ((), ())), preferred_element_type=F32)


def _head_sum(z, bd_ref):
    bd = bd_ref[...]
    hi = z.astype(BF16)
    lo = (z - hi.astype(F32)).astype(BF16)
    outs = []
    for c in range(D_MODEL // MXU_DIM):
        cs = slice(c * MXU_DIM, (c + 1) * MXU_DIM)
        outs.append(_dot(hi[:, cs], bd) + _dot(lo[:, cs], bd))
    return jnp.concatenate(outs, axis=1)


def _sigmoid(x):
    return 1.0 / (1.0 + jnp.exp(-x))


def _pick_tile(n, pref):
    t = min(n, pref)
    assert n % t == 0, (n, t)
    return t


def _mlp_apply(x, g_ref, wu_ref, wd_ref, gf_ref, final_norm, side=()):
    h = _rms(x, g_ref[...]).astype(BF16)
    acc = x
    for j in range(D_FF // FF_CHUNK):
        u = _dot(h, wu_ref[:, j * FF_CHUNK:(j + 1) * FF_CHUNK])
        u = jnp.square(jnp.maximum(u, 0.0)).astype(BF16)
        acc = acc + _dot(u, wd_ref[j * FF_CHUNK:(j + 1) * FF_CHUNK, :])
        if j < len(side):
            side[j]()
    if final_norm:
        acc = _rms(acc, gf_ref[...])
    return acc


def _mlp_specs():
    return [_const_spec((1, D_MODEL)), _const_spec((D_MODEL, D_FF)), _const_spec((D_FF, D_MODEL)),
            _const_spec((1, D_MODEL))]


def _mlp_kernel(x_ref, g_ref, wu_ref, wd_ref, gf_ref, o_ref, *, final_norm):
    o_ref[...] = _mlp_apply(x_ref[...], g_ref, wu_ref, wd_ref, gf_ref, final_norm)


def _mlp(x, g, wu, wd, gf, final_norm):
    M = x.shape[0]
    tm = _pick_tile(M, 512)
    return pl.pallas_call(
        functools.partial(_mlp_kernel, final_norm=final_norm),
        grid=(M // tm,),
        in_specs=[pl.BlockSpec((tm, D_MODEL), lambda i: (i, 0))] + _mlp_specs(),
        out_specs=pl.BlockSpec((tm, D_MODEL), lambda i: (i, 0)),
        out_shape=SDS((M, D_MODEL), F32),
        compiler_params=_cparams(("parallel",)),
        name="mlp",
    )(x, g, wu, wd, gf)


def _pool_kernel(x_ref, xp_ref, hist_ref, g_ref, w_ref, sc_ref, *rest, tt, n_hist, mlp, final_norm):
    mlp_refs, (o_ref, hs_ref) = rest[:-2], rest[-2:]
    i = pl.program_id(1)
    g = g_ref[...]
    x = x_ref[...]
    h = _rms(x, g)
    prev = jnp.where(i == 0, hist_ref[...], _rms(xp_ref[...], g))
    hs_ref[...] = h[tt - POOL_PAD:, :]
    full = jnp.concatenate([prev, h], axis=0)
    nsub = POOL_MLP_SUBTILES if mlp else 1
    rows = tt // nsub

    def group(r, gi):
        win = POOL_WINDOWS[gi]
        rs = slice(r * rows, (r + 1) * rows)
        cs = slice(gi * POOL_GW, (gi + 1) * POOL_GW)
        s = full[r * rows:(r + 1) * rows + POOL_PAD, cs]
        d = 1
        while d < win:
            s = s + pltpu.roll(s, d, 0)
            d *= 2
        t1 = lax.broadcasted_iota(jnp.int32, (rows, POOL_GW), 0) + (i * tt + r * rows + 1 + n_hist)
        pooled = s[POOL_PAD:, :] / jnp.minimum(t1, win).astype(F32) - h[rs, cs]
        return x[rs, cs] + _dot(pooled.astype(BF16), w_ref[gi]) * sc_ref[:, cs]

    cols = [group(0, gi) for gi in range(POOL_GROUPS)]
    if not mlp:
        o_ref[...] = jnp.concatenate(cols, axis=1)
        return
    outs = []
    for r in range(nsub):
        y = jnp.concatenate(cols, axis=1)
        cols = []
        side = [functools.partial(lambda gi, rn: cols.append(group(rn, gi)), gi, r + 1)
                for gi in range(POOL_GROUPS)] if r + 1 < nsub else []
        outs.append(_mlp_apply(y, *mlp_refs, final_norm, side))
    o_ref[...] = jnp.concatenate(outs, axis=0)


def _pool_layer(x, hist, n_hist, g, w_bf, scale, mlp_args, final_norm):
    B, T, _ = x.shape
    tt = _pick_tile(T, 512)
    rb = tt // POOL_PAD
    return pl.pallas_call(
        functools.partial(_pool_kernel, tt=tt, n_hist=n_hist, mlp=bool(mlp_args), final_norm=final_norm),
        grid=(B, T // tt),
        in_specs=[
            pl.BlockSpec((None, tt, D_MODEL), lambda b, i: (b, i, 0)),
            pl.BlockSpec((None, POOL_PAD, D_MODEL), lambda b, i: (b, jnp.maximum(i * rb - 1, 0), 0)),
            pl.BlockSpec((None, POOL_PAD, D_MODEL), lambda b, i: (b, 0, 0)),
            _const_spec((1, D_MODEL)),
            _const_spec((POOL_GROUPS, POOL_GW, POOL_GW)),
            _const_spec((1, D_MODEL)),
        ] + (_mlp_specs() if mlp_args else []),
        out_specs=[
            pl.BlockSpec((None, tt, D_MODEL), lambda b, i: (b, i, 0)),
            pl.BlockSpec((None, POOL_PAD, D_MODEL), lambda b, i: (b, 0, 0)),
        ],
        out_shape=[SDS((B, T, D_MODEL), F32), SDS((B, POOL_PAD, D_MODEL), F32)],
        compiler_params=_cparams(("parallel", "arbitrary")),
        name="pool_mlp" if mlp_args else "pool",
    )(x, x, hist, g, w_bf, scale, *mlp_args)


def _rope_tables(pos):
    T = pos.shape[0]
    posf = pos.astype(F32)

    def head(d):
        rd = d // ROPE_FRACTION
        half = rd // 2
        inv = ROPE_THETA ** (-jnp.arange(half, dtype=F32) / half)
        ang = posf[:, None] * inv[None, :]
        cos, sin = jnp.cos(ang), jnp.sin(ang)
        z = lambda n: jnp.zeros((T, n), F32)
        c = jnp.concatenate([cos, cos, jnp.ones((T, d - rd), F32)], axis=1)
        sa = jnp.concatenate([-sin, z(d - half)], axis=1)
        sb = jnp.concatenate([z(half), sin, z(d - rd)], axis=1)
        return c, sa, sb

    qa = head(ATTN_HEAD_DIM)
    ia = head(IDX_DIM)
    i2 = tuple(jnp.concatenate([t, t], axis=1) for t in ia)
    ones, zeros = jnp.ones((T, IDX_DIM), F32), jnp.zeros((T, IDX_DIM), F32)
    ik = (jnp.concatenate([ia[0], ones], axis=1), jnp.concatenate([ia[1], zeros], axis=1),
          jnp.concatenate([ia[2], zeros], axis=1))
    return jnp.concatenate(list(qa) + list(i2) + list(ik), axis=1)


def _aproj_kernel(x_ref, g_ref, w_ref, tab_ref, q_ref, qi_ref, kis_ref, k_ref, v_ref, ki_ref, kb_ref, vb_ref, kib_ref):
    h = _rms(x_ref[...], g_ref[...]).astype(BF16)
    p = _dot(h, w_ref[...])

    def rope(x, kind, half):
        c = tab_ref[:, (3 * kind) * LANES:(3 * kind + 1) * LANES]
        sa = tab_ref[:, (3 * kind + 1) * LANES:(3 * kind + 2) * LANES]
        sb = tab_ref[:, (3 * kind + 2) * LANES:(3 * kind + 3) * LANES]
        return x * c + pltpu.roll(x, LANES - half, 1) * sa + pltpu.roll(x, half, 1) * sb

    slab = lambda off: p[:, off:off + LANES]
    qh = ATTN_HEAD_DIM // ROPE_FRACTION // 2
    ih = IDX_DIM // ROPE_FRACTION // 2
    qscale = ATTN_HEAD_DIM ** -0.5 * math.log2(math.e)
    for s in range(ATTN_HEADS):
        q_ref[:, s * LANES:(s + 1) * LANES] = (rope(slab(s * LANES), 0, qh) * qscale).astype(BF16)
    k = rope(slab(OFF_K), 0, qh)
    v = slab(OFF_V)
    for s in range(IDX_HEADS * IDX_DIM // LANES):
        qi_ref[:, s * LANES:(s + 1) * LANES] = rope(slab(OFF_QI + s * LANES), 1, ih).astype(BF16)
    kis = rope(slab(OFF_KI), 2, ih)
    kis_ref[...] = kis
    k_ref[...] = k
    v_ref[...] = v
    ki_ref[...] = kis[:, :IDX_DIM]
    kb_ref[...] = k.astype(BF16)
    lane = lax.broadcasted_iota(jnp.int32, v.shape, 1)
    vb_ref[:, :ATTN_HEAD_DIM] = v.astype(BF16)
    vb_ref[:, ATTN_HEAD_DIM:] = jnp.where(lane == 0, 1.0, 0.0).astype(BF16)
    kib_ref[...] = kis[:, :IDX_DIM].astype(BF16)


def _attn_project(x, g, w_in_pad, tab):
    B, T, _ = x.shape
    tt = _pick_tile(T, 512)
    QI_W = IDX_HEADS * IDX_DIM
    widths = (Q_W, QI_W, LANES, ATTN_HEAD_DIM, ATTN_HEAD_DIM, IDX_DIM, ATTN_HEAD_DIM, 2 * ATTN_HEAD_DIM, IDX_DIM)
    dtypes = (BF16, BF16, F32, F32, F32, F32, BF16, BF16, BF16)
    return pl.pallas_call(
        _aproj_kernel,
        grid=(T // tt, B),
        in_specs=[
            pl.BlockSpec((None, tt, D_MODEL), lambda i, b: (b, i, 0)),
            _const_spec((1, D_MODEL)),
            _const_spec((D_MODEL, ATTN_IN_PAD)),
            pl.BlockSpec((tt, 9 * LANES), lambda i, b: (i, 0)),
        ],
        out_specs=[pl.BlockSpec((None, tt, w), lambda i, b: (b, i, 0)) for w in widths],
        out_shape=[SDS((B, T, w), d) for w, d in zip(widths, dtypes)],
        compiler_params=_cparams(("arbitrary", "arbitrary")),
        name="attn_proj",
    )(x, g, w_in_pad, tab)


def _attn_kernel(x_ref, q_ref, qi_ref, wi_ref, ki_ref, k_ref, v_ref, wo_ref, o_ref,
                 key_scr, hi_scr, lo_scr, bias_scr, jstar_scr, o_scr, *, tq, L, q_off, n_keys, topk):
    j = pl.program_id(1)
    qpos = lax.broadcasted_iota(jnp.int32, (tq, L), 0) + (q_off + j * tq)
    kpos = lax.broadcasted_iota(jnp.int32, (tq, L), 1)
    adm = (kpos < n_keys) & ((kpos // CHUNK) <= (qpos // CHUNK))

    qi = qi_ref[...]
    ki = ki_ref[...]
    wis = wi_ref[...] * (IDX_DIM ** -0.5 * IDX_HEADS ** -0.5)
    score = jnp.zeros((tq, L), F32)
    for h in range(IDX_HEADS):
        d = _dot_nt(qi[:, h * IDX_DIM:(h + 1) * IDX_DIM], ki)
        score = score + jnp.maximum(d, 0.0) * wis[:, IDX_DIM + h:IDX_DIM + h + 1]
    score = jnp.where(adm, score, NEG_INF)

    kb = pltpu.bitcast(score, jnp.int32)
    key = jnp.where(kb >= 0, kb, kb ^ jnp.int32(0x7FFFFFFF))
    key_scr[...] = key
    hi_scr[...] = (key >> 16).astype(jnp.int16)
    lo_scr[...] = ((key & 0xFFFF) - HALF16).astype(jnp.int16)

    def count(mask):
        return jnp.sum(jnp.where(mask, 1.0, 0.0), axis=1, keepdims=True)

    def count16(mask):
        c = jnp.where(mask, jnp.int16(1), jnp.int16(0))
        acc = c[:, :LANES]
        for t in range(1, L // LANES):
            acc = acc + c[:, t * LANES:(t + 1) * LANES]
        return jnp.sum(acc.astype(F32), axis=1, keepdims=True)

    def kth16(ref, kth):
        def body(it, lo_u):
            trial_u = lo_u | lax.shift_left(jnp.int32(1), 15 - it)
            c = count16(ref[...] >= (trial_u - HALF16).astype(jnp.int16))
            return jnp.where(c >= kth, trial_u, lo_u)
        return lax.fori_loop(0, 16, body, jnp.zeros((tq, 1), jnp.int32))

    p_hi = kth16(hi_scr, topk) - HALF16
    p16 = p_hi.astype(jnp.int16)
    hi = hi_scr[...]
    above = count16(hi > p16)
    lo_scr[...] = jnp.where(hi == p16, lo_scr[...], jnp.int16(-HALF16))
    q_u = kth16(lo_scr, topk - above)
    lo = p_hi * (2 * HALF16) + q_u
    q16 = (q_u - HALF16).astype(jnp.int16)
    need = topk - above - count16(lo_scr[...] > q16)
    n_eq = count16((hi == p16) & (lo_scr[...] == q16))

    gt = key_scr[...] > lo
    eq = key_scr[...] == lo
    lo_real = lo > jnp.int32(HALF_NEG_KEY)
    tie = lo_real & (n_eq > need)
    jstar_scr[...] = jnp.full((tq, 1), L, jnp.int32)

    @pl.when(jnp.max(jnp.where(tie, 1.0, 0.0)) > 0.0)
    def _():
        nbits = max(1, (L - 1).bit_length())

        def ibody(it, lo_i):
            trial = lo_i + lax.shift_left(jnp.int32(1), nbits - 1 - it)
            c = count((kpos < trial) & (key_scr[...] == lo))
            return jnp.where(c < need, trial, lo_i)

        lo_i = lax.fori_loop(0, nbits, ibody, jnp.zeros((tq, 1), jnp.int32))
        jstar_scr[...] = jnp.where(tie, lo_i, L)

    keep = gt | (eq & (kpos <= jstar_scr[...]))
    bias_scr[...] = jnp.where(adm & keep, 0.0, NEG_INF)

    kk = k_ref[...]
    vv = v_ref[...]
    for h in range(ATTN_HEADS):
        hs = slice(h * ATTN_HEAD_DIM, (h + 1) * ATTN_HEAD_DIM)
        lg = _dot_nt(q_ref[:, hs], kk) + bias_scr[...]
        m = jnp.max(lg, axis=1, keepdims=True)
        pv = _dot(jnp.exp2(lg - m).astype(BF16), vv)
        o_scr[:, hs] = (pv[:, :ATTN_HEAD_DIM] / pv[:, ATTN_HEAD_DIM:ATTN_HEAD_DIM + 1]).astype(BF16)
    o_ref[...] = x_ref[...] + _dot(o_scr[...], wo_ref[...])


def _attn_layer(x, q, qi, kis, ki_all, k_all, v_all, wo_bf, q_off, n_keys, topk, tq, j0, nq, L):
    B, T, _ = x.shape
    QI_W = IDX_HEADS * IDX_DIM
    return pl.pallas_call(
        functools.partial(_attn_kernel, tq=tq, L=L, q_off=q_off + j0 * tq, n_keys=n_keys, topk=topk),
        grid=(B, nq),
        in_specs=[
            pl.BlockSpec((None, tq, D_MODEL), lambda b, j: (b, j0 + j, 0)),
            pl.BlockSpec((None, tq, Q_W), lambda b, j: (b, j0 + j, 0)),
            pl.BlockSpec((None, tq, QI_W), lambda b, j: (b, j0 + j, 0)),
            pl.BlockSpec((None, tq, LANES), lambda b, j: (b, j0 + j, 0)),
            pl.BlockSpec((None, L, IDX_DIM), lambda b, j: (b, 0, 0)),
            pl.BlockSpec((None, L, ATTN_HEAD_DIM), lambda b, j: (b, 0, 0)),
            pl.BlockSpec((None, L, 2 * ATTN_HEAD_DIM), lambda b, j: (b, 0, 0)),
            _const_spec((Q_W, D_MODEL)),
        ],
        out_specs=pl.BlockSpec((None, tq, D_MODEL), lambda b, j: (b, j0 + j, 0)),
        out_shape=SDS((B, T, D_MODEL), F32),
        input_output_aliases={0: 0},
        scratch_shapes=[
            pltpu.VMEM((tq, L), jnp.int32),
            pltpu.VMEM((tq, L), jnp.int16),
            pltpu.VMEM((tq, L), jnp.int16),
            pltpu.VMEM((tq, L), F32),
            pltpu.VMEM((tq, 1), jnp.int32),
            pltpu.VMEM((tq, Q_W), BF16),
        ],
        compiler_params=_cparams(("parallel", "arbitrary")),
        name="attn",
    )(x, q, qi, kis, ki_all, k_all, v_all, wo_bf)


def _rproj_kernel(x_ref, xp_ref, sh_ref, g_ref, mix_ref, w0_ref, w1_ref, w2_ref, a0_ref, a1_ref, a2_ref,
                  g1_ref, g2_ref, kk_ref, ka_ref, wr_ref, wk_ref, wv_ref, bd_ref,
                  r_o, lw_o, k_o, v_o, na_o, b_o, g_o, hl_o, *, tt):
    i = pl.program_id(1)
    g = g_ref[...]
    h = _rms(x_ref[...], g)
    hl_o[...] = h[tt - SUBLANES:, :]
    prev = jnp.where(i == 0, sh_ref[...], _rms(xp_ref[SUBLANES - 1:SUBLANES, :], g))
    row = lax.broadcasted_iota(jnp.int32, (tt, D_MODEL), 0)
    xx = jnp.where(row == 0, prev, pltpu.roll(h, 1, 0)) - h
    lerp = lambda n: (h + xx * mix_ref[n:n + 1, :]).astype(BF16)
    r = _dot(lerp(0), wr_ref[...])
    wl = w0_ref[...] + _dot(jnp.tanh(_dot(lerp(1), w1_ref[...])).astype(BF16), w2_ref[...])
    lw = -_sigmoid(wl) * math.exp(-0.5)
    k = _dot(lerp(2), wk_ref[...])
    v = _dot(lerp(3), wv_ref[...])
    a = _sigmoid(a0_ref[...] + _dot(_dot(lerp(4), a1_ref[...]).astype(BF16), a2_ref[...]))
    gate = _dot(_sigmoid(_dot(lerp(5), g1_ref[...])).astype(BF16), g2_ref[...])
    kk = k * kk_ref[...]
    kk = kk * lax.rsqrt(jnp.maximum(_head_sum(kk * kk, bd_ref), 1e-24))
    r_o[...] = r
    lw_o[...] = lw
    k_o[...] = k * (1.0 + (a - 1.0) * ka_ref[...])
    v_o[...] = v
    na_o[...] = -kk
    b_o[...] = kk * a
    g_o[...] = gate


def _rwkv_project(x, shift_prev, g, rw):
    B, T, _ = x.shape
    tt = _pick_tile(T, 256)
    rb = tt // SUBLANES
    tok = pl.BlockSpec((None, tt, D_MODEL), lambda b, i: (b, i, 0))
    consts = [g, rw["mix"], rw["w0"], rw["w1"], rw["w2"], rw["a0"], rw["a1"], rw["a2"], rw["g1"], rw["g2"],
              rw["k_k"], rw["k_a"], rw["w_r"], rw["w_k"], rw["w_v"], rw["bd"]]
    return pl.pallas_call(
        functools.partial(_rproj_kernel, tt=tt),
        grid=(B, T // tt),
        in_specs=[
            tok,
            pl.BlockSpec((None, SUBLANES, D_MODEL), lambda b, i: (b, jnp.maximum(i * rb - 1, 0), 0)),
            pl.BlockSpec((None, 1, D_MODEL), lambda b, i: (b, 0, 0)),
        ] + [_const_spec(c.shape) for c in consts],
        out_specs=[tok] * 7 + [pl.BlockSpec((None, SUBLANES, D_MODEL), lambda b, i: (b, 0, 0))],
        out_shape=[SDS((B, T, D_MODEL), F32)] * 7 + [SDS((B, SUBLANES, D_MODEL), F32)],
        compiler_params=_cparams(("parallel", "arbitrary")),
        name="rwkv_proj",
    )(x, x, shift_prev, *consts)


def _scan_kernel(r_ref, lw_ref, k_ref, v_ref, a_ref, b_ref, s0_ref, y_ref, st_ref, s_scr, *, nb, tt, C):
    N = RWKV_HEAD
    assert C == N and 2 * N == LANES

    hp = LANES // N
    zero = jnp.zeros((N, N), F32)

    @pl.when(pl.program_id(1) == 0)
    def _():
        for s in range(nb):
            for p in range(D_MODEL // LANES):
                blocks = [jnp.concatenate([s0_ref[s, hp * p + h] if g == h else zero for g in range(hp)], axis=1)
                          for h in range(hp)]
                s_scr[s, p] = jnp.concatenate(blocks, axis=0).T

    row_w = lax.broadcasted_iota(jnp.int32, (C, D_MODEL), 0)
    lane = lax.broadcasted_iota(jnp.int32, (C, LANES), 1)
    h0 = lane < N
    r1 = lax.broadcasted_iota(jnp.int32, (C, 2 * C), 0)
    c1 = lax.broadcasted_iota(jnp.int32, (C, 2 * C), 1) % C
    strict = r1 > c1
    r2 = lax.broadcasted_iota(jnp.int32, (C, 4 * C), 0)
    c2 = lax.broadcasted_iota(jnp.int32, (C, 4 * C), 1) % C
    incl = r2 >= c2
    rs = lax.broadcasted_iota(jnp.int32, (LANES, LANES), 0)
    cs = lax.broadcasted_iota(jnp.int32, (LANES, LANES), 1)
    same_head = (rs < N) == (cs < N)
    eye = rs == cs

    def split(x):
        return jnp.concatenate([jnp.where(h0, x, 0.0), jnp.where(h0, 0.0, x)], axis=0)

    def chunk(c, carry):
        sl = pl.ds(pl.multiple_of(c * C, C), C)
        n_pairs = D_MODEL // LANES
        at, rt, bt, kt, vv, wc = [], [], [], [], [], []
        for s in range(nb):
            lw = lw_ref[s, sl, :]
            cum = lw
            d = 1
            while d < C:
                cum = cum + jnp.where(row_w >= d, pltpu.roll(cum, d, 0), 0.0)
                d *= 2
            e_w = jnp.exp(cum)
            e_n = jnp.exp(-cum)
            rows = (a_ref[s, sl, :] * jnp.exp(cum - lw), r_ref[s, sl, :] * e_w, b_ref[s, sl, :] * e_n,
                    k_ref[s, sl, :] * e_n, v_ref[s, sl, :], e_w[C - 1:C, :])
            for dst, src in zip((at, rt, bt, kt, vv, wc), rows):
                dst.extend(src[:, p * LANES:(p + 1) * LANES] for p in range(n_pairs))
        pairs = range(nb * n_pairs)
        st = [s_scr[p // n_pairs, p % n_pairs] for p in pairs]
        ar = [jnp.concatenate([at[p], rt[p]], axis=0).astype(BF16) for p in pairs]
        v2 = [split(vv[p]).astype(BF16) for p in pairs]
        g = [_dot_nt(ar[p], jnp.concatenate([split(bt[p]), split(kt[p])], axis=0).astype(BF16)) for p in pairs]
        hm = [_dot(ar[p], st[p].astype(BF16)) for p in pairs]
        pw = [jnp.where(strict, g[p][:C, :2 * C], 0.0) for p in pairs]
        u = [hm[p][:C] + _dot(jnp.where(strict, g[p][:C, 2 * C:], 0.0).astype(BF16), v2[p]) for p in pairs]
        n = 1
        while n < C:
            pb = [pw[p].astype(BF16) for p in pairs]
            u = [u[p] + _dot(pb[p], split(u[p]).astype(BF16)) for p in pairs]
            n *= 2
            if n < C:
                pw = [_dot(pb[p], split(pw[p]).astype(BF16)) for p in pairs]
        ys = [hm[p][C:] + _dot(jnp.where(incl, g[p][C:, :], 0.0).astype(BF16),
                               jnp.concatenate([split(u[p]).astype(BF16), v2[p]], axis=0)) for p in pairs]
        for s in range(nb):
            y_ref[s, sl, :] = jnp.concatenate(ys[s * n_pairs:(s + 1) * n_pairs], axis=1)
        for p in pairs:
            bk = jnp.concatenate([bt[p], kt[p]], axis=0).astype(BF16)
            uvp = jnp.concatenate([u[p], vv[p]], axis=0).astype(BF16)
            upd = jnp.where(same_head, _dot_tn(bk, uvp), 0.0)
            w_col = jnp.sum(jnp.where(eye, wc[p], 0.0), axis=1, keepdims=True)
            s_scr[p // n_pairs, p % n_pairs] = (st[p] + upd) * w_col
        return carry

    lax.fori_loop(0, tt // C, chunk, 0)

    @pl.when(pl.program_id(1) == pl.num_programs(1) - 1)
    def _():
        for s in range(nb):
            for p in range(D_MODEL // LANES):
                t = s_scr[s, p].T
                for h in range(hp):
                    st_ref[s, hp * p + h] = t[h * N:(h + 1) * N, h * N:(h + 1) * N]


def _rwkv_scan(r, lw, k, v, na, b, wkv0):
    B, T, _ = r.shape
    C = SCAN_CHUNK
    t_pad = ((T + C - 1) // C) * C
    seqs = (r, lw, k, v, na, b)
    if t_pad != T:
        seqs = tuple(jnp.pad(a, ((0, 0), (0, t_pad - T), (0, 0))) for a in seqs)
    tt = _pick_tile(t_pad, 128)
    nb = _pick_tile(B, SCAN_SEQS)
    n_pairs = D_MODEL // LANES
    tok = pl.BlockSpec((nb, tt, D_MODEL), lambda b_, i: (b_, i, 0))
    st = pl.BlockSpec((nb, RWKV_HEADS, RWKV_HEAD, RWKV_HEAD), lambda b_, i: (b_, 0, 0, 0))
    y, s_t = pl.pallas_call(
        functools.partial(_scan_kernel, nb=nb, tt=tt, C=C),
        grid=(B // nb, t_pad // tt),
        in_specs=[tok] * 6 + [st],
        out_specs=[tok, st],
        out_shape=[SDS((B, t_pad, D_MODEL), F32), SDS((B, RWKV_HEADS, RWKV_HEAD, RWKV_HEAD), F32)],
        scratch_shapes=[pltpu.VMEM((nb, n_pairs, LANES, LANES), F32)],
        compiler_params=_cparams(("parallel", "arbitrary")),
        name="rwkv_scan",
    )(*seqs, wkv0)
    return (y if t_pad == T else y[:, :T]), s_t


def _rout_kernel(x_ref, y_ref, r_ref, k_ref, v_ref, g_ref, rk_ref, lg_ref, lb_ref, wo_ref, bd_ref, o_ref):
    y = y_ref[...]
    inv_n = 1.0 / RWKV_HEAD
    mu = _head_sum(y, bd_ref) * inv_n
    yc = y - mu
    var = _head_sum(yc * yc, bd_ref) * inv_n
    yn = yc * lax.rsqrt(var + LNX_EPS) * lg_ref[...] + lb_ref[...]
    bonus = _head_sum(r_ref[...] * k_ref[...] * rk_ref[...], bd_ref) * v_ref[...]
    z = ((yn + bonus) * g_ref[...]).astype(BF16)
    o_ref[...] = x_ref[...] + _dot(z, wo_ref[...])


def _rwkv_out(x, y, r, k, v, gate, rw):
    M = x.shape[0]
    tm = _pick_tile(M, 512)
    tok = pl.BlockSpec((tm, D_MODEL), lambda i: (i, 0))
    consts = [rw["r_k"], rw["lnx_g"], rw["lnx_b"], rw["w_o"], rw["bd"]]
    return pl.pallas_call(
        _rout_kernel,
        grid=(M // tm,),
        in_specs=[tok] * 6 + [_const_spec(c.shape) for c in consts],
        out_specs=tok,
        out_shape=SDS((M, D_MODEL), F32),
        compiler_params=_cparams(("parallel",)),
        name="rwkv_out",
    )(x, y, r, k, v, gate, *consts)


def _pool_block(x, state, n_hist, g, w_bf, scale, mlp_args=(), final_norm=False):
    B = x.shape[0]
    if state is None:
        hist = jnp.zeros((B, POOL_PAD, D_MODEL), F32)
    else:
        hist = jnp.pad(state, ((0, 0), (POOL_PAD - POOL_HIST, 0), (0, 0)))
    out, hs = _pool_layer(x, hist, n_hist, g, w_bf, scale, mlp_args, final_norm)
    return out, hs[:, POOL_PAD - POOL_HIST:]


def _attn_block(x, cache, g, w_in_pad, wo_bf, q_off, tq):
    B, T, _ = x.shape
    tab = _rope_tables(q_off + jnp.arange(T))
    q, qi, kis, k_new, v_new, ki_new, kb, vb, kib = _attn_project(x, g, w_in_pad, tab)
    keys = (kib, kb, vb)
    if cache is not None:
        ck, cv, cki = cache
        ones_col = jnp.zeros(cv.shape, BF16).at[:, :, 0].set(1.0)
        past = (cki.astype(BF16), ck.astype(BF16), jnp.concatenate([cv.astype(BF16), ones_col], axis=2))
        keys = tuple(jnp.concatenate([c, n], axis=1) for c, n in zip(past, keys))
    n_keys = keys[0].shape[1]
    L = ((n_keys + LANES - 1) // LANES) * LANES
    if L != n_keys:
        keys = tuple(jnp.pad(a, ((0, 0), (0, L - n_keys), (0, 0))) for a in keys)
    topk = min(TOPK_MAX, n_keys // 4)
    nq = T // tq
    if cache is None and nq % ATTN_BUCKET_QBLOCKS == 0:
        groups = [(j0, ATTN_BUCKET_QBLOCKS, (j0 + ATTN_BUCKET_QBLOCKS) * tq)
                  for j0 in range(0, nq, ATTN_BUCKET_QBLOCKS)]
    else:
        groups = [(0, nq, L)]
    out = x
    for j0, n, l_g in groups:
        out = _attn_layer(out, q, qi, kis, keys[0], keys[1], keys[2], wo_bf, q_off, min(n_keys, l_g), topk, tq,
                          j0, n, l_g)
    return out, k_new, v_new, ki_new


def _rwkv_block(x, shift_prev, wkv0, g, rw):
    B, T, _ = x.shape
    r, lw, k, v, na, b, gate, hl = _rwkv_project(x, shift_prev, g, rw)
    y, s_t = _rwkv_scan(r, lw, k, v, na, b, wkv0)
    flat = lambda a: a.reshape(B * T, D_MODEL)
    out = _rwkv_out(flat(x), flat(y), flat(r), flat(k), flat(v), flat(gate), rw).reshape(B, T, D_MODEL)
    return out, hl[:, SUBLANES - 1:], s_t


def kernel(x_prompt, x_sample, state_pool, cache_k, cache_v, cache_kidx, state_shift, state_wkv, ln1_g, ln2_g, w_up, w_down, ln_f_g, pool_w, pool_scale, attn_w_in, attn_w_out, rwkv_mix, rwkv_w0, rwkv_w1, rwkv_w2, rwkv_a0, rwkv_a1, rwkv_a2, rwkv_g1, rwkv_g2, rwkv_k_k, rwkv_k_a, rwkv_r_k, rwkv_w_r, rwkv_w_k, rwkv_w_v, rwkv_w_o, rwkv_lnx_g, rwkv_lnx_b):
    xp, xs = x_prompt, x_sample
    bp, sp, _ = xp.shape
    bs, ss, _ = xs.shape
    past = cache_k.shape[2]
    row = lambda a: a.reshape(1, -1)
    bf = lambda a: a.astype(BF16)
    head_of = jnp.arange(MXU_DIM) // RWKV_HEAD
    bd_mat = (head_of[:, None] == head_of[None, :]).astype(BF16)
    outs = {n: [] for n in ("pool_p", "pool_s", "k_p", "k_s", "v_p", "v_s", "ki_p", "ki_s",
                            "sh_p", "sh_s", "wkv_p", "wkv_s")}
    for i in range(DEPTH):
        j = i // N_MIXERS
        g1 = row(ln1_g[i])
        last = i == DEPTH - 1
        mlp_args = (row(ln2_g[i]), bf(w_up[i]), bf(w_down[i]), row(ln_f_g))
        prompt_mlp_done = False
        if i % N_MIXERS == 0:
            w_bf = bf(pool_w[j])
            sc = row(pool_scale[j])
            prompt_mlp_done = sp >= POOL_MLP_MIN_ROWS
            xp, st_p = _pool_block(xp, None, 0, g1, w_bf, sc, mlp_args if prompt_mlp_done else (), last)
            xs, st_s = _pool_block(xs, state_pool[j], past, g1, w_bf, sc)
            outs["pool_p"].append(st_p)
            outs["pool_s"].append(st_s)
        elif i % N_MIXERS == 1:
            w_in_pad = jnp.pad(bf(attn_w_in[j]), ((0, 0), (0, ATTN_IN_PAD - ATTN_IN_W)))
            wo_bf = bf(attn_w_out[j])
            xp, kp, vp, kip = _attn_block(xp, None, g1, w_in_pad, wo_bf, 0, 256)
            xs, kn, vn, kin = _attn_block(xs, (cache_k[j], cache_v[j], cache_kidx[j]), g1, w_in_pad, wo_bf,
                                          past, ss)
            for n, a in (("k_p", kp), ("v_p", vp), ("ki_p", kip), ("k_s", kn), ("v_s", vn), ("ki_s", kin)):
                outs[n].append(a)
        else:
            rw = dict(mix=rwkv_mix[j], w0=row(rwkv_w0[j]), w1=bf(rwkv_w1[j]), w2=bf(rwkv_w2[j]),
                      a0=row(rwkv_a0[j]), a1=bf(rwkv_a1[j]), a2=bf(rwkv_a2[j]), g1=bf(rwkv_g1[j]),
                      g2=bf(rwkv_g2[j]), k_k=row(rwkv_k_k[j]), k_a=row(rwkv_k_a[j]), r_k=row(rwkv_r_k[j]),
                      w_r=bf(rwkv_w_r[j]), w_k=bf(rwkv_w_k[j]), w_v=bf(rwkv_w_v[j]), w_o=bf(rwkv_w_o[j]),
                      lnx_g=row(rwkv_lnx_g[j]), lnx_b=row(rwkv_lnx_b[j]), bd=bd_mat)
            zero_shift = jnp.zeros((bp, 1, D_MODEL), F32)
            zero_wkv = jnp.zeros((bp, RWKV_HEADS, RWKV_HEAD, RWKV_HEAD), F32)
            xp, shp, wp = _rwkv_block(xp, zero_shift, zero_wkv, g1, rw)
            xs, shs, wsn = _rwkv_block(xs, state_shift[j], state_wkv[j], g1, rw)
            outs["sh_p"].append(shp)
            outs["sh_s"].append(shs)
            outs["wkv_p"].append(wp)
            outs["wkv_s"].append(wsn)
        if not prompt_mlp_done:
            xp = _mlp(xp.reshape(bp * sp, D_MODEL), *mlp_args, last).reshape(bp, sp, D_MODEL)
        xs = _mlp(xs.reshape(bs * ss, D_MODEL), *mlp_args, last).reshape(bs, ss, D_MODEL)
    st = lambda n: jnp.stack(outs[n], 0)
    return (xp, xs, st("pool_p"), st("pool_s"), st("k_p"), st("k_s"), st("v_p"), st("v_s"),
            st("ki_p"), st("ki_s"), st("sh_p"), st("sh_s"), st("wkv_p"), st("wkv_s"))
```

```python
import functools
import math

import jax
import jax.numpy as jnp
import numpy as np
from jax import lax
from jax.experimental import pallas as pl
from jax.experimental.pallas import tpu as pltpu

F32 = jnp.float32
BF16 = jnp.bfloat16
SDS = jax.ShapeDtypeStruct

D_MODEL = 1024
DEPTH = 4
N_MIXERS = 3
CHUNK = 64
D_FF = 4 * D_MODEL
RMS_EPS = 1e-6
POOL_WINDOWS = (2, 4, 8, 16)
POOL_GROUPS = 4
POOL_GW = D_MODEL // POOL_GROUPS
POOL_HIST = max(POOL_WINDOWS) - 1
POOL_PAD = POOL_HIST + 1
ATTN_HEADS = 8
ATTN_HEAD_DIM = D_MODEL // ATTN_HEADS
IDX_HEADS = 8
IDX_DIM = 64
TOPK_MAX = 256
ROPE_THETA = 500000.0
ROPE_FRACTION = 4
NEG_INF = -1e30
Q_W = ATTN_HEADS * ATTN_HEAD_DIM
OFF_K = Q_W
OFF_V = OFF_K + ATTN_HEAD_DIM
OFF_QI = OFF_V + ATTN_HEAD_DIM
OFF_KI = OFF_QI + IDX_HEADS * IDX_DIM
OFF_WI = OFF_KI + IDX_DIM
ATTN_IN_W = OFF_WI + IDX_HEADS
RWKV_HEAD = 64
RWKV_HEADS = D_MODEL // RWKV_HEAD
LNX_EPS = 64e-5

LANES = 128
SUBLANES = 8
ATTN_IN_PAD = ((ATTN_IN_W + LANES - 1) // LANES) * LANES
MXU_DIM = 256
VMEM_LIMIT = 56 * 1024 * 1024
INT_MIN = -2 ** 31
HALF_NEG_KEY = int(np.float32(0.5 * NEG_INF).view(np.int32)) ^ 0x7FFFFFFF
POOL_MLP_MIN_ROWS = 512
POOL_MLP_SUBTILES = 2
ATTN_BUCKET_QBLOCKS = 1
SCAN_CHUNK = 64
SCAN_SEQS = 2
FF_CHUNK = 1024


def _cparams(sem):
    return pltpu.CompilerParams(dimension_semantics=sem, vmem_limit_bytes=VMEM_LIMIT)


def _const_spec(shape):
    nd = len(shape)
    return pl.BlockSpec(shape, lambda *_: (0,) * nd, pipeline_mode=pl.Buffered(1))


def _rms(x, g):
    ms = jnp.mean(x * x, axis=-1, keepdims=True)
    return x * lax.rsqrt(ms + RMS_EPS) * g


def _dot(a, b):
    return jnp.dot(a, b, preferred_element_type=F32)


def _dot_nt(a, b):
    return lax.dot_general(a, b, (((1,), (1,)), ((), ())), preferred_element_type=F32)


def _dot_tn(a, b):
    return lax.dot_general(a, b, (((0,), (0,)), ((), ())), preferred_element_type=F32)


def _head_sum(z, bd_ref):
    bd = bd_ref[...]
    hi = z.astype(BF16)
    lo = (z - hi.astype(F32)).astype(BF16)
    outs = []
    for c in range(D_MODEL // MXU_DIM):
        cs = slice(c * MXU_DIM, (c + 1) * MXU_DIM)
        outs.append(_dot(hi[:, cs], bd) + _dot(lo[:, cs], bd))
    return jnp.concatenate(outs, axis=1)


def _sigmoid(x):
    return 1.0 / (1.0 + jnp.exp(-x))


def _pick_tile(n, pref):
    t = min(n, pref)
    assert n % t == 0, (n, t)
    return t


def _mlp_apply(x, g_ref, wu_ref, wd_ref, gf_ref, final_norm, side=()):
    h = _rms(x, g_ref[...]).astype(BF16)
    acc = x
    for j in range(D_FF // FF_CHUNK):
        u = _dot(h, wu_ref[:, j * FF_CHUNK:(j + 1) * FF_CHUNK])
        u = jnp.square(jnp.maximum(u, 0.0)).astype(BF16)
        acc = acc + _dot(u, wd_ref[j * FF_CHUNK:(j + 1) * FF_CHUNK, :])
        if j < len(side):
            side[j]()
    if final_norm:
        acc = _rms(acc, gf_ref[...])
    return acc


def _mlp_specs():
    return [_const_spec((1, D_MODEL)), _const_spec((D_MODEL, D_FF)), _const_spec((D_FF, D_MODEL)),
            _const_spec((1, D_MODEL))]


def _mlp_kernel(x_ref, g_ref, wu_ref, wd_ref, gf_ref, o_ref, *, final_norm):
    o_ref[...] = _mlp_apply(x_ref[...], g_ref, wu_ref, wd_ref, gf_ref, final_norm)


def _mlp(x, g, wu, wd, gf, final_norm):
    M = x.shape[0]
    tm = _pick_tile(M, 512)
    return pl.pallas_call(
        functools.partial(_mlp_kernel, final_norm=final_norm),
        grid=(M // tm,),
        in_specs=[pl.BlockSpec((tm, D_MODEL), lambda i: (i, 0))] + _mlp_specs(),
        out_specs=pl.BlockSpec((tm, D_MODEL), lambda i: (i, 0)),
        out_shape=SDS((M, D_MODEL), F32),
        compiler_params=_cparams(("parallel",)),
        name="mlp",
    )(x, g, wu, wd, gf)


def _pool_kernel(x_ref, xp_ref, hist_ref, g_ref, w_ref, sc_ref, *rest, tt, n_hist, mlp, final_norm):
    mlp_refs, (o_ref, hs_ref) = rest[:-2], rest[-2:]
    i = pl.program_id(1)
    g = g_ref[...]
    x = x_ref[...]
    h = _rms(x, g)
    prev = jnp.where(i == 0, hist_ref[...], _rms(xp_ref[...], g))
    hs_ref[...] = h[tt - POOL_PAD:, :]
    full = jnp.concatenate([prev, h], axis=0)
    nsub = POOL_MLP_SUBTILES if mlp else 1
    rows = tt // nsub

    def group(r, gi):
        win = POOL_WINDOWS[gi]
        rs = slice(r * rows, (r + 1) * rows)
        cs = slice(gi * POOL_GW, (gi + 1) * POOL_GW)
        s = full[r * rows:(r + 1) * rows + POOL_PAD, cs]
        d = 1
        while d < win:
            s = s + pltpu.roll(s, d, 0)
            d *= 2
        t1 = lax.broadcasted_iota(jnp.int32, (rows, POOL_GW), 0) + (i * tt + r * rows + 1 + n_hist)
        pooled = s[POOL_PAD:, :] / jnp.minimum(t1, win).astype(F32) - h[rs, cs]
        return x[rs, cs] + _dot(pooled.astype(BF16), w_ref[gi]) * sc_ref[:, cs]

    cols = [group(0, gi) for gi in range(POOL_GROUPS)]
    if not mlp:
        o_ref[...] = jnp.concatenate(cols, axis=1)
        return
    outs = []
    for r in range(nsub):
        y = jnp.concatenate(cols, axis=1)
        cols = []
        side = [functools.partial(lambda gi, rn: cols.append(group(rn, gi)), gi, r + 1)
                for gi in range(POOL_GROUPS)] if r + 1 < nsub else []
        outs.append(_mlp_apply(y, *mlp_refs, final_norm, side))
    o_ref[...] = jnp.concatenate(outs, axis=0)


def _pool_layer(x, hist, n_hist, g, w_bf, scale, mlp_args, final_norm):
    B, T, _ = x.shape
    tt = _pick_tile(T, 512)
    rb = tt // POOL_PAD
    return pl.pallas_call(
        functools.partial(_pool_kernel, tt=tt, n_hist=n_hist, mlp=bool(mlp_args), final_norm=final_norm),
        grid=(B, T // tt),
        in_specs=[
            pl.BlockSpec((None, tt, D_MODEL), lambda b, i: (b, i, 0)),
            pl.BlockSpec((None, POOL_PAD, D_MODEL), lambda b, i: (b, jnp.maximum(i * rb - 1, 0), 0)),
            pl.BlockSpec((None, POOL_PAD, D_MODEL), lambda b, i: (b, 0, 0)),
            _const_spec((1, D_MODEL)),
            _const_spec((POOL_GROUPS, POOL_GW, POOL_GW)),
            _const_spec((1, D_MODEL)),
        ] + (_mlp_specs() if mlp_args else []),
        out_specs=[
            pl.BlockSpec((None, tt, D_MODEL), lambda b, i: (b, i, 0)),
            pl.BlockSpec((None, POOL_PAD, D_MODEL), lambda b, i: (b, 0, 0)),
        ],
        out_shape=[SDS((B, T, D_MODEL), F32), SDS((B, POOL_PAD, D_MODEL), F32)],
        compiler_params=_cparams(("parallel", "arbitrary")),
        name="pool_mlp" if mlp_args else "pool",
    )(x, x, hist, g, w_bf, scale, *mlp_args)


def _rope_tables(pos):
    T = pos.shape[0]
    posf = pos.astype(F32)

    def head(d):
        rd = d // ROPE_FRACTION
        half = rd // 2
        inv = ROPE_THETA ** (-jnp.arange(half, dtype=F32) / half)
        ang = posf[:, None] * inv[None, :]
        cos, sin = jnp.cos(ang), jnp.sin(ang)
        z = lambda n: jnp.zeros((T, n), F32)
        c = jnp.concatenate([cos, cos, jnp.ones((T, d - rd), F32)], axis=1)
        sa = jnp.concatenate([-sin, z(d - half)], axis=1)
        sb = jnp.concatenate([z(half), sin, z(d - rd)], axis=1)
        return c, sa, sb

    qa = head(ATTN_HEAD_DIM)
    ia = head(IDX_DIM)
    i2 = tuple(jnp.concatenate([t, t], axis=1) for t in ia)
    ones, zeros = jnp.ones((T, IDX_DIM), F32), jnp.zeros((T, IDX_DIM), F32)
    ik = (jnp.concatenate([ia[0], ones], axis=1), jnp.concatenate([ia[1], zeros], axis=1),
          jnp.concatenate([ia[2], zeros], axis=1))
    return jnp.concatenate(list(qa) + list(i2) + list(ik), axis=1)


def _aproj_kernel(x_ref, g_ref, w_ref, tab_ref, q_ref, qi_ref, kis_ref, k_ref, v_ref, ki_ref, kb_ref, vb_ref, kib_ref):
    h = _rms(x_ref[...], g_ref[...]).astype(BF16)
    p = _dot(h, w_ref[...])

    def rope(x, kind, half):
        c = tab_ref[:, (3 * kind) * LANES:(3 * kind + 1) * LANES]
        sa = tab_ref[:, (3 * kind + 1) * LANES:(3 * kind + 2) * LANES]
        sb = tab_ref[:, (3 * kind + 2) * LANES:(3 * kind + 3) * LANES]
        return x * c + pltpu.roll(x, LANES - half, 1) * sa + pltpu.roll(x, half, 1) * sb

    slab = lambda off: p[:, off:off + LANES]
    qh = ATTN_HEAD_DIM // ROPE_FRACTION // 2
    ih = IDX_DIM // ROPE_FRACTION // 2
    qscale = ATTN_HEAD_DIM ** -0.5 * math.log2(math.e)
    for s in range(ATTN_HEADS):
        q_ref[:, s * LANES:(s + 1) * LANES] = (rope(slab(s * LANES), 0, qh) * qscale).astype(BF16)
    k = rope(slab(OFF_K), 0, qh)
    v = slab(OFF_V)
    for s in range(IDX_HEADS * IDX_DIM // LANES):
        qi_ref[:, s * LANES:(s + 1) * LANES] = rope(slab(OFF_QI + s * LANES), 1, ih).astype(BF16)
    kis = rope(slab(OFF_KI), 2, ih)
    kis_ref[...] = kis
    k_ref[...] = k
    v_ref[...] = v
    ki_ref[...] = kis[:, :IDX_DIM]
    kb_ref[...] = k.astype(BF16)
    lane = lax.broadcasted_iota(jnp.int32, v.shape, 1)
    vb_ref[:, :ATTN_HEAD_DIM] = v.astype(BF16)
    vb_ref[:, ATTN_HEAD_DIM:] = jnp.where(lane == 0, 1.0, 0.0).astype(BF16)
    kib_ref[...] = kis[:, :IDX_DIM].astype(BF16)


def _attn_project(x, g, w_in_pad, tab):
    B, T, _ = x.shape
    tt = _pick_tile(T, 512)
    QI_W = IDX_HEADS * IDX_DIM
    widths = (Q_W, QI_W, LANES, ATTN_HEAD_DIM, ATTN_HEAD_DIM, IDX_DIM, ATTN_HEAD_DIM, 2 * ATTN_HEAD_DIM, IDX_DIM)
    dtypes = (BF16, BF16, F32, F32, F32, F32, BF16, BF16, BF16)
    return pl.pallas_call(
        _aproj_kernel,
        grid=(T // tt, B),
        in_specs=[
            pl.BlockSpec((None, tt, D_MODEL), lambda i, b: (b, i, 0)),
            _const_spec((1, D_MODEL)),
            _const_spec((D_MODEL, ATTN_IN_PAD)),
            pl.BlockSpec((tt, 9 * LANES), lambda i, b: (i, 0)),
        ],
        out_specs=[pl.BlockSpec((None, tt, w), lambda i, b: (b, i, 0)) for w in widths],
        out_shape=[SDS((B, T, w), d) for w, d in zip(widths, dtypes)],
        compiler_params=_cparams(("arbitrary", "arbitrary")),
        name="attn_proj",
    )(x, g, w_in_pad, tab)


def _score_key(score):
    kb = pltpu.bitcast(score, jnp.int32)
    return jnp.where(kb >= 0, kb, kb ^ jnp.int32(0x7FFFFFFF))


def _select_topk(key_scr, jstar_scr, kpos, topk, axis):
    one = jstar_scr.shape
    n_idx = key_scr.shape[axis]

    def count(mask):
        return jnp.sum(jnp.where(mask, 1.0, 0.0), axis=axis, keepdims=True)

    def vbody(it, lo_u):
        trial_u = lo_u | lax.shift_left(jnp.int32(1), 31 - it)
        c = count(key_scr[...] >= (trial_u ^ jnp.int32(INT_MIN)))
        return jnp.where(c >= topk, trial_u, lo_u)

    lo = lax.fori_loop(0, 32, vbody, jnp.zeros(one, jnp.int32)) ^ jnp.int32(INT_MIN)
    gt = key_scr[...] > lo
    eq = key_scr[...] == lo
    need = topk - count(gt)
    tie = (lo > jnp.int32(HALF_NEG_KEY)) & (count(eq) > need)
    jstar_scr[...] = jnp.full(one, n_idx, jnp.int32)

    @pl.when(jnp.max(jnp.where(tie, 1.0, 0.0)) > 0.0)
    def _():
        nbits = max(1, (n_idx - 1).bit_length())

        def ibody(it, lo_i):
            trial = lo_i + lax.shift_left(jnp.int32(1), nbits - 1 - it)
            c = count((kpos < trial) & (key_scr[...] == lo))
            return jnp.where(c < need, trial, lo_i)

        lo_i = lax.fori_loop(0, nbits, ibody, jnp.zeros(one, jnp.int32))
        jstar_scr[...] = jnp.where(tie, lo_i, n_idx)

    return gt | (eq & (kpos <= jstar_scr[...]))


def _attn_kernel(x_ref, q_ref, qi_ref, wi_ref, ki_ref, k_ref, v_ref, wo_ref, o_ref,
                 key_scr, bias_scr, jstar_scr, o_scr, *, tq, L, q_off, n_keys, topk):
    j = pl.program_id(1)
    qpos = lax.broadcasted_iota(jnp.int32, (tq, L), 0) + (q_off + j * tq)
    kpos = lax.broadcasted_iota(jnp.int32, (tq, L), 1)
    adm = (kpos < n_keys) & ((kpos // CHUNK) <= (qpos // CHUNK))

    stack = tq * max(IDX_HEADS, ATTN_HEADS) <= MXU_DIM
    qi = qi_ref[...]
    ki = ki_ref[...]
    wis = wi_ref[...] * (IDX_DIM ** -0.5 * IDX_HEADS ** -0.5)
    qi_h = [qi[:, h * IDX_DIM:(h + 1) * IDX_DIM] for h in range(IDX_HEADS)]
    if stack:
        d_all = _dot_nt(jnp.concatenate(qi_h, axis=0), ki)
        dots = [d_all[h * tq:(h + 1) * tq] for h in range(IDX_HEADS)]
    else:
        dots = [_dot_nt(qh, ki) for qh in qi_h]
    score = jnp.zeros((tq, L), F32)
    for h in range(IDX_HEADS):
        score = score + jnp.maximum(dots[h], 0.0) * wis[:, IDX_DIM + h:IDX_DIM + h + 1]
    score = jnp.where(adm, score, NEG_INF)

    key_scr[...] = _score_key(score)
    keep = _select_topk(key_scr, jstar_scr, kpos, topk, axis=1)
    bias_scr[...] = jnp.where(adm & keep, 0.0, NEG_INF)

    kk = k_ref[...]
    vv = v_ref[...]
    hss = [slice(h * ATTN_HEAD_DIM, (h + 1) * ATTN_HEAD_DIM) for h in range(ATTN_HEADS)]

    def probs(lg):
        lg = lg + bias_scr[...]
        return jnp.exp2(lg - jnp.max(lg, axis=1, keepdims=True)).astype(BF16)

    if stack:
        lg_all = _dot_nt(jnp.concatenate([q_ref[:, hs] for hs in hss], axis=0), kk)
        p_all = jnp.concatenate([probs(lg_all[h * tq:(h + 1) * tq]) for h in range(ATTN_HEADS)], axis=0)
        pv_all = _dot(p_all, vv)
    for h, hs in enumerate(hss):
        pv = pv_all[h * tq:(h + 1) * tq] if stack else _dot(probs(_dot_nt(q_ref[:, hs], kk)), vv)
        o_scr[:, hs] = (pv[:, :ATTN_HEAD_DIM] / pv[:, ATTN_HEAD_DIM:ATTN_HEAD_DIM + 1]).astype(BF16)
    o_ref[...] = x_ref[...] + _dot(o_scr[...], wo_ref[...])


def _attn_layer(x, q, qi, kis, ki_all, k_all, v_all, wo_bf, q_off, n_keys, topk, tq, j0, nq, L):
    B, T, _ = x.shape
    QI_W = IDX_HEADS * IDX_DIM
    return pl.pallas_call(
        functools.partial(_attn_kernel, tq=tq, L=L, q_off=q_off + j0 * tq, n_keys=n_keys, topk=topk),
        grid=(B, nq),
        in_specs=[
            pl.BlockSpec((None, tq, D_MODEL), lambda b, j: (b, j0 + j, 0)),
            pl.BlockSpec((None, tq, Q_W), lambda b, j: (b, j0 + j, 0)),
            pl.BlockSpec((None, tq, QI_W), lambda b, j: (b, j0 + j, 0)),
            pl.BlockSpec((None, tq, LANES), lambda b, j: (b, j0 + j, 0)),
            pl.BlockSpec((None, L, IDX_DIM), lambda b, j: (b, 0, 0)),
            pl.BlockSpec((None, L, ATTN_HEAD_DIM), lambda b, j: (b, 0, 0)),
            pl.BlockSpec((None, L, 2 * ATTN_HEAD_DIM), lambda b, j: (b, 0, 0)),
            _const_spec((Q_W, D_MODEL)),
        ],
        out_specs=pl.BlockSpec((None, tq, D_MODEL), lambda b, j: (b, j0 + j, 0)),
        out_shape=SDS((B, T, D_MODEL), F32),
        input_output_aliases={0: 0},
        scratch_shapes=[
            pltpu.VMEM((tq, L), jnp.int32),
            pltpu.VMEM((tq, L), F32),
            pltpu.VMEM((tq, 1), jnp.int32),
            pltpu.VMEM((tq, Q_W), BF16),
        ],
        compiler_params=_cparams(("parallel", "arbitrary")),
        name="attn",
    )(x, q, qi, kis, ki_all, k_all, v_all, wo_bf)


def _rproj_kernel(x_ref, xp_ref, sh_ref, g_ref, mix_ref, w0_ref, w1_ref, w2_ref, a0_ref, a1_ref, a2_ref,
                  g1_ref, g2_ref, kk_ref, ka_ref, wr_ref, wk_ref, wv_ref, bd_ref,
                  r_o, lw_o, k_o, v_o, na_o, b_o, g_o, hl_o, *, tt):
    i = pl.program_id(1)
    g = g_ref[...]
    h = _rms(x_ref[...], g)
    hl_o[...] = h[tt - SUBLANES:, :]
    prev = jnp.where(i == 0, sh_ref[...], _rms(xp_ref[SUBLANES - 1:SUBLANES, :], g))
    row = lax.broadcasted_iota(jnp.int32, (tt, D_MODEL), 0)
    xx = jnp.where(row == 0, prev, pltpu.roll(h, 1, 0)) - h
    lerp = lambda n: (h + xx * mix_ref[n:n + 1, :]).astype(BF16)
    r = _dot(lerp(0), wr_ref[...])
    wl = w0_ref[...] + _dot(jnp.tanh(_dot(lerp(1), w1_ref[...])).astype(BF16), w2_ref[...])
    lw = -_sigmoid(wl) * math.exp(-0.5)
    k = _dot(lerp(2), wk_ref[...])
    v = _dot(lerp(3), wv_ref[...])
    a = _sigmoid(a0_ref[...] + _dot(_dot(lerp(4), a1_ref[...]).astype(BF16), a2_ref[...]))
    gate = _dot(_sigmoid(_dot(lerp(5), g1_ref[...])).astype(BF16), g2_ref[...])
    kk = k * kk_ref[...]
    kk = kk * lax.rsqrt(jnp.maximum(_head_sum(kk * kk, bd_ref), 1e-24))
    r_o[...] = r
    lw_o[...] = lw
    k_o[...] = k * (1.0 + (a - 1.0) * ka_ref[...])
    v_o[...] = v
    na_o[...] = -kk
    b_o[...] = kk * a
    g_o[...] = gate


def _rwkv_project(x, shift_prev, g, rw):
    B, T, _ = x.shape
    tt = _pick_tile(T, 256)
    rb = tt // SUBLANES
    tok = pl.BlockSpec((None, tt, D_MODEL), lambda b, i: (b, i, 0))
    consts = [g, rw["mix"], rw["w0"], rw["w1"], rw["w2"], rw["a0"], rw["a1"], rw["a2"], rw["g1"], rw["g2"],
              rw["k_k"], rw["k_a"], rw["w_r"], rw["w_k"], rw["w_v"], rw["bd"]]
    return pl.pallas_call(
        functools.partial(_rproj_kernel, tt=tt),
        grid=(B, T // tt),
        in_specs=[
            tok,
            pl.BlockSpec((None, SUBLANES, D_MODEL), lambda b, i: (b, jnp.maximum(i * rb - 1, 0), 0)),
            pl.BlockSpec((None, 1, D_MODEL), lambda b, i: (b, 0, 0)),
        ] + [_const_spec(c.shape) for c in consts],
        out_specs=[tok] * 7 + [pl.BlockSpec((None, SUBLANES, D_MODEL), lambda b, i: (b, 0, 0))],
        out_shape=[SDS((B, T, D_MODEL), F32)] * 7 + [SDS((B, SUBLANES, D_MODEL), F32)],
        compiler_params=_cparams(("parallel", "arbitrary")),
        name="rwkv_proj",
    )(x, x, shift_prev, *consts)


def _scan_kernel(r_ref, lw_ref, k_ref, v_ref, a_ref, b_ref, g_ref, s0_ref, rk_ref, lg_ref, lb_ref,
                 y_ref, st_ref, s_scr, *, nb, tt, C):
    N = RWKV_HEAD
    assert C == N and 2 * N == LANES

    hp = LANES // N
    zero = jnp.zeros((N, N), F32)

    @pl.when(pl.program_id(1) == 0)
    def _():
        for s in range(nb):
            for p in range(D_MODEL // LANES):
                blocks = [jnp.concatenate([s0_ref[s, hp * p + h] if g == h else zero for g in range(hp)], axis=1)
                          for h in range(hp)]
                s_scr[s, p] = jnp.concatenate(blocks, axis=0).T

    row_w = lax.broadcasted_iota(jnp.int32, (C, D_MODEL), 0)
    lane = lax.broadcasted_iota(jnp.int32, (C, LANES), 1)
    h0 = lane < N
    r1 = lax.broadcasted_iota(jnp.int32, (C, 2 * C), 0)
    c1 = lax.broadcasted_iota(jnp.int32, (C, 2 * C), 1) % C
    strict = r1 > c1
    r2 = lax.broadcasted_iota(jnp.int32, (C, 4 * C), 0)
    c2 = lax.broadcasted_iota(jnp.int32, (C, 4 * C), 1) % C
    incl = r2 >= c2
    rs = lax.broadcasted_iota(jnp.int32, (LANES, LANES), 0)
    cs = lax.broadcasted_iota(jnp.int32, (LANES, LANES), 1)
    same_head = (rs < N) == (cs < N)
    eye = rs == cs

    def split(x):
        return jnp.concatenate([jnp.where(h0, x, 0.0), jnp.where(h0, 0.0, x)], axis=0)

    def chunk(c, carry):
        sl = pl.ds(pl.multiple_of(c * C, C), C)
        n_pairs = D_MODEL // LANES
        at, rt, bt, kt, vv, wc = [], [], [], [], [], []
        for s in range(nb):
            lw = lw_ref[s, sl, :]
            cum = lw
            d = 1
            while d < C:
                cum = cum + jnp.where(row_w >= d, pltpu.roll(cum, d, 0), 0.0)
                d *= 2
            e_w = jnp.exp(cum)
            e_n = jnp.exp(-cum)
            rows = (a_ref[s, sl, :] * jnp.exp(cum - lw), r_ref[s, sl, :] * e_w, b_ref[s, sl, :] * e_n,
                    k_ref[s, sl, :] * e_n, v_ref[s, sl, :], e_w[C - 1:C, :])
            for dst, src in zip((at, rt, bt, kt, vv, wc), rows):
                dst.extend(src[:, p * LANES:(p + 1) * LANES] for p in range(n_pairs))
        pairs = range(nb * n_pairs)
        st = [s_scr[p // n_pairs, p % n_pairs] for p in pairs]
        ar = [jnp.concatenate([at[p], rt[p]], axis=0).astype(BF16) for p in pairs]
        v2 = [split(vv[p]).astype(BF16) for p in pairs]
        g = [_dot_nt(ar[p], jnp.concatenate([split(bt[p]), split(kt[p])], axis=0).astype(BF16)) for p in pairs]
        hm = [_dot(ar[p], st[p].astype(BF16)) for p in pairs]
        pw = [jnp.where(strict, g[p][:C, :2 * C], 0.0) for p in pairs]
        u = [hm[p][:C] + _dot(jnp.where(strict, g[p][:C, 2 * C:], 0.0).astype(BF16), v2[p]) for p in pairs]
        n = 1
        while n < C:
            pb = [pw[p].astype(BF16) for p in pairs]
            u = [u[p] + _dot(pb[p], split(u[p]).astype(BF16)) for p in pairs]
            n *= 2
            if n < C:
                pw = [_dot(pb[p], split(pw[p]).astype(BF16)) for p in pairs]
        ys = [hm[p][C:] + _dot(jnp.where(incl, g[p][C:, :], 0.0).astype(BF16),
                               jnp.concatenate([split(u[p]).astype(BF16), v2[p]], axis=0)) for p in pairs]
        for p in pairs:
            bk = jnp.concatenate([bt[p], kt[p]], axis=0).astype(BF16)
            uvp = jnp.concatenate([u[p], vv[p]], axis=0).astype(BF16)
            upd = jnp.where(same_head, _dot_tn(bk, uvp), 0.0)
            w_col = jnp.sum(jnp.where(eye, wc[p], 0.0), axis=1, keepdims=True)
            s_scr[p // n_pairs, p % n_pairs] = (st[p] + upd) * w_col
        inv_n = 1.0 / N

        def head_mean(z):
            s0 = jnp.sum(jnp.where(h0, z, 0.0), axis=1, keepdims=True)
            s1 = jnp.sum(jnp.where(h0, 0.0, z), axis=1, keepdims=True)
            return jnp.where(h0, s0, s1) * inv_n

        for s in range(nb):
            zs = []
            for q in range(n_pairs):
                p = s * n_pairs + q
                ps = slice(q * LANES, (q + 1) * LANES)
                yc = ys[p] - head_mean(ys[p])
                yn = yc * lax.rsqrt(head_mean(yc * yc) + LNX_EPS) * lg_ref[:, ps] + lb_ref[:, ps]
                rk = r_ref[s, sl, ps] * k_ref[s, sl, ps] * rk_ref[:, ps]
                zs.append((yn + head_mean(rk) * N * vv[p]) * g_ref[s, sl, ps])
            y_ref[s, sl, :] = jnp.concatenate(zs, axis=1).astype(BF16)
        return carry

    lax.fori_loop(0, tt // C, chunk, 0)

    @pl.when(pl.program_id(1) == pl.num_programs(1) - 1)
    def _():
        for s in range(nb):
            for p in range(D_MODEL // LANES):
                t = s_scr[s, p].T
                for h in range(hp):
                    st_ref[s, hp * p + h] = t[h * N:(h + 1) * N, h * N:(h + 1) * N]


def _rwkv_scan(r, lw, k, v, na, b, gate, wkv0, rw):
    B, T, _ = r.shape
    C = SCAN_CHUNK
    t_pad = ((T + C - 1) // C) * C
    seqs = (r, lw, k, v, na, b, gate)
    if t_pad != T:
        seqs = tuple(jnp.pad(a, ((0, 0), (0, t_pad - T), (0, 0))) for a in seqs)
    tt = _pick_tile(t_pad, 128)
    nb = _pick_tile(B, SCAN_SEQS)
    n_pairs = D_MODEL // LANES
    tok = pl.BlockSpec((nb, tt, D_MODEL), lambda b_, i: (b_, i, 0))
    st = pl.BlockSpec((nb, RWKV_HEADS, RWKV_HEAD, RWKV_HEAD), lambda b_, i: (b_, 0, 0, 0))
    consts = [rw["r_k"], rw["lnx_g"], rw["lnx_b"]]
    y, s_t = pl.pallas_call(
        functools.partial(_scan_kernel, nb=nb, tt=tt, C=C),
        grid=(B // nb, t_pad // tt),
        in_specs=[tok] * 7 + [st] + [_const_spec(c.shape) for c in consts],
        out_specs=[tok, st],
        out_shape=[SDS((B, t_pad, D_MODEL), BF16), SDS((B, RWKV_HEADS, RWKV_HEAD, RWKV_HEAD), F32)],
        scratch_shapes=[pltpu.VMEM((nb, n_pairs, LANES, LANES), F32)],
        compiler_params=_cparams(("parallel", "arbitrary")),
        name="rwkv_scan",
    )(*seqs, wkv0, *consts)
    return (y if t_pad == T else y[:, :T]), s_t


def _wo_mlp_kernel(x_ref, z_ref, wo_ref, g_ref, wu_ref, wd_ref, gf_ref, o_ref, *, final_norm):
    x = x_ref[...] + _dot(z_ref[...], wo_ref[...])
    o_ref[...] = _mlp_apply(x, g_ref, wu_ref, wd_ref, gf_ref, final_norm)


def _wo_mlp(x, z, wo, mlp_args, final_norm):
    M = x.shape[0]
    tm = _pick_tile(M, 512)
    tok = pl.BlockSpec((tm, D_MODEL), lambda i: (i, 0))
    return pl.pallas_call(
        functools.partial(_wo_mlp_kernel, final_norm=final_norm),
        grid=(M // tm,),
        in_specs=[tok, tok, _const_spec(wo.shape)] + _mlp_specs(),
        out_specs=tok,
        out_shape=SDS((M, D_MODEL), F32),
        compiler_params=_cparams(("parallel",)),
        name="wo_mlp",
    )(x, z, wo, *mlp_args)


def _pool_block(x, state, n_hist, g, w_bf, scale, mlp_args=(), final_norm=False):
    B = x.shape[0]
    if state is None:
        hist = jnp.zeros((B, POOL_PAD, D_MODEL), F32)
    else:
        hist = jnp.pad(state, ((0, 0), (POOL_PAD - POOL_HIST, 0), (0, 0)))
    out, hs = _pool_layer(x, hist, n_hist, g, w_bf, scale, mlp_args, final_norm)
    return out, hs[:, POOL_PAD - POOL_HIST:]


def _attn_block(x, cache, g, w_in_pad, wo_bf, q_off, tq):
    B, T, _ = x.shape
    tab = _rope_tables(q_off + jnp.arange(T))
    q, qi, kis, k_new, v_new, ki_new, kb, vb, kib = _attn_project(x, g, w_in_pad, tab)
    keys = (kib, kb, vb)
    if cache is not None:
        ck, cv, cki = cache
        ones_col = jnp.zeros(cv.shape, BF16).at[:, :, 0].set(1.0)
        past = (cki.astype(BF16), ck.astype(BF16), jnp.concatenate([cv.astype(BF16), ones_col], axis=2))
        keys = tuple(jnp.concatenate([c, n], axis=1) for c, n in zip(past, keys))
    n_keys = keys[0].shape[1]
    L = ((n_keys + LANES - 1) // LANES) * LANES
    if L != n_keys:
        keys = tuple(jnp.pad(a, ((0, 0), (0, L - n_keys), (0, 0))) for a in keys)
    topk = min(TOPK_MAX, n_keys // 4)
    nq = T // tq
    if cache is None and nq % ATTN_BUCKET_QBLOCKS == 0:
        groups = [(j0, ATTN_BUCKET_QBLOCKS, (j0 + ATTN_BUCKET_QBLOCKS) * tq)
                  for j0 in range(0, nq, ATTN_BUCKET_QBLOCKS)]
    else:
        groups = [(0, nq, L)]
    out = x
    for j0, n, l_g in groups:
        out = _attn_layer(out, q, qi, kis, keys[0], keys[1], keys[2], wo_bf, q_off, min(n_keys, l_g), topk, tq,
                          j0, n, l_g)
    return out, k_new, v_new, ki_new


def _rwkv_block(x, shift_prev, wkv0, g, rw, mlp_args, final_norm):
    B, T, _ = x.shape
    r, lw, k, v, na, b, gate, hl = _rwkv_project(x, shift_prev, g, rw)
    z, s_t = _rwkv_scan(r, lw, k, v, na, b, gate, wkv0, rw)
    flat = lambda a: a.reshape(B * T, D_MODEL)
    out = _wo_mlp(flat(x), flat(z), rw["w_o"], mlp_args, final_norm).reshape(B, T, D_MODEL)
    return out, hl[:, SUBLANES - 1:], s_t


def kernel(x_prompt, x_sample, state_pool, cache_k, cache_v, cache_kidx, state_shift, state_wkv, ln1_g, ln2_g, w_up, w_down, ln_f_g, pool_w, pool_scale, attn_w_in, attn_w_out, rwkv_mix, rwkv_w0, rwkv_w1, rwkv_w2, rwkv_a0, rwkv_a1, rwkv_a2, rwkv_g1, rwkv_g2, rwkv_k_k, rwkv_k_a, rwkv_r_k, rwkv_w_r, rwkv_w_k, rwkv_w_v, rwkv_w_o, rwkv_lnx_g, rwkv_lnx_b):
    xp, xs = x_prompt, x_sample
    bp, sp, _ = xp.shape
    bs, ss, _ = xs.shape
    past = cache_k.shape[2]
    row = lambda a: a.reshape(1, -1)
    bf = lambda a: a.astype(BF16)
    head_of = jnp.arange(MXU_DIM) // RWKV_HEAD
    bd_mat = (head_of[:, None] == head_of[None, :]).astype(BF16)
    outs = {n: [] for n in ("pool_p", "pool_s", "k_p", "k_s", "v_p", "v_s", "ki_p", "ki_s",
                            "sh_p", "sh_s", "wkv_p", "wkv_s")}
    for i in range(DEPTH):
        j = i // N_MIXERS
        g1 = row(ln1_g[i])
        last = i == DEPTH - 1
        mlp_args = (row(ln2_g[i]), bf(w_up[i]), bf(w_down[i]), row(ln_f_g))
        prompt_mlp_done = sample_mlp_done = False
        if i % N_MIXERS == 0:
            w_bf = bf(pool_w[j])
            sc = row(pool_scale[j])
            prompt_mlp_done = sp >= POOL_MLP_MIN_ROWS
            xp, st_p = _pool_block(xp, None, 0, g1, w_bf, sc, mlp_args if prompt_mlp_done else (), last)
            xs, st_s = _pool_block(xs, state_pool[j], past, g1, w_bf, sc)
            outs["pool_p"].append(st_p)
            outs["pool_s"].append(st_s)
        elif i % N_MIXERS == 1:
            w_in_pad = jnp.pad(bf(attn_w_in[j]), ((0, 0), (0, ATTN_IN_PAD - ATTN_IN_W)))
            wo_bf = bf(attn_w_out[j])
            xp, kp, vp, kip = _attn_block(xp, None, g1, w_in_pad, wo_bf, 0, 256)
            xs, kn, vn, kin = _attn_block(xs, (cache_k[j], cache_v[j], cache_kidx[j]), g1, w_in_pad, wo_bf,
                                          past, ss)
            for n, a in (("k_p", kp), ("v_p", vp), ("ki_p", kip), ("k_s", kn), ("v_s", vn), ("ki_s", kin)):
                outs[n].append(a)
        else:
            rw = dict(mix=rwkv_mix[j], w0=row(rwkv_w0[j]), w1=bf(rwkv_w1[j]), w2=bf(rwkv_w2[j]),
                      a0=row(rwkv_a0[j]), a1=bf(rwkv_a1[j]), a2=bf(rwkv_a2[j]), g1=bf(rwkv_g1[j]),
                      g2=bf(rwkv_g2[j]), k_k=row(rwkv_k_k[j]), k_a=row(rwkv_k_a[j]), r_k=row(rwkv_r_k[j]),
                      w_r=bf(rwkv_w_r[j]), w_k=bf(rwkv_w_k[j]), w_v=bf(rwkv_w_v[j]), w_o=bf(rwkv_w_o[j]),
                      lnx_g=row(rwkv_lnx_g[j]), lnx_b=row(rwkv_lnx_b[j]), bd=bd_mat)
            zero_shift = jnp.zeros((bp, 1, D_MODEL), F32)
            zero_wkv = jnp.zeros((bp, RWKV_HEADS, RWKV_HEAD, RWKV_HEAD), F32)
            xp, shp, wp = _rwkv_block(xp, zero_shift, zero_wkv, g1, rw, mlp_args, last)
            xs, shs, wsn = _rwkv_block(xs, state_shift[j], state_wkv[j], g1, rw, mlp_args, last)
            prompt_mlp_done = sample_mlp_done = True
            outs["sh_p"].append(shp)
            outs["sh_s"].append(shs)
            outs["wkv_p"].append(wp)
            outs["wkv_s"].append(wsn)
        if not prompt_mlp_done:
            xp = _mlp(xp.reshape(bp * sp, D_MODEL), *mlp_args, last).reshape(bp, sp, D_MODEL)
        if not sample_mlp_done:
            xs = _mlp(xs.reshape(bs * ss, D_MODEL), *mlp_args, last).reshape(bs, ss, D_MODEL)
    st = lambda n: jnp.stack(outs[n], 0)
    return (xp, xs, st("pool_p"), st("pool_s"), st("k_p"), st("k_s"), st("v_p"), st("v_s"),
            st("ki_p"), st("ki_s"), st("sh_p"), st("sh_s"), st("wkv_p"), st("wkv_s"))
```

```python
import functools
import math

import jax
import jax.numpy as jnp
import numpy as np
from jax import lax
from jax.experimental import pallas as pl
from jax.experimental.pallas import tpu as pltpu

F32 = jnp.float32
BF16 = jnp.bfloat16
SDS = jax.ShapeDtypeStruct

D_MODEL = 1024
DEPTH = 4
N_MIXERS = 3
CHUNK = 64
D_FF = 4 * D_MODEL
RMS_EPS = 1e-6
POOL_WINDOWS = (2, 4, 8, 16)
POOL_GROUPS = 4
POOL_GW = D_MODEL // POOL_GROUPS
POOL_HIST = max(POOL_WINDOWS) - 1
POOL_PAD = POOL_HIST + 1
ATTN_HEADS = 8
ATTN_HEAD_DIM = D_MODEL // ATTN_HEADS
IDX_HEADS = 8
IDX_DIM = 64
TOPK_MAX = 256
ROPE_THETA = 500000.0
ROPE_FRACTION = 4
NEG_INF = -1e30
Q_W = ATTN_HEADS * ATTN_HEAD_DIM
OFF_K = Q_W
OFF_V = OFF_K + ATTN_HEAD_DIM
OFF_QI = OFF_V + ATTN_HEAD_DIM
OFF_KI = OFF_QI + IDX_HEADS * IDX_DIM
OFF_WI = OFF_KI + IDX_DIM
ATTN_IN_W = OFF_WI + IDX_HEADS
RWKV_HEAD = 64
RWKV_HEADS = D_MODEL // RWKV_HEAD
LNX_EPS = 64e-5

LANES = 128
SUBLANES = 8
ATTN_IN_PAD = ((ATTN_IN_W + LANES - 1) // LANES) * LANES
MXU_DIM = 256
VMEM_LIMIT = 56 * 1024 * 1024
INT_MIN = -2 ** 31
HALF_NEG_KEY = int(np.float32(0.5 * NEG_INF).view(np.int32)) ^ 0x7FFFFFFF
POOL_MLP_MIN_ROWS = 512
POOL_MLP_SUBTILES = 2
ATTN_BUCKET_QBLOCKS = 1
SCAN_CHUNK = 64
SCAN_SEQS = 2
FF_CHUNK = 1024


def _cparams(sem):
    return pltpu.CompilerParams(dimension_semantics=sem, vmem_limit_bytes=VMEM_LIMIT)


def _const_spec(shape):
    nd = len(shape)
    return pl.BlockSpec(shape, lambda *_: (0,) * nd, pipeline_mode=pl.Buffered(1))


def _rms(x, g):
    ms = jnp.mean(x * x, axis=-1, keepdims=True)
    return x * lax.rsqrt(ms + RMS_EPS) * g


def _dot(a, b):
    return jnp.dot(a, b, preferred_element_type=F32)


def _dot_nt(a, b):
    return lax.dot_general(a, b, (((1,), (1,)), ((), ())), preferred_element_type=F32)


def _dot_tn(a, b):
    return lax.dot_general(a, b, (((0,), (0,)), ((), ())), preferred_element_type=F32)


def _head_sum(z, bd_ref):
    bd = bd_ref[...]
    hi = z.astype(BF16)
    lo = (z - hi.astype(F32)).astype(BF16)
    outs = []
    for c in range(D_MODEL // MXU_DIM):
        cs = slice(c * MXU_DIM, (c + 1) * MXU_DIM)
        outs.append(_dot(hi[:, cs], bd) + _dot(lo[:, cs], bd))
    return jnp.concatenate(outs, axis=1)


def _sigmoid(x):
    return 1.0 / (1.0 + jnp.exp(-x))


def _pick_tile(n, pref):
    t = min(n, pref)
    assert n % t == 0, (n, t)
    return t


def _mlp_apply(x, g_ref, wu_ref, wd_ref, gf_ref, final_norm, side=()):
    h = _rms(x, g_ref[...]).astype(BF16)
    acc = x
    for j in range(D_FF // FF_CHUNK):
        u = _dot(h, wu_ref[:, j * FF_CHUNK:(j + 1) * FF_CHUNK])
        u = jnp.square(jnp.maximum(u, 0.0)).astype(BF16)
        acc = acc + _dot(u, wd_ref[j * FF_CHUNK:(j + 1) * FF_CHUNK, :])
        if j < len(side):
            side[j]()
    if final_norm:
        acc = _rms(acc, gf_ref[...])
    return acc


def _mlp_specs():
    return [_const_spec((1, D_MODEL)), _const_spec((D_MODEL, D_FF)), _const_spec((D_FF, D_MODEL)),
            _const_spec((1, D_MODEL))]


def _mlp_kernel(x_ref, g_ref, wu_ref, wd_ref, gf_ref, o_ref, *, final_norm):
    o_ref[...] = _mlp_apply(x_ref[...], g_ref, wu_ref, wd_ref, gf_ref, final_norm)


def _mlp(x, g, wu, wd, gf, final_norm):
    M = x.shape[0]
    tm = _pick_tile(M, 512)
    return pl.pallas_call(
        functools.partial(_mlp_kernel, final_norm=final_norm),
        grid=(M // tm,),
        in_specs=[pl.BlockSpec((tm, D_MODEL), lambda i: (i, 0))] + _mlp_specs(),
        out_specs=pl.BlockSpec((tm, D_MODEL), lambda i: (i, 0)),
        out_shape=SDS((M, D_MODEL), F32),
        compiler_params=_cparams(("parallel",)),
        name="mlp",
    )(x, g, wu, wd, gf)


def _pool_kernel(x_ref, xp_ref, hist_ref, g_ref, w_ref, sc_ref, *rest, tt, n_hist, mlp, final_norm):
    mlp_refs, (o_ref, hs_ref) = rest[:-2], rest[-2:]
    i = pl.program_id(1)
    g = g_ref[...]
    x = x_ref[...]
    h = _rms(x, g)
    prev = jnp.where(i == 0, hist_ref[...], _rms(xp_ref[...], g))
    hs_ref[...] = h[tt - POOL_PAD:, :]
    full = jnp.concatenate([prev, h], axis=0)
    nsub = POOL_MLP_SUBTILES if mlp else 1
    rows = tt // nsub

    def group(r, gi):
        win = POOL_WINDOWS[gi]
        rs = slice(r * rows, (r + 1) * rows)
        cs = slice(gi * POOL_GW, (gi + 1) * POOL_GW)
        s = full[r * rows:(r + 1) * rows + POOL_PAD, cs]
        d = 1
        while d < win:
            s = s + pltpu.roll(s, d, 0)
            d *= 2
        t1 = lax.broadcasted_iota(jnp.int32, (rows, POOL_GW), 0) + (i * tt + r * rows + 1 + n_hist)
        pooled = s[POOL_PAD:, :] / jnp.minimum(t1, win).astype(F32) - h[rs, cs]
        return x[rs, cs] + _dot(pooled.astype(BF16), w_ref[gi]) * sc_ref[:, cs]

    cols = [group(0, gi) for gi in range(POOL_GROUPS)]
    if not mlp:
        o_ref[...] = jnp.concatenate(cols, axis=1)
        return
    outs = []
    for r in range(nsub):
        y = jnp.concatenate(cols, axis=1)
        cols = []
        side = [functools.partial(lambda gi, rn: cols.append(group(rn, gi)), gi, r + 1)
                for gi in range(POOL_GROUPS)] if r + 1 < nsub else []
        outs.append(_mlp_apply(y, *mlp_refs, final_norm, side))
    o_ref[...] = jnp.concatenate(outs, axis=0)


def _pool_layer(x, hist, n_hist, g, w_bf, scale, mlp_args, final_norm):
    B, T, _ = x.shape
    tt = _pick_tile(T, 512)
    rb = tt // POOL_PAD
    return pl.pallas_call(
        functools.partial(_pool_kernel, tt=tt, n_hist=n_hist, mlp=bool(mlp_args), final_norm=final_norm),
        grid=(B, T // tt),
        in_specs=[
            pl.BlockSpec((None, tt, D_MODEL), lambda b, i: (b, i, 0)),
            pl.BlockSpec((None, POOL_PAD, D_MODEL), lambda b, i: (b, jnp.maximum(i * rb - 1, 0), 0)),
            pl.BlockSpec((None, POOL_PAD, D_MODEL), lambda b, i: (b, 0, 0)),
            _const_spec((1, D_MODEL)),
            _const_spec((POOL_GROUPS, POOL_GW, POOL_GW)),
            _const_spec((1, D_MODEL)),
        ] + (_mlp_specs() if mlp_args else []),
        out_specs=[
            pl.BlockSpec((None, tt, D_MODEL), lambda b, i: (b, i, 0)),
            pl.BlockSpec((None, POOL_PAD, D_MODEL), lambda b, i: (b, 0, 0)),
        ],
        out_shape=[SDS((B, T, D_MODEL), F32), SDS((B, POOL_PAD, D_MODEL), F32)],
        compiler_params=_cparams(("parallel", "arbitrary")),
        name="pool_mlp" if mlp_args else "pool",
    )(x, x, hist, g, w_bf, scale, *mlp_args)


def _rope_tables(pos):
    T = pos.shape[0]
    posf = pos.astype(F32)

    def head(d):
        rd = d // ROPE_FRACTION
        half = rd // 2
        inv = ROPE_THETA ** (-jnp.arange(half, dtype=F32) / half)
        ang = posf[:, None] * inv[None, :]
        cos, sin = jnp.cos(ang), jnp.sin(ang)
        z = lambda n: jnp.zeros((T, n), F32)
        c = jnp.concatenate([cos, cos, jnp.ones((T, d - rd), F32)], axis=1)
        sa = jnp.concatenate([-sin, z(d - half)], axis=1)
        sb = jnp.concatenate([z(half), sin, z(d - rd)], axis=1)
        return c, sa, sb

    qa = head(ATTN_HEAD_DIM)
    ia = head(IDX_DIM)
    i2 = tuple(jnp.concatenate([t, t], axis=1) for t in ia)
    ones, zeros = jnp.ones((T, IDX_DIM), F32), jnp.zeros((T, IDX_DIM), F32)
    ik = (jnp.concatenate([ia[0], ones], axis=1), jnp.concatenate([ia[1], zeros], axis=1),
          jnp.concatenate([ia[2], zeros], axis=1))
    return jnp.concatenate(list(qa) + list(i2) + list(ik), axis=1)


def _aproj_kernel(x_ref, g_ref, w_ref, tab_ref, q_ref, qi_ref, kis_ref, k_ref, v_ref, ki_ref, kb_ref, vb_ref, kib_ref):
    h = _rms(x_ref[...], g_ref[...]).astype(BF16)
    p = _dot(h, w_ref[...])

    def rope(x, kind, half):
        c = tab_ref[:, (3 * kind) * LANES:(3 * kind + 1) * LANES]
        sa = tab_ref[:, (3 * kind + 1) * LANES:(3 * kind + 2) * LANES]
        sb = tab_ref[:, (3 * kind + 2) * LANES:(3 * kind + 3) * LANES]
        return x * c + pltpu.roll(x, LANES - half, 1) * sa + pltpu.roll(x, half, 1) * sb

    slab = lambda off: p[:, off:off + LANES]
    qh = ATTN_HEAD_DIM // ROPE_FRACTION // 2
    ih = IDX_DIM // ROPE_FRACTION // 2
    qscale = ATTN_HEAD_DIM ** -0.5 * math.log2(math.e)
    for s in range(ATTN_HEADS):
        q_ref[s] = (rope(slab(s * LANES), 0, qh) * qscale).astype(BF16)
    k = rope(slab(OFF_K), 0, qh)
    v = slab(OFF_V)
    for s in range(IDX_HEADS * IDX_DIM // LANES):
        qi_ref[:, s * LANES:(s + 1) * LANES] = rope(slab(OFF_QI + s * LANES), 1, ih).astype(BF16)
    kis = rope(slab(OFF_KI), 2, ih)
    kis_ref[...] = kis
    k_ref[...] = k
    v_ref[...] = v
    ki_ref[...] = kis[:, :IDX_DIM]
    kb_ref[...] = k.astype(BF16)
    lane = lax.broadcasted_iota(jnp.int32, v.shape, 1)
    vb_ref[:, :ATTN_HEAD_DIM] = v.astype(BF16)
    vb_ref[:, ATTN_HEAD_DIM:] = jnp.where(lane == 0, 1.0, 0.0).astype(BF16)
    kib_ref[...] = kis[:, :IDX_DIM].astype(BF16)


def _attn_project(x, g, w_in_pad, tab):
    B, T, _ = x.shape
    tt = _pick_tile(T, 512)
    QI_W = IDX_HEADS * IDX_DIM
    widths = (QI_W, LANES, ATTN_HEAD_DIM, ATTN_HEAD_DIM, IDX_DIM, ATTN_HEAD_DIM, 2 * ATTN_HEAD_DIM, IDX_DIM)
    dtypes = (BF16, F32, F32, F32, F32, BF16, BF16, BF16)
    q_spec = pl.BlockSpec((None, ATTN_HEADS, tt, ATTN_HEAD_DIM), lambda i, b: (b, 0, i, 0))
    return pl.pallas_call(
        _aproj_kernel,
        grid=(T // tt, B),
        in_specs=[
            pl.BlockSpec((None, tt, D_MODEL), lambda i, b: (b, i, 0)),
            _const_spec((1, D_MODEL)),
            _const_spec((D_MODEL, ATTN_IN_PAD)),
            pl.BlockSpec((tt, 9 * LANES), lambda i, b: (i, 0)),
        ],
        out_specs=[q_spec] + [pl.BlockSpec((None, tt, w), lambda i, b: (b, i, 0)) for w in widths],
        out_shape=[SDS((B, ATTN_HEADS, T, ATTN_HEAD_DIM), BF16)] + [SDS((B, T, w), d) for w, d in zip(widths, dtypes)],
        compiler_params=_cparams(("arbitrary", "arbitrary")),
        name="attn_proj",
    )(x, g, w_in_pad, tab)


def _score_key(score):
    kb = pltpu.bitcast(score, jnp.int32)
    return jnp.where(kb >= 0, kb, kb ^ jnp.int32(0x7FFFFFFF))


KEY_BITS = 32


def _count(mask, axis):
    return jnp.sum(jnp.where(mask, 1.0, 0.0), axis=axis, keepdims=True)


def _kth_key_step(key_scr, lo_u, it, topk, axis):
    trial_u = lo_u | lax.shift_left(jnp.int32(1), KEY_BITS - 1 - it)
    c = _count(key_scr[...] >= (trial_u ^ jnp.int32(INT_MIN)), axis)
    return jnp.where(c >= topk, trial_u, lo_u)


def _select_topk(key_scr, jstar_scr, kpos, topk, axis):
    step = lambda it, lo_u: _kth_key_step(key_scr, lo_u, it, topk, axis)
    lo_u = lax.fori_loop(0, KEY_BITS, step, jnp.zeros(jstar_scr.shape, jnp.int32))
    return _finish_topk(key_scr, jstar_scr, kpos, lo_u, topk, axis)


def _finish_topk(key_scr, jstar_scr, kpos, lo_u, topk, axis):
    one = jstar_scr.shape
    n_idx = key_scr.shape[axis]
    count = functools.partial(_count, axis=axis)
    lo = lo_u ^ jnp.int32(INT_MIN)
    gt = key_scr[...] > lo
    eq = key_scr[...] == lo
    need = topk - count(gt)
    tie = (lo > jnp.int32(HALF_NEG_KEY)) & (count(eq) > need)
    jstar_scr[...] = jnp.full(one, n_idx, jnp.int32)

    @pl.when(jnp.max(jnp.where(tie, 1.0, 0.0)) > 0.0)
    def _():
        nbits = max(1, (n_idx - 1).bit_length())

        def ibody(it, lo_i):
            trial = lo_i + lax.shift_left(jnp.int32(1), nbits - 1 - it)
            c = count((kpos < trial) & (key_scr[...] == lo))
            return jnp.where(c < need, trial, lo_i)

        lo_i = lax.fori_loop(0, nbits, ibody, jnp.zeros(one, jnp.int32))
        jstar_scr[...] = jnp.where(tie, lo_i, n_idx)

    return gt | (eq & (kpos <= jstar_scr[...]))


def _attn_kernel(x_ref, q_ref, qi_ref, wi_ref, ki_ref, k_ref, v_ref, wo_ref, o_ref,
                 key_scr, bias_scr, jstar_scr, o_scr, *, tq, L, q_off, n_keys, topk):
    j = pl.program_id(1)
    qpos = lax.broadcasted_iota(jnp.int32, (tq, L), 0) + (q_off + j * tq)
    kpos = lax.broadcasted_iota(jnp.int32, (tq, L), 1)
    adm = (kpos < n_keys) & ((kpos // CHUNK) <= (qpos // CHUNK))

    stack = tq * max(IDX_HEADS, ATTN_HEADS) <= MXU_DIM
    qi = qi_ref[...]
    ki = ki_ref[...]
    wis = wi_ref[...] * (IDX_DIM ** -0.5 * IDX_HEADS ** -0.5)
    qi_h = [qi[:, h * IDX_DIM:(h + 1) * IDX_DIM] for h in range(IDX_HEADS)]
    if stack:
        d_all = _dot_nt(jnp.concatenate(qi_h, axis=0), ki)
        dots = [d_all[h * tq:(h + 1) * tq] for h in range(IDX_HEADS)]
    else:
        dots = [_dot_nt(qh, ki) for qh in qi_h]
    score = jnp.zeros((tq, L), F32)
    for h in range(IDX_HEADS):
        score = score + jnp.maximum(dots[h], 0.0) * wis[:, IDX_DIM + h:IDX_DIM + h + 1]
    score = jnp.where(adm, score, NEG_INF)

    key_scr[...] = _score_key(score)
    keep = _select_topk(key_scr, jstar_scr, kpos, topk, axis=1)
    bias_scr[...] = jnp.where(adm & keep, 0.0, NEG_INF)

    kk = k_ref[...]
    vv = v_ref[...]
    hss = [slice(h * ATTN_HEAD_DIM, (h + 1) * ATTN_HEAD_DIM) for h in range(ATTN_HEADS)]

    def probs(lg):
        lg = lg + bias_scr[...]
        return jnp.exp2(lg - jnp.max(lg, axis=1, keepdims=True)).astype(BF16)

    if stack:
        lg_all = _dot_nt(jnp.concatenate([q_ref[h] for h in range(ATTN_HEADS)], axis=0), kk)
        p_all = jnp.concatenate([probs(lg_all[h * tq:(h + 1) * tq]) for h in range(ATTN_HEADS)], axis=0)
        pv_all = _dot(p_all, vv)
    for h, hs in enumerate(hss):
        pv = pv_all[h * tq:(h + 1) * tq] if stack else _dot(probs(_dot_nt(q_ref[h], kk)), vv)
        o_scr[:, hs] = (pv[:, :ATTN_HEAD_DIM] / pv[:, ATTN_HEAD_DIM:ATTN_HEAD_DIM + 1]).astype(BF16)
    o_ref[...] = x_ref[...] + _dot(o_scr[...], wo_ref[...])


def _attn_layer(x, q, qi, kis, ki_all, k_all, v_all, wo_bf, q_off, n_keys, topk, tq, j0, nq, L):
    B, T, _ = x.shape
    QI_W = IDX_HEADS * IDX_DIM
    return pl.pallas_call(
        functools.partial(_attn_kernel, tq=tq, L=L, q_off=q_off + j0 * tq, n_keys=n_keys, topk=topk),
        grid=(B, nq),
        in_specs=[
            pl.BlockSpec((None, tq, D_MODEL), lambda b, j: (b, j0 + j, 0)),
            pl.BlockSpec((None, ATTN_HEADS, tq, ATTN_HEAD_DIM), lambda b, j: (b, 0, j0 + j, 0)),
            pl.BlockSpec((None, tq, QI_W), lambda b, j: (b, j0 + j, 0)),
            pl.BlockSpec((None, tq, LANES), lambda b, j: (b, j0 + j, 0)),
            pl.BlockSpec((None, L, IDX_DIM), lambda b, j: (b, 0, 0)),
            pl.BlockSpec((None, L, ATTN_HEAD_DIM), lambda b, j: (b, 0, 0)),
            pl.BlockSpec((None, L, 2 * ATTN_HEAD_DIM), lambda b, j: (b, 0, 0)),
            _const_spec((Q_W, D_MODEL)),
        ],
        out_specs=pl.BlockSpec((None, tq, D_MODEL), lambda b, j: (b, j0 + j, 0)),
        out_shape=SDS((B, T, D_MODEL), F32),
        input_output_aliases={0: 0},
        scratch_shapes=[
            pltpu.VMEM((tq, L), jnp.int32),
            pltpu.VMEM((tq, L), F32),
            pltpu.VMEM((tq, 1), jnp.int32),
            pltpu.VMEM((tq, Q_W), BF16),
        ],
        compiler_params=_cparams(("parallel", "arbitrary")),
        name="attn",
    )(x, q, qi, kis, ki_all, k_all, v_all, wo_bf)


def _attn_pipe_kernel(x_ref, q_ref, qi_ref, wi_ref, ki_ref, k_ref, v_ref, wo_ref, o_ref,
                      key_scr, bias_scr, lg_scr, jstar_scr, o_scr, *, tq, L, q_off, n_keys, topk):
    s = pl.program_id(0)
    cur = s % 2

    @pl.when(s == 0)
    def _():
        bias_scr[...] = jnp.zeros(bias_scr.shape, F32)

    qpos = lax.broadcasted_iota(jnp.int32, (tq, L), 0) + q_off
    kpos = lax.broadcasted_iota(jnp.int32, (tq, L), 1)
    adm = (kpos < n_keys) & ((kpos // CHUNK) <= (qpos // CHUNK))

    qi = qi_ref[...]
    ki = ki_ref[...]
    wis = wi_ref[...] * (IDX_DIM ** -0.5 * IDX_HEADS ** -0.5)
    score = jnp.zeros((tq, L), F32)
    for h in range(IDX_HEADS):
        d = _dot_nt(qi[:, h * IDX_DIM:(h + 1) * IDX_DIM], ki)
        score = score + jnp.maximum(d, 0.0) * wis[:, IDX_DIM + h:IDX_DIM + h + 1]
    key_scr[...] = _score_key(jnp.where(adm, score, NEG_INF))

    steps_per_head = KEY_BITS // ATTN_HEADS

    lg_scr[0] = _dot_nt(q_ref[0], k_ref[...])

    def head(h, lo_u):
        step = lambda i, lo: _kth_key_step(key_scr, lo, h * steps_per_head + i, topk, 1)
        lo_u = step(0, lo_u)
        lg = lg_scr[h % 2] + bias_scr[1 - cur]
        lg_scr[(h + 1) % 2] = _dot_nt(q_ref[jnp.minimum(h + 1, ATTN_HEADS - 1)], k_ref[...])
        lo_u = step(1, lo_u)
        p = jnp.exp2(lg - jnp.max(lg, axis=1, keepdims=True)).astype(BF16)
        lo_u = step(2, lo_u)
        pv = _dot(p, v_ref[...])
        for i in range(3, steps_per_head):
            lo_u = step(i, lo_u)
        o_scr[h] = (pv[:, :ATTN_HEAD_DIM] / pv[:, ATTN_HEAD_DIM:ATTN_HEAD_DIM + 1]).astype(BF16)
        return lo_u

    lo_u = lax.fori_loop(0, ATTN_HEADS, head, jnp.zeros((tq, 1), jnp.int32))
    keep = _finish_topk(key_scr, jstar_scr, kpos, lo_u, topk, 1)
    bias_scr[cur] = jnp.where(adm & keep, 0.0, NEG_INF)

    attn = _dot(jnp.concatenate([o_scr[h] for h in range(ATTN_HEADS)], axis=1), wo_ref[...])
    o_ref[...] = x_ref[...] + jnp.where(s > 0, attn, 0.0)


def _attn_pipe_layer(x, q, qi, kis, ki_all, k_all, v_all, wo_bf, q_off, n_keys, topk, tq, j0, L):
    B, T, _ = x.shape
    QI_W = IDX_HEADS * IDX_DIM
    prev = lambda s: jnp.maximum(s - 1, 0)
    this = lambda s: jnp.minimum(s, B - 1)
    return pl.pallas_call(
        functools.partial(_attn_pipe_kernel, tq=tq, L=L, q_off=q_off + j0 * tq, n_keys=n_keys, topk=topk),
        grid=(B + 1,),
        in_specs=[
            pl.BlockSpec((None, tq, D_MODEL), lambda s: (prev(s), j0, 0)),
            pl.BlockSpec((None, ATTN_HEADS, tq, ATTN_HEAD_DIM), lambda s: (prev(s), 0, j0, 0)),
            pl.BlockSpec((None, tq, QI_W), lambda s: (this(s), j0, 0)),
            pl.BlockSpec((None, tq, LANES), lambda s: (this(s), j0, 0)),
            pl.BlockSpec((None, L, IDX_DIM), lambda s: (this(s), 0, 0)),
            pl.BlockSpec((None, L, ATTN_HEAD_DIM), lambda s: (prev(s), 0, 0)),
            pl.BlockSpec((None, L, 2 * ATTN_HEAD_DIM), lambda s: (prev(s), 0, 0)),
            _const_spec((Q_W, D_MODEL)),
        ],
        out_specs=pl.BlockSpec((None, tq, D_MODEL), lambda s: (prev(s), j0, 0)),
        out_shape=SDS((B, T, D_MODEL), F32),
        input_output_aliases={0: 0},
        scratch_shapes=[
            pltpu.VMEM((tq, L), jnp.int32),
            pltpu.VMEM((2, tq, L), F32),
            pltpu.VMEM((2, tq, L), F32),
            pltpu.VMEM((tq, 1), jnp.int32),
            pltpu.VMEM((ATTN_HEADS, tq, ATTN_HEAD_DIM), BF16),
        ],
        compiler_params=_cparams(("arbitrary",)),
        name="attn_pipe",
    )(x, q, qi, kis, ki_all, k_all, v_all, wo_bf)


def _rproj_kernel(x_ref, xp_ref, sh_ref, g_ref, mix_ref, w0_ref, w1_ref, w2_ref, a0_ref, a1_ref, a2_ref,
                  g1_ref, g2_ref, kk_ref, ka_ref, wr_ref, wk_ref, wv_ref, bd_ref,
                  r_o, lw_o, k_o, v_o, na_o, b_o, g_o, hl_o, *, tt):
    i = pl.program_id(1)
    g = g_ref[...]
    h = _rms(x_ref[...], g)
    hl_o[...] = h[tt - SUBLANES:, :]
    prev = jnp.where(i == 0, sh_ref[...], _rms(xp_ref[SUBLANES - 1:SUBLANES, :], g))
    row = lax.broadcasted_iota(jnp.int32, (tt, D_MODEL), 0)
    xx = jnp.where(row == 0, prev, pltpu.roll(h, 1, 0)) - h
    lerp = lambda n: (h + xx * mix_ref[n:n + 1, :]).astype(BF16)
    r = _dot(lerp(0), wr_ref[...])
    wl = w0_ref[...] + _dot(jnp.tanh(_dot(lerp(1), w1_ref[...])).astype(BF16), w2_ref[...])
    lw = -_sigmoid(wl) * math.exp(-0.5)
    k = _dot(lerp(2), wk_ref[...])
    v = _dot(lerp(3), wv_ref[...])
    a = _sigmoid(a0_ref[...] + _dot(_dot(lerp(4), a1_ref[...]).astype(BF16), a2_ref[...]))
    gate = _dot(_sigmoid(_dot(lerp(5), g1_ref[...])).astype(BF16), g2_ref[...])
    kk = k * kk_ref[...]
    kk = kk * lax.rsqrt(jnp.maximum(_head_sum(kk * kk, bd_ref), 1e-24))
    r_o[...] = r
    lw_o[...] = lw
    k_o[...] = k * (1.0 + (a - 1.0) * ka_ref[...])
    v_o[...] = v
    na_o[...] = -kk
    b_o[...] = kk * a
    g_o[...] = gate


def _rwkv_project(x, shift_prev, g, rw):
    B, T, _ = x.shape
    tt = _pick_tile(T, 256)
    rb = tt // SUBLANES
    tok = pl.BlockSpec((None, tt, D_MODEL), lambda b, i: (b, i, 0))
    consts = [g, rw["mix"], rw["w0"], rw["w1"], rw["w2"], rw["a0"], rw["a1"], rw["a2"], rw["g1"], rw["g2"],
              rw["k_k"], rw["k_a"], rw["w_r"], rw["w_k"], rw["w_v"], rw["bd"]]
    return pl.pallas_call(
        functools.partial(_rproj_kernel, tt=tt),
        grid=(B, T // tt),
        in_specs=[
            tok,
            pl.BlockSpec((None, SUBLANES, D_MODEL), lambda b, i: (b, jnp.maximum(i * rb - 1, 0), 0)),
            pl.BlockSpec((None, 1, D_MODEL), lambda b, i: (b, 0, 0)),
        ] + [_const_spec(c.shape) for c in consts],
        out_specs=[tok] * 7 + [pl.BlockSpec((None, SUBLANES, D_MODEL), lambda b, i: (b, 0, 0))],
        out_shape=[SDS((B, T, D_MODEL), F32)] * 7 + [SDS((B, SUBLANES, D_MODEL), F32)],
        compiler_params=_cparams(("parallel", "arbitrary")),
        name="rwkv_proj",
    )(x, x, shift_prev, *consts)


def _scan_kernel(r_ref, lw_ref, k_ref, v_ref, a_ref, b_ref, g_ref, s0_ref, rk_ref, lg_ref, lb_ref,
                 y_ref, st_ref, s_scr, *, nb, tt, C):
    N = RWKV_HEAD
    assert C == N and 2 * N == LANES

    hp = LANES // N
    zero = jnp.zeros((N, N), F32)

    @pl.when(pl.program_id(1) == 0)
    def _():
        for s in range(nb):
            for p in range(D_MODEL // LANES):
                blocks = [jnp.concatenate([s0_ref[s, hp * p + h] if g == h else zero for g in range(hp)], axis=1)
                          for h in range(hp)]
                s_scr[s, p] = jnp.concatenate(blocks, axis=0).T

    row_w = lax.broadcasted_iota(jnp.int32, (C, D_MODEL), 0)
    lane = lax.broadcasted_iota(jnp.int32, (C, LANES), 1)
    h0 = lane < N
    r1 = lax.broadcasted_iota(jnp.int32, (C, 2 * C), 0)
    c1 = lax.broadcasted_iota(jnp.int32, (C, 2 * C), 1) % C
    strict = r1 > c1
    r2 = lax.broadcasted_iota(jnp.int32, (C, 4 * C), 0)
    c2 = lax.broadcasted_iota(jnp.int32, (C, 4 * C), 1) % C
    incl = r2 >= c2
    rs = lax.broadcasted_iota(jnp.int32, (LANES, LANES), 0)
    cs = lax.broadcasted_iota(jnp.int32, (LANES, LANES), 1)
    same_head = (rs < N) == (cs < N)
    eye = rs == cs

    def split(x):
        return jnp.concatenate([jnp.where(h0, x, 0.0), jnp.where(h0, 0.0, x)], axis=0)

    def chunk(c, carry):
        sl = pl.ds(pl.multiple_of(c * C, C), C)
        n_pairs = D_MODEL // LANES
        at, rt, bt, kt, vv, wc = [], [], [], [], [], []
        for s in range(nb):
            lw = lw_ref[s, sl, :]
            cum = lw
            d = 1
            while d < C:
                cum = cum + jnp.where(row_w >= d, pltpu.roll(cum, d, 0), 0.0)
                d *= 2
            e_w = jnp.exp(cum)
            e_n = jnp.exp(-cum)
            rows = (a_ref[s, sl, :] * jnp.exp(cum - lw), r_ref[s, sl, :] * e_w, b_ref[s, sl, :] * e_n,
                    k_ref[s, sl, :] * e_n, v_ref[s, sl, :], e_w[C - 1:C, :])
            for dst, src in zip((at, rt, bt, kt, vv, wc), rows):
                dst.extend(src[:, p * LANES:(p + 1) * LANES] for p in range(n_pairs))
        pairs = range(nb * n_pairs)
        st = [s_scr[p // n_pairs, p % n_pairs] for p in pairs]
        ar = [jnp.concatenate([at[p], rt[p]], axis=0).astype(BF16) for p in pairs]
        v2 = [split(vv[p]).astype(BF16) for p in pairs]
        g = [_dot_nt(ar[p], jnp.concatenate([split(bt[p]), split(kt[p])], axis=0).astype(BF16)) for p in pairs]
        hm = [_dot(ar[p], st[p].astype(BF16)) for p in pairs]
        pw = [jnp.where(strict, g[p][:C, :2 * C], 0.0) for p in pairs]
        u = [hm[p][:C] + _dot(jnp.where(strict, g[p][:C, 2 * C:], 0.0).astype(BF16), v2[p]) for p in pairs]
        n = 1
        while n < C:
            pb = [pw[p].astype(BF16) for p in pairs]
            u = [u[p] + _dot(pb[p], split(u[p]).astype(BF16)) for p in pairs]
            n *= 2
            if n < C:
                pw = [_dot(pb[p], split(pw[p]).astype(BF16)) for p in pairs]
        ys = [hm[p][C:] + _dot(jnp.where(incl, g[p][C:, :], 0.0).astype(BF16),
                               jnp.concatenate([split(u[p]).astype(BF16), v2[p]], axis=0)) for p in pairs]
        for p in pairs:
            bk = jnp.concatenate([bt[p], kt[p]], axis=0).astype(BF16)
            uvp = jnp.concatenate([u[p], vv[p]], axis=0).astype(BF16)
            upd = jnp.where(same_head, _dot_tn(bk, uvp), 0.0)
            w_col = jnp.sum(jnp.where(eye, wc[p], 0.0), axis=1, keepdims=True)
            s_scr[p // n_pairs, p % n_pairs] = (st[p] + upd) * w_col
        inv_n = 1.0 / N

        def head_mean(z):
            s0 = jnp.sum(jnp.where(h0, z, 0.0), axis=1, keepdims=True)
            s1 = jnp.sum(jnp.where(h0, 0.0, z), axis=1, keepdims=True)
            return jnp.where(h0, s0, s1) * inv_n

        for s in range(nb):
            zs = []
            for q in range(n_pairs):
                p = s * n_pairs + q
                ps = slice(q * LANES, (q + 1) * LANES)
                yc = ys[p] - head_mean(ys[p])
                yn = yc * lax.rsqrt(head_mean(yc * yc) + LNX_EPS) * lg_ref[:, ps] + lb_ref[:, ps]
                rk = r_ref[s, sl, ps] * k_ref[s, sl, ps] * rk_ref[:, ps]
                zs.append((yn + head_mean(rk) * N * vv[p]) * g_ref[s, sl, ps])
            y_ref[s, sl, :] = jnp.concatenate(zs, axis=1).astype(BF16)
        return carry

    lax.fori_loop(0, tt // C, chunk, 0)

    @pl.when(pl.program_id(1) == pl.num_programs(1) - 1)
    def _():
        for s in range(nb):
            for p in range(D_MODEL // LANES):
                t = s_scr[s, p].T
                for h in range(hp):
                    st_ref[s, hp * p + h] = t[h * N:(h + 1) * N, h * N:(h + 1) * N]


def _rwkv_scan(r, lw, k, v, na, b, gate, wkv0, rw):
    B, T, _ = r.shape
    C = SCAN_CHUNK
    t_pad = ((T + C - 1) // C) * C
    seqs = (r, lw, k, v, na, b, gate)
    if t_pad != T:
        seqs = tuple(jnp.pad(a, ((0, 0), (0, t_pad - T), (0, 0))) for a in seqs)
    tt = _pick_tile(t_pad, 128)
    nb = _pick_tile(B, SCAN_SEQS)
    n_pairs = D_MODEL // LANES
    tok = pl.BlockSpec((nb, tt, D_MODEL), lambda b_, i: (b_, i, 0))
    st = pl.BlockSpec((nb, RWKV_HEADS, RWKV_HEAD, RWKV_HEAD), lambda b_, i: (b_, 0, 0, 0))
    consts = [rw["r_k"], rw["lnx_g"], rw["lnx_b"]]
    y, s_t = pl.pallas_call(
        functools.partial(_scan_kernel, nb=nb, tt=tt, C=C),
        grid=(B // nb, t_pad // tt),
        in_specs=[tok] * 7 + [st] + [_const_spec(c.shape) for c in consts],
        out_specs=[tok, st],
        out_shape=[SDS((B, t_pad, D_MODEL), BF16), SDS((B, RWKV_HEADS, RWKV_HEAD, RWKV_HEAD), F32)],
        scratch_shapes=[pltpu.VMEM((nb, n_pairs, LANES, LANES), F32)],
        compiler_params=_cparams(("parallel", "arbitrary")),
        name="rwkv_scan",
    )(*seqs, wkv0, *consts)
    return (y if t_pad == T else y[:, :T]), s_t


def _wo_mlp_kernel(x_ref, z_ref, wo_ref, g_ref, wu_ref, wd_ref, gf_ref, o_ref, *, final_norm):
    x = x_ref[...] + _dot(z_ref[...], wo_ref[...])
    o_ref[...] = _mlp_apply(x, g_ref, wu_ref, wd_ref, gf_ref, final_norm)


def _wo_mlp(x, z, wo, mlp_args, final_norm):
    M = x.shape[0]
    tm = _pick_tile(M, 512)
    tok = pl.BlockSpec((tm, D_MODEL), lambda i: (i, 0))
    return pl.pallas_call(
        functools.partial(_wo_mlp_kernel, final_norm=final_norm),
        grid=(M // tm,),
        in_specs=[tok, tok, _const_spec(wo.shape)] + _mlp_specs(),
        out_specs=tok,
        out_shape=SDS((M, D_MODEL), F32),
        compiler_params=_cparams(("parallel",)),
        name="wo_mlp",
    )(x, z, wo, *mlp_args)


def _pool_block(x, state, n_hist, g, w_bf, scale, mlp_args=(), final_norm=False):
    B = x.shape[0]
    if state is None:
        hist = jnp.zeros((B, POOL_PAD, D_MODEL), F32)
    else:
        hist = jnp.pad(state, ((0, 0), (POOL_PAD - POOL_HIST, 0), (0, 0)))
    out, hs = _pool_layer(x, hist, n_hist, g, w_bf, scale, mlp_args, final_norm)
    return out, hs[:, POOL_PAD - POOL_HIST:]


def _attn_block(x, cache, g, w_in_pad, wo_bf, q_off, tq):
    B, T, _ = x.shape
    tab = _rope_tables(q_off + jnp.arange(T))
    q, qi, kis, k_new, v_new, ki_new, kb, vb, kib = _attn_project(x, g, w_in_pad, tab)
    keys = (kib, kb, vb)
    if cache is not None:
        ck, cv, cki = cache
        ones_col = jnp.zeros(cv.shape, BF16).at[:, :, 0].set(1.0)
        past = (cki.astype(BF16), ck.astype(BF16), jnp.concatenate([cv.astype(BF16), ones_col], axis=2))
        keys = tuple(jnp.concatenate([c, n], axis=1) for c, n in zip(past, keys))
    n_keys = keys[0].shape[1]
    L = ((n_keys + LANES - 1) // LANES) * LANES
    if L != n_keys:
        keys = tuple(jnp.pad(a, ((0, 0), (0, L - n_keys), (0, 0))) for a in keys)
    topk = min(TOPK_MAX, n_keys // 4)
    nq = T // tq
    if cache is None and nq % ATTN_BUCKET_QBLOCKS == 0:
        groups = [(j0, ATTN_BUCKET_QBLOCKS, (j0 + ATTN_BUCKET_QBLOCKS) * tq)
                  for j0 in range(0, nq, ATTN_BUCKET_QBLOCKS)]
    else:
        groups = [(0, nq, L)]
    out = x
    for j0, n, l_g in groups:
        if cache is None and n == 1:
            out = _attn_pipe_layer(out, q, qi, kis, keys[0], keys[1], keys[2], wo_bf, q_off, min(n_keys, l_g),
                                   topk, tq, j0, l_g)
        else:
            out = _attn_layer(out, q, qi, kis, keys[0], keys[1], keys[2], wo_bf, q_off, min(n_keys, l_g), topk,
                              tq, j0, n, l_g)
    return out, k_new, v_new, ki_new


def _rwkv_block(x, shift_prev, wkv0, g, rw, mlp_args, final_norm):
    B, T, _ = x.shape
    r, lw, k, v, na, b, gate, hl = _rwkv_project(x, shift_prev, g, rw)
    z, s_t = _rwkv_scan(r, lw, k, v, na, b, gate, wkv0, rw)
    flat = lambda a: a.reshape(B * T, D_MODEL)
    out = _wo_mlp(flat(x), flat(z), rw["w_o"], mlp_args, final_norm).reshape(B, T, D_MODEL)
    return out, hl[:, SUBLANES - 1:], s_t


def kernel(x_prompt, x_sample, state_pool, cache_k, cache_v, cache_kidx, state_shift, state_wkv, ln1_g, ln2_g, w_up, w_down, ln_f_g, pool_w, pool_scale, attn_w_in, attn_w_out, rwkv_mix, rwkv_w0, rwkv_w1, rwkv_w2, rwkv_a0, rwkv_a1, rwkv_a2, rwkv_g1, rwkv_g2, rwkv_k_k, rwkv_k_a, rwkv_r_k, rwkv_w_r, rwkv_w_k, rwkv_w_v, rwkv_w_o, rwkv_lnx_g, rwkv_lnx_b):
    xp, xs = x_prompt, x_sample
    bp, sp, _ = xp.shape
    bs, ss, _ = xs.shape
    past = cache_k.shape[2]
    row = lambda a: a.reshape(1, -1)
    bf = lambda a: a.astype(BF16)
    head_of = jnp.arange(MXU_DIM) // RWKV_HEAD
    bd_mat = (head_of[:, None] == head_of[None, :]).astype(BF16)
    outs = {n: [] for n in ("pool_p", "pool_s", "k_p", "k_s", "v_p", "v_s", "ki_p", "ki_s",
                            "sh_p", "sh_s", "wkv_p", "wkv_s")}
    for i in range(DEPTH):
        j = i // N_MIXERS
        g1 = row(ln1_g[i])
        last = i == DEPTH - 1
        mlp_args = (row(ln2_g[i]), bf(w_up[i]), bf(w_down[i]), row(ln_f_g))
        prompt_mlp_done = sample_mlp_done = False
        if i % N_MIXERS == 0:
            w_bf = bf(pool_w[j])
            sc = row(pool_scale[j])
            prompt_mlp_done = sp >= POOL_MLP_MIN_ROWS
            xp, st_p = _pool_block(xp, None, 0, g1, w_bf, sc, mlp_args if prompt_mlp_done else (), last)
            xs, st_s = _pool_block(xs, state_pool[j], past, g1, w_bf, sc)
            outs["pool_p"].append(st_p)
            outs["pool_s"].append(st_s)
        elif i % N_MIXERS == 1:
            w_in_pad = jnp.pad(bf(attn_w_in[j]), ((0, 0), (0, ATTN_IN_PAD - ATTN_IN_W)))
            wo_bf = bf(attn_w_out[j])
            xp, kp, vp, kip = _attn_block(xp, None, g1, w_in_pad, wo_bf, 0, 256)
            xs, kn, vn, kin = _attn_block(xs, (cache_k[j], cache_v[j], cache_kidx[j]), g1, w_in_pad, wo_bf,
                                          past, ss)
            for n, a in (("k_p", kp), ("v_p", vp), ("ki_p", kip), ("k_s", kn), ("v_s", vn), ("ki_s", kin)):
                outs[n].append(a)
        else:
            rw = dict(mix=rwkv_mix[j], w0=row(rwkv_w0[j]), w1=bf(rwkv_w1[j]), w2=bf(rwkv_w2[j]),
                      a0=row(rwkv_a0[j]), a1=bf(rwkv_a1[j]), a2=bf(rwkv_a2[j]), g1=bf(rwkv_g1[j]),
                      g2=bf(rwkv_g2[j]), k_k=row(rwkv_k_k[j]), k_a=row(rwkv_k_a[j]), r_k=row(rwkv_r_k[j]),
                      w_r=bf(rwkv_w_r[j]), w_k=bf(rwkv_w_k[j]), w_v=bf(rwkv_w_v[j]), w_o=bf(rwkv_w_o[j]),
                      lnx_g=row(rwkv_lnx_g[j]), lnx_b=row(rwkv_lnx_b[j]), bd=bd_mat)
            zero_shift = jnp.zeros((bp, 1, D_MODEL), F32)
            zero_wkv = jnp.zeros((bp, RWKV_HEADS, RWKV_HEAD, RWKV_HEAD), F32)
            xp, shp, wp = _rwkv_block(xp, zero_shift, zero_wkv, g1, rw, mlp_args, last)
            xs, shs, wsn = _rwkv_block(xs, state_shift[j], state_wkv[j], g1, rw, mlp_args, last)
            prompt_mlp_done = sample_mlp_done = True
            outs["sh_p"].append(shp)
            outs["sh_s"].append(shs)
            outs["wkv_p"].append(wp)
            outs["wkv_s"].append(wsn)
        if not prompt_mlp_done:
            xp = _mlp(xp.reshape(bp * sp, D_MODEL), *mlp_args, last).reshape(bp, sp, D_MODEL)
        if not sample_mlp_done:
            xs = _mlp(xs.reshape(bs * ss, D_MODEL), *mlp_args, last).reshape(bs, ss, D_MODEL)
    st = lambda n: jnp.stack(outs[n], 0)
    return (xp, xs, st("pool_p"), st("pool_s"), st("k_p"), st("k_s"), st("v_p"), st("v_s"),
            st("ki_p"), st("ki_s"), st("sh_p"), st("sh_s"), st("wkv_p"), st("wkv_s"))
```

```python
import functools
import math

import jax
import jax.numpy as jnp
import numpy as np
from jax import lax
from jax.experimental import pallas as pl
from jax.experimental.pallas import tpu as pltpu

F32 = jnp.float32
BF16 = jnp.bfloat16
SDS = jax.ShapeDtypeStruct

D_MODEL = 1024
DEPTH = 4
N_MIXERS = 3
CHUNK = 64
D_FF = 4 * D_MODEL
RMS_EPS = 1e-6
POOL_WINDOWS = (2, 4, 8, 16)
POOL_GROUPS = 4
POOL_GW = D_MODEL // POOL_GROUPS
POOL_HIST = max(POOL_WINDOWS) - 1
POOL_PAD = POOL_HIST + 1
ATTN_HEADS = 8
ATTN_HEAD_DIM = D_MODEL // ATTN_HEADS
IDX_HEADS = 8
IDX_DIM = 64
TOPK_MAX = 256
ROPE_THETA = 500000.0
ROPE_FRACTION = 4
NEG_INF = -1e30
Q_W = ATTN_HEADS * ATTN_HEAD_DIM
OFF_K = Q_W
OFF_V = OFF_K + ATTN_HEAD_DIM
OFF_QI = OFF_V + ATTN_HEAD_DIM
OFF_KI = OFF_QI + IDX_HEADS * IDX_DIM
OFF_WI = OFF_KI + IDX_DIM
ATTN_IN_W = OFF_WI + IDX_HEADS
RWKV_HEAD = 64
RWKV_HEADS = D_MODEL // RWKV_HEAD
LNX_EPS = 64e-5

LANES = 128
SUBLANES = 8
ATTN_IN_PAD = ((ATTN_IN_W + LANES - 1) // LANES) * LANES
MXU_DIM = 256
VMEM_LIMIT = 56 * 1024 * 1024
INT_MIN = -2 ** 31
HALF_NEG_KEY = int(np.float32(0.5 * NEG_INF).view(np.int32)) ^ 0x7FFFFFFF
POOL_MLP_MIN_ROWS = 512
POOL_MLP_SUBTILES = 2
ATTN_Q_BLOCK = 256
SCAN_CHUNK = 64
SCAN_SEQS = 2
FF_CHUNK = 1024


def _cparams(sem):
    return pltpu.CompilerParams(dimension_semantics=sem, vmem_limit_bytes=VMEM_LIMIT)


def _const_spec(shape):
    nd = len(shape)
    return pl.BlockSpec(shape, lambda *_: (0,) * nd, pipeline_mode=pl.Buffered(1))


def _rms(x, g):
    ms = jnp.mean(x * x, axis=-1, keepdims=True)
    return x * lax.rsqrt(ms + RMS_EPS) * g


def _dot(a, b):
    return jnp.dot(a, b, preferred_element_type=F32)


def _dot_nt(a, b):
    return lax.dot_general(a, b, (((1,), (1,)), ((), ())), preferred_element_type=F32)


def _dot_tn(a, b):
    return lax.dot_general(a, b, (((0,), (0,)), ((), ())), preferred_element_type=F32)


def _head_sum(z, bd_ref):
    bd = bd_ref[...]
    hi = z.astype(BF16)
    lo = (z - hi.astype(F32)).astype(BF16)
    outs = []
    for c in range(D_MODEL // MXU_DIM):
        cs = slice(c * MXU_DIM, (c + 1) * MXU_DIM)
        outs.append(_dot(hi[:, cs], bd) + _dot(lo[:, cs], bd))
    return jnp.concatenate(outs, axis=1)


def _sigmoid(x):
    return 1.0 / (1.0 + jnp.exp(-x))


def _pick_tile(n, pref):
    t = min(n, pref)
    assert n % t == 0, (n, t)
    return t


def _mlp_apply(x, g_ref, wu_ref, wd_ref, gf_ref, final_norm, side=()):
    h = _rms(x, g_ref[...]).astype(BF16)
    acc = x
    for j in range(D_FF // FF_CHUNK):
        u = _dot(h, wu_ref[:, j * FF_CHUNK:(j + 1) * FF_CHUNK])
        u = jnp.square(jnp.maximum(u, 0.0)).astype(BF16)
        acc = acc + _dot(u, wd_ref[j * FF_CHUNK:(j + 1) * FF_CHUNK, :])
        if j < len(side):
            side[j]()
    if final_norm:
        acc = _rms(acc, gf_ref[...])
    return acc


def _mlp_specs(layer):
    pick = lambda *_: (layer, 0, 0)
    return [_const_spec((1, D_MODEL)),
            pl.BlockSpec((None, D_MODEL, D_FF), pick, pipeline_mode=pl.Buffered(1)),
            pl.BlockSpec((None, D_FF, D_MODEL), pick, pipeline_mode=pl.Buffered(1)),
            _const_spec((1, D_MODEL))]


def _mlp_kernel(x_ref, g_ref, wu_ref, wd_ref, gf_ref, o_ref, *, final_norm):
    o_ref[...] = _mlp_apply(x_ref[...], g_ref, wu_ref, wd_ref, gf_ref, final_norm)


def _mlp(x, mlp_args, final_norm):
    M = x.shape[0]
    tm = _pick_tile(M, 512)
    return pl.pallas_call(
        functools.partial(_mlp_kernel, final_norm=final_norm),
        grid=(M // tm,),
        in_specs=[pl.BlockSpec((tm, D_MODEL), lambda i: (i, 0))] + _mlp_specs(mlp_args[4]),
        out_specs=pl.BlockSpec((tm, D_MODEL), lambda i: (i, 0)),
        out_shape=SDS((M, D_MODEL), F32),
        compiler_params=_cparams(("parallel",)),
        name="mlp",
    )(x, *mlp_args[:4])


def _pool_kernel(x_ref, xp_ref, hist_ref, g_ref, w_ref, sc_ref, *rest, tt, n_hist, mlp, final_norm):
    mlp_refs, (o_ref, hs_ref) = rest[:-2], rest[-2:]
    i = pl.program_id(1)
    g = g_ref[...]
    x = x_ref[...]
    h = _rms(x, g)
    prev = jnp.where(i == 0, hist_ref[...], _rms(xp_ref[...], g))
    hs_ref[...] = h[tt - POOL_PAD:, :]
    full = jnp.concatenate([prev, h], axis=0)
    nsub = POOL_MLP_SUBTILES if mlp else 1
    rows = tt // nsub

    def group(r, gi):
        win = POOL_WINDOWS[gi]
        rs = slice(r * rows, (r + 1) * rows)
        cs = slice(gi * POOL_GW, (gi + 1) * POOL_GW)
        s = full[r * rows:(r + 1) * rows + POOL_PAD, cs]
        d = 1
        while d < win:
            s = s + pltpu.roll(s, d, 0)
            d *= 2
        t1 = lax.broadcasted_iota(jnp.int32, (rows, POOL_GW), 0) + (i * tt + r * rows + 1 + n_hist)
        pooled = s[POOL_PAD:, :] / jnp.minimum(t1, win).astype(F32) - h[rs, cs]
        return x[rs, cs] + _dot(pooled.astype(BF16), w_ref[gi]) * sc_ref[:, cs]

    cols = [group(0, gi) for gi in range(POOL_GROUPS)]
    if not mlp:
        o_ref[...] = jnp.concatenate(cols, axis=1)
        return
    outs = []
    for r in range(nsub):
        y = jnp.concatenate(cols, axis=1)
        cols = []
        side = [functools.partial(lambda gi, rn: cols.append(group(rn, gi)), gi, r + 1)
                for gi in range(POOL_GROUPS)] if r + 1 < nsub else []
        outs.append(_mlp_apply(y, *mlp_refs, final_norm, side))
    o_ref[...] = jnp.concatenate(outs, axis=0)


def _pool_layer(x, hist, n_hist, g, w_bf, scale, mlp_args, final_norm):
    B, T, _ = x.shape
    tt = _pick_tile(T, 512)
    rb = tt // POOL_PAD
    return pl.pallas_call(
        functools.partial(_pool_kernel, tt=tt, n_hist=n_hist, mlp=bool(mlp_args), final_norm=final_norm),
        grid=(B, T // tt),
        in_specs=[
            pl.BlockSpec((None, tt, D_MODEL), lambda b, i: (b, i, 0)),
            pl.BlockSpec((None, POOL_PAD, D_MODEL), lambda b, i: (b, jnp.maximum(i * rb - 1, 0), 0)),
            pl.BlockSpec((None, POOL_PAD, D_MODEL), lambda b, i: (b, 0, 0)),
            _const_spec((1, D_MODEL)),
            _const_spec((POOL_GROUPS, POOL_GW, POOL_GW)),
            _const_spec((1, D_MODEL)),
        ] + (_mlp_specs(mlp_args[4]) if mlp_args else []),
        out_specs=[
            pl.BlockSpec((None, tt, D_MODEL), lambda b, i: (b, i, 0)),
            pl.BlockSpec((None, POOL_PAD, D_MODEL), lambda b, i: (b, 0, 0)),
        ],
        out_shape=[SDS((B, T, D_MODEL), F32), SDS((B, POOL_PAD, D_MODEL), F32)],
        compiler_params=_cparams(("parallel", "arbitrary")),
        name="pool_mlp" if mlp_args else "pool",
    )(x, x, hist, g, w_bf, scale, *mlp_args[:4])


def _rope_tables(pos0, T):
    pos = np.arange(pos0, pos0 + T, dtype=np.float64)

    def head(d):
        rd = d // ROPE_FRACTION
        half = rd // 2
        inv = ROPE_THETA ** (-np.arange(half, dtype=np.float64) / half)
        ang = pos[:, None] * inv[None, :]
        cos, sin = np.cos(ang), np.sin(ang)
        z = lambda n: np.zeros((T, n))
        c = np.concatenate([cos, cos, np.ones((T, d - rd))], axis=1)
        sa = np.concatenate([-sin, z(d - half)], axis=1)
        sb = np.concatenate([z(half), sin, z(d - rd)], axis=1)
        return c, sa, sb

    qa = head(ATTN_HEAD_DIM)
    ia = head(IDX_DIM)
    i2 = tuple(np.concatenate([t, t], axis=1) for t in ia)
    ones, zeros = np.ones((T, IDX_DIM)), np.zeros((T, IDX_DIM))
    ik = (np.concatenate([ia[0], ones], axis=1), np.concatenate([ia[1], zeros], axis=1),
          np.concatenate([ia[2], zeros], axis=1))
    return jnp.asarray(np.concatenate(list(qa) + list(i2) + list(ik), axis=1), F32)


def _aproj_kernel(x_ref, g_ref, w_ref, tab_ref, q_ref, qi_ref, kis_ref, k_ref, v_ref, ki_ref, kb_ref, vb_ref, kib_ref):
    h = _rms(x_ref[...], g_ref[...]).astype(BF16)
    p = _dot(h, w_ref[...])

    def rope(x, kind, half):
        c = tab_ref[:, (3 * kind) * LANES:(3 * kind + 1) * LANES]
        sa = tab_ref[:, (3 * kind + 1) * LANES:(3 * kind + 2) * LANES]
        sb = tab_ref[:, (3 * kind + 2) * LANES:(3 * kind + 3) * LANES]
        return x * c + pltpu.roll(x, LANES - half, 1) * sa + pltpu.roll(x, half, 1) * sb

    slab = lambda off: p[:, off:off + LANES]
    qh = ATTN_HEAD_DIM // ROPE_FRACTION // 2
    ih = IDX_DIM // ROPE_FRACTION // 2
    qscale = ATTN_HEAD_DIM ** -0.5 * math.log2(math.e)
    for s in range(ATTN_HEADS):
        q_ref[s] = (rope(slab(s * LANES), 0, qh) * qscale).astype(BF16)
    k = rope(slab(OFF_K), 0, qh)
    v = slab(OFF_V)
    for s in range(IDX_HEADS * IDX_DIM // LANES):
        qi_ref[:, s * LANES:(s + 1) * LANES] = rope(slab(OFF_QI + s * LANES), 1, ih).astype(BF16)
    kis = rope(slab(OFF_KI), 2, ih)
    kis_ref[...] = kis
    k_ref[...] = k
    v_ref[...] = v
    ki_ref[...] = kis[:, :IDX_DIM]
    kb_ref[...] = k.astype(BF16)
    lane = lax.broadcasted_iota(jnp.int32, v.shape, 1)
    vb_ref[:, :ATTN_HEAD_DIM] = v.astype(BF16)
    vb_ref[:, ATTN_HEAD_DIM:] = jnp.where(lane == 0, 1.0, 0.0).astype(BF16)
    kib_ref[...] = kis[:, :IDX_DIM].astype(BF16)


def _attn_project(x, g, w_in_pad, tab):
    B, T, _ = x.shape
    tt = _pick_tile(T, 512)
    QI_W = IDX_HEADS * IDX_DIM
    widths = (QI_W, LANES, ATTN_HEAD_DIM, ATTN_HEAD_DIM, IDX_DIM, ATTN_HEAD_DIM, 2 * ATTN_HEAD_DIM, IDX_DIM)
    dtypes = (BF16, F32, F32, F32, F32, BF16, BF16, BF16)
    q_spec = pl.BlockSpec((None, ATTN_HEADS, tt, ATTN_HEAD_DIM), lambda i, b: (b, 0, i, 0))
    return pl.pallas_call(
        _aproj_kernel,
        grid=(T // tt, B),
        in_specs=[
            pl.BlockSpec((None, tt, D_MODEL), lambda i, b: (b, i, 0)),
            _const_spec((1, D_MODEL)),
            _const_spec((D_MODEL, ATTN_IN_PAD)),
            pl.BlockSpec((tt, 9 * LANES), lambda i, b: (i, 0)),
        ],
        out_specs=[q_spec] + [pl.BlockSpec((None, tt, w), lambda i, b: (b, i, 0)) for w in widths],
        out_shape=[SDS((B, ATTN_HEADS, T, ATTN_HEAD_DIM), BF16)] + [SDS((B, T, w), d) for w, d in zip(widths, dtypes)],
        compiler_params=_cparams(("arbitrary", "arbitrary")),
        name="attn_proj",
    )(x, g, w_in_pad, tab)


def _score_key(score):
    kb = pltpu.bitcast(score, jnp.int32)
    return jnp.where(kb >= 0, kb, kb ^ jnp.int32(0x7FFFFFFF))


KEY_BITS = 32


def _count(mask, axis):
    return jnp.sum(jnp.where(mask, 1.0, 0.0), axis=axis, keepdims=True)


def _kth_key_step(key_scr, lo_u, it, topk, axis):
    trial_u = lo_u | lax.shift_left(jnp.int32(1), KEY_BITS - 1 - it)
    c = _count(key_scr[...] >= (trial_u ^ jnp.int32(INT_MIN)), axis)
    return jnp.where(c >= topk, trial_u, lo_u)


def _select_topk(key_scr, jstar_scr, kpos, topk, axis):
    step = lambda it, lo_u: _kth_key_step(key_scr, lo_u, it, topk, axis)
    lo_u = lax.fori_loop(0, KEY_BITS, step, jnp.zeros(jstar_scr.shape, jnp.int32))
    return _finish_topk(key_scr, jstar_scr, kpos, lo_u, topk, axis)


def _finish_topk(key_scr, jstar_scr, kpos, lo_u, topk, axis):
    one = jstar_scr.shape
    n_idx = key_scr.shape[axis]
    count = functools.partial(_count, axis=axis)
    lo = lo_u ^ jnp.int32(INT_MIN)
    gt = key_scr[...] > lo
    eq = key_scr[...] == lo
    need = topk - count(gt)
    tie = (lo > jnp.int32(HALF_NEG_KEY)) & (count(eq) > need)
    jstar_scr[...] = jnp.full(one, n_idx, jnp.int32)

    @pl.when(jnp.max(jnp.where(tie, 1.0, 0.0)) > 0.0)
    def _():
        nbits = max(1, (n_idx - 1).bit_length())

        def ibody(it, lo_i):
            trial = lo_i + lax.shift_left(jnp.int32(1), nbits - 1 - it)
            c = count((kpos < trial) & (key_scr[...] == lo))
            return jnp.where(c < need, trial, lo_i)

        lo_i = lax.fori_loop(0, nbits, ibody, jnp.zeros(one, jnp.int32))
        jstar_scr[...] = jnp.where(tie, lo_i, n_idx)

    return gt | (eq & (kpos <= jstar_scr[...]))


def _attn_decode_kernel(x_ref, q_ref, qi_ref, wi_ref, kib_ref, kb_ref, vb_ref, cki_ref, ck_ref, cv_ref, wo_ref,
                        o_ref, ki_scr, k_scr, v_scr, key_scr, bias_scr, jstar_scr, o_scr, *, tq, P, L, topk):
    n_keys = P + tq

    @pl.when(pl.program_id(0) == 0)
    def _():
        lane = lax.broadcasted_iota(jnp.int32, (P, ATTN_HEAD_DIM), 1)
        v_scr[:P, ATTN_HEAD_DIM:] = jnp.where(lane == 0, 1.0, 0.0).astype(BF16)
        ki_scr[n_keys:, :] = jnp.zeros((L - n_keys, IDX_DIM), BF16)
        k_scr[n_keys:, :] = jnp.zeros((L - n_keys, ATTN_HEAD_DIM), BF16)
        v_scr[n_keys:, :] = jnp.zeros((L - n_keys, 2 * ATTN_HEAD_DIM), BF16)

    ki_scr[:P, :] = cki_ref[...].astype(BF16)
    ki_scr[P:n_keys, :] = kib_ref[...]
    k_scr[:P, :] = ck_ref[...].astype(BF16)
    k_scr[P:n_keys, :] = kb_ref[...]
    v_scr[:P, :ATTN_HEAD_DIM] = cv_ref[...].astype(BF16)
    v_scr[P:n_keys, :] = vb_ref[...]

    qpos = lax.broadcasted_iota(jnp.int32, (tq, L), 0) + P
    kpos = lax.broadcasted_iota(jnp.int32, (tq, L), 1)
    adm = (kpos < n_keys) & ((kpos // CHUNK) <= (qpos // CHUNK))

    qi = qi_ref[...]
    wis = wi_ref[...] * (IDX_DIM ** -0.5 * IDX_HEADS ** -0.5)
    d_all = _dot_nt(jnp.concatenate([qi[:, h * IDX_DIM:(h + 1) * IDX_DIM] for h in range(IDX_HEADS)], axis=0),
                    ki_scr[...])
    score = jnp.zeros((tq, L), F32)
    for h in range(IDX_HEADS):
        score = score + jnp.maximum(d_all[h * tq:(h + 1) * tq], 0.0) * wis[:, IDX_DIM + h:IDX_DIM + h + 1]
    key_scr[...] = _score_key(jnp.where(adm, score, NEG_INF))
    keep = _select_topk(key_scr, jstar_scr, kpos, topk, axis=1)
    bias_scr[...] = jnp.where(adm & keep, 0.0, NEG_INF)

    def probs(lg):
        lg = lg + bias_scr[...]
        return jnp.exp2(lg - jnp.max(lg, axis=1, keepdims=True)).astype(BF16)

    lg_all = _dot_nt(jnp.concatenate([q_ref[h] for h in range(ATTN_HEADS)], axis=0), k_scr[...])
    p_all = jnp.concatenate([probs(lg_all[h * tq:(h + 1) * tq]) for h in range(ATTN_HEADS)], axis=0)
    pv_all = _dot(p_all, v_scr[...])
    for h in range(ATTN_HEADS):
        pv = pv_all[h * tq:(h + 1) * tq]
        o_scr[:, h * ATTN_HEAD_DIM:(h + 1) * ATTN_HEAD_DIM] = (
            pv[:, :ATTN_HEAD_DIM] / pv[:, ATTN_HEAD_DIM:ATTN_HEAD_DIM + 1]).astype(BF16)
    o_ref[...] = x_ref[...] + _dot(o_scr[...], wo_ref[...])


def _attn_decode_layer(x, q, qi, kis, kib, kb, vb, cache, wo_bf, topk):
    B, T, _ = x.shape
    ck, cv, cki = cache
    P = ck.shape[1]
    L = ((P + T + LANES - 1) // LANES) * LANES
    assert T * max(IDX_HEADS, ATTN_HEADS) <= MXU_DIM, "decode block: all heads of all new tokens in one row tile"
    assert P % 16 == 0 and T % 16 == 0, "bf16 row tiles"
    QI_W = IDX_HEADS * IDX_DIM
    seq = lambda *shape: pl.BlockSpec((None,) + shape, lambda b: (b,) + (0,) * len(shape))
    return pl.pallas_call(
        functools.partial(_attn_decode_kernel, tq=T, P=P, L=L, topk=topk),
        grid=(B,),
        in_specs=[
            seq(T, D_MODEL), seq(ATTN_HEADS, T, ATTN_HEAD_DIM), seq(T, QI_W), seq(T, LANES),
            seq(T, IDX_DIM), seq(T, ATTN_HEAD_DIM), seq(T, 2 * ATTN_HEAD_DIM),
            seq(P, IDX_DIM), seq(P, ATTN_HEAD_DIM), seq(P, ATTN_HEAD_DIM),
            _const_spec((Q_W, D_MODEL)),
        ],
        out_specs=seq(T, D_MODEL),
        out_shape=SDS((B, T, D_MODEL), F32),
        scratch_shapes=[
            pltpu.VMEM((L, IDX_DIM), BF16),
            pltpu.VMEM((L, ATTN_HEAD_DIM), BF16),
            pltpu.VMEM((L, 2 * ATTN_HEAD_DIM), BF16),
            pltpu.VMEM((T, L), jnp.int32),
            pltpu.VMEM((T, L), F32),
            pltpu.VMEM((T, 1), jnp.int32),
            pltpu.VMEM((T, Q_W), BF16),
        ],
        compiler_params=_cparams(("arbitrary",)),
        name="attn_decode",
    )(x, q, qi, kis, kib, kb, vb, cki, ck, cv, wo_bf)


def _attn_pipe_kernel(x_ref, q_ref, qi_ref, wi_ref, ki_ref, k_ref, v_ref, wo_ref, o_ref,
                      key_scr, bias_scr, lg_scr, jstar_scr, o_scr, *, tq, L, q_off, n_keys, topk):
    s = pl.program_id(0)
    cur = s % 2

    @pl.when(s == 0)
    def _():
        bias_scr[...] = jnp.zeros(bias_scr.shape, F32)

    qpos = lax.broadcasted_iota(jnp.int32, (tq, L), 0) + q_off
    kpos = lax.broadcasted_iota(jnp.int32, (tq, L), 1)
    adm = (kpos < n_keys) & ((kpos // CHUNK) <= (qpos // CHUNK))

    qi = qi_ref[...]
    ki = ki_ref[...]
    wis = wi_ref[...] * (IDX_DIM ** -0.5 * IDX_HEADS ** -0.5)
    score = jnp.zeros((tq, L), F32)
    for h in range(IDX_HEADS):
        d = _dot_nt(qi[:, h * IDX_DIM:(h + 1) * IDX_DIM], ki)
        score = score + jnp.maximum(d, 0.0) * wis[:, IDX_DIM + h:IDX_DIM + h + 1]
    key_scr[...] = _score_key(jnp.where(adm, score, NEG_INF))

    steps_per_head = KEY_BITS // ATTN_HEADS

    lg_scr[0] = _dot_nt(q_ref[0], k_ref[...])

    def head(h, lo_u):
        step = lambda i, lo: _kth_key_step(key_scr, lo, h * steps_per_head + i, topk, 1)
        lo_u = step(0, lo_u)
        lg = lg_scr[h % 2] + bias_scr[1 - cur]
        lg_scr[(h + 1) % 2] = _dot_nt(q_ref[jnp.minimum(h + 1, ATTN_HEADS - 1)], k_ref[...])
        lo_u = step(1, lo_u)
        p = jnp.exp2(lg - jnp.max(lg, axis=1, keepdims=True)).astype(BF16)
        lo_u = step(2, lo_u)
        pv = _dot(p, v_ref[...])
        for i in range(3, steps_per_head):
            lo_u = step(i, lo_u)
        o_scr[h] = (pv[:, :ATTN_HEAD_DIM] / pv[:, ATTN_HEAD_DIM:ATTN_HEAD_DIM + 1]).astype(BF16)
        return lo_u

    lo_u = lax.fori_loop(0, ATTN_HEADS, head, jnp.zeros((tq, 1), jnp.int32))
    keep = _finish_topk(key_scr, jstar_scr, kpos, lo_u, topk, 1)
    bias_scr[cur] = jnp.where(adm & keep, 0.0, NEG_INF)

    attn = _dot(jnp.concatenate([o_scr[h] for h in range(ATTN_HEADS)], axis=1), wo_ref[...])
    o_ref[...] = x_ref[...] + jnp.where(s > 0, attn, 0.0)


def _attn_pipe_layer(x, q, qi, kis, ki_all, k_all, v_all, wo_bf, q_off, n_keys, topk, tq, j0, L):
    B, T, _ = x.shape
    QI_W = IDX_HEADS * IDX_DIM
    prev = lambda s: jnp.maximum(s - 1, 0)
    this = lambda s: jnp.minimum(s, B - 1)
    return pl.pallas_call(
        functools.partial(_attn_pipe_kernel, tq=tq, L=L, q_off=q_off + j0 * tq, n_keys=n_keys, topk=topk),
        grid=(B + 1,),
        in_specs=[
            pl.BlockSpec((None, tq, D_MODEL), lambda s: (prev(s), j0, 0)),
            pl.BlockSpec((None, ATTN_HEADS, tq, ATTN_HEAD_DIM), lambda s: (prev(s), 0, j0, 0)),
            pl.BlockSpec((None, tq, QI_W), lambda s: (this(s), j0, 0)),
            pl.BlockSpec((None, tq, LANES), lambda s: (this(s), j0, 0)),
            pl.BlockSpec((None, L, IDX_DIM), lambda s: (this(s), 0, 0)),
            pl.BlockSpec((None, L, ATTN_HEAD_DIM), lambda s: (prev(s), 0, 0)),
            pl.BlockSpec((None, L, 2 * ATTN_HEAD_DIM), lambda s: (prev(s), 0, 0)),
            _const_spec((Q_W, D_MODEL)),
        ],
        out_specs=pl.BlockSpec((None, tq, D_MODEL), lambda s: (prev(s), j0, 0)),
        out_shape=SDS((B, T, D_MODEL), F32),
        input_output_aliases={0: 0},
        scratch_shapes=[
            pltpu.VMEM((tq, L), jnp.int32),
            pltpu.VMEM((2, tq, L), F32),
            pltpu.VMEM((2, tq, L), F32),
            pltpu.VMEM((tq, 1), jnp.int32),
            pltpu.VMEM((ATTN_HEADS, tq, ATTN_HEAD_DIM), BF16),
        ],
        compiler_params=_cparams(("arbitrary",)),
        name="attn_pipe",
    )(x, q, qi, kis, ki_all, k_all, v_all, wo_bf)


def _rproj_kernel(x_ref, xp_ref, sh_ref, g_ref, mix_ref, w0_ref, w1_ref, w2_ref, a0_ref, a1_ref, a2_ref,
                  g1_ref, g2_ref, kk_ref, ka_ref, wr_ref, wk_ref, wv_ref, bd_ref,
                  r_o, lw_o, k_o, v_o, na_o, b_o, g_o, hl_o, *, tt):
    i = pl.program_id(1)
    g = g_ref[...]
    h = _rms(x_ref[...], g)
    hl_o[...] = h[tt - SUBLANES:, :]
    prev = jnp.where(i == 0, sh_ref[...], _rms(xp_ref[SUBLANES - 1:SUBLANES, :], g))
    row = lax.broadcasted_iota(jnp.int32, (tt, D_MODEL), 0)
    xx = jnp.where(row == 0, prev, pltpu.roll(h, 1, 0)) - h
    lerp = lambda n: (h + xx * mix_ref[n:n + 1, :]).astype(BF16)
    r = _dot(lerp(0), wr_ref[...])
    wl = w0_ref[...] + _dot(jnp.tanh(_dot(lerp(1), w1_ref[...])).astype(BF16), w2_ref[...])
    lw = -_sigmoid(wl) * math.exp(-0.5)
    k = _dot(lerp(2), wk_ref[...])
    v = _dot(lerp(3), wv_ref[...])
    a = _sigmoid(a0_ref[...] + _dot(_dot(lerp(4), a1_ref[...]).astype(BF16), a2_ref[...]))
    gate = _dot(_sigmoid(_dot(lerp(5), g1_ref[...])).astype(BF16), g2_ref[...])
    kk = k * kk_ref[...]
    kk = kk * lax.rsqrt(jnp.maximum(_head_sum(kk * kk, bd_ref), 1e-24))
    r_o[...] = r
    lw_o[...] = lw
    k_o[...] = k * (1.0 + (a - 1.0) * ka_ref[...])
    v_o[...] = v
    na_o[...] = -kk
    b_o[...] = kk * a
    g_o[...] = gate


def _rwkv_project(x, shift_prev, g, rw):
    B, T, _ = x.shape
    tt = _pick_tile(T, 256)
    rb = tt // SUBLANES
    tok = pl.BlockSpec((None, tt, D_MODEL), lambda b, i: (b, i, 0))
    consts = [g, rw["mix"], rw["w0"], rw["w1"], rw["w2"], rw["a0"], rw["a1"], rw["a2"], rw["g1"], rw["g2"],
              rw["k_k"], rw["k_a"], rw["w_r"], rw["w_k"], rw["w_v"], rw["bd"]]
    return pl.pallas_call(
        functools.partial(_rproj_kernel, tt=tt),
        grid=(B, T // tt),
        in_specs=[
            tok,
            pl.BlockSpec((None, SUBLANES, D_MODEL), lambda b, i: (b, jnp.maximum(i * rb - 1, 0), 0)),
            pl.BlockSpec((None, 1, D_MODEL), lambda b, i: (b, 0, 0)),
        ] + [_const_spec(c.shape) for c in consts],
        out_specs=[tok] * 7 + [pl.BlockSpec((None, SUBLANES, D_MODEL), lambda b, i: (b, 0, 0))],
        out_shape=[SDS((B, T, D_MODEL), F32)] * 7 + [SDS((B, SUBLANES, D_MODEL), F32)],
        compiler_params=_cparams(("parallel", "arbitrary")),
        name="rwkv_proj",
    )(x, x, shift_prev, *consts)


def _scan_kernel(r_ref, lw_ref, k_ref, v_ref, a_ref, b_ref, g_ref, s0_ref, rk_ref, lg_ref, lb_ref,
                 y_ref, st_ref, s_scr, *, nb, tt, C):
    N = RWKV_HEAD
    assert C == N and 2 * N == LANES

    hp = LANES // N
    zero = jnp.zeros((N, N), F32)

    @pl.when(pl.program_id(1) == 0)
    def _():
        for s in range(nb):
            for p in range(D_MODEL // LANES):
                blocks = [jnp.concatenate([s0_ref[s, hp * p + h] if g == h else zero for g in range(hp)], axis=1)
                          for h in range(hp)]
                s_scr[s, p] = jnp.concatenate(blocks, axis=0).T

    row_w = lax.broadcasted_iota(jnp.int32, (C, D_MODEL), 0)
    lane = lax.broadcasted_iota(jnp.int32, (C, LANES), 1)
    h0 = lane < N
    r1 = lax.broadcasted_iota(jnp.int32, (C, 2 * C), 0)
    c1 = lax.broadcasted_iota(jnp.int32, (C, 2 * C), 1) % C
    strict = r1 > c1
    r2 = lax.broadcasted_iota(jnp.int32, (C, 4 * C), 0)
    c2 = lax.broadcasted_iota(jnp.int32, (C, 4 * C), 1) % C
    incl = r2 >= c2
    rs = lax.broadcasted_iota(jnp.int32, (LANES, LANES), 0)
    cs = lax.broadcasted_iota(jnp.int32, (LANES, LANES), 1)
    same_head = (rs < N) == (cs < N)
    eye = rs == cs

    def split(x):
        return jnp.concatenate([jnp.where(h0, x, 0.0), jnp.where(h0, 0.0, x)], axis=0)

    def chunk(c, carry):
        sl = pl.ds(pl.multiple_of(c * C, C), C)
        n_pairs = D_MODEL // LANES
        at, rt, bt, kt, vv, wc = [], [], [], [], [], []
        for s in range(nb):
            lw = lw_ref[s, sl, :]
            cum = lw
            d = 1
            while d < C:
                cum = cum + jnp.where(row_w >= d, pltpu.roll(cum, d, 0), 0.0)
                d *= 2
            e_w = jnp.exp(cum)
            e_n = jnp.exp(-cum)
            rows = (a_ref[s, sl, :] * jnp.exp(cum - lw), r_ref[s, sl, :] * e_w, b_ref[s, sl, :] * e_n,
                    k_ref[s, sl, :] * e_n, v_ref[s, sl, :], e_w[C - 1:C, :])
            for dst, src in zip((at, rt, bt, kt, vv, wc), rows):
                dst.extend(src[:, p * LANES:(p + 1) * LANES] for p in range(n_pairs))
        pairs = range(nb * n_pairs)
        st = [s_scr[p // n_pairs, p % n_pairs] for p in pairs]
        ar = [jnp.concatenate([at[p], rt[p]], axis=0).astype(BF16) for p in pairs]
        v2 = [split(vv[p]).astype(BF16) for p in pairs]
        g = [_dot_nt(ar[p], jnp.concatenate([split(bt[p]), split(kt[p])], axis=0).astype(BF16)) for p in pairs]
        hm = [_dot(ar[p], st[p].astype(BF16)) for p in pairs]
        pw = [jnp.where(strict, g[p][:C, :2 * C], 0.0) for p in pairs]
        u = [hm[p][:C] + _dot(jnp.where(strict, g[p][:C, 2 * C:], 0.0).astype(BF16), v2[p]) for p in pairs]
        n = 1
        while n < C:
            pb = [pw[p].astype(BF16) for p in pairs]
            u = [u[p] + _dot(pb[p], split(u[p]).astype(BF16)) for p in pairs]
            n *= 2
            if n < C:
                pw = [_dot(pb[p], split(pw[p]).astype(BF16)) for p in pairs]
        ys = [hm[p][C:] + _dot(jnp.where(incl, g[p][C:, :], 0.0).astype(BF16),
                               jnp.concatenate([split(u[p]).astype(BF16), v2[p]], axis=0)) for p in pairs]
        for p in pairs:
            bk = jnp.concatenate([bt[p], kt[p]], axis=0).astype(BF16)
            uvp = jnp.concatenate([u[p], vv[p]], axis=0).astype(BF16)
            upd = jnp.where(same_head, _dot_tn(bk, uvp), 0.0)
            w_col = jnp.sum(jnp.where(eye, wc[p], 0.0), axis=1, keepdims=True)
            s_scr[p // n_pairs, p % n_pairs] = (st[p] + upd) * w_col
        inv_n = 1.0 / N

        def head_mean(z):
            s0 = jnp.sum(jnp.where(h0, z, 0.0), axis=1, keepdims=True)
            s1 = jnp.sum(jnp.where(h0, 0.0, z), axis=1, keepdims=True)
            return jnp.where(h0, s0, s1) * inv_n

        for s in range(nb):
            zs = []
            for q in range(n_pairs):
                p = s * n_pairs + q
                ps = slice(q * LANES, (q + 1) * LANES)
                yc = ys[p] - head_mean(ys[p])
                yn = yc * lax.rsqrt(head_mean(yc * yc) + LNX_EPS) * lg_ref[:, ps] + lb_ref[:, ps]
                rk = r_ref[s, sl, ps] * k_ref[s, sl, ps] * rk_ref[:, ps]
                zs.append((yn + head_mean(rk) * N * vv[p]) * g_ref[s, sl, ps])
            y_ref[s, sl, :] = jnp.concatenate(zs, axis=1).astype(BF16)
        return carry

    lax.fori_loop(0, tt // C, chunk, 0)

    @pl.when(pl.program_id(1) == pl.num_programs(1) - 1)
    def _():
        for s in range(nb):
            for p in range(D_MODEL // LANES):
                t = s_scr[s, p].T
                for h in range(hp):
                    st_ref[s, hp * p + h] = t[h * N:(h + 1) * N, h * N:(h + 1) * N]


def _rwkv_scan(r, lw, k, v, na, b, gate, wkv0, rw):
    B, T, _ = r.shape
    C = SCAN_CHUNK
    t_pad = ((T + C - 1) // C) * C
    seqs = (r, lw, k, v, na, b, gate)
    if t_pad != T:
        seqs = tuple(jnp.pad(a, ((0, 0), (0, t_pad - T), (0, 0))) for a in seqs)
    tt = _pick_tile(t_pad, 128)
    nb = _pick_tile(B, SCAN_SEQS)
    n_pairs = D_MODEL // LANES
    tok = pl.BlockSpec((nb, tt, D_MODEL), lambda b_, i: (b_, i, 0))
    st = pl.BlockSpec((nb, RWKV_HEADS, RWKV_HEAD, RWKV_HEAD), lambda b_, i: (b_, 0, 0, 0))
    consts = [rw["r_k"], rw["lnx_g"], rw["lnx_b"]]
    y, s_t = pl.pallas_call(
        functools.partial(_scan_kernel, nb=nb, tt=tt, C=C),
        grid=(B // nb, t_pad // tt),
        in_specs=[tok] * 7 + [st] + [_const_spec(c.shape) for c in consts],
        out_specs=[tok, st],
        out_shape=[SDS((B, t_pad, D_MODEL), BF16), SDS((B, RWKV_HEADS, RWKV_HEAD, RWKV_HEAD), F32)],
        scratch_shapes=[pltpu.VMEM((nb, n_pairs, LANES, LANES), F32)],
        compiler_params=_cparams(("parallel", "arbitrary")),
        name="rwkv_scan",
    )(*seqs, wkv0, *consts)
    return (y if t_pad == T else y[:, :T]), s_t


def _wo_mlp_kernel(x_ref, z_ref, wo_ref, g_ref, wu_ref, wd_ref, gf_ref, o_ref, *, final_norm):
    x = x_ref[...] + _dot(z_ref[...], wo_ref[...])
    o_ref[...] = _mlp_apply(x, g_ref, wu_ref, wd_ref, gf_ref, final_norm)


def _wo_mlp(x, z, wo, mlp_args, final_norm):
    M = x.shape[0]
    tm = _pick_tile(M, 512)
    tok = pl.BlockSpec((tm, D_MODEL), lambda i: (i, 0))
    return pl.pallas_call(
        functools.partial(_wo_mlp_kernel, final_norm=final_norm),
        grid=(M // tm,),
        in_specs=[tok, tok, _const_spec(wo.shape)] + _mlp_specs(mlp_args[4]),
        out_specs=tok,
        out_shape=SDS((M, D_MODEL), F32),
        compiler_params=_cparams(("parallel",)),
        name="wo_mlp",
    )(x, z, wo, *mlp_args[:4])


def _pool_block(x, state, n_hist, g, w_bf, scale, mlp_args=(), final_norm=False):
    B = x.shape[0]
    if state is None:
        hist = jnp.zeros((B, POOL_PAD, D_MODEL), F32)
    else:
        hist = jnp.pad(state, ((0, 0), (POOL_PAD - POOL_HIST, 0), (0, 0)))
    out, hs = _pool_layer(x, hist, n_hist, g, w_bf, scale, mlp_args, final_norm)
    return out, hs[:, POOL_PAD - POOL_HIST:]


def _attn_block(x, cache, g, w_in_pad, wo_bf):
    B, T, _ = x.shape
    past = 0 if cache is None else cache[0].shape[1]
    tab = _rope_tables(past, T)
    q, qi, kis, k_new, v_new, ki_new, kb, vb, kib = _attn_project(x, g, w_in_pad, tab)
    topk = min(TOPK_MAX, (past + T) // 4)
    if cache is not None:
        return _attn_decode_layer(x, q, qi, kis, kib, kb, vb, cache, wo_bf, topk), k_new, v_new, ki_new
    tq = _pick_tile(T, ATTN_Q_BLOCK)
    out = x
    for j0 in range(T // tq):
        l_g = (j0 + 1) * tq
        out = _attn_pipe_layer(out, q, qi, kis, kib, kb, vb, wo_bf, 0, l_g, topk, tq, j0, l_g)
    return out, k_new, v_new, ki_new


def _rwkv_block(x, shift_prev, wkv0, g, rw, mlp_args, final_norm):
    B, T, _ = x.shape
    r, lw, k, v, na, b, gate, hl = _rwkv_project(x, shift_prev, g, rw)
    z, s_t = _rwkv_scan(r, lw, k, v, na, b, gate, wkv0, rw)
    flat = lambda a: a.reshape(B * T, D_MODEL)
    out = _wo_mlp(flat(x), flat(z), rw["w_o"], mlp_args, final_norm).reshape(B, T, D_MODEL)
    return out, hl[:, SUBLANES - 1:], s_t


def kernel(x_prompt, x_sample, state_pool, cache_k, cache_v, cache_kidx, state_shift, state_wkv, ln1_g, ln2_g, w_up, w_down, ln_f_g, pool_w, pool_scale, attn_w_in, attn_w_out, rwkv_mix, rwkv_w0, rwkv_w1, rwkv_w2, rwkv_a0, rwkv_a1, rwkv_a2, rwkv_g1, rwkv_g2, rwkv_k_k, rwkv_k_a, rwkv_r_k, rwkv_w_r, rwkv_w_k, rwkv_w_v, rwkv_w_o, rwkv_lnx_g, rwkv_lnx_b):
    xp, xs = x_prompt, x_sample
    bp, sp, _ = xp.shape
    bs, ss, _ = xs.shape
    past = cache_k.shape[2]
    row = lambda a: a.reshape(1, -1)
    bf = lambda a: a.astype(BF16)
    wu_all, wd_all = bf(w_up), bf(w_down)
    head_of = jnp.arange(MXU_DIM) // RWKV_HEAD
    bd_mat = (head_of[:, None] == head_of[None, :]).astype(BF16)
    outs = {n: [] for n in ("pool_p", "pool_s", "k_p", "k_s", "v_p", "v_s", "ki_p", "ki_s",
                            "sh_p", "sh_s", "wkv_p", "wkv_s")}
    for i in range(DEPTH):
        j = i // N_MIXERS
        g1 = row(ln1_g[i])
        last = i == DEPTH - 1
        mlp_args = (row(ln2_g[i]), wu_all, wd_all, row(ln_f_g), i)
        prompt_mlp_done = sample_mlp_done = False
        if i % N_MIXERS == 0:
            w_bf = bf(pool_w[j])
            sc = row(pool_scale[j])
            prompt_mlp_done = sp >= POOL_MLP_MIN_ROWS
            xp, st_p = _pool_block(xp, None, 0, g1, w_bf, sc, mlp_args if prompt_mlp_done else (), last)
            xs, st_s = _pool_block(xs, state_pool[j], past, g1, w_bf, sc)
            outs["pool_p"].append(st_p)
            outs["pool_s"].append(st_s)
        elif i % N_MIXERS == 1:
            w_in_pad = jnp.pad(bf(attn_w_in[j]), ((0, 0), (0, ATTN_IN_PAD - ATTN_IN_W)))
            wo_bf = bf(attn_w_out[j])
            xp, kp, vp, kip = _attn_block(xp, None, g1, w_in_pad, wo_bf)
            xs, kn, vn, kin = _attn_block(xs, (cache_k[j], cache_v[j], cache_kidx[j]), g1, w_in_pad, wo_bf)
            for n, a in (("k_p", kp), ("v_p", vp), ("ki_p", kip), ("k_s", kn), ("v_s", vn), ("ki_s", kin)):
                outs[n].append(a)
        else:
            rw = dict(mix=rwkv_mix[j], w0=row(rwkv_w0[j]), w1=bf(rwkv_w1[j]), w2=bf(rwkv_w2[j]),
                      a0=row(rwkv_a0[j]), a1=bf(rwkv_a1[j]), a2=bf(rwkv_a2[j]), g1=bf(rwkv_g1[j]),
                      g2=bf(rwkv_g2[j]), k_k=row(rwkv_k_k[j]), k_a=row(rwkv_k_a[j]), r_k=row(rwkv_r_k[j]),
                      w_r=bf(rwkv_w_r[j]), w_k=bf(rwkv_w_k[j]), w_v=bf(rwkv_w_v[j]), w_o=bf(rwkv_w_o[j]),
                      lnx_g=row(rwkv_lnx_g[j]), lnx_b=row(rwkv_lnx_b[j]), bd=bd_mat)
            zero_shift = jnp.zeros((bp, 1, D_MODEL), F32)
            zero_wkv = jnp.zeros((bp, RWKV_HEADS, RWKV_HEAD, RWKV_HEAD), F32)
            xp, shp, wp = _rwkv_block(xp, zero_shift, zero_wkv, g1, rw, mlp_args, last)
            xs, shs, wsn = _rwkv_block(xs, state_shift[j], state_wkv[j], g1, rw, mlp_args, last)
            prompt_mlp_done = sample_mlp_done = True
            outs["sh_p"].append(shp)
            outs["sh_s"].append(shs)
            outs["wkv_p"].append(wp)
            outs["wkv_s"].append(wsn)
        if not prompt_mlp_done:
            xp = _mlp(xp.reshape(bp * sp, D_MODEL), mlp_args, last).reshape(bp, sp, D_MODEL)
        if not sample_mlp_done:
            xs = _mlp(xs.reshape(bs * ss, D_MODEL), mlp_args, last).reshape(bs, ss, D_MODEL)
    st = lambda n: jnp.stack(outs[n], 0)
    return (xp, xs, st("pool_p"), st("pool_s"), st("k_p"), st("k_s"), st("v_p"), st("v_s"),
            st("ki_p"), st("ki_s"), st("sh_p"), st("sh_s"), st("wkv_p"), st("wkv_s"))
```

```python
import functools
import math

import jax
import jax.numpy as jnp
import numpy as np
from jax import lax
from jax.experimental import pallas as pl
from jax.experimental.pallas import tpu as pltpu

F32 = jnp.float32
BF16 = jnp.bfloat16
SDS = jax.ShapeDtypeStruct

D_MODEL = 1024
DEPTH = 4
N_MIXERS = 3
CHUNK = 64
D_FF = 4 * D_MODEL
RMS_EPS = 1e-6
POOL_WINDOWS = (2, 4, 8, 16)
POOL_GROUPS = 4
POOL_GW = D_MODEL // POOL_GROUPS
POOL_HIST = max(POOL_WINDOWS) - 1
POOL_PAD = POOL_HIST + 1
ATTN_HEADS = 8
ATTN_HEAD_DIM = D_MODEL // ATTN_HEADS
IDX_HEADS = 8
IDX_DIM = 64
TOPK_MAX = 256
ROPE_THETA = 500000.0
ROPE_FRACTION = 4
NEG_INF = -1e30
Q_W = ATTN_HEADS * ATTN_HEAD_DIM
OFF_K = Q_W
OFF_V = OFF_K + ATTN_HEAD_DIM
OFF_QI = OFF_V + ATTN_HEAD_DIM
OFF_KI = OFF_QI + IDX_HEADS * IDX_DIM
OFF_WI = OFF_KI + IDX_DIM
ATTN_IN_W = OFF_WI + IDX_HEADS
RWKV_HEAD = 64
RWKV_HEADS = D_MODEL // RWKV_HEAD
LNX_EPS = 64e-5

LANES = 128
SUBLANES = 8
ATTN_IN_PAD = ((ATTN_IN_W + LANES - 1) // LANES) * LANES
MXU_DIM = 256
VMEM_LIMIT = 56 * 1024 * 1024
INT_MIN = -2 ** 31
HALF_NEG_KEY = int(np.float32(0.5 * NEG_INF).view(np.int32)) ^ 0x7FFFFFFF
POOL_MLP_MIN_ROWS = 512
POOL_MLP_SUBTILES = 2
ATTN_Q_BLOCK = 256
SCAN_CHUNK = 64
SCAN_SEQS = 2
FF_CHUNK = 1024


def _cparams(sem):
    return pltpu.CompilerParams(dimension_semantics=sem, vmem_limit_bytes=VMEM_LIMIT)


def _const_spec(shape):
    nd = len(shape)
    return pl.BlockSpec(shape, lambda *_: (0,) * nd, pipeline_mode=pl.Buffered(1))


def _rms(x, g):
    ms = jnp.mean(x * x, axis=-1, keepdims=True)
    return x * lax.rsqrt(ms + RMS_EPS) * g


def _dot(a, b):
    return jnp.dot(a, b, preferred_element_type=F32)


def _dot_nt(a, b):
    return lax.dot_general(a, b, (((1,), (1,)), ((), ())), preferred_element_type=F32)


def _dot_tn(a, b):
    return lax.dot_general(a, b, (((0,), (0,)), ((), ())), preferred_element_type=F32)


def _head_sum(z, bd_ref):
    bd = bd_ref[...]
    hi = z.astype(BF16)
    lo = (z - hi.astype(F32)).astype(BF16)
    outs = []
    for c in range(D_MODEL // MXU_DIM):
        cs = slice(c * MXU_DIM, (c + 1) * MXU_DIM)
        outs.append(_dot(hi[:, cs], bd) + _dot(lo[:, cs], bd))
    return jnp.concatenate(outs, axis=1)


def _sigmoid(x):
    return 1.0 / (1.0 + jnp.exp(-x))


def _pick_tile(n, pref):
    t = min(n, pref)
    assert n % t == 0, (n, t)
    return t


def _mlp_apply(x, g_ref, wu_ref, wd_ref, gf_ref, final_norm, side=()):
    h = _rms(x, g_ref[...]).astype(BF16)
    acc = x
    for j in range(D_FF // FF_CHUNK):
        u = _dot(h, wu_ref[:, j * FF_CHUNK:(j + 1) * FF_CHUNK])
        u = jnp.square(jnp.maximum(u, 0.0)).astype(BF16)
        acc = acc + _dot(u, wd_ref[j * FF_CHUNK:(j + 1) * FF_CHUNK, :])
        if j < len(side):
            side[j]()
    if final_norm:
        acc = _rms(acc, gf_ref[...])
    return acc


def _mlp_specs(layer):
    pick = lambda *_: (layer, 0, 0)
    return [_const_spec((1, D_MODEL)),
            pl.BlockSpec((None, D_MODEL, D_FF), pick, pipeline_mode=pl.Buffered(1)),
            pl.BlockSpec((None, D_FF, D_MODEL), pick, pipeline_mode=pl.Buffered(1)),
            _const_spec((1, D_MODEL))]


def _mlp_kernel(x_ref, g_ref, wu_ref, wd_ref, gf_ref, o_ref, *, final_norm):
    o_ref[...] = _mlp_apply(x_ref[...], g_ref, wu_ref, wd_ref, gf_ref, final_norm)


def _mlp(x, mlp_args, final_norm):
    M = x.shape[0]
    tm = _pick_tile(M, 512)
    return pl.pallas_call(
        functools.partial(_mlp_kernel, final_norm=final_norm),
        grid=(M // tm,),
        in_specs=[pl.BlockSpec((tm, D_MODEL), lambda i: (i, 0))] + _mlp_specs(mlp_args[4]),
        out_specs=pl.BlockSpec((tm, D_MODEL), lambda i: (i, 0)),
        out_shape=SDS((M, D_MODEL), F32),
        compiler_params=_cparams(("parallel",)),
        name="mlp",
    )(x, *mlp_args[:4])


def _pool_kernel(x_ref, xp_ref, hist_ref, g_ref, w_ref, sc_ref, *rest, tt, n_hist, mlp, final_norm):
    mlp_refs, (o_ref, hs_ref) = rest[:-2], rest[-2:]
    i = pl.program_id(1)
    g = g_ref[...]
    x = x_ref[...]
    h = _rms(x, g)
    prev = jnp.where(i == 0, hist_ref[...], _rms(xp_ref[...], g))
    hs_ref[...] = h[tt - POOL_PAD:, :]
    full = jnp.concatenate([prev, h], axis=0)
    nsub = POOL_MLP_SUBTILES if mlp else 1
    rows = tt // nsub

    def group(r, gi):
        win = POOL_WINDOWS[gi]
        rs = slice(r * rows, (r + 1) * rows)
        cs = slice(gi * POOL_GW, (gi + 1) * POOL_GW)
        s = full[r * rows:(r + 1) * rows + POOL_PAD, cs]
        d = 1
        while d < win:
            s = s + pltpu.roll(s, d, 0)
            d *= 2
        t1 = lax.broadcasted_iota(jnp.int32, (POOL_PAD, POOL_GW), 0) + (i * tt + r * rows + 1 + n_hist)
        cnt = jnp.concatenate([jnp.minimum(t1, win).astype(F32), jnp.full((rows - POOL_PAD, POOL_GW), win, F32)],
                              axis=0)
        pooled = s[POOL_PAD:, :] / cnt - h[rs, cs]
        return x[rs, cs] + _dot(pooled.astype(BF16), w_ref[gi]) * sc_ref[:, cs]

    cols = [group(0, gi) for gi in range(POOL_GROUPS)]
    if not mlp:
        o_ref[...] = jnp.concatenate(cols, axis=1)
        return
    outs = []
    for r in range(nsub):
        y = jnp.concatenate(cols, axis=1)
        cols = []
        side = [functools.partial(lambda gi, rn: cols.append(group(rn, gi)), gi, r + 1)
                for gi in range(POOL_GROUPS)] if r + 1 < nsub else []
        outs.append(_mlp_apply(y, *mlp_refs, final_norm, side))
    o_ref[...] = jnp.concatenate(outs, axis=0)


def _pool_layer(x, hist, n_hist, g, w_bf, scale, mlp_args, final_norm):
    B, T, _ = x.shape
    tt = _pick_tile(T, 512)
    rb = tt // POOL_PAD
    return pl.pallas_call(
        functools.partial(_pool_kernel, tt=tt, n_hist=n_hist, mlp=bool(mlp_args), final_norm=final_norm),
        grid=(B, T // tt),
        in_specs=[
            pl.BlockSpec((None, tt, D_MODEL), lambda b, i: (b, i, 0)),
            pl.BlockSpec((None, POOL_PAD, D_MODEL), lambda b, i: (b, jnp.maximum(i * rb - 1, 0), 0)),
            pl.BlockSpec((None, POOL_PAD, D_MODEL), lambda b, i: (b, 0, 0)),
            _const_spec((1, D_MODEL)),
            _const_spec((POOL_GROUPS, POOL_GW, POOL_GW)),
            _const_spec((1, D_MODEL)),
        ] + (_mlp_specs(mlp_args[4]) if mlp_args else []),
        out_specs=[
            pl.BlockSpec((None, tt, D_MODEL), lambda b, i: (b, i, 0)),
            pl.BlockSpec((None, POOL_PAD, D_MODEL), lambda b, i: (b, 0, 0)),
        ],
        out_shape=[SDS((B, T, D_MODEL), F32), SDS((B, POOL_PAD, D_MODEL), F32)],
        compiler_params=_cparams(("parallel", "arbitrary")),
        name="pool_mlp" if mlp_args else "pool",
    )(x, x, hist, g, w_bf, scale, *mlp_args[:4])


def _rope_tables(pos0, T):
    pos = np.arange(pos0, pos0 + T, dtype=np.float64)

    def head(d):
        rd = d // ROPE_FRACTION
        half = rd // 2
        inv = ROPE_THETA ** (-np.arange(half, dtype=np.float64) / half)
        ang = pos[:, None] * inv[None, :]
        cos, sin = np.cos(ang), np.sin(ang)
        z = lambda n: np.zeros((T, n))
        c = np.concatenate([cos, cos, np.ones((T, d - rd))], axis=1)
        sa = np.concatenate([-sin, z(d - half)], axis=1)
        sb = np.concatenate([z(half), sin, z(d - rd)], axis=1)
        return c, sa, sb

    qa = head(ATTN_HEAD_DIM)
    ia = head(IDX_DIM)
    i2 = tuple(np.concatenate([t, t], axis=1) for t in ia)
    ones, zeros = np.ones((T, IDX_DIM)), np.zeros((T, IDX_DIM))
    ik = (np.concatenate([ia[0], ones], axis=1), np.concatenate([ia[1], zeros], axis=1),
          np.concatenate([ia[2], zeros], axis=1))
    return jnp.asarray(np.concatenate(list(qa) + list(i2) + list(ik), axis=1), F32)


def _aproj_kernel(x_ref, g_ref, w_ref, tab_ref, q_ref, qi_ref, kis_ref, k_ref, v_ref, ki_ref, kb_ref, vb_ref, kib_ref):
    h = _rms(x_ref[...], g_ref[...]).astype(BF16)
    p = _dot(h, w_ref[...])

    def rope(x, kind, half):
        c = tab_ref[:, (3 * kind) * LANES:(3 * kind + 1) * LANES]
        sa = tab_ref[:, (3 * kind + 1) * LANES:(3 * kind + 2) * LANES]
        sb = tab_ref[:, (3 * kind + 2) * LANES:(3 * kind + 3) * LANES]
        return x * c + pltpu.roll(x, LANES - half, 1) * sa + pltpu.roll(x, half, 1) * sb

    slab = lambda off: p[:, off:off + LANES]
    qh = ATTN_HEAD_DIM // ROPE_FRACTION // 2
    ih = IDX_DIM // ROPE_FRACTION // 2
    qscale = ATTN_HEAD_DIM ** -0.5 * math.log2(math.e)
    for s in range(ATTN_HEADS):
        q_ref[s] = (rope(slab(s * LANES), 0, qh) * qscale).astype(BF16)
    k = rope(slab(OFF_K), 0, qh)
    v = slab(OFF_V)
    for s in range(IDX_HEADS * IDX_DIM // LANES):
        qi_ref[:, s * LANES:(s + 1) * LANES] = rope(slab(OFF_QI + s * LANES), 1, ih).astype(BF16)
    kis = rope(slab(OFF_KI), 2, ih)
    kis_ref[...] = kis
    k_ref[...] = k
    v_ref[...] = v
    ki_ref[...] = kis[:, :IDX_DIM]
    kb_ref[...] = k.astype(BF16)
    lane = lax.broadcasted_iota(jnp.int32, v.shape, 1)
    vb_ref[:, :ATTN_HEAD_DIM] = v.astype(BF16)
    vb_ref[:, ATTN_HEAD_DIM:] = jnp.where(lane == 0, 1.0, 0.0).astype(BF16)
    kib_ref[...] = kis[:, :IDX_DIM].astype(BF16)


def _attn_project(x, g, w_in_pad, tab):
    B, T, _ = x.shape
    tt = _pick_tile(T, 512)
    QI_W = IDX_HEADS * IDX_DIM
    widths = (QI_W, LANES, ATTN_HEAD_DIM, ATTN_HEAD_DIM, IDX_DIM, ATTN_HEAD_DIM, 2 * ATTN_HEAD_DIM, IDX_DIM)
    dtypes = (BF16, F32, F32, F32, F32, BF16, BF16, BF16)
    q_spec = pl.BlockSpec((None, ATTN_HEADS, tt, ATTN_HEAD_DIM), lambda i, b: (b, 0, i, 0))
    return pl.pallas_call(
        _aproj_kernel,
        grid=(T // tt, B),
        in_specs=[
            pl.BlockSpec((None, tt, D_MODEL), lambda i, b: (b, i, 0)),
            _const_spec((1, D_MODEL)),
            _const_spec((D_MODEL, ATTN_IN_PAD)),
            pl.BlockSpec((tt, 9 * LANES), lambda i, b: (i, 0)),
        ],
        out_specs=[q_spec] + [pl.BlockSpec((None, tt, w), lambda i, b: (b, i, 0)) for w in widths],
        out_shape=[SDS((B, ATTN_HEADS, T, ATTN_HEAD_DIM), BF16)] + [SDS((B, T, w), d) for w, d in zip(widths, dtypes)],
        compiler_params=_cparams(("arbitrary", "arbitrary")),
        name="attn_proj",
    )(x, g, w_in_pad, tab)


def _score_key(score):
    kb = pltpu.bitcast(score, jnp.int32)
    return jnp.where(kb >= 0, kb, kb ^ jnp.int32(0x7FFFFFFF))


KEY_BITS = 32


def _count(mask, axis):
    return jnp.sum(jnp.where(mask, 1.0, 0.0), axis=axis, keepdims=True)


def _kth_key_step(key_scr, lo_u, it, topk, axis):
    trial_u = lo_u | lax.shift_left(jnp.int32(1), KEY_BITS - 1 - it)
    c = _count(key_scr[...] >= (trial_u ^ jnp.int32(INT_MIN)), axis)
    return jnp.where(c >= topk, trial_u, lo_u)


def _select_topk(key_scr, jstar_scr, kpos, topk, axis):
    step = lambda it, lo_u: _kth_key_step(key_scr, lo_u, it, topk, axis)
    lo_u = lax.fori_loop(0, KEY_BITS, step, jnp.zeros(jstar_scr.shape, jnp.int32))
    return _finish_topk(key_scr, jstar_scr, kpos, lo_u, topk, axis)


def _finish_topk(key_scr, jstar_scr, kpos, lo_u, topk, axis):
    one = jstar_scr.shape
    n_idx = key_scr.shape[axis]
    count = functools.partial(_count, axis=axis)
    lo = lo_u ^ jnp.int32(INT_MIN)
    gt = key_scr[...] > lo
    eq = key_scr[...] == lo
    need = topk - count(gt)
    tie = (lo > jnp.int32(HALF_NEG_KEY)) & (count(eq) > need)
    jstar_scr[...] = jnp.full(one, n_idx, jnp.int32)

    @pl.when(jnp.max(jnp.where(tie, 1.0, 0.0)) > 0.0)
    def _():
        nbits = max(1, (n_idx - 1).bit_length())

        def ibody(it, lo_i):
            trial = lo_i + lax.shift_left(jnp.int32(1), nbits - 1 - it)
            c = count((kpos < trial) & (key_scr[...] == lo))
            return jnp.where(c < need, trial, lo_i)

        lo_i = lax.fori_loop(0, nbits, ibody, jnp.zeros(one, jnp.int32))
        jstar_scr[...] = jnp.where(tie, lo_i, n_idx)

    return gt | (eq & (kpos <= jstar_scr[...]))


def _attn_decode_kernel(x_ref, q_ref, qi_ref, wi_ref, kib_ref, kb_ref, vb_ref, cki_ref, ck_ref, cv_ref, wo_ref,
                        o_ref, ki_scr, k_scr, v_scr, key_scr, bias_scr, jstar_scr, o_scr, *, tq, P, L, topk):
    n_keys = P + tq

    @pl.when(pl.program_id(0) == 0)
    def _():
        lane = lax.broadcasted_iota(jnp.int32, (P, ATTN_HEAD_DIM), 1)
        v_scr[:P, ATTN_HEAD_DIM:] = jnp.where(lane == 0, 1.0, 0.0).astype(BF16)
        ki_scr[n_keys:, :] = jnp.zeros((L - n_keys, IDX_DIM), BF16)
        k_scr[n_keys:, :] = jnp.zeros((L - n_keys, ATTN_HEAD_DIM), BF16)
        v_scr[n_keys:, :] = jnp.zeros((L - n_keys, 2 * ATTN_HEAD_DIM), BF16)

    ki_scr[:P, :] = cki_ref[...].astype(BF16)
    ki_scr[P:n_keys, :] = kib_ref[...]
    k_scr[:P, :] = ck_ref[...].astype(BF16)
    k_scr[P:n_keys, :] = kb_ref[...]
    v_scr[:P, :ATTN_HEAD_DIM] = cv_ref[...].astype(BF16)
    v_scr[P:n_keys, :] = vb_ref[...]

    qpos = lax.broadcasted_iota(jnp.int32, (tq, L), 0) + P
    kpos = lax.broadcasted_iota(jnp.int32, (tq, L), 1)
    adm = (kpos < n_keys) & ((kpos // CHUNK) <= (qpos // CHUNK))

    qi = qi_ref[...]
    wis = wi_ref[...] * (IDX_DIM ** -0.5 * IDX_HEADS ** -0.5)
    d_all = _dot_nt(jnp.concatenate([qi[:, h * IDX_DIM:(h + 1) * IDX_DIM] for h in range(IDX_HEADS)], axis=0),
                    ki_scr[...])
    score = jnp.zeros((tq, L), F32)
    for h in range(IDX_HEADS):
        score = score + jnp.maximum(d_all[h * tq:(h + 1) * tq], 0.0) * wis[:, IDX_DIM + h:IDX_DIM + h + 1]
    key_scr[...] = _score_key(jnp.where(adm, score, NEG_INF))
    keep = _select_topk(key_scr, jstar_scr, kpos, topk, axis=1)
    bias_scr[...] = jnp.where(adm & keep, 0.0, NEG_INF)

    def probs(lg):
        lg = lg + bias_scr[...]
        return jnp.exp2(lg - jnp.max(lg, axis=1, keepdims=True)).astype(BF16)

    lg_all = _dot_nt(jnp.concatenate([q_ref[h] for h in range(ATTN_HEADS)], axis=0), k_scr[...])
    p_all = jnp.concatenate([probs(lg_all[h * tq:(h + 1) * tq]) for h in range(ATTN_HEADS)], axis=0)
    pv_all = _dot(p_all, v_scr[...])
    for h in range(ATTN_HEADS):
        pv = pv_all[h * tq:(h + 1) * tq]
        o_scr[:, h * ATTN_HEAD_DIM:(h + 1) * ATTN_HEAD_DIM] = (
            pv[:, :ATTN_HEAD_DIM] / pv[:, ATTN_HEAD_DIM:ATTN_HEAD_DIM + 1]).astype(BF16)
    o_ref[...] = x_ref[...] + _dot(o_scr[...], wo_ref[...])


def _attn_decode_layer(x, q, qi, kis, kib, kb, vb, cache, wo_bf, topk):
    B, T, _ = x.shape
    ck, cv, cki = cache
    P = ck.shape[1]
    L = ((P + T + LANES - 1) // LANES) * LANES
    assert T * max(IDX_HEADS, ATTN_HEADS) <= MXU_DIM, "decode block: all heads of all new tokens in one row tile"
    assert P % 16 == 0 and T % 16 == 0, "bf16 row tiles"
    QI_W = IDX_HEADS * IDX_DIM
    seq = lambda *shape: pl.BlockSpec((None,) + shape, lambda b: (b,) + (0,) * len(shape))
    return pl.pallas_call(
        functools.partial(_attn_decode_kernel, tq=T, P=P, L=L, topk=topk),
        grid=(B,),
        in_specs=[
            seq(T, D_MODEL), seq(ATTN_HEADS, T, ATTN_HEAD_DIM), seq(T, QI_W), seq(T, LANES),
            seq(T, IDX_DIM), seq(T, ATTN_HEAD_DIM), seq(T, 2 * ATTN_HEAD_DIM),
            seq(P, IDX_DIM), seq(P, ATTN_HEAD_DIM), seq(P, ATTN_HEAD_DIM),
            _const_spec((Q_W, D_MODEL)),
        ],
        out_specs=seq(T, D_MODEL),
        out_shape=SDS((B, T, D_MODEL), F32),
        scratch_shapes=[
            pltpu.VMEM((L, IDX_DIM), BF16),
            pltpu.VMEM((L, ATTN_HEAD_DIM), BF16),
            pltpu.VMEM((L, 2 * ATTN_HEAD_DIM), BF16),
            pltpu.VMEM((T, L), jnp.int32),
            pltpu.VMEM((T, L), F32),
            pltpu.VMEM((T, 1), jnp.int32),
            pltpu.VMEM((T, Q_W), BF16),
        ],
        compiler_params=_cparams(("arbitrary",)),
        name="attn_decode",
    )(x, q, qi, kis, kib, kb, vb, cki, ck, cv, wo_bf)


def _attn_pipe_kernel(x_ref, q_ref, qi_ref, wi_ref, ki_ref, k_ref, v_ref, wo_ref, o_ref,
                      key_scr, bias_scr, lg_scr, jstar_scr, o_scr, *, tq, L, q_off, n_keys, topk):
    s = pl.program_id(0)
    cur = s % 2

    @pl.when(s == 0)
    def _():
        bias_scr[...] = jnp.zeros(bias_scr.shape, F32)

    qpos = lax.broadcasted_iota(jnp.int32, (tq, L), 0) + q_off
    kpos = lax.broadcasted_iota(jnp.int32, (tq, L), 1)
    adm = (kpos < n_keys) & ((kpos // CHUNK) <= (qpos // CHUNK))

    select_all = L <= topk
    if not select_all:
        qi = qi_ref[...]
        ki = ki_ref[...]
        wis = wi_ref[...] * (IDX_DIM ** -0.5 * IDX_HEADS ** -0.5)
        score = jnp.zeros((tq, L), F32)
        for h in range(IDX_HEADS):
            d = _dot_nt(qi[:, h * IDX_DIM:(h + 1) * IDX_DIM], ki)
            score = score + jnp.maximum(d, 0.0) * wis[:, IDX_DIM + h:IDX_DIM + h + 1]
        key_scr[...] = _score_key(jnp.where(adm, score, NEG_INF))

    steps_per_head = KEY_BITS // ATTN_HEADS

    lg_scr[0] = _dot_nt(q_ref[0], k_ref[...])

    def head(h, lo_u):
        if select_all:
            step = lambda i, lo: lo
        else:
            step = lambda i, lo: _kth_key_step(key_scr, lo, h * steps_per_head + i, topk, 1)
        lo_u = step(0, lo_u)
        lg = lg_scr[h % 2] + bias_scr[1 - cur]
        lg_scr[(h + 1) % 2] = _dot_nt(q_ref[jnp.minimum(h + 1, ATTN_HEADS - 1)], k_ref[...])
        lo_u = step(1, lo_u)
        p = jnp.exp2(lg - jnp.max(lg, axis=1, keepdims=True)).astype(BF16)
        lo_u = step(2, lo_u)
        pv = _dot(p, v_ref[...])
        for i in range(3, steps_per_head):
            lo_u = step(i, lo_u)
        o_scr[h] = (pv[:, :ATTN_HEAD_DIM] / pv[:, ATTN_HEAD_DIM:ATTN_HEAD_DIM + 1]).astype(BF16)
        return lo_u

    lo_u = lax.fori_loop(0, ATTN_HEADS, head, jnp.zeros((tq, 1), jnp.int32))
    keep = adm if select_all else adm & _finish_topk(key_scr, jstar_scr, kpos, lo_u, topk, 1)
    bias_scr[cur] = jnp.where(keep, 0.0, NEG_INF)

    attn = _dot(jnp.concatenate([o_scr[h] for h in range(ATTN_HEADS)], axis=1), wo_ref[...])
    o_ref[...] = x_ref[...] + jnp.where(s > 0, attn, 0.0)


def _attn_pipe_layer(x, q, qi, kis, ki_all, k_all, v_all, wo_bf, q_off, n_keys, topk, tq, j0, L):
    B, T, _ = x.shape
    QI_W = IDX_HEADS * IDX_DIM
    prev = lambda s: jnp.maximum(s - 1, 0)
    this = lambda s: jnp.minimum(s, B - 1)
    return pl.pallas_call(
        functools.partial(_attn_pipe_kernel, tq=tq, L=L, q_off=q_off + j0 * tq, n_keys=n_keys, topk=topk),
        grid=(B + 1,),
        in_specs=[
            pl.BlockSpec((None, tq, D_MODEL), lambda s: (prev(s), j0, 0)),
            pl.BlockSpec((None, ATTN_HEADS, tq, ATTN_HEAD_DIM), lambda s: (prev(s), 0, j0, 0)),
            pl.BlockSpec((None, tq, QI_W), lambda s: (this(s), j0, 0)),
            pl.BlockSpec((None, tq, LANES), lambda s: (this(s), j0, 0)),
            pl.BlockSpec((None, L, IDX_DIM), lambda s: (this(s), 0, 0)),
            pl.BlockSpec((None, L, ATTN_HEAD_DIM), lambda s: (prev(s), 0, 0)),
            pl.BlockSpec((None, L, 2 * ATTN_HEAD_DIM), lambda s: (prev(s), 0, 0)),
            _const_spec((Q_W, D_MODEL)),
        ],
        out_specs=pl.BlockSpec((None, tq, D_MODEL), lambda s: (prev(s), j0, 0)),
        out_shape=SDS((B, T, D_MODEL), F32),
        input_output_aliases={0: 0},
        scratch_shapes=[
            pltpu.VMEM((tq, L), jnp.int32),
            pltpu.VMEM((2, tq, L), F32),
            pltpu.VMEM((2, tq, L), F32),
            pltpu.VMEM((tq, 1), jnp.int32),
            pltpu.VMEM((ATTN_HEADS, tq, ATTN_HEAD_DIM), BF16),
        ],
        compiler_params=_cparams(("arbitrary",)),
        name="attn_pipe",
    )(x, q, qi, kis, ki_all, k_all, v_all, wo_bf)


def _rproj_kernel(x_ref, xp_ref, sh_ref, g_ref, mix_ref, w0_ref, w1_ref, w2_ref, a0_ref, a1_ref, a2_ref,
                  g1_ref, g2_ref, kk_ref, ka_ref, wr_ref, wk_ref, wv_ref, bd_ref,
                  r_o, lw_o, k_o, v_o, na_o, b_o, g_o, hl_o, *, tt):
    i = pl.program_id(1)
    g = g_ref[...]
    h = _rms(x_ref[...], g)
    hl_o[...] = h[tt - SUBLANES:, :]
    prev = jnp.where(i == 0, sh_ref[...], _rms(xp_ref[SUBLANES - 1:SUBLANES, :], g))
    row = lax.broadcasted_iota(jnp.int32, (tt, D_MODEL), 0)
    xx = jnp.where(row == 0, prev, pltpu.roll(h, 1, 0)) - h
    lerp = lambda n: (h + xx * mix_ref[n:n + 1, :]).astype(BF16)
    r = _dot(lerp(0), wr_ref[...])
    wl = w0_ref[...] + _dot(jnp.tanh(_dot(lerp(1), w1_ref[...])).astype(BF16), w2_ref[...])
    lw = -_sigmoid(wl) * math.exp(-0.5)
    k = _dot(lerp(2), wk_ref[...])
    v = _dot(lerp(3), wv_ref[...])
    a = _sigmoid(a0_ref[...] + _dot(_dot(lerp(4), a1_ref[...]).astype(BF16), a2_ref[...]))
    gate = _dot(_sigmoid(_dot(lerp(5), g1_ref[...])).astype(BF16), g2_ref[...])
    kk = k * kk_ref[...]
    kk = kk * lax.rsqrt(jnp.maximum(_head_sum(kk * kk, bd_ref), 1e-24))
    r_o[...] = r
    lw_o[...] = lw
    k_o[...] = k * (1.0 + (a - 1.0) * ka_ref[...])
    v_o[...] = v
    na_o[...] = -kk
    b_o[...] = kk * a
    g_o[...] = gate


def _rwkv_project(x, shift_prev, g, rw):
    B, T, _ = x.shape
    tt = _pick_tile(T, 256)
    rb = tt // SUBLANES
    tok = pl.BlockSpec((None, tt, D_MODEL), lambda b, i: (b, i, 0))
    consts = [g, rw["mix"], rw["w0"], rw["w1"], rw["w2"], rw["a0"], rw["a1"], rw["a2"], rw["g1"], rw["g2"],
              rw["k_k"], rw["k_a"], rw["w_r"], rw["w_k"], rw["w_v"], rw["bd"]]
    return pl.pallas_call(
        functools.partial(_rproj_kernel, tt=tt),
        grid=(B, T // tt),
        in_specs=[
            tok,
            pl.BlockSpec((None, SUBLANES, D_MODEL), lambda b, i: (b, jnp.maximum(i * rb - 1, 0), 0)),
            pl.BlockSpec((None, 1, D_MODEL), lambda b, i: (b, 0, 0)),
        ] + [_const_spec(c.shape) for c in consts],
        out_specs=[tok] * 7 + [pl.BlockSpec((None, SUBLANES, D_MODEL), lambda b, i: (b, 0, 0))],
        out_shape=[SDS((B, T, D_MODEL), F32)] * 7 + [SDS((B, SUBLANES, D_MODEL), F32)],
        compiler_params=_cparams(("parallel", "arbitrary")),
        name="rwkv_proj",
    )(x, x, shift_prev, *consts)


def _scan_kernel(r_ref, lw_ref, k_ref, v_ref, a_ref, b_ref, g_ref, s0_ref, rk_ref, lg_ref, lb_ref,
                 y_ref, st_ref, s_scr, *, nb, tt, C):
    N = RWKV_HEAD
    assert C == N and 2 * N == LANES

    hp = LANES // N
    zero = jnp.zeros((N, N), F32)

    @pl.when(pl.program_id(1) == 0)
    def _():
        for s in range(nb):
            for p in range(D_MODEL // LANES):
                blocks = [jnp.concatenate([s0_ref[s, hp * p + h] if g == h else zero for g in range(hp)], axis=1)
                          for h in range(hp)]
                s_scr[s, p] = jnp.concatenate(blocks, axis=0).T

    row_w = lax.broadcasted_iota(jnp.int32, (C, D_MODEL), 0)
    lane = lax.broadcasted_iota(jnp.int32, (C, LANES), 1)
    h0 = lane < N
    r1 = lax.broadcasted_iota(jnp.int32, (C, 2 * C), 0)
    c1 = lax.broadcasted_iota(jnp.int32, (C, 2 * C), 1) % C
    strict = r1 > c1
    r2 = lax.broadcasted_iota(jnp.int32, (C, 4 * C), 0)
    c2 = lax.broadcasted_iota(jnp.int32, (C, 4 * C), 1) % C
    incl = r2 >= c2
    rs = lax.broadcasted_iota(jnp.int32, (LANES, LANES), 0)
    cs = lax.broadcasted_iota(jnp.int32, (LANES, LANES), 1)
    same_head = (rs < N) == (cs < N)
    eye = rs == cs

    def split(x):
        return jnp.concatenate([jnp.where(h0, x, 0.0), jnp.where(h0, 0.0, x)], axis=0)

    def chunk(c, carry):
        sl = pl.ds(pl.multiple_of(c * C, C), C)
        n_pairs = D_MODEL // LANES
        at, rt, bt, kt, vv, wc = [], [], [], [], [], []
        for s in range(nb):
            lw = lw_ref[s, sl, :]
            cum = lw
            d = 1
            while d < C:
                cum = cum + jnp.where(row_w >= d, pltpu.roll(cum, d, 0), 0.0)
                d *= 2
            e_w = jnp.exp(cum)
            e_n = jnp.exp(-cum)
            rows = (a_ref[s, sl, :] * jnp.exp(cum - lw), r_ref[s, sl, :] * e_w, b_ref[s, sl, :] * e_n,
                    k_ref[s, sl, :] * e_n, v_ref[s, sl, :], e_w[C - 1:C, :])
            for dst, src in zip((at, rt, bt, kt, vv, wc), rows):
                dst.extend(src[:, p * LANES:(p + 1) * LANES] for p in range(n_pairs))
        pairs = range(nb * n_pairs)
        st = [s_scr[p // n_pairs, p % n_pairs] for p in pairs]
        ar = [jnp.concatenate([at[p], rt[p]], axis=0).astype(BF16) for p in pairs]
        v2 = [split(vv[p]).astype(BF16) for p in pairs]
        g = [_dot_nt(ar[p], jnp.concatenate([split(bt[p]), split(kt[p])], axis=0).astype(BF16)) for p in pairs]
        hm = [_dot(ar[p], st[p].astype(BF16)) for p in pairs]
        pw = [jnp.where(strict, g[p][:C, :2 * C], 0.0) for p in pairs]
        u = [hm[p][:C] + _dot(jnp.where(strict, g[p][:C, 2 * C:], 0.0).astype(BF16), v2[p]) for p in pairs]
        n = 1
        while n < C:
            pb = [pw[p].astype(BF16) for p in pairs]
            u = [u[p] + _dot(pb[p], split(u[p]).astype(BF16)) for p in pairs]
            n *= 2
            if n < C:
                pw = [_dot(pb[p], split(pw[p]).astype(BF16)) for p in pairs]
        ys = [hm[p][C:] + _dot(jnp.where(incl, g[p][C:, :], 0.0).astype(BF16),
                               jnp.concatenate([split(u[p]).astype(BF16), v2[p]], axis=0)) for p in pairs]
        for p in pairs:
            bk = jnp.concatenate([bt[p], kt[p]], axis=0).astype(BF16)
            uvp = jnp.concatenate([u[p], vv[p]], axis=0).astype(BF16)
            upd = jnp.where(same_head, _dot_tn(bk, uvp), 0.0)
            w_col = jnp.sum(jnp.where(eye, wc[p], 0.0), axis=1, keepdims=True)
            s_scr[p // n_pairs, p % n_pairs] = (st[p] + upd) * w_col
        inv_n = 1.0 / N

        def head_mean(z):
            s0 = jnp.sum(jnp.where(h0, z, 0.0), axis=1, keepdims=True)
            s1 = jnp.sum(jnp.where(h0, 0.0, z), axis=1, keepdims=True)
            return jnp.where(h0, s0, s1) * inv_n

        for s in range(nb):
            zs = []
            for q in range(n_pairs):
                p = s * n_pairs + q
                ps = slice(q * LANES, (q + 1) * LANES)
                yc = ys[p] - head_mean(ys[p])
                yn = yc * lax.rsqrt(head_mean(yc * yc) + LNX_EPS) * lg_ref[:, ps] + lb_ref[:, ps]
                rk = r_ref[s, sl, ps] * k_ref[s, sl, ps] * rk_ref[:, ps]
                zs.append((yn + head_mean(rk) * N * vv[p]) * g_ref[s, sl, ps])
            y_ref[s, sl, :] = jnp.concatenate(zs, axis=1).astype(BF16)
        return carry

    lax.fori_loop(0, tt // C, chunk, 0)

    @pl.when(pl.program_id(1) == pl.num_programs(1) - 1)
    def _():
        for s in range(nb):
            for p in range(D_MODEL // LANES):
                t = s_scr[s, p].T
                for h in range(hp):
                    st_ref[s, hp * p + h] = t[h * N:(h + 1) * N, h * N:(h + 1) * N]


def _rwkv_scan(r, lw, k, v, na, b, gate, wkv0, rw):
    B, T, _ = r.shape
    C = SCAN_CHUNK
    t_pad = ((T + C - 1) // C) * C
    seqs = (r, lw, k, v, na, b, gate)
    if t_pad != T:
        seqs = tuple(jnp.pad(a, ((0, 0), (0, t_pad - T), (0, 0))) for a in seqs)
    tt = _pick_tile(t_pad, 128)
    nb = _pick_tile(B, SCAN_SEQS)
    n_pairs = D_MODEL // LANES
    tok = pl.BlockSpec((nb, tt, D_MODEL), lambda b_, i: (b_, i, 0))
    st = pl.BlockSpec((nb, RWKV_HEADS, RWKV_HEAD, RWKV_HEAD), lambda b_, i: (b_, 0, 0, 0))
    consts = [rw["r_k"], rw["lnx_g"], rw["lnx_b"]]
    y, s_t = pl.pallas_call(
        functools.partial(_scan_kernel, nb=nb, tt=tt, C=C),
        grid=(B // nb, t_pad // tt),
        in_specs=[tok] * 7 + [st] + [_const_spec(c.shape) for c in consts],
        out_specs=[tok, st],
        out_shape=[SDS((B, t_pad, D_MODEL), BF16), SDS((B, RWKV_HEADS, RWKV_HEAD, RWKV_HEAD), F32)],
        scratch_shapes=[pltpu.VMEM((nb, n_pairs, LANES, LANES), F32)],
        compiler_params=_cparams(("parallel", "arbitrary")),
        name="rwkv_scan",
    )(*seqs, wkv0, *consts)
    return (y if t_pad == T else y[:, :T]), s_t


def _wo_mlp_kernel(x_ref, z_ref, wo_ref, g_ref, wu_ref, wd_ref, gf_ref, o_ref, *, final_norm):
    x = x_ref[...] + _dot(z_ref[...], wo_ref[...])
    o_ref[...] = _mlp_apply(x, g_ref, wu_ref, wd_ref, gf_ref, final_norm)


def _wo_mlp(x, z, wo, mlp_args, final_norm):
    M = x.shape[0]
    tm = _pick_tile(M, 512)
    tok = pl.BlockSpec((tm, D_MODEL), lambda i: (i, 0))
    return pl.pallas_call(
        functools.partial(_wo_mlp_kernel, final_norm=final_norm),
        grid=(M // tm,),
        in_specs=[tok, tok, _const_spec(wo.shape)] + _mlp_specs(mlp_args[4]),
        out_specs=tok,
        out_shape=SDS((M, D_MODEL), F32),
        compiler_params=_cparams(("parallel",)),
        name="wo_mlp",
    )(x, z, wo, *mlp_args[:4])


def _pool_block(x, state, n_hist, g, w_bf, scale, mlp_args=(), final_norm=False):
    B = x.shape[0]
    if state is None:
        hist = jnp.zeros((B, POOL_PAD, D_MODEL), F32)
    else:
        hist = jnp.pad(state, ((0, 0), (POOL_PAD - POOL_HIST, 0), (0, 0)))
    out, hs = _pool_layer(x, hist, n_hist, g, w_bf, scale, mlp_args, final_norm)
    return out, hs[:, POOL_PAD - POOL_HIST:]


def _attn_block(x, cache, g, w_in_pad, wo_bf):
    B, T, _ = x.shape
    past = 0 if cache is None else cache[0].shape[1]
    tab = _rope_tables(past, T)
    q, qi, kis, k_new, v_new, ki_new, kb, vb, kib = _attn_project(x, g, w_in_pad, tab)
    topk = min(TOPK_MAX, (past + T) // 4)
    if cache is not None:
        return _attn_decode_layer(x, q, qi, kis, kib, kb, vb, cache, wo_bf, topk), k_new, v_new, ki_new
    tq = _pick_tile(T, ATTN_Q_BLOCK)
    out = x
    for j0 in range(T // tq):
        l_g = (j0 + 1) * tq
        out = _attn_pipe_layer(out, q, qi, kis, kib, kb, vb, wo_bf, 0, l_g, topk, tq, j0, l_g)
    return out, k_new, v_new, ki_new


def _rwkv_block(x, shift_prev, wkv0, g, rw, mlp_args, final_norm):
    B, T, _ = x.shape
    r, lw, k, v, na, b, gate, hl = _rwkv_project(x, shift_prev, g, rw)
    z, s_t = _rwkv_scan(r, lw, k, v, na, b, gate, wkv0, rw)
    flat = lambda a: a.reshape(B * T, D_MODEL)
    out = _wo_mlp(flat(x), flat(z), rw["w_o"], mlp_args, final_norm).reshape(B, T, D_MODEL)
    return out, hl[:, SUBLANES - 1:], s_t


def kernel(x_prompt, x_sample, state_pool, cache_k, cache_v, cache_kidx, state_shift, state_wkv, ln1_g, ln2_g, w_up, w_down, ln_f_g, pool_w, pool_scale, attn_w_in, attn_w_out, rwkv_mix, rwkv_w0, rwkv_w1, rwkv_w2, rwkv_a0, rwkv_a1, rwkv_a2, rwkv_g1, rwkv_g2, rwkv_k_k, rwkv_k_a, rwkv_r_k, rwkv_w_r, rwkv_w_k, rwkv_w_v, rwkv_w_o, rwkv_lnx_g, rwkv_lnx_b):
    xp, xs = x_prompt, x_sample
    bp, sp, _ = xp.shape
    bs, ss, _ = xs.shape
    past = cache_k.shape[2]
    row = lambda a: a.reshape(1, -1)
    bf = lambda a: a.astype(BF16)
    wu_all, wd_all = bf(w_up), bf(w_down)
    head_of = jnp.arange(MXU_DIM) // RWKV_HEAD
    bd_mat = (head_of[:, None] == head_of[None, :]).astype(BF16)
    outs = {n: [] for n in ("pool_p", "pool_s", "k_p", "k_s", "v_p", "v_s", "ki_p", "ki_s",
                            "sh_p", "sh_s", "wkv_p", "wkv_s")}
    for i in range(DEPTH):
        j = i // N_MIXERS
        g1 = row(ln1_g[i])
        last = i == DEPTH - 1
        mlp_args = (row(ln2_g[i]), wu_all, wd_all, row(ln_f_g), i)
        prompt_mlp_done = sample_mlp_done = False
        if i % N_MIXERS == 0:
            w_bf = bf(pool_w[j])
            sc = row(pool_scale[j])
            prompt_mlp_done = sp >= POOL_MLP_MIN_ROWS
            xp, st_p = _pool_block(xp, None, 0, g1, w_bf, sc, mlp_args if prompt_mlp_done else (), last)
            xs, st_s = _pool_block(xs, state_pool[j], past, g1, w_bf, sc)
            outs["pool_p"].append(st_p)
            outs["pool_s"].append(st_s)
        elif i % N_MIXERS == 1:
            w_in_pad = jnp.pad(bf(attn_w_in[j]), ((0, 0), (0, ATTN_IN_PAD - ATTN_IN_W)))
            wo_bf = bf(attn_w_out[j])
            xp, kp, vp, kip = _attn_block(xp, None, g1, w_in_pad, wo_bf)
            xs, kn, vn, kin = _attn_block(xs, (cache_k[j], cache_v[j], cache_kidx[j]), g1, w_in_pad, wo_bf)
            for n, a in (("k_p", kp), ("v_p", vp), ("ki_p", kip), ("k_s", kn), ("v_s", vn), ("ki_s", kin)):
                outs[n].append(a)
        else:
            rw = dict(mix=rwkv_mix[j], w0=row(rwkv_w0[j]), w1=bf(rwkv_w1[j]), w2=bf(rwkv_w2[j]),
                      a0=row(rwkv_a0[j]), a1=bf(rwkv_a1[j]), a2=bf(rwkv_a2[j]), g1=bf(rwkv_g1[j]),
                      g2=bf(rwkv_g2[j]), k_k=row(rwkv_k_k[j]), k_a=row(rwkv_k_a[j]), r_k=row(rwkv_r_k[j]),
                      w_r=bf(rwkv_w_r[j]), w_k=bf(rwkv_w_k[j]), w_v=bf(rwkv_w_v[j]), w_o=bf(rwkv_w_o[j]),
                      lnx_g=row(rwkv_lnx_g[j]), lnx_b=row(rwkv_lnx_b[j]), bd=bd_mat)
            zero_shift = jnp.zeros((bp, 1, D_MODEL), F32)
            zero_wkv = jnp.zeros((bp, RWKV_HEADS, RWKV_HEAD, RWKV_HEAD), F32)
            xp, shp, wp = _rwkv_block(xp, zero_shift, zero_wkv, g1, rw, mlp_args, last)
            xs, shs, wsn = _rwkv_block(xs, state_shift[j], state_wkv[j], g1, rw, mlp_args, last)
            prompt_mlp_done = sample_mlp_done = True
            outs["sh_p"].append(shp)
            outs["sh_s"].append(shs)
            outs["wkv_p"].append(wp)
            outs["wkv_s"].append(wsn)
        if not prompt_mlp_done:
            xp = _mlp(xp.reshape(bp * sp, D_MODEL), mlp_args, last).reshape(bp, sp, D_MODEL)
        if not sample_mlp_done:
            xs = _mlp(xs.reshape(bs * ss, D_MODEL), mlp_args, last).reshape(bs, ss, D_MODEL)
    st = lambda n: jnp.stack(outs[n], 0)
    return (xp, xs, st("pool_p"), st("pool_s"), st("k_p"), st("k_s"), st("v_p"), st("v_s"),
            st("ki_p"), st("ki_s"), st("sh_p"), st("sh_s"), st("wkv_p"), st("wkv_s"))
```

```python
import functools
import math

import jax
import jax.numpy as jnp
import numpy as np
from jax import lax
from jax.experimental import pallas as pl
from jax.experimental.pallas import tpu as pltpu

F32 = jnp.float32
BF16 = jnp.bfloat16
SDS = jax.ShapeDtypeStruct

D_MODEL = 1024
DEPTH = 4
N_MIXERS = 3
CHUNK = 64
D_FF = 4 * D_MODEL
RMS_EPS = 1e-6
POOL_WINDOWS = (2, 4, 8, 16)
POOL_GROUPS = 4
POOL_GW = D_MODEL // POOL_GROUPS
POOL_HIST = max(POOL_WINDOWS) - 1
POOL_PAD = POOL_HIST + 1
ATTN_HEADS = 8
ATTN_HEAD_DIM = D_MODEL // ATTN_HEADS
IDX_HEADS = 8
IDX_DIM = 64
TOPK_MAX = 256
ROPE_THETA = 500000.0
ROPE_FRACTION = 4
NEG_INF = -1e30
Q_W = ATTN_HEADS * ATTN_HEAD_DIM
OFF_K = Q_W
OFF_V = OFF_K + ATTN_HEAD_DIM
OFF_QI = OFF_V + ATTN_HEAD_DIM
OFF_KI = OFF_QI + IDX_HEADS * IDX_DIM
OFF_WI = OFF_KI + IDX_DIM
ATTN_IN_W = OFF_WI + IDX_HEADS
RWKV_HEAD = 64
RWKV_HEADS = D_MODEL // RWKV_HEAD
LNX_EPS = 64e-5

LANES = 128
SUBLANES = 8
ATTN_IN_PAD = ((ATTN_IN_W + LANES - 1) // LANES) * LANES
MXU_DIM = 256
VMEM_LIMIT = 56 * 1024 * 1024
INT_MIN = -2 ** 31
HALF_NEG_KEY = int(np.float32(0.5 * NEG_INF).view(np.int32)) ^ 0x7FFFFFFF
POOL_MLP_MIN_ROWS = 512
POOL_MLP_SUBTILES = 2
ATTN_Q_BLOCK = 256
SCAN_CHUNK = 64
SCAN_SEQS = 2
FF_CHUNK = 1024


def _cparams(sem):
    return pltpu.CompilerParams(dimension_semantics=sem, vmem_limit_bytes=VMEM_LIMIT)


def _const_spec(shape):
    nd = len(shape)
    return pl.BlockSpec(shape, lambda *_: (0,) * nd, pipeline_mode=pl.Buffered(1))


def _rms(x, g):
    ms = jnp.mean(x * x, axis=-1, keepdims=True)
    return x * lax.rsqrt(ms + RMS_EPS) * g


def _dot(a, b):
    return jnp.dot(a, b, preferred_element_type=F32)


def _dot_nt(a, b):
    return lax.dot_general(a, b, (((1,), (1,)), ((), ())), preferred_element_type=F32)


def _dot_tn(a, b):
    return lax.dot_general(a, b, (((0,), (0,)), ((), ())), preferred_element_type=F32)


def _head_sum(z, bd_ref):
    bd = bd_ref[...]
    hi = z.astype(BF16)
    lo = (z - hi.astype(F32)).astype(BF16)
    outs = []
    for c in range(D_MODEL // MXU_DIM):
        cs = slice(c * MXU_DIM, (c + 1) * MXU_DIM)
        outs.append(_dot(hi[:, cs], bd) + _dot(lo[:, cs], bd))
    return jnp.concatenate(outs, axis=1)


def _sigmoid(x):
    return 1.0 / (1.0 + jnp.exp(-x))


def _pick_tile(n, pref):
    t = min(n, pref)
    assert n % t == 0, (n, t)
    return t


def _mlp_apply(x, g_ref, wu_ref, wd_ref, gf_ref, final_norm, side=()):
    h = _rms(x, g_ref[...]).astype(BF16)
    acc = x
    for j in range(D_FF // FF_CHUNK):
        u = _dot(h, wu_ref[:, j * FF_CHUNK:(j + 1) * FF_CHUNK])
        u = jnp.square(jnp.maximum(u, 0.0)).astype(BF16)
        acc = acc + _dot(u, wd_ref[j * FF_CHUNK:(j + 1) * FF_CHUNK, :])
        if j < len(side):
            side[j]()
    if final_norm:
        acc = _rms(acc, gf_ref[...])
    return acc


def _mlp_specs(layer):
    pick = lambda *_: (layer, 0, 0)
    return [_const_spec((1, D_MODEL)),
            pl.BlockSpec((None, D_MODEL, D_FF), pick, pipeline_mode=pl.Buffered(1)),
            pl.BlockSpec((None, D_FF, D_MODEL), pick, pipeline_mode=pl.Buffered(1)),
            _const_spec((1, D_MODEL))]


def _mlp_kernel(x_ref, g_ref, wu_ref, wd_ref, gf_ref, o_ref, *, final_norm):
    o_ref[...] = _mlp_apply(x_ref[...], g_ref, wu_ref, wd_ref, gf_ref, final_norm)


def _mlp(x, mlp_args, final_norm):
    M = x.shape[0]
    tm = _pick_tile(M, 512)
    return pl.pallas_call(
        functools.partial(_mlp_kernel, final_norm=final_norm),
        grid=(M // tm,),
        in_specs=[pl.BlockSpec((tm, D_MODEL), lambda i: (i, 0))] + _mlp_specs(mlp_args[4]),
        out_specs=pl.BlockSpec((tm, D_MODEL), lambda i: (i, 0)),
        out_shape=SDS((M, D_MODEL), F32),
        compiler_params=_cparams(("parallel",)),
        name="mlp",
    )(x, *mlp_args[:4])


def _pool_kernel(x_ref, xp_ref, hist_ref, g_ref, w_ref, sc_ref, *rest, tt, n_hist, mlp, final_norm):
    mlp_refs, (o_ref, hs_ref) = rest[:-2], rest[-2:]
    i = pl.program_id(1)
    g = g_ref[...]
    x = x_ref[...]
    h = _rms(x, g)
    prev = jnp.where(i == 0, hist_ref[...], _rms(xp_ref[...], g))
    hs_ref[...] = h[tt - POOL_PAD:, :]
    full = jnp.concatenate([prev, h], axis=0)
    nsub = POOL_MLP_SUBTILES if mlp else 1
    rows = tt // nsub

    def group(r, gi):
        win = POOL_WINDOWS[gi]
        rs = slice(r * rows, (r + 1) * rows)
        cs = slice(gi * POOL_GW, (gi + 1) * POOL_GW)
        s = full[r * rows:(r + 1) * rows + POOL_PAD, cs]
        d = 1
        while d < win:
            s = s + pltpu.roll(s, d, 0)
            d *= 2
        t1 = lax.broadcasted_iota(jnp.int32, (POOL_PAD, POOL_GW), 0) + (i * tt + r * rows + 1 + n_hist)
        cnt = jnp.concatenate([jnp.minimum(t1, win).astype(F32), jnp.full((rows - POOL_PAD, POOL_GW), win, F32)],
                              axis=0)
        pooled = s[POOL_PAD:, :] / cnt - h[rs, cs]
        return x[rs, cs] + _dot(pooled.astype(BF16), w_ref[gi]) * sc_ref[:, cs]

    cols = [group(0, gi) for gi in range(POOL_GROUPS)]
    if not mlp:
        o_ref[...] = jnp.concatenate(cols, axis=1)
        return
    outs = []
    for r in range(nsub):
        y = jnp.concatenate(cols, axis=1)
        cols = []
        side = [functools.partial(lambda gi, rn: cols.append(group(rn, gi)), gi, r + 1)
                for gi in range(POOL_GROUPS)] if r + 1 < nsub else []
        outs.append(_mlp_apply(y, *mlp_refs, final_norm, side))
    o_ref[...] = jnp.concatenate(outs, axis=0)


def _pool_layer(x, hist, n_hist, g, w_bf, scale, mlp_args, final_norm):
    B, T, _ = x.shape
    tt = _pick_tile(T, 512)
    rb = tt // POOL_PAD
    return pl.pallas_call(
        functools.partial(_pool_kernel, tt=tt, n_hist=n_hist, mlp=bool(mlp_args), final_norm=final_norm),
        grid=(B, T // tt),
        in_specs=[
            pl.BlockSpec((None, tt, D_MODEL), lambda b, i: (b, i, 0)),
            pl.BlockSpec((None, POOL_PAD, D_MODEL), lambda b, i: (b, jnp.maximum(i * rb - 1, 0), 0)),
            pl.BlockSpec((None, POOL_PAD, D_MODEL), lambda b, i: (b, 0, 0)),
            _const_spec((1, D_MODEL)),
            _const_spec((POOL_GROUPS, POOL_GW, POOL_GW)),
            _const_spec((1, D_MODEL)),
        ] + (_mlp_specs(mlp_args[4]) if mlp_args else []),
        out_specs=[
            pl.BlockSpec((None, tt, D_MODEL), lambda b, i: (b, i, 0)),
            pl.BlockSpec((None, POOL_PAD, D_MODEL), lambda b, i: (b, 0, 0)),
        ],
        out_shape=[SDS((B, T, D_MODEL), F32), SDS((B, POOL_PAD, D_MODEL), F32)],
        compiler_params=_cparams(("parallel", "arbitrary")),
        name="pool_mlp" if mlp_args else "pool",
    )(x, x, hist, g, w_bf, scale, *mlp_args[:4])


def _rope_tables(pos0, T):
    pos = np.arange(pos0, pos0 + T, dtype=np.float64)

    def head(d):
        rd = d // ROPE_FRACTION
        half = rd // 2
        inv = ROPE_THETA ** (-np.arange(half, dtype=np.float64) / half)
        ang = pos[:, None] * inv[None, :]
        cos, sin = np.cos(ang), np.sin(ang)
        z = lambda n: np.zeros((T, n))
        c = np.concatenate([cos, cos, np.ones((T, d - rd))], axis=1)
        sa = np.concatenate([-sin, z(d - half)], axis=1)
        sb = np.concatenate([z(half), sin, z(d - rd)], axis=1)
        return c, sa, sb

    qa = head(ATTN_HEAD_DIM)
    ia = head(IDX_DIM)
    i2 = tuple(np.concatenate([t, t], axis=1) for t in ia)
    ones, zeros = np.ones((T, IDX_DIM)), np.zeros((T, IDX_DIM))
    ik = (np.concatenate([ia[0], ones], axis=1), np.concatenate([ia[1], zeros], axis=1),
          np.concatenate([ia[2], zeros], axis=1))
    return jnp.asarray(np.concatenate(list(qa) + list(i2) + list(ik), axis=1), F32)


def _aproj_kernel(x_ref, g_ref, w_ref, tab_ref, q_ref, qi_ref, kis_ref, k_ref, v_ref, ki_ref, kb_ref, vb_ref, kib_ref):
    h = _rms(x_ref[...], g_ref[...]).astype(BF16)
    p = _dot(h, w_ref[...])

    def rope(x, kind, half):
        c = tab_ref[:, (3 * kind) * LANES:(3 * kind + 1) * LANES]
        sa = tab_ref[:, (3 * kind + 1) * LANES:(3 * kind + 2) * LANES]
        sb = tab_ref[:, (3 * kind + 2) * LANES:(3 * kind + 3) * LANES]
        return x * c + pltpu.roll(x, LANES - half, 1) * sa + pltpu.roll(x, half, 1) * sb

    slab = lambda off: p[:, off:off + LANES]
    qh = ATTN_HEAD_DIM // ROPE_FRACTION // 2
    ih = IDX_DIM // ROPE_FRACTION // 2
    qscale = ATTN_HEAD_DIM ** -0.5 * math.log2(math.e)
    for s in range(ATTN_HEADS):
        q_ref[s] = (rope(slab(s * LANES), 0, qh) * qscale).astype(BF16)
    k = rope(slab(OFF_K), 0, qh)
    v = slab(OFF_V)
    for s in range(IDX_HEADS * IDX_DIM // LANES):
        qi_ref[:, s * LANES:(s + 1) * LANES] = rope(slab(OFF_QI + s * LANES), 1, ih).astype(BF16)
    kis = rope(slab(OFF_KI), 2, ih)
    kis_ref[...] = kis
    k_ref[...] = k
    v_ref[...] = v
    ki_ref[...] = kis[:, :IDX_DIM]
    kb_ref[...] = k.astype(BF16)
    lane = lax.broadcasted_iota(jnp.int32, v.shape, 1)
    vb_ref[:, :ATTN_HEAD_DIM] = v.astype(BF16)
    vb_ref[:, ATTN_HEAD_DIM:] = jnp.where(lane == 0, 1.0, 0.0).astype(BF16)
    kib_ref[...] = kis[:, :IDX_DIM].astype(BF16)


def _attn_project(x, g, w_in_pad, tab):
    B, T, _ = x.shape
    tt = _pick_tile(T, 512)
    QI_W = IDX_HEADS * IDX_DIM
    widths = (QI_W, LANES, ATTN_HEAD_DIM, ATTN_HEAD_DIM, IDX_DIM, ATTN_HEAD_DIM, 2 * ATTN_HEAD_DIM, IDX_DIM)
    dtypes = (BF16, F32, F32, F32, F32, BF16, BF16, BF16)
    q_spec = pl.BlockSpec((None, ATTN_HEADS, tt, ATTN_HEAD_DIM), lambda i, b: (b, 0, i, 0))
    return pl.pallas_call(
        _aproj_kernel,
        grid=(T // tt, B),
        in_specs=[
            pl.BlockSpec((None, tt, D_MODEL), lambda i, b: (b, i, 0)),
            _const_spec((1, D_MODEL)),
            _const_spec((D_MODEL, ATTN_IN_PAD)),
            pl.BlockSpec((tt, 9 * LANES), lambda i, b: (i, 0)),
        ],
        out_specs=[q_spec] + [pl.BlockSpec((None, tt, w), lambda i, b: (b, i, 0)) for w in widths],
        out_shape=[SDS((B, ATTN_HEADS, T, ATTN_HEAD_DIM), BF16)] + [SDS((B, T, w), d) for w, d in zip(widths, dtypes)],
        compiler_params=_cparams(("arbitrary", "arbitrary")),
        name="attn_proj",
    )(x, g, w_in_pad, tab)


def _score_key(score):
    kb = pltpu.bitcast(score, jnp.int32)
    return jnp.where(kb >= 0, kb, kb ^ jnp.int32(0x7FFFFFFF))


KEY_BITS = 32


def _count(mask, axis):
    return jnp.sum(jnp.where(mask, 1.0, 0.0), axis=axis, keepdims=True)


def _kth_key_step(key_scr, lo_u, it, topk, axis):
    trial_u = lo_u | lax.shift_left(jnp.int32(1), KEY_BITS - 1 - it)
    c = _count(key_scr[...] >= (trial_u ^ jnp.int32(INT_MIN)), axis)
    return jnp.where(c >= topk, trial_u, lo_u)


def _select_topk(key_scr, jstar_scr, kpos, topk, axis):
    step = lambda it, lo_u: _kth_key_step(key_scr, lo_u, it, topk, axis)
    lo_u = lax.fori_loop(0, KEY_BITS, step, jnp.zeros(jstar_scr.shape, jnp.int32))
    return _finish_topk(key_scr, jstar_scr, kpos, lo_u, topk, axis)


def _finish_topk(key_scr, jstar_scr, kpos, lo_u, topk, axis):
    one = jstar_scr.shape
    n_idx = key_scr.shape[axis]
    count = functools.partial(_count, axis=axis)
    lo = lo_u ^ jnp.int32(INT_MIN)
    gt = key_scr[...] > lo
    eq = key_scr[...] == lo
    need = topk - count(gt)
    tie = (lo > jnp.int32(HALF_NEG_KEY)) & (count(eq) > need)
    jstar_scr[...] = jnp.full(one, n_idx, jnp.int32)

    @pl.when(jnp.max(jnp.where(tie, 1.0, 0.0)) > 0.0)
    def _():
        nbits = max(1, (n_idx - 1).bit_length())

        def ibody(it, lo_i):
            trial = lo_i + lax.shift_left(jnp.int32(1), nbits - 1 - it)
            c = count((kpos < trial) & (key_scr[...] == lo))
            return jnp.where(c < need, trial, lo_i)

        lo_i = lax.fori_loop(0, nbits, ibody, jnp.zeros(one, jnp.int32))
        jstar_scr[...] = jnp.where(tie, lo_i, n_idx)

    return gt | (eq & (kpos <= jstar_scr[...]))


def _attn_decode_kernel(x_ref, q_ref, qi_ref, wi_ref, kib_ref, kb_ref, vb_ref, cki_ref, ck_ref, cv_ref, wo_ref,
                        o_ref, ki_scr, k_scr, v_scr, key_scr, bias_scr, jstar_scr, o_scr, *, tq, P, L, topk):
    n_keys = P + tq

    @pl.when(pl.program_id(0) == 0)
    def _():
        lane = lax.broadcasted_iota(jnp.int32, (P, ATTN_HEAD_DIM), 1)
        v_scr[:P, ATTN_HEAD_DIM:] = jnp.where(lane == 0, 1.0, 0.0).astype(BF16)
        ki_scr[n_keys:, :] = jnp.zeros((L - n_keys, IDX_DIM), BF16)
        k_scr[n_keys:, :] = jnp.zeros((L - n_keys, ATTN_HEAD_DIM), BF16)
        v_scr[n_keys:, :] = jnp.zeros((L - n_keys, 2 * ATTN_HEAD_DIM), BF16)

    ki_scr[:P, :] = cki_ref[...].astype(BF16)
    ki_scr[P:n_keys, :] = kib_ref[...]
    k_scr[:P, :] = ck_ref[...].astype(BF16)
    k_scr[P:n_keys, :] = kb_ref[...]
    v_scr[:P, :ATTN_HEAD_DIM] = cv_ref[...].astype(BF16)
    v_scr[P:n_keys, :] = vb_ref[...]

    qpos = lax.broadcasted_iota(jnp.int32, (tq, L), 0) + P
    kpos = lax.broadcasted_iota(jnp.int32, (tq, L), 1)
    adm = (kpos < n_keys) & ((kpos // CHUNK) <= (qpos // CHUNK))

    qi = qi_ref[...]
    wis = wi_ref[...] * (IDX_DIM ** -0.5 * IDX_HEADS ** -0.5)
    d_all = _dot_nt(jnp.concatenate([qi[:, h * IDX_DIM:(h + 1) * IDX_DIM] for h in range(IDX_HEADS)], axis=0),
                    ki_scr[...])
    score = jnp.zeros((tq, L), F32)
    for h in range(IDX_HEADS):
        score = score + jnp.maximum(d_all[h * tq:(h + 1) * tq], 0.0) * wis[:, IDX_DIM + h:IDX_DIM + h + 1]
    key_scr[...] = _score_key(jnp.where(adm, score, NEG_INF))
    keep = _select_topk(key_scr, jstar_scr, kpos, topk, axis=1)
    bias_scr[...] = jnp.where(adm & keep, 0.0, NEG_INF)

    def probs(lg):
        lg = lg + bias_scr[...]
        return jnp.exp2(lg - jnp.max(lg, axis=1, keepdims=True)).astype(BF16)

    lg_all = _dot_nt(jnp.concatenate([q_ref[h] for h in range(ATTN_HEADS)], axis=0), k_scr[...])
    p_all = jnp.concatenate([probs(lg_all[h * tq:(h + 1) * tq]) for h in range(ATTN_HEADS)], axis=0)
    pv_all = _dot(p_all, v_scr[...])
    for h in range(ATTN_HEADS):
        pv = pv_all[h * tq:(h + 1) * tq]
        o_scr[:, h * ATTN_HEAD_DIM:(h + 1) * ATTN_HEAD_DIM] = (
            pv[:, :ATTN_HEAD_DIM] / pv[:, ATTN_HEAD_DIM:ATTN_HEAD_DIM + 1]).astype(BF16)
    o_ref[...] = x_ref[...] + _dot(o_scr[...], wo_ref[...])


def _attn_decode_layer(x, q, qi, kis, kib, kb, vb, cache, wo_bf, topk):
    B, T, _ = x.shape
    ck, cv, cki = cache
    P = ck.shape[1]
    L = ((P + T + LANES - 1) // LANES) * LANES
    assert T * max(IDX_HEADS, ATTN_HEADS) <= MXU_DIM, "decode block: all heads of all new tokens in one row tile"
    assert P % 16 == 0 and T % 16 == 0, "bf16 row tiles"
    QI_W = IDX_HEADS * IDX_DIM
    seq = lambda *shape: pl.BlockSpec((None,) + shape, lambda b: (b,) + (0,) * len(shape))
    return pl.pallas_call(
        functools.partial(_attn_decode_kernel, tq=T, P=P, L=L, topk=topk),
        grid=(B,),
        in_specs=[
            seq(T, D_MODEL), seq(ATTN_HEADS, T, ATTN_HEAD_DIM), seq(T, QI_W), seq(T, LANES),
            seq(T, IDX_DIM), seq(T, ATTN_HEAD_DIM), seq(T, 2 * ATTN_HEAD_DIM),
            seq(P, IDX_DIM), seq(P, ATTN_HEAD_DIM), seq(P, ATTN_HEAD_DIM),
            _const_spec((Q_W, D_MODEL)),
        ],
        out_specs=seq(T, D_MODEL),
        out_shape=SDS((B, T, D_MODEL), F32),
        scratch_shapes=[
            pltpu.VMEM((L, IDX_DIM), BF16),
            pltpu.VMEM((L, ATTN_HEAD_DIM), BF16),
            pltpu.VMEM((L, 2 * ATTN_HEAD_DIM), BF16),
            pltpu.VMEM((T, L), jnp.int32),
            pltpu.VMEM((T, L), F32),
            pltpu.VMEM((T, 1), jnp.int32),
            pltpu.VMEM((T, Q_W), BF16),
        ],
        compiler_params=_cparams(("arbitrary",)),
        name="attn_decode",
    )(x, q, qi, kis, kib, kb, vb, cki, ck, cv, wo_bf)


def _attn_pipe_kernel(x_ref, q_ref, qi_ref, wi_ref, ki_ref, k_ref, v_ref, wo_ref, o_ref,
                      key_scr, bias_scr, lg_scr, jstar_scr, o_scr, *, tq, L, q_off, n_keys, topk):
    s = pl.program_id(0)
    cur = s % 2

    @pl.when(s == 0)
    def _():
        bias_scr[...] = jnp.zeros(bias_scr.shape, F32)

    qpos = lax.broadcasted_iota(jnp.int32, (tq, L), 0) + q_off
    kpos = lax.broadcasted_iota(jnp.int32, (tq, L), 1)
    adm = (kpos < n_keys) & ((kpos // CHUNK) <= (qpos // CHUNK))

    select_all = L <= topk
    if not select_all:
        qi = qi_ref[...]
        ki = ki_ref[...]
        wis = wi_ref[...] * (IDX_DIM ** -0.5 * IDX_HEADS ** -0.5)
        score = jnp.zeros((tq, L), F32)
        for h in range(IDX_HEADS):
            d = _dot_nt(qi[:, h * IDX_DIM:(h + 1) * IDX_DIM], ki)
            score = score + jnp.maximum(d, 0.0) * wis[:, IDX_DIM + h:IDX_DIM + h + 1]
        key_scr[...] = _score_key(jnp.where(adm, score, NEG_INF))

    steps_per_head = KEY_BITS // ATTN_HEADS

    lg_scr[0] = _dot_nt(q_ref[0], k_ref[...])

    def head(h, lo_u):
        if select_all:
            step = lambda i, lo: lo
        else:
            step = lambda i, lo: _kth_key_step(key_scr, lo, h * steps_per_head + i, topk, 1)
        lo_u = step(0, lo_u)
        lg = lg_scr[h % 2] + bias_scr[1 - cur]
        lg_scr[(h + 1) % 2] = _dot_nt(q_ref[jnp.minimum(h + 1, ATTN_HEADS - 1)], k_ref[...])
        lo_u = step(1, lo_u)
        p = jnp.exp2(lg - jnp.max(lg, axis=1, keepdims=True)).astype(BF16)
        lo_u = step(2, lo_u)
        pv = _dot(p, v_ref[...])
        for i in range(3, steps_per_head):
            lo_u = step(i, lo_u)
        o_scr[h] = (pv[:, :ATTN_HEAD_DIM] / pv[:, ATTN_HEAD_DIM:ATTN_HEAD_DIM + 1]).astype(BF16)
        return lo_u

    lo_u = lax.fori_loop(0, ATTN_HEADS, head, jnp.zeros((tq, 1), jnp.int32))
    keep = adm if select_all else adm & _finish_topk(key_scr, jstar_scr, kpos, lo_u, topk, 1)
    bias_scr[cur] = jnp.where(keep, 0.0, NEG_INF)

    attn = _dot(jnp.concatenate([o_scr[h] for h in range(ATTN_HEADS)], axis=1), wo_ref[...])
    o_ref[...] = x_ref[...] + jnp.where(s > 0, attn, 0.0)


def _attn_pipe_layer(x, q, qi, kis, ki_all, k_all, v_all, wo_bf, q_off, n_keys, topk, tq, j0, L):
    B, T, _ = x.shape
    QI_W = IDX_HEADS * IDX_DIM
    prev = lambda s: jnp.maximum(s - 1, 0)
    this = lambda s: jnp.minimum(s, B - 1)
    return pl.pallas_call(
        functools.partial(_attn_pipe_kernel, tq=tq, L=L, q_off=q_off + j0 * tq, n_keys=n_keys, topk=topk),
        grid=(B + 1,),
        in_specs=[
            pl.BlockSpec((None, tq, D_MODEL), lambda s: (prev(s), j0, 0)),
            pl.BlockSpec((None, ATTN_HEADS, tq, ATTN_HEAD_DIM), lambda s: (prev(s), 0, j0, 0)),
            pl.BlockSpec((None, tq, QI_W), lambda s: (this(s), j0, 0)),
            pl.BlockSpec((None, tq, LANES), lambda s: (this(s), j0, 0)),
            pl.BlockSpec((None, L, IDX_DIM), lambda s: (this(s), 0, 0)),
            pl.BlockSpec((None, L, ATTN_HEAD_DIM), lambda s: (prev(s), 0, 0)),
            pl.BlockSpec((None, L, 2 * ATTN_HEAD_DIM), lambda s: (prev(s), 0, 0)),
            _const_spec((Q_W, D_MODEL)),
        ],
        out_specs=pl.BlockSpec((None, tq, D_MODEL), lambda s: (prev(s), j0, 0)),
        out_shape=SDS((B, T, D_MODEL), F32),
        input_output_aliases={0: 0},
        scratch_shapes=[
            pltpu.VMEM((tq, L), jnp.int32),
            pltpu.VMEM((2, tq, L), F32),
            pltpu.VMEM((2, tq, L), F32),
            pltpu.VMEM((tq, 1), jnp.int32),
            pltpu.VMEM((ATTN_HEADS, tq, ATTN_HEAD_DIM), BF16),
        ],
        compiler_params=_cparams(("arbitrary",)),
        name="attn_pipe",
    )(x, q, qi, kis, ki_all, k_all, v_all, wo_bf)


def _rproj_kernel(x_ref, xp_ref, sh_ref, g_ref, mix_ref, w0_ref, w1_ref, w2_ref, a0_ref, a1_ref, a2_ref,
                  g1_ref, g2_ref, kk_ref, ka_ref, wr_ref, wk_ref, wv_ref, bd_ref,
                  r_o, lw_o, k_o, v_o, na_o, b_o, g_o, hl_o, *, tt):
    i = pl.program_id(1)
    g = g_ref[...]
    h = _rms(x_ref[...], g)
    hl_o[...] = h[tt - SUBLANES:, :]
    prev = jnp.where(i == 0, sh_ref[...], _rms(xp_ref[SUBLANES - 1:SUBLANES, :], g))
    row = lax.broadcasted_iota(jnp.int32, (tt, D_MODEL), 0)
    xx = jnp.where(row == 0, prev, pltpu.roll(h, 1, 0)) - h
    lerp = lambda n: (h + xx * mix_ref[n:n + 1, :]).astype(BF16)
    r = _dot(lerp(0), wr_ref[...])
    wl = w0_ref[...] + _dot(jnp.tanh(_dot(lerp(1), w1_ref[...])).astype(BF16), w2_ref[...])
    lw = -_sigmoid(wl) * math.exp(-0.5)
    k = _dot(lerp(2), wk_ref[...])
    v = _dot(lerp(3), wv_ref[...])
    a = _sigmoid(a0_ref[...] + _dot(_dot(lerp(4), a1_ref[...]).astype(BF16), a2_ref[...]))
    gate = _dot(_sigmoid(_dot(lerp(5), g1_ref[...])).astype(BF16), g2_ref[...])
    kk = k * kk_ref[...]
    kk = kk * lax.rsqrt(jnp.maximum(_head_sum(kk * kk, bd_ref), 1e-24))
    r_o[...] = r
    lw_o[...] = lw
    k_o[...] = k * (1.0 + (a - 1.0) * ka_ref[...])
    v_o[...] = v
    na_o[...] = -kk
    b_o[...] = kk * a
    g_o[...] = gate


def _rwkv_project(x, shift_prev, g, rw):
    B, T, _ = x.shape
    tt = _pick_tile(T, 256)
    rb = tt // SUBLANES
    tok = pl.BlockSpec((None, tt, D_MODEL), lambda b, i: (b, i, 0))
    consts = [g, rw["mix"], rw["w0"], rw["w1"], rw["w2"], rw["a0"], rw["a1"], rw["a2"], rw["g1"], rw["g2"],
              rw["k_k"], rw["k_a"], rw["w_r"], rw["w_k"], rw["w_v"], rw["bd"]]
    return pl.pallas_call(
        functools.partial(_rproj_kernel, tt=tt),
        grid=(B, T // tt),
        in_specs=[
            tok,
            pl.BlockSpec((None, SUBLANES, D_MODEL), lambda b, i: (b, jnp.maximum(i * rb - 1, 0), 0)),
            pl.BlockSpec((None, 1, D_MODEL), lambda b, i: (b, 0, 0)),
        ] + [_const_spec(c.shape) for c in consts],
        out_specs=[tok] * 7 + [pl.BlockSpec((None, SUBLANES, D_MODEL), lambda b, i: (b, 0, 0))],
        out_shape=[SDS((B, T, D_MODEL), F32)] * 7 + [SDS((B, SUBLANES, D_MODEL), F32)],
        compiler_params=_cparams(("parallel", "arbitrary")),
        name="rwkv_proj",
    )(x, x, shift_prev, *consts)


def _scan_kernel(r_ref, lw_ref, k_ref, v_ref, a_ref, b_ref, g_ref, s0_ref, rk_ref, lg_ref, lb_ref,
                 y_ref, st_ref, s_scr, *, nb, tt, C):
    N = RWKV_HEAD
    assert C == N and 2 * N == LANES

    hp = LANES // N
    zero = jnp.zeros((N, N), F32)

    @pl.when(pl.program_id(1) == 0)
    def _():
        for s in range(nb):
            for p in range(D_MODEL // LANES):
                blocks = [jnp.concatenate([s0_ref[s, hp * p + h] if g == h else zero for g in range(hp)], axis=1)
                          for h in range(hp)]
                s_scr[s, p] = jnp.concatenate(blocks, axis=0).T

    row_w = lax.broadcasted_iota(jnp.int32, (C, D_MODEL), 0)
    lane = lax.broadcasted_iota(jnp.int32, (C, LANES), 1)
    h0 = lane < N
    r1 = lax.broadcasted_iota(jnp.int32, (C, 2 * C), 0)
    c1 = lax.broadcasted_iota(jnp.int32, (C, 2 * C), 1) % C
    strict = r1 > c1
    r2 = lax.broadcasted_iota(jnp.int32, (C, 4 * C), 0)
    c2 = lax.broadcasted_iota(jnp.int32, (C, 4 * C), 1) % C
    incl = r2 >= c2
    rs = lax.broadcasted_iota(jnp.int32, (LANES, LANES), 0)
    cs = lax.broadcasted_iota(jnp.int32, (LANES, LANES), 1)
    same_head = (rs < N) == (cs < N)
    eye = rs == cs

    def split(x):
        return jnp.concatenate([jnp.where(h0, x, 0.0), jnp.where(h0, 0.0, x)], axis=0)

    def chunk(c, carry):
        sl = pl.ds(pl.multiple_of(c * C, C), C)
        n_pairs = D_MODEL // LANES
        at, rt, bt, kt, vv, wc = [], [], [], [], [], []
        for s in range(nb):
            lw = lw_ref[s, sl, :]
            cum = lw
            d = 1
            while d < C:
                cum = cum + jnp.where(row_w >= d, pltpu.roll(cum, d, 0), 0.0)
                d *= 2
            e_w = jnp.exp(cum)
            e_n = jnp.exp(-cum)
            rows = (a_ref[s, sl, :] * jnp.exp(cum - lw), r_ref[s, sl, :] * e_w, b_ref[s, sl, :] * e_n,
                    k_ref[s, sl, :] * e_n, v_ref[s, sl, :], e_w[C - 1:C, :])
            for dst, src in zip((at, rt, bt, kt, vv, wc), rows):
                dst.extend(src[:, p * LANES:(p + 1) * LANES] for p in range(n_pairs))
        pairs = range(nb * n_pairs)
        st = [s_scr[p // n_pairs, p % n_pairs] for p in pairs]
        ar = [jnp.concatenate([at[p], rt[p]], axis=0).astype(BF16) for p in pairs]
        v2 = [split(vv[p]).astype(BF16) for p in pairs]
        g = [_dot_nt(ar[p], jnp.concatenate([split(bt[p]), split(kt[p])], axis=0).astype(BF16)) for p in pairs]
        hm = [_dot(ar[p], st[p].astype(BF16)) for p in pairs]
        pw = [jnp.where(strict, g[p][:C, :2 * C], 0.0) for p in pairs]
        u = [hm[p][:C] + _dot(jnp.where(strict, g[p][:C, 2 * C:], 0.0).astype(BF16), v2[p]) for p in pairs]
        n = 1
        while n < C:
            pb = [pw[p].astype(BF16) for p in pairs]
            u = [u[p] + _dot(pb[p], split(u[p]).astype(BF16)) for p in pairs]
            n *= 2
            if n < C:
                pw = [_dot(pb[p], split(pw[p]).astype(BF16)) for p in pairs]
        ys = [hm[p][C:] + _dot(jnp.where(incl, g[p][C:, :], 0.0).astype(BF16),
                               jnp.concatenate([split(u[p]).astype(BF16), v2[p]], axis=0)) for p in pairs]
        for p in pairs:
            bk = jnp.concatenate([bt[p], kt[p]], axis=0).astype(BF16)
            uvp = jnp.concatenate([u[p], vv[p]], axis=0).astype(BF16)
            upd = jnp.where(same_head, _dot_tn(bk, uvp), 0.0)
            w_col = jnp.sum(jnp.where(eye, wc[p], 0.0), axis=1, keepdims=True)
            s_scr[p // n_pairs, p % n_pairs] = (st[p] + upd) * w_col
        inv_n = 1.0 / N

        def head_mean(z):
            s0 = jnp.sum(jnp.where(h0, z, 0.0), axis=1, keepdims=True)
            s1 = jnp.sum(jnp.where(h0, 0.0, z), axis=1, keepdims=True)
            return jnp.where(h0, s0, s1) * inv_n

        for s in range(nb):
            zs = []
            for q in range(n_pairs):
                p = s * n_pairs + q
                ps = slice(q * LANES, (q + 1) * LANES)
                yc = ys[p] - head_mean(ys[p])
                yn = yc * lax.rsqrt(head_mean(yc * yc) + LNX_EPS) * lg_ref[:, ps] + lb_ref[:, ps]
                rk = r_ref[s, sl, ps] * k_ref[s, sl, ps] * rk_ref[:, ps]
                zs.append((yn + head_mean(rk) * N * vv[p]) * g_ref[s, sl, ps])
            y_ref[s, sl, :] = jnp.concatenate(zs, axis=1).astype(BF16)
        return carry

    lax.fori_loop(0, tt // C, chunk, 0)

    @pl.when(pl.program_id(1) == pl.num_programs(1) - 1)
    def _():
        for s in range(nb):
            for p in range(D_MODEL // LANES):
                t = s_scr[s, p].T
                for h in range(hp):
                    st_ref[s, hp * p + h] = t[h * N:(h + 1) * N, h * N:(h + 1) * N]


def _rwkv_scan(r, lw, k, v, na, b, gate, wkv0, rw):
    B, T, _ = r.shape
    C = SCAN_CHUNK
    t_pad = ((T + C - 1) // C) * C
    seqs = (r, lw, k, v, na, b, gate)
    if t_pad != T:
        seqs = tuple(jnp.pad(a, ((0, 0), (0, t_pad - T), (0, 0))) for a in seqs)
    tt = _pick_tile(t_pad, 128)
    nb = _pick_tile(B, SCAN_SEQS)
    n_pairs = D_MODEL // LANES
    tok = pl.BlockSpec((nb, tt, D_MODEL), lambda b_, i: (b_, i, 0))
    st = pl.BlockSpec((nb, RWKV_HEADS, RWKV_HEAD, RWKV_HEAD), lambda b_, i: (b_, 0, 0, 0))
    consts = [rw["r_k"], rw["lnx_g"], rw["lnx_b"]]
    y, s_t = pl.pallas_call(
        functools.partial(_scan_kernel, nb=nb, tt=tt, C=C),
        grid=(B // nb, t_pad // tt),
        in_specs=[tok] * 7 + [st] + [_const_spec(c.shape) for c in consts],
        out_specs=[tok, st],
        out_shape=[SDS((B, t_pad, D_MODEL), BF16), SDS((B, RWKV_HEADS, RWKV_HEAD, RWKV_HEAD), F32)],
        scratch_shapes=[pltpu.VMEM((nb, n_pairs, LANES, LANES), F32)],
        compiler_params=_cparams(("parallel", "arbitrary")),
        name="rwkv_scan",
    )(*seqs, wkv0, *consts)
    return (y if t_pad == T else y[:, :T]), s_t


def _wo_mlp_kernel(x_ref, z_ref, wo_ref, g_ref, wu_ref, wd_ref, gf_ref, o_ref, *, final_norm):
    x = x_ref[...] + _dot(z_ref[...], wo_ref[...])
    o_ref[...] = _mlp_apply(x, g_ref, wu_ref, wd_ref, gf_ref, final_norm)


def _wo_mlp(x, z, wo, mlp_args, final_norm):
    M = x.shape[0]
    tm = _pick_tile(M, 512)
    tok = pl.BlockSpec((tm, D_MODEL), lambda i: (i, 0))
    return pl.pallas_call(
        functools.partial(_wo_mlp_kernel, final_norm=final_norm),
        grid=(M // tm,),
        in_specs=[tok, tok, _const_spec(wo.shape)] + _mlp_specs(mlp_args[4]),
        out_specs=tok,
        out_shape=SDS((M, D_MODEL), F32),
        compiler_params=_cparams(("parallel",)),
        name="wo_mlp",
    )(x, z, wo, *mlp_args[:4])


def _pool_block(x, state, n_hist, g, w_bf, scale, mlp_args=(), final_norm=False):
    B = x.shape[0]
    if state is None:
        hist = jnp.zeros((B, POOL_PAD, D_MODEL), F32)
    else:
        hist = jnp.pad(state, ((0, 0), (POOL_PAD - POOL_HIST, 0), (0, 0)))
    out, hs = _pool_layer(x, hist, n_hist, g, w_bf, scale, mlp_args, final_norm)
    return out, hs[:, POOL_PAD - POOL_HIST:]


def _attn_block(x, cache, g, w_in_pad, wo_bf):
    B, T, _ = x.shape
    past = 0 if cache is None else cache[0].shape[1]
    tab = _rope_tables(past, T)
    q, qi, kis, k_new, v_new, ki_new, kb, vb, kib = _attn_project(x, g, w_in_pad, tab)
    topk = min(TOPK_MAX, (past + T) // 4)
    if cache is not None:
        return _attn_decode_layer(x, q, qi, kis, kib, kb, vb, cache, wo_bf, topk), k_new, v_new, ki_new
    tq = _pick_tile(T, ATTN_Q_BLOCK)
    out = x
    for j0 in range(T // tq):
        l_g = (j0 + 1) * tq
        out = _attn_pipe_layer(out, q, qi, kis, kib, kb, vb, wo_bf, 0, l_g, topk, tq, j0, l_g)
    return out, k_new, v_new, ki_new


def _rwkv_block(x, shift_prev, wkv0, g, rw, mlp_args, final_norm):
    B, T, _ = x.shape
    r, lw, k, v, na, b, gate, hl = _rwkv_project(x, shift_prev, g, rw)
    z, s_t = _rwkv_scan(r, lw, k, v, na, b, gate, wkv0, rw)
    flat = lambda a: a.reshape(B * T, D_MODEL)
    out = _wo_mlp(flat(x), flat(z), rw["w_o"], mlp_args, final_norm).reshape(B, T, D_MODEL)
    return out, hl[:, SUBLANES - 1:], s_t


def kernel(x_prompt, x_sample, state_pool, cache_k, cache_v, cache_kidx, state_shift, state_wkv, ln1_g, ln2_g, w_up, w_down, ln_f_g, pool_w, pool_scale, attn_w_in, attn_w_out, rwkv_mix, rwkv_w0, rwkv_w1, rwkv_w2, rwkv_a0, rwkv_a1, rwkv_a2, rwkv_g1, rwkv_g2, rwkv_k_k, rwkv_k_a, rwkv_r_k, rwkv_w_r, rwkv_w_k, rwkv_w_v, rwkv_w_o, rwkv_lnx_g, rwkv_lnx_b):
    xp, xs = x_prompt, x_sample
    bp, sp, _ = xp.shape
    bs, ss, _ = xs.shape
    past = cache_k.shape[2]
    row = lambda a: a.reshape(1, -1)
    bf = lambda a: a.astype(BF16)
    wu_all, wd_all = bf(w_up), bf(w_down)
    head_of = jnp.arange(MXU_DIM) // RWKV_HEAD
    bd_mat = (head_of[:, None] == head_of[None, :]).astype(BF16)
    outs = {n: [] for n in ("pool_p", "pool_s", "k_p", "k_s", "v_p", "v_s", "ki_p", "ki_s",
                            "sh_p", "sh_s", "wkv_p", "wkv_s")}
    for i in range(DEPTH):
        j = i // N_MIXERS
        g1 = row(ln1_g[i])
        last = i == DEPTH - 1
        mlp_args = (row(ln2_g[i]), wu_all, wd_all, row(ln_f_g), i)
        prompt_mlp_done = sample_mlp_done = False
        if i % N_MIXERS == 0:
            w_bf = bf(pool_w[j])
            sc = row(pool_scale[j])
            prompt_mlp_done = sp >= POOL_MLP_MIN_ROWS
            xp, st_p = _pool_block(xp, None, 0, g1, w_bf, sc, mlp_args if prompt_mlp_done else (), last)
            xs, st_s = _pool_block(xs, state_pool[j], past, g1, w_bf, sc)
            outs["pool_p"].append(st_p)
            outs["pool_s"].append(st_s)
        elif i % N_MIXERS == 1:
            w_in_pad = jnp.pad(bf(attn_w_in[j]), ((0, 0), (0, ATTN_IN_PAD - ATTN_IN_W)))
            wo_bf = bf(attn_w_out[j])
            xp, kp, vp, kip = _attn_block(xp, None, g1, w_in_pad, wo_bf)
            xs, kn, vn, kin = _attn_block(xs, (cache_k[j], cache_v[j], cache_kidx[j]), g1, w_in_pad, wo_bf)
            for n, a in (("k_p", kp), ("v_p", vp), ("ki_p", kip), ("k_s", kn), ("v_s", vn), ("ki_s", kin)):
                outs[n].append(a)
        else:
            rw = dict(mix=rwkv_mix[j], w0=row(rwkv_w0[j]), w1=bf(rwkv_w1[j]), w2=bf(rwkv_w2[j]),
                      a0=row(rwkv_a0[j]), a1=bf(rwkv_a1[j]), a2=bf(rwkv_a2[j]), g1=bf(rwkv_g1[j]),
                      g2=bf(rwkv_g2[j]), k_k=row(rwkv_k_k[j]), k_a=row(rwkv_k_a[j]), r_k=row(rwkv_r_k[j]),
                      w_r=bf(rwkv_w_r[j]), w_k=bf(rwkv_w_k[j]), w_v=bf(rwkv_w_v[j]), w_o=bf(rwkv_w_o[j]),
                      lnx_g=row(rwkv_lnx_g[j]), lnx_b=row(rwkv_lnx_b[j]), bd=bd_mat)
            zero_shift = jnp.zeros((bp, 1, D_MODEL), F32)
            zero_wkv = jnp.zeros((bp, RWKV_HEADS, RWKV_HEAD, RWKV_HEAD), F32)
            xp, shp, wp = _rwkv_block(xp, zero_shift, zero_wkv, g1, rw, mlp_args, last)
            xs, shs, wsn = _rwkv_block(xs, state_shift[j], state_wkv[j], g1, rw, mlp_args, last)
            prompt_mlp_done = sample_mlp_done = True
            outs["sh_p"].append(shp)
            outs["sh_s"].append(shs)
            outs["wkv_p"].append(wp)
            outs["wkv_s"].append(wsn)
        if not prompt_mlp_done:
            xp = _mlp(xp.reshape(bp * sp, D_MODEL), mlp_args, last).reshape(bp, sp, D_MODEL)
        if not sample_mlp_done:
            xs = _mlp(xs.reshape(bs * ss, D_MODEL), mlp_args, last).reshape(bs, ss, D_MODEL)
    st = lambda n: outs[n][0][None] if len(outs[n]) == 1 else jnp.stack(outs[n], 0)
    return (xp, xs, st("pool_p"), st("pool_s"), st("k_p"), st("k_s"), st("v_p"), st("v_s"),
            st("ki_p"), st("ki_s"), st("sh_p"), st("sh_s"), st("wkv_p"), st("wkv_s"))
```

```python
import functools
import math

import jax
import jax.numpy as jnp
import numpy as np
from jax import lax
from jax.experimental import pallas as pl
from jax.experimental.pallas import tpu as pltpu

F32 = jnp.float32
BF16 = jnp.bfloat16
SDS = jax.ShapeDtypeStruct

D_MODEL = 1024
DEPTH = 4
N_MIXERS = 3
CHUNK = 64
D_FF = 4 * D_MODEL
RMS_EPS = 1e-6
POOL_WINDOWS = (2, 4, 8, 16)
POOL_GROUPS = 4
POOL_GW = D_MODEL // POOL_GROUPS
POOL_HIST = max(POOL_WINDOWS) - 1
POOL_PAD = POOL_HIST + 1
ATTN_HEADS = 8
ATTN_HEAD_DIM = D_MODEL // ATTN_HEADS
IDX_HEADS = 8
IDX_DIM = 64
TOPK_MAX = 256
ROPE_THETA = 500000.0
ROPE_FRACTION = 4
NEG_INF = -1e30
Q_W = ATTN_HEADS * ATTN_HEAD_DIM
OFF_K = Q_W
OFF_V = OFF_K + ATTN_HEAD_DIM
OFF_QI = OFF_V + ATTN_HEAD_DIM
OFF_KI = OFF_QI + IDX_HEADS * IDX_DIM
OFF_WI = OFF_KI + IDX_DIM
ATTN_IN_W = OFF_WI + IDX_HEADS
RWKV_HEAD = 64
RWKV_HEADS = D_MODEL // RWKV_HEAD
LNX_EPS = 64e-5

LANES = 128
SUBLANES = 8
ATTN_IN_PAD = ((ATTN_IN_W + LANES - 1) // LANES) * LANES
MXU_DIM = 256
VMEM_LIMIT = 56 * 1024 * 1024
INT_MIN = -2 ** 31
HALF_NEG_KEY = int(np.float32(0.5 * NEG_INF).view(np.int32)) ^ 0x7FFFFFFF
POOL_MLP_MIN_ROWS = 512
POOL_MLP_SUBTILES = 2
ATTN_Q_BLOCK = 256
APROJ_ROWS = 512
RPROJ_ROWS = 256
SCAN_CHUNK = 64
SCAN_SEQS = 2
FF_CHUNK = 1024


def _cparams(sem):
    return pltpu.CompilerParams(dimension_semantics=sem, vmem_limit_bytes=VMEM_LIMIT)


def _const_spec(shape):
    nd = len(shape)
    return pl.BlockSpec(shape, lambda *_: (0,) * nd, pipeline_mode=pl.Buffered(1))


def _rms(x, g):
    ms = jnp.mean(x * x, axis=-1, keepdims=True)
    return x * lax.rsqrt(ms + RMS_EPS) * g


def _dot(a, b):
    return jnp.dot(a, b, preferred_element_type=F32)


def _dot_nt(a, b):
    return lax.dot_general(a, b, (((1,), (1,)), ((), ())), preferred_element_type=F32)


def _dot_tn(a, b):
    return lax.dot_general(a, b, (((0,), (0,)), ((), ())), preferred_element_type=F32)


def _head_sum(z, bd_ref):
    bd = bd_ref[...]
    hi = z.astype(BF16)
    lo = (z - hi.astype(F32)).astype(BF16)
    outs = []
    for c in range(D_MODEL // MXU_DIM):
        cs = slice(c * MXU_DIM, (c + 1) * MXU_DIM)
        outs.append(_dot(hi[:, cs], bd) + _dot(lo[:, cs], bd))
    return jnp.concatenate(outs, axis=1)


def _sigmoid(x):
    return 1.0 / (1.0 + jnp.exp(-x))


def _pick_tile(n, pref):
    t = min(n, pref)
    assert n % t == 0, (n, t)
    return t


def _mlp_apply(x, g_ref, wu_ref, wd_ref, gf_ref, final_norm, side=()):
    h = _rms(x, g_ref[...]).astype(BF16)
    acc = x
    for j in range(D_FF // FF_CHUNK):
        u = _dot(h, wu_ref[:, j * FF_CHUNK:(j + 1) * FF_CHUNK])
        u = jnp.square(jnp.maximum(u, 0.0)).astype(BF16)
        acc = acc + _dot(u, wd_ref[j * FF_CHUNK:(j + 1) * FF_CHUNK, :])
        if j < len(side):
            side[j]()
    if final_norm:
        acc = _rms(acc, gf_ref[...])
    return acc


def _mlp_specs(layer):
    pick = lambda *_: (layer, 0, 0)
    return [_const_spec((1, D_MODEL)),
            pl.BlockSpec((None, D_MODEL, D_FF), pick, pipeline_mode=pl.Buffered(1)),
            pl.BlockSpec((None, D_FF, D_MODEL), pick, pipeline_mode=pl.Buffered(1)),
            _const_spec((1, D_MODEL))]


def _mlp_kernel(x_ref, g_ref, wu_ref, wd_ref, gf_ref, o_ref, *, final_norm):
    o_ref[...] = _mlp_apply(x_ref[...], g_ref, wu_ref, wd_ref, gf_ref, final_norm)


def _mlp(x, mlp_args, final_norm):
    M = x.shape[0]
    tm = _pick_tile(M, 512)
    return pl.pallas_call(
        functools.partial(_mlp_kernel, final_norm=final_norm),
        grid=(M // tm,),
        in_specs=[pl.BlockSpec((tm, D_MODEL), lambda i: (i, 0))] + _mlp_specs(mlp_args[4]),
        out_specs=pl.BlockSpec((tm, D_MODEL), lambda i: (i, 0)),
        out_shape=SDS((M, D_MODEL), F32),
        compiler_params=_cparams(("parallel",)),
        name="mlp",
    )(x, *mlp_args[:4])


def _pool_kernel(x_ref, xp_ref, hist_ref, g_ref, w_ref, sc_ref, *rest, tt, n_hist, mlp, final_norm):
    mlp_refs, (o_ref, hs_ref) = rest[:-2], rest[-2:]
    i = pl.program_id(1)
    g = g_ref[...]
    x = x_ref[...]
    h = _rms(x, g)
    prev = jnp.where(i == 0, hist_ref[...], _rms(xp_ref[...], g))
    hs_ref[...] = h[tt - POOL_PAD:, :]
    full = jnp.concatenate([prev, h], axis=0)
    nsub = POOL_MLP_SUBTILES if mlp else 1
    rows = tt // nsub

    def group(r, gi):
        win = POOL_WINDOWS[gi]
        rs = slice(r * rows, (r + 1) * rows)
        cs = slice(gi * POOL_GW, (gi + 1) * POOL_GW)
        s = full[r * rows:(r + 1) * rows + POOL_PAD, cs]
        d = 1
        while d < win:
            s = s + pltpu.roll(s, d, 0)
            d *= 2
        t1 = lax.broadcasted_iota(jnp.int32, (POOL_PAD, POOL_GW), 0) + (i * tt + r * rows + 1 + n_hist)
        cnt = jnp.concatenate([jnp.minimum(t1, win).astype(F32), jnp.full((rows - POOL_PAD, POOL_GW), win, F32)],
                              axis=0)
        pooled = s[POOL_PAD:, :] / cnt - h[rs, cs]
        return x[rs, cs] + _dot(pooled.astype(BF16), w_ref[gi]) * sc_ref[:, cs]

    cols = [group(0, gi) for gi in range(POOL_GROUPS)]
    if not mlp:
        o_ref[...] = jnp.concatenate(cols, axis=1)
        return
    outs = []
    for r in range(nsub):
        y = jnp.concatenate(cols, axis=1)
        cols = []
        side = [functools.partial(lambda gi, rn: cols.append(group(rn, gi)), gi, r + 1)
                for gi in range(POOL_GROUPS)] if r + 1 < nsub else []
        outs.append(_mlp_apply(y, *mlp_refs, final_norm, side))
    o_ref[...] = jnp.concatenate(outs, axis=0)


def _pool_layer(x, hist, n_hist, g, w_bf, scale, mlp_args, final_norm):
    B, T, _ = x.shape
    tt = _pick_tile(T, 512)
    rb = tt // POOL_PAD
    return pl.pallas_call(
        functools.partial(_pool_kernel, tt=tt, n_hist=n_hist, mlp=bool(mlp_args), final_norm=final_norm),
        grid=(B, T // tt),
        in_specs=[
            pl.BlockSpec((None, tt, D_MODEL), lambda b, i: (b, i, 0)),
            pl.BlockSpec((None, POOL_PAD, D_MODEL), lambda b, i: (b, jnp.maximum(i * rb - 1, 0), 0)),
            pl.BlockSpec((None, POOL_PAD, D_MODEL), lambda b, i: (b, 0, 0)),
            _const_spec((1, D_MODEL)),
            _const_spec((POOL_GROUPS, POOL_GW, POOL_GW)),
            _const_spec((1, D_MODEL)),
        ] + (_mlp_specs(mlp_args[4]) if mlp_args else []),
        out_specs=[
            pl.BlockSpec((None, tt, D_MODEL), lambda b, i: (b, i, 0)),
            pl.BlockSpec((None, POOL_PAD, D_MODEL), lambda b, i: (b, 0, 0)),
        ],
        out_shape=[SDS((B, T, D_MODEL), F32), SDS((B, POOL_PAD, D_MODEL), F32)],
        compiler_params=_cparams(("parallel", "arbitrary")),
        name="pool_mlp" if mlp_args else "pool",
    )(x, x, hist, g, w_bf, scale, *mlp_args[:4])


def _rope_tables(pos0, T):
    pos = np.arange(pos0, pos0 + T, dtype=np.float64)

    def head(d):
        rd = d // ROPE_FRACTION
        half = rd // 2
        inv = ROPE_THETA ** (-np.arange(half, dtype=np.float64) / half)
        ang = pos[:, None] * inv[None, :]
        cos, sin = np.cos(ang), np.sin(ang)
        z = lambda n: np.zeros((T, n))
        c = np.concatenate([cos, cos, np.ones((T, d - rd))], axis=1)
        sa = np.concatenate([-sin, z(d - half)], axis=1)
        sb = np.concatenate([z(half), sin, z(d - rd)], axis=1)
        return c, sa, sb

    qa = head(ATTN_HEAD_DIM)
    ia = head(IDX_DIM)
    i2 = tuple(np.concatenate([t, t], axis=1) for t in ia)
    ones, zeros = np.ones((T, IDX_DIM)), np.zeros((T, IDX_DIM))
    ik = (np.concatenate([ia[0], ones], axis=1), np.concatenate([ia[1], zeros], axis=1),
          np.concatenate([ia[2], zeros], axis=1))
    return jnp.asarray(np.concatenate(list(qa) + list(i2) + list(ik), axis=1), F32)


def _aproj_kernel(x_ref, g_ref, w_ref, tab_ref, q_ref, qi_ref, kis_ref, k_ref, v_ref, ki_ref, kb_ref, vb_ref, kib_ref,
                  *, nb, tt):
    h = _rms(x_ref[...].reshape(nb * tt, D_MODEL), g_ref[...]).astype(BF16)
    p = _dot(h, w_ref[...])

    def rope(x, kind, half):
        c = tab_ref[:, (3 * kind) * LANES:(3 * kind + 1) * LANES]
        sa = tab_ref[:, (3 * kind + 1) * LANES:(3 * kind + 2) * LANES]
        sb = tab_ref[:, (3 * kind + 2) * LANES:(3 * kind + 3) * LANES]
        return x * c + pltpu.roll(x, LANES - half, 1) * sa + pltpu.roll(x, half, 1) * sb

    def put(ref, val, *lead):
        for s in range(nb):
            ref[(s,) + lead] = val[s * tt:(s + 1) * tt]

    slab = lambda off: p[:, off:off + LANES]
    qh = ATTN_HEAD_DIM // ROPE_FRACTION // 2
    ih = IDX_DIM // ROPE_FRACTION // 2
    qscale = ATTN_HEAD_DIM ** -0.5 * math.log2(math.e)
    for hd in range(ATTN_HEADS):
        put(q_ref, (rope(slab(hd * LANES), 0, qh) * qscale).astype(BF16), hd)
    k = rope(slab(OFF_K), 0, qh)
    v = slab(OFF_V)
    put(qi_ref, jnp.concatenate([rope(slab(OFF_QI + c * LANES), 1, ih).astype(BF16)
                                 for c in range(IDX_HEADS * IDX_DIM // LANES)], axis=1))
    kis = rope(slab(OFF_KI), 2, ih)
    put(kis_ref, kis)
    put(k_ref, k)
    put(v_ref, v)
    put(ki_ref, kis[:, :IDX_DIM])
    put(kb_ref, k.astype(BF16))
    lane = lax.broadcasted_iota(jnp.int32, v.shape, 1)
    put(vb_ref, jnp.concatenate([v.astype(BF16), jnp.where(lane == 0, 1.0, 0.0).astype(BF16)], axis=1))
    put(kib_ref, kis[:, :IDX_DIM].astype(BF16))


def _attn_project(x, g, w_in_pad, tab):
    B, T, _ = x.shape
    tt = _pick_tile(T, APROJ_ROWS)
    nb = _pick_tile(B, max(1, APROJ_ROWS // tt))
    if nb > 1:
        tab = jnp.tile(tab, (nb, 1))
    QI_W = IDX_HEADS * IDX_DIM
    widths = (QI_W, LANES, ATTN_HEAD_DIM, ATTN_HEAD_DIM, IDX_DIM, ATTN_HEAD_DIM, 2 * ATTN_HEAD_DIM, IDX_DIM)
    dtypes = (BF16, F32, F32, F32, F32, BF16, BF16, BF16)
    q_spec = pl.BlockSpec((nb, ATTN_HEADS, tt, ATTN_HEAD_DIM), lambda i, b: (b, 0, i, 0))
    return pl.pallas_call(
        functools.partial(_aproj_kernel, nb=nb, tt=tt),
        grid=(T // tt, B // nb),
        in_specs=[
            pl.BlockSpec((nb, tt, D_MODEL), lambda i, b: (b, i, 0)),
            _const_spec((1, D_MODEL)),
            _const_spec((D_MODEL, ATTN_IN_PAD)),
            pl.BlockSpec((nb * tt, 9 * LANES), lambda i, b: (i, 0)),
        ],
        out_specs=[q_spec] + [pl.BlockSpec((nb, tt, w), lambda i, b: (b, i, 0)) for w in widths],
        out_shape=[SDS((B, ATTN_HEADS, T, ATTN_HEAD_DIM), BF16)] + [SDS((B, T, w), d) for w, d in zip(widths, dtypes)],
        compiler_params=_cparams(("arbitrary", "arbitrary")),
        name="attn_proj",
    )(x, g, w_in_pad, tab)


def _score_key(score):
    kb = pltpu.bitcast(score, jnp.int32)
    return jnp.where(kb >= 0, kb, kb ^ jnp.int32(0x7FFFFFFF))


KEY_BITS = 32


def _count(mask, axis):
    return jnp.sum(jnp.where(mask, 1.0, 0.0), axis=axis, keepdims=True)


def _kth_key_step(key_scr, lo_u, it, topk, axis):
    trial_u = lo_u | lax.shift_left(jnp.int32(1), KEY_BITS - 1 - it)
    c = _count(key_scr[...] >= (trial_u ^ jnp.int32(INT_MIN)), axis)
    return jnp.where(c >= topk, trial_u, lo_u)


def _select_topk(key_scr, jstar_scr, kpos, topk, axis):
    step = lambda it, lo_u: _kth_key_step(key_scr, lo_u, it, topk, axis)
    lo_u = lax.fori_loop(0, KEY_BITS, step, jnp.zeros(jstar_scr.shape, jnp.int32))
    return _finish_topk(key_scr, jstar_scr, kpos, lo_u, topk, axis)


def _finish_topk(key_scr, jstar_scr, kpos, lo_u, topk, axis):
    one = jstar_scr.shape
    n_idx = key_scr.shape[axis]
    count = functools.partial(_count, axis=axis)
    lo = lo_u ^ jnp.int32(INT_MIN)
    gt = key_scr[...] > lo
    eq = key_scr[...] == lo
    need = topk - count(gt)
    tie = (lo > jnp.int32(HALF_NEG_KEY)) & (count(eq) > need)
    jstar_scr[...] = jnp.full(one, n_idx, jnp.int32)

    @pl.when(jnp.max(jnp.where(tie, 1.0, 0.0)) > 0.0)
    def _():
        nbits = max(1, (n_idx - 1).bit_length())

        def ibody(it, lo_i):
            trial = lo_i + lax.shift_left(jnp.int32(1), nbits - 1 - it)
            c = count((kpos < trial) & (key_scr[...] == lo))
            return jnp.where(c < need, trial, lo_i)

        lo_i = lax.fori_loop(0, nbits, ibody, jnp.zeros(one, jnp.int32))
        jstar_scr[...] = jnp.where(tie, lo_i, n_idx)

    return gt | (eq & (kpos <= jstar_scr[...]))


def _attn_decode_kernel(x_ref, q_ref, qi_ref, wi_ref, kib_ref, kb_ref, vb_ref, cki_ref, ck_ref, cv_ref, wo_ref,
                        o_ref, ki_scr, k_scr, v_scr, key_scr, bias_scr, jstar_scr, o_scr, *, tq, P, L, topk):
    n_keys = P + tq

    @pl.when(pl.program_id(0) == 0)
    def _():
        lane = lax.broadcasted_iota(jnp.int32, (P, ATTN_HEAD_DIM), 1)
        v_scr[:P, ATTN_HEAD_DIM:] = jnp.where(lane == 0, 1.0, 0.0).astype(BF16)
        ki_scr[n_keys:, :] = jnp.zeros((L - n_keys, IDX_DIM), BF16)
        k_scr[n_keys:, :] = jnp.zeros((L - n_keys, ATTN_HEAD_DIM), BF16)
        v_scr[n_keys:, :] = jnp.zeros((L - n_keys, 2 * ATTN_HEAD_DIM), BF16)

    ki_scr[:P, :] = cki_ref[...].astype(BF16)
    ki_scr[P:n_keys, :] = kib_ref[...]
    k_scr[:P, :] = ck_ref[...].astype(BF16)
    k_scr[P:n_keys, :] = kb_ref[...]
    v_scr[:P, :ATTN_HEAD_DIM] = cv_ref[...].astype(BF16)
    v_scr[P:n_keys, :] = vb_ref[...]

    qpos = lax.broadcasted_iota(jnp.int32, (tq, L), 0) + P
    kpos = lax.broadcasted_iota(jnp.int32, (tq, L), 1)
    adm = (kpos < n_keys) & ((kpos // CHUNK) <= (qpos // CHUNK))

    qi = qi_ref[...]
    wis = wi_ref[...] * (IDX_DIM ** -0.5 * IDX_HEADS ** -0.5)
    d_all = _dot_nt(jnp.concatenate([qi[:, h * IDX_DIM:(h + 1) * IDX_DIM] for h in range(IDX_HEADS)], axis=0),
                    ki_scr[...])
    score = jnp.zeros((tq, L), F32)
    for h in range(IDX_HEADS):
        score = score + jnp.maximum(d_all[h * tq:(h + 1) * tq], 0.0) * wis[:, IDX_DIM + h:IDX_DIM + h + 1]
    key_scr[...] = _score_key(jnp.where(adm, score, NEG_INF))
    keep = _select_topk(key_scr, jstar_scr, kpos, topk, axis=1)
    bias_scr[...] = jnp.where(adm & keep, 0.0, NEG_INF)

    def probs(lg):
        lg = lg + bias_scr[...]
        return jnp.exp2(lg - jnp.max(lg, axis=1, keepdims=True)).astype(BF16)

    lg_all = _dot_nt(jnp.concatenate([q_ref[h] for h in range(ATTN_HEADS)], axis=0), k_scr[...])
    p_all = jnp.concatenate([probs(lg_all[h * tq:(h + 1) * tq]) for h in range(ATTN_HEADS)], axis=0)
    pv_all = _dot(p_all, v_scr[...])
    for h in range(ATTN_HEADS):
        pv = pv_all[h * tq:(h + 1) * tq]
        o_scr[:, h * ATTN_HEAD_DIM:(h + 1) * ATTN_HEAD_DIM] = (
            pv[:, :ATTN_HEAD_DIM] / pv[:, ATTN_HEAD_DIM:ATTN_HEAD_DIM + 1]).astype(BF16)
    o_ref[...] = x_ref[...] + _dot(o_scr[...], wo_ref[...])


def _attn_decode_layer(x, q, qi, kis, kib, kb, vb, cache, wo_bf, topk):
    B, T, _ = x.shape
    ck, cv, cki, layer = cache
    P = ck.shape[2]
    past = lambda w: pl.BlockSpec((None, None, P, w), lambda b: (layer, b, 0, 0))
    L = ((P + T + LANES - 1) // LANES) * LANES
    assert T * max(IDX_HEADS, ATTN_HEADS) <= MXU_DIM, "decode block: all heads of all new tokens in one row tile"
    assert P % 16 == 0 and T % 16 == 0, "bf16 row tiles"
    QI_W = IDX_HEADS * IDX_DIM
    seq = lambda *shape: pl.BlockSpec((None,) + shape, lambda b: (b,) + (0,) * len(shape))
    return pl.pallas_call(
        functools.partial(_attn_decode_kernel, tq=T, P=P, L=L, topk=topk),
        grid=(B,),
        in_specs=[
            seq(T, D_MODEL), seq(ATTN_HEADS, T, ATTN_HEAD_DIM), seq(T, QI_W), seq(T, LANES),
            seq(T, IDX_DIM), seq(T, ATTN_HEAD_DIM), seq(T, 2 * ATTN_HEAD_DIM),
            past(IDX_DIM), past(ATTN_HEAD_DIM), past(ATTN_HEAD_DIM),
            _const_spec((Q_W, D_MODEL)),
        ],
        out_specs=seq(T, D_MODEL),
        out_shape=SDS((B, T, D_MODEL), F32),
        scratch_shapes=[
            pltpu.VMEM((L, IDX_DIM), BF16),
            pltpu.VMEM((L, ATTN_HEAD_DIM), BF16),
            pltpu.VMEM((L, 2 * ATTN_HEAD_DIM), BF16),
            pltpu.VMEM((T, L), jnp.int32),
            pltpu.VMEM((T, L), F32),
            pltpu.VMEM((T, 1), jnp.int32),
            pltpu.VMEM((T, Q_W), BF16),
        ],
        compiler_params=_cparams(("arbitrary",)),
        name="attn_decode",
    )(x, q, qi, kis, kib, kb, vb, cki, ck, cv, wo_bf)


def _attn_pipe_kernel(x_ref, q_ref, qi_ref, wi_ref, ki_ref, k_ref, v_ref, wo_ref, o_ref,
                      key_scr, bias_scr, lg_scr, jstar_scr, o_scr, *, tq, L, q_off, n_keys, topk):
    s = pl.program_id(0)
    cur = s % 2

    @pl.when(s == 0)
    def _():
        bias_scr[...] = jnp.zeros(bias_scr.shape, F32)

    qpos = lax.broadcasted_iota(jnp.int32, (tq, L), 0) + q_off
    kpos = lax.broadcasted_iota(jnp.int32, (tq, L), 1)
    adm = (kpos < n_keys) & ((kpos // CHUNK) <= (qpos // CHUNK))

    select_all = L <= topk
    if not select_all:
        qi = qi_ref[...]
        ki = ki_ref[...]
        wis = wi_ref[...] * (IDX_DIM ** -0.5 * IDX_HEADS ** -0.5)
        score = jnp.zeros((tq, L), F32)
        for h in range(IDX_HEADS):
            d = _dot_nt(qi[:, h * IDX_DIM:(h + 1) * IDX_DIM], ki)
            score = score + jnp.maximum(d, 0.0) * wis[:, IDX_DIM + h:IDX_DIM + h + 1]
        key_scr[...] = _score_key(jnp.where(adm, score, NEG_INF))

    steps_per_head = KEY_BITS // ATTN_HEADS

    lg_scr[0] = _dot_nt(q_ref[0], k_ref[...])

    def head(h, lo_u):
        if select_all:
            step = lambda i, lo: lo
        else:
            step = lambda i, lo: _kth_key_step(key_scr, lo, h * steps_per_head + i, topk, 1)
        lo_u = step(0, lo_u)
        lg = lg_scr[h % 2] + bias_scr[1 - cur]
        lg_scr[(h + 1) % 2] = _dot_nt(q_ref[jnp.minimum(h + 1, ATTN_HEADS - 1)], k_ref[...])
        lo_u = step(1, lo_u)
        p = jnp.exp2(lg - jnp.max(lg, axis=1, keepdims=True)).astype(BF16)
        lo_u = step(2, lo_u)
        pv = _dot(p, v_ref[...])
        for i in range(3, steps_per_head):
            lo_u = step(i, lo_u)
        o_scr[h] = (pv[:, :ATTN_HEAD_DIM] / pv[:, ATTN_HEAD_DIM:ATTN_HEAD_DIM + 1]).astype(BF16)
        return lo_u

    lo_u = lax.fori_loop(0, ATTN_HEADS, head, jnp.zeros((tq, 1), jnp.int32))
    attn = _dot(jnp.concatenate([o_scr[h] for h in range(ATTN_HEADS)], axis=1), wo_ref[...])
    o_ref[...] = x_ref[...] + jnp.where(s > 0, attn, 0.0)

    keep = adm if select_all else adm & _finish_topk(key_scr, jstar_scr, kpos, lo_u, topk, 1)
    bias_scr[cur] = jnp.where(keep, 0.0, NEG_INF)


def _attn_pipe_layer(x, q, qi, kis, ki_all, k_all, v_all, wo_bf, q_off, n_keys, topk, tq, j0, L):
    B, T, _ = x.shape
    QI_W = IDX_HEADS * IDX_DIM
    prev = lambda s: jnp.maximum(s - 1, 0)
    this = lambda s: jnp.minimum(s, B - 1)
    return pl.pallas_call(
        functools.partial(_attn_pipe_kernel, tq=tq, L=L, q_off=q_off + j0 * tq, n_keys=n_keys, topk=topk),
        grid=(B + 1,),
        in_specs=[
            pl.BlockSpec((None, tq, D_MODEL), lambda s: (prev(s), j0, 0)),
            pl.BlockSpec((None, ATTN_HEADS, tq, ATTN_HEAD_DIM), lambda s: (prev(s), 0, j0, 0)),
            pl.BlockSpec((None, tq, QI_W), lambda s: (this(s), j0, 0)),
            pl.BlockSpec((None, tq, LANES), lambda s: (this(s), j0, 0)),
            pl.BlockSpec((None, L, IDX_DIM), lambda s: (this(s), 0, 0)),
            pl.BlockSpec((None, L, ATTN_HEAD_DIM), lambda s: (prev(s), 0, 0)),
            pl.BlockSpec((None, L, 2 * ATTN_HEAD_DIM), lambda s: (prev(s), 0, 0)),
            _const_spec((Q_W, D_MODEL)),
        ],
        out_specs=pl.BlockSpec((None, tq, D_MODEL), lambda s: (prev(s), j0, 0)),
        out_shape=SDS((B, T, D_MODEL), F32),
        input_output_aliases={0: 0},
        scratch_shapes=[
            pltpu.VMEM((tq, L), jnp.int32),
            pltpu.VMEM((2, tq, L), F32),
            pltpu.VMEM((2, tq, L), F32),
            pltpu.VMEM((tq, 1), jnp.int32),
            pltpu.VMEM((ATTN_HEADS, tq, ATTN_HEAD_DIM), BF16),
        ],
        compiler_params=_cparams(("arbitrary",)),
        name="attn_pipe",
    )(x, q, qi, kis, ki_all, k_all, v_all, wo_bf)


def _rproj_kernel(x_ref, xp_ref, sh_ref, g_ref, mix_ref, w0_ref, w1_ref, w2_ref, a0_ref, a1_ref, a2_ref,
                  g1_ref, g2_ref, kk_ref, ka_ref, wr_ref, wk_ref, wv_ref, bd_ref,
                  r_o, lw_o, k_o, v_o, na_o, b_o, g_o, hl_o, *, nb, tt):
    i = pl.program_id(1)
    g = g_ref[...]
    row = lax.broadcasted_iota(jnp.int32, (tt, D_MODEL), 0)
    hs, xxs = [], []
    for s in range(nb):
        h_s = _rms(x_ref[s], g)
        hl_o[s] = h_s[tt - SUBLANES:, :]
        prev = jnp.where(i == 0, sh_ref[s], _rms(xp_ref[s, SUBLANES - 1:SUBLANES, :], g))
        hs.append(h_s)
        xxs.append(jnp.where(row == 0, prev, pltpu.roll(h_s, 1, 0)) - h_s)
    h = jnp.concatenate(hs, axis=0)
    xx = jnp.concatenate(xxs, axis=0)
    lerp = lambda n: (h + xx * mix_ref[n:n + 1, :]).astype(BF16)
    r = _dot(lerp(0), wr_ref[...])
    wl = w0_ref[...] + _dot(jnp.tanh(_dot(lerp(1), w1_ref[...])).astype(BF16), w2_ref[...])
    lw = -_sigmoid(wl) * math.exp(-0.5)
    k = _dot(lerp(2), wk_ref[...])
    v = _dot(lerp(3), wv_ref[...])
    a = _sigmoid(a0_ref[...] + _dot(_dot(lerp(4), a1_ref[...]).astype(BF16), a2_ref[...]))
    gate = _dot(_sigmoid(_dot(lerp(5), g1_ref[...])).astype(BF16), g2_ref[...])
    kk = k * kk_ref[...]
    kk = kk * lax.rsqrt(jnp.maximum(_head_sum(kk * kk, bd_ref), 1e-24))
    outs = ((r_o, r), (lw_o, lw), (k_o, k * (1.0 + (a - 1.0) * ka_ref[...])), (v_o, v), (na_o, -kk), (b_o, kk * a),
            (g_o, gate))
    for ref, val in outs:
        for s in range(nb):
            ref[s] = val[s * tt:(s + 1) * tt]


def _rwkv_project(x, shift_prev, g, rw):
    B, T, _ = x.shape
    tt = _pick_tile(T, RPROJ_ROWS)
    nb = _pick_tile(B, max(1, RPROJ_ROWS // tt))
    rb = tt // SUBLANES
    tok = pl.BlockSpec((nb, tt, D_MODEL), lambda b, i: (b, i, 0))
    consts = [g, rw["mix"], rw["w0"], rw["w1"], rw["w2"], rw["a0"], rw["a1"], rw["a2"], rw["g1"], rw["g2"],
              rw["k_k"], rw["k_a"], rw["w_r"], rw["w_k"], rw["w_v"], rw["bd"]]
    return pl.pallas_call(
        functools.partial(_rproj_kernel, nb=nb, tt=tt),
        grid=(B // nb, T // tt),
        in_specs=[
            tok,
            pl.BlockSpec((nb, SUBLANES, D_MODEL), lambda b, i: (b, jnp.maximum(i * rb - 1, 0), 0)),
            pl.BlockSpec((nb, 1, D_MODEL), lambda b, i: (b, 0, 0)),
        ] + [_const_spec(c.shape) for c in consts],
        out_specs=[tok] * 7 + [pl.BlockSpec((nb, SUBLANES, D_MODEL), lambda b, i: (b, 0, 0))],
        out_shape=[SDS((B, T, D_MODEL), F32)] * 7 + [SDS((B, SUBLANES, D_MODEL), F32)],
        compiler_params=_cparams(("parallel", "arbitrary")),
        name="rwkv_proj",
    )(x, x, shift_prev, *consts)


def _scan_kernel(r_ref, lw_ref, k_ref, v_ref, a_ref, b_ref, g_ref, s0_ref, rk_ref, lg_ref, lb_ref,
                 y_ref, st_ref, s_scr, *, nb, tt, C):
    N = RWKV_HEAD
    assert C == N and 2 * N == LANES

    hp = LANES // N
    zero = jnp.zeros((N, N), F32)

    @pl.when(pl.program_id(1) == 0)
    def _():
        for s in range(nb):
            for p in range(D_MODEL // LANES):
                blocks = [jnp.concatenate([s0_ref[s, hp * p + h] if g == h else zero for g in range(hp)], axis=1)
                          for h in range(hp)]
                s_scr[s, p] = jnp.concatenate(blocks, axis=0).T

    row_w = lax.broadcasted_iota(jnp.int32, (C, D_MODEL), 0)
    lane = lax.broadcasted_iota(jnp.int32, (C, LANES), 1)
    h0 = lane < N
    r1 = lax.broadcasted_iota(jnp.int32, (C, 2 * C), 0)
    c1 = lax.broadcasted_iota(jnp.int32, (C, 2 * C), 1) % C
    strict = r1 > c1
    r2 = lax.broadcasted_iota(jnp.int32, (C, 4 * C), 0)
    c2 = lax.broadcasted_iota(jnp.int32, (C, 4 * C), 1) % C
    incl = r2 >= c2
    rs = lax.broadcasted_iota(jnp.int32, (LANES, LANES), 0)
    cs = lax.broadcasted_iota(jnp.int32, (LANES, LANES), 1)
    same_head = (rs < N) == (cs < N)
    eye = rs == cs

    def split(x):
        return jnp.concatenate([jnp.where(h0, x, 0.0), jnp.where(h0, 0.0, x)], axis=0)

    def chunk(c, carry):
        sl = pl.ds(pl.multiple_of(c * C, C), C)
        n_pairs = D_MODEL // LANES
        at, rt, bt, kt, vv, wc = [], [], [], [], [], []
        for s in range(nb):
            lw = lw_ref[s, sl, :]
            cum = lw
            d = 1
            while d < C:
                cum = cum + jnp.where(row_w >= d, pltpu.roll(cum, d, 0), 0.0)
                d *= 2
            e_w = jnp.exp(cum)
            e_n = jnp.exp(-cum)
            rows = (a_ref[s, sl, :] * jnp.exp(cum - lw), r_ref[s, sl, :] * e_w, b_ref[s, sl, :] * e_n,
                    k_ref[s, sl, :] * e_n, v_ref[s, sl, :], e_w[C - 1:C, :])
            for dst, src in zip((at, rt, bt, kt, vv, wc), rows):
                dst.extend(src[:, p * LANES:(p + 1) * LANES] for p in range(n_pairs))
        pairs = range(nb * n_pairs)
        st = [s_scr[p // n_pairs, p % n_pairs] for p in pairs]
        ar = [jnp.concatenate([at[p], rt[p]], axis=0).astype(BF16) for p in pairs]
        v2 = [split(vv[p]).astype(BF16) for p in pairs]
        g = [_dot_nt(ar[p], jnp.concatenate([split(bt[p]), split(kt[p])], axis=0).astype(BF16)) for p in pairs]
        hm = [_dot(ar[p], st[p].astype(BF16)) for p in pairs]
        pw = [jnp.where(strict, g[p][:C, :2 * C], 0.0) for p in pairs]
        u = [hm[p][:C] + _dot(jnp.where(strict, g[p][:C, 2 * C:], 0.0).astype(BF16), v2[p]) for p in pairs]
        n = 1
        while n < C:
            pb = [pw[p].astype(BF16) for p in pairs]
            u = [u[p] + _dot(pb[p], split(u[p]).astype(BF16)) for p in pairs]
            n *= 2
            if n < C:
                pw = [_dot(pb[p], split(pw[p]).astype(BF16)) for p in pairs]
        ys = [hm[p][C:] + _dot(jnp.where(incl, g[p][C:, :], 0.0).astype(BF16),
                               jnp.concatenate([split(u[p]).astype(BF16), v2[p]], axis=0)) for p in pairs]
        for p in pairs:
            bk = jnp.concatenate([bt[p], kt[p]], axis=0).astype(BF16)
            uvp = jnp.concatenate([u[p], vv[p]], axis=0).astype(BF16)
            upd = jnp.where(same_head, _dot_tn(bk, uvp), 0.0)
            w_col = jnp.sum(jnp.where(eye, wc[p], 0.0), axis=1, keepdims=True)
            s_scr[p // n_pairs, p % n_pairs] = (st[p] + upd) * w_col
        inv_n = 1.0 / N

        def head_mean(z):
            s0 = jnp.sum(jnp.where(h0, z, 0.0), axis=1, keepdims=True)
            s1 = jnp.sum(jnp.where(h0, 0.0, z), axis=1, keepdims=True)
            return jnp.where(h0, s0, s1) * inv_n

        for s in range(nb):
            zs = []
            for q in range(n_pairs):
                p = s * n_pairs + q
                ps = slice(q * LANES, (q + 1) * LANES)
                yc = ys[p] - head_mean(ys[p])
                yn = yc * lax.rsqrt(head_mean(yc * yc) + LNX_EPS) * lg_ref[:, ps] + lb_ref[:, ps]
                rk = r_ref[s, sl, ps] * k_ref[s, sl, ps] * rk_ref[:, ps]
                zs.append((yn + head_mean(rk) * N * vv[p]) * g_ref[s, sl, ps])
            y_ref[s, sl, :] = jnp.concatenate(zs, axis=1).astype(BF16)
        return carry

    lax.fori_loop(0, tt // C, chunk, 0)

    @pl.when(pl.program_id(1) == pl.num_programs(1) - 1)
    def _():
        for s in range(nb):
            for p in range(D_MODEL // LANES):
                t = s_scr[s, p].T
                for h in range(hp):
                    st_ref[s, hp * p + h] = t[h * N:(h + 1) * N, h * N:(h + 1) * N]


def _rwkv_scan(r, lw, k, v, na, b, gate, wkv0, rw):
    B, T, _ = r.shape
    C = SCAN_CHUNK
    t_pad = ((T + C - 1) // C) * C
    seqs = (r, lw, k, v, na, b, gate)
    if t_pad != T:
        seqs = tuple(jnp.pad(a, ((0, 0), (0, t_pad - T), (0, 0))) for a in seqs)
    tt = _pick_tile(t_pad, 128)
    nb = _pick_tile(B, SCAN_SEQS)
    n_pairs = D_MODEL // LANES
    tok = pl.BlockSpec((nb, tt, D_MODEL), lambda b_, i: (b_, i, 0))
    st = pl.BlockSpec((nb, RWKV_HEADS, RWKV_HEAD, RWKV_HEAD), lambda b_, i: (b_, 0, 0, 0))
    consts = [rw["r_k"], rw["lnx_g"], rw["lnx_b"]]
    y, s_t = pl.pallas_call(
        functools.partial(_scan_kernel, nb=nb, tt=tt, C=C),
        grid=(B // nb, t_pad // tt),
        in_specs=[tok] * 7 + [st] + [_const_spec(c.shape) for c in consts],
        out_specs=[tok, st],
        out_shape=[SDS((B, t_pad, D_MODEL), BF16), SDS((B, RWKV_HEADS, RWKV_HEAD, RWKV_HEAD), F32)],
        scratch_shapes=[pltpu.VMEM((nb, n_pairs, LANES, LANES), F32)],
        compiler_params=_cparams(("parallel", "arbitrary")),
        name="rwkv_scan",
    )(*seqs, wkv0, *consts)
    return (y if t_pad == T else y[:, :T]), s_t


def _wo_mlp_kernel(x_ref, z_ref, wo_ref, g_ref, wu_ref, wd_ref, gf_ref, o_ref, *, final_norm):
    x = x_ref[...] + _dot(z_ref[...], wo_ref[...])
    o_ref[...] = _mlp_apply(x, g_ref, wu_ref, wd_ref, gf_ref, final_norm)


def _wo_mlp(x, z, wo, mlp_args, final_norm):
    M = x.shape[0]
    tm = _pick_tile(M, 512)
    tok = pl.BlockSpec((tm, D_MODEL), lambda i: (i, 0))
    return pl.pallas_call(
        functools.partial(_wo_mlp_kernel, final_norm=final_norm),
        grid=(M // tm,),
        in_specs=[tok, tok, _const_spec(wo.shape)] + _mlp_specs(mlp_args[4]),
        out_specs=tok,
        out_shape=SDS((M, D_MODEL), F32),
        compiler_params=_cparams(("parallel",)),
        name="wo_mlp",
    )(x, z, wo, *mlp_args[:4])


def _pool_block(x, state, n_hist, g, w_bf, scale, mlp_args=(), final_norm=False):
    B = x.shape[0]
    if state is None:
        hist = jnp.zeros((B, POOL_PAD, D_MODEL), F32)
    else:
        hist = jnp.pad(state, ((0, 0), (POOL_PAD - POOL_HIST, 0), (0, 0)))
    out, hs = _pool_layer(x, hist, n_hist, g, w_bf, scale, mlp_args, final_norm)
    return out, hs[:, POOL_PAD - POOL_HIST:]


def _attn_block(x, cache, g, w_in_pad, wo_bf):
    B, T, _ = x.shape
    past = 0 if cache is None else cache[0].shape[2]
    tab = _rope_tables(past, T)
    q, qi, kis, k_new, v_new, ki_new, kb, vb, kib = _attn_project(x, g, w_in_pad, tab)
    topk = min(TOPK_MAX, (past + T) // 4)
    if cache is not None:
        return _attn_decode_layer(x, q, qi, kis, kib, kb, vb, cache, wo_bf, topk), k_new, v_new, ki_new
    tq = _pick_tile(T, ATTN_Q_BLOCK)
    out = x
    for j0 in range(T // tq):
        l_g = (j0 + 1) * tq
        out = _attn_pipe_layer(out, q, qi, kis, kib, kb, vb, wo_bf, 0, l_g, topk, tq, j0, l_g)
    return out, k_new, v_new, ki_new


def _rwkv_block(x, shift_prev, wkv0, g, rw, mlp_args, final_norm):
    B, T, _ = x.shape
    r, lw, k, v, na, b, gate, hl = _rwkv_project(x, shift_prev, g, rw)
    z, s_t = _rwkv_scan(r, lw, k, v, na, b, gate, wkv0, rw)
    flat = lambda a: a.reshape(B * T, D_MODEL)
    out = _wo_mlp(flat(x), flat(z), rw["w_o"], mlp_args, final_norm).reshape(B, T, D_MODEL)
    return out, hl[:, SUBLANES - 1:], s_t


def kernel(x_prompt, x_sample, state_pool, cache_k, cache_v, cache_kidx, state_shift, state_wkv, ln1_g, ln2_g, w_up, w_down, ln_f_g, pool_w, pool_scale, attn_w_in, attn_w_out, rwkv_mix, rwkv_w0, rwkv_w1, rwkv_w2, rwkv_a0, rwkv_a1, rwkv_a2, rwkv_g1, rwkv_g2, rwkv_k_k, rwkv_k_a, rwkv_r_k, rwkv_w_r, rwkv_w_k, rwkv_w_v, rwkv_w_o, rwkv_lnx_g, rwkv_lnx_b):
    xp, xs = x_prompt, x_sample
    bp, sp, _ = xp.shape
    bs, ss, _ = xs.shape
    past = cache_k.shape[2]
    row = lambda a: a.reshape(1, -1)
    bf = lambda a: a.astype(BF16)
    wu_all, wd_all = bf(w_up), bf(w_down)
    head_of = jnp.arange(MXU_DIM) // RWKV_HEAD
    bd_mat = (head_of[:, None] == head_of[None, :]).astype(BF16)
    outs = {n: [] for n in ("pool_p", "pool_s", "k_p", "k_s", "v_p", "v_s", "ki_p", "ki_s",
                            "sh_p", "sh_s", "wkv_p", "wkv_s")}
    for i in range(DEPTH):
        j = i // N_MIXERS
        g1 = row(ln1_g[i])
        last = i == DEPTH - 1
        mlp_args = (row(ln2_g[i]), wu_all, wd_all, row(ln_f_g), i)
        prompt_mlp_done = sample_mlp_done = False
        if i % N_MIXERS == 0:
            w_bf = bf(pool_w[j])
            sc = row(pool_scale[j])
            prompt_mlp_done = sp >= POOL_MLP_MIN_ROWS
            xp, st_p = _pool_block(xp, None, 0, g1, w_bf, sc, mlp_args if prompt_mlp_done else (), last)
            xs, st_s = _pool_block(xs, state_pool[j], past, g1, w_bf, sc)
            outs["pool_p"].append(st_p)
            outs["pool_s"].append(st_s)
        elif i % N_MIXERS == 1:
            w_in_pad = jnp.pad(bf(attn_w_in[j]), ((0, 0), (0, ATTN_IN_PAD - ATTN_IN_W)))
            wo_bf = bf(attn_w_out[j])
            xp, kp, vp, kip = _attn_block(xp, None, g1, w_in_pad, wo_bf)
            xs, kn, vn, kin = _attn_block(xs, (cache_k, cache_v, cache_kidx, j), g1, w_in_pad, wo_bf)
            for n, a in (("k_p", kp), ("v_p", vp), ("ki_p", kip), ("k_s", kn), ("v_s", vn), ("ki_s", kin)):
                outs[n].append(a)
        else:
            rw = dict(mix=rwkv_mix[j], w0=row(rwkv_w0[j]), w1=bf(rwkv_w1[j]), w2=bf(rwkv_w2[j]),
                      a0=row(rwkv_a0[j]), a1=bf(rwkv_a1[j]), a2=bf(rwkv_a2[j]), g1=bf(rwkv_g1[j]),
                      g2=bf(rwkv_g2[j]), k_k=row(rwkv_k_k[j]), k_a=row(rwkv_k_a[j]), r_k=row(rwkv_r_k[j]),
                      w_r=bf(rwkv_w_r[j]), w_k=bf(rwkv_w_k[j]), w_v=bf(rwkv_w_v[j]), w_o=bf(rwkv_w_o[j]),
                      lnx_g=row(rwkv_lnx_g[j]), lnx_b=row(rwkv_lnx_b[j]), bd=bd_mat)
            zero_shift = jnp.zeros((bp, 1, D_MODEL), F32)
            zero_wkv = jnp.zeros((bp, RWKV_HEADS, RWKV_HEAD, RWKV_HEAD), F32)
            xp, shp, wp = _rwkv_block(xp, zero_shift, zero_wkv, g1, rw, mlp_args, last)
            xs, shs, wsn = _rwkv_block(xs, state_shift[j], state_wkv[j], g1, rw, mlp_args, last)
            prompt_mlp_done = sample_mlp_done = True
            outs["sh_p"].append(shp)
            outs["sh_s"].append(shs)
            outs["wkv_p"].append(wp)
            outs["wkv_s"].append(wsn)
        if not prompt_mlp_done:
            xp = _mlp(xp.reshape(bp * sp, D_MODEL), mlp_args, last).reshape(bp, sp, D_MODEL)
        if not sample_mlp_done:
            xs = _mlp(xs.reshape(bs * ss, D_MODEL), mlp_args, last).reshape(bs, ss, D_MODEL)
    st = lambda n: outs[n][0][None] if len(outs[n]) == 1 else jnp.stack(outs[n], 0)
    return (xp, xs, st("pool_p"), st("pool_s"), st("k_p"), st("k_s"), st("v_p"), st("v_s"),
            st("ki_p"), st("ki_s"), st("sh_p"), st("sh_s"), st("wkv_p"), st("wkv_s"))
```

```python
import functools
import math

import jax
import jax.numpy as jnp
import numpy as np
from jax import lax
from jax.experimental import pallas as pl
from jax.experimental.pallas import tpu as pltpu

F32 = jnp.float32
BF16 = jnp.bfloat16
SDS = jax.ShapeDtypeStruct

D_MODEL = 1024
DEPTH = 4
N_MIXERS = 3
CHUNK = 64
D_FF = 4 * D_MODEL
RMS_EPS = 1e-6
POOL_WINDOWS = (2, 4, 8, 16)
POOL_GROUPS = 4
POOL_GW = D_MODEL // POOL_GROUPS
POOL_HIST = max(POOL_WINDOWS) - 1
POOL_PAD = POOL_HIST + 1
ATTN_HEADS = 8
ATTN_HEAD_DIM = D_MODEL // ATTN_HEADS
IDX_HEADS = 8
IDX_DIM = 64
TOPK_MAX = 256
ROPE_THETA = 500000.0
ROPE_FRACTION = 4
NEG_INF = -1e30
Q_W = ATTN_HEADS * ATTN_HEAD_DIM
OFF_K = Q_W
OFF_V = OFF_K + ATTN_HEAD_DIM
OFF_QI = OFF_V + ATTN_HEAD_DIM
OFF_KI = OFF_QI + IDX_HEADS * IDX_DIM
OFF_WI = OFF_KI + IDX_DIM
ATTN_IN_W = OFF_WI + IDX_HEADS
RWKV_HEAD = 64
RWKV_HEADS = D_MODEL // RWKV_HEAD
LNX_EPS = 64e-5

LANES = 128
SUBLANES = 8
ATTN_IN_PAD = ((ATTN_IN_W + LANES - 1) // LANES) * LANES
MXU_DIM = 256
VMEM_LIMIT = 56 * 1024 * 1024
INT_MIN = -2 ** 31
HALF_NEG_KEY = int(np.float32(0.5 * NEG_INF).view(np.int32)) ^ 0x7FFFFFFF
POOL_MLP_MIN_ROWS = 512
POOL_MLP_SUBTILES = 2
ATTN_Q_BLOCK = 256
DECODE_SEQS = 2
APROJ_ROWS = 512
RPROJ_ROWS = 256
SCAN_CHUNK = 64
SCAN_SEQS = 2
FF_CHUNK = 1024


def _cparams(sem):
    return pltpu.CompilerParams(dimension_semantics=sem, vmem_limit_bytes=VMEM_LIMIT)


def _const_spec(shape):
    nd = len(shape)
    return pl.BlockSpec(shape, lambda *_: (0,) * nd, pipeline_mode=pl.Buffered(1))


def _rms(x, g):
    ms = jnp.mean(x * x, axis=-1, keepdims=True)
    return x * lax.rsqrt(ms + RMS_EPS) * g


def _dot(a, b):
    return jnp.dot(a, b, preferred_element_type=F32)


def _dot_nt(a, b):
    return lax.dot_general(a, b, (((1,), (1,)), ((), ())), preferred_element_type=F32)


def _dot_tn(a, b):
    return lax.dot_general(a, b, (((0,), (0,)), ((), ())), preferred_element_type=F32)


def _head_sum(z, bd_ref):
    bd = bd_ref[...]
    hi = z.astype(BF16)
    lo = (z - hi.astype(F32)).astype(BF16)
    outs = []
    for c in range(D_MODEL // MXU_DIM):
        cs = slice(c * MXU_DIM, (c + 1) * MXU_DIM)
        outs.append(_dot(hi[:, cs], bd) + _dot(lo[:, cs], bd))
    return jnp.concatenate(outs, axis=1)


def _sigmoid(x):
    return 1.0 / (1.0 + jnp.exp(-x))


def _pick_tile(n, pref):
    t = min(n, pref)
    assert n % t == 0, (n, t)
    return t


def _mlp_apply(x, g_ref, wu_ref, wd_ref, gf_ref, final_norm, side=()):
    h = _rms(x, g_ref[...]).astype(BF16)
    acc = x
    for j in range(D_FF // FF_CHUNK):
        u = _dot(h, wu_ref[:, j * FF_CHUNK:(j + 1) * FF_CHUNK])
        u = jnp.square(jnp.maximum(u, 0.0)).astype(BF16)
        acc = acc + _dot(u, wd_ref[j * FF_CHUNK:(j + 1) * FF_CHUNK, :])
        if j < len(side):
            side[j]()
    if final_norm:
        acc = _rms(acc, gf_ref[...])
    return acc


def _mlp_specs(layer):
    pick = lambda *_: (layer, 0, 0)
    return [_const_spec((1, D_MODEL)),
            pl.BlockSpec((None, D_MODEL, D_FF), pick, pipeline_mode=pl.Buffered(1)),
            pl.BlockSpec((None, D_FF, D_MODEL), pick, pipeline_mode=pl.Buffered(1)),
            _const_spec((1, D_MODEL))]


def _mlp_kernel(x_ref, g_ref, wu_ref, wd_ref, gf_ref, o_ref, *, final_norm):
    o_ref[...] = _mlp_apply(x_ref[...], g_ref, wu_ref, wd_ref, gf_ref, final_norm)


def _mlp(x, mlp_args, final_norm):
    M = x.shape[0]
    tm = _pick_tile(M, 512)
    return pl.pallas_call(
        functools.partial(_mlp_kernel, final_norm=final_norm),
        grid=(M // tm,),
        in_specs=[pl.BlockSpec((tm, D_MODEL), lambda i: (i, 0))] + _mlp_specs(mlp_args[4]),
        out_specs=pl.BlockSpec((tm, D_MODEL), lambda i: (i, 0)),
        out_shape=SDS((M, D_MODEL), F32),
        compiler_params=_cparams(("parallel",)),
        name="mlp",
    )(x, *mlp_args[:4])


def _pool_kernel(x_ref, xp_ref, hist_ref, g_ref, w_ref, sc_ref, *rest, tt, n_hist, mlp, final_norm):
    mlp_refs, (o_ref, hs_ref) = rest[:-2], rest[-2:]
    i = pl.program_id(1)
    g = g_ref[...]
    x = x_ref[...]
    h = _rms(x, g)
    prev = jnp.where(i == 0, hist_ref[...], _rms(xp_ref[...], g))
    hs_ref[...] = h[tt - POOL_PAD:, :]
    full = jnp.concatenate([prev, h], axis=0)
    nsub = POOL_MLP_SUBTILES if mlp else 1
    rows = tt // nsub

    def group(r, gi):
        win = POOL_WINDOWS[gi]
        rs = slice(r * rows, (r + 1) * rows)
        cs = slice(gi * POOL_GW, (gi + 1) * POOL_GW)
        s = full[r * rows:(r + 1) * rows + POOL_PAD, cs]
        d = 1
        while d < win:
            s = s + pltpu.roll(s, d, 0)
            d *= 2
        t1 = lax.broadcasted_iota(jnp.int32, (POOL_PAD, POOL_GW), 0) + (i * tt + r * rows + 1 + n_hist)
        cnt = jnp.concatenate([jnp.minimum(t1, win).astype(F32), jnp.full((rows - POOL_PAD, POOL_GW), win, F32)],
                              axis=0)
        pooled = s[POOL_PAD:, :] / cnt - h[rs, cs]
        return x[rs, cs] + _dot(pooled.astype(BF16), w_ref[gi]) * sc_ref[:, cs]

    cols = [group(0, gi) for gi in range(POOL_GROUPS)]
    if not mlp:
        o_ref[...] = jnp.concatenate(cols, axis=1)
        return
    outs = []
    for r in range(nsub):
        y = jnp.concatenate(cols, axis=1)
        cols = []
        side = [functools.partial(lambda gi, rn: cols.append(group(rn, gi)), gi, r + 1)
                for gi in range(POOL_GROUPS)] if r + 1 < nsub else []
        outs.append(_mlp_apply(y, *mlp_refs, final_norm, side))
    o_ref[...] = jnp.concatenate(outs, axis=0)


def _pool_layer(x, hist, n_hist, g, w_bf, scale, mlp_args, final_norm):
    B, T, _ = x.shape
    tt = _pick_tile(T, 512)
    rb = tt // POOL_PAD
    return pl.pallas_call(
        functools.partial(_pool_kernel, tt=tt, n_hist=n_hist, mlp=bool(mlp_args), final_norm=final_norm),
        grid=(B, T // tt),
        in_specs=[
            pl.BlockSpec((None, tt, D_MODEL), lambda b, i: (b, i, 0)),
            pl.BlockSpec((None, POOL_PAD, D_MODEL), lambda b, i: (b, jnp.maximum(i * rb - 1, 0), 0)),
            pl.BlockSpec((None, POOL_PAD, D_MODEL), lambda b, i: (b, 0, 0)),
            _const_spec((1, D_MODEL)),
            _const_spec((POOL_GROUPS, POOL_GW, POOL_GW)),
            _const_spec((1, D_MODEL)),
        ] + (_mlp_specs(mlp_args[4]) if mlp_args else []),
        out_specs=[
            pl.BlockSpec((None, tt, D_MODEL), lambda b, i: (b, i, 0)),
            pl.BlockSpec((None, POOL_PAD, D_MODEL), lambda b, i: (b, 0, 0)),
        ],
        out_shape=[SDS((B, T, D_MODEL), F32), SDS((B, POOL_PAD, D_MODEL), F32)],
        compiler_params=_cparams(("parallel", "arbitrary")),
        name="pool_mlp" if mlp_args else "pool",
    )(x, x, hist, g, w_bf, scale, *mlp_args[:4])


def _rope_tables(pos0, T):
    pos = np.arange(pos0, pos0 + T, dtype=np.float64)

    def head(d):
        rd = d // ROPE_FRACTION
        half = rd // 2
        inv = ROPE_THETA ** (-np.arange(half, dtype=np.float64) / half)
        ang = pos[:, None] * inv[None, :]
        cos, sin = np.cos(ang), np.sin(ang)
        z = lambda n: np.zeros((T, n))
        c = np.concatenate([cos, cos, np.ones((T, d - rd))], axis=1)
        sa = np.concatenate([-sin, z(d - half)], axis=1)
        sb = np.concatenate([z(half), sin, z(d - rd)], axis=1)
        return c, sa, sb

    qa = head(ATTN_HEAD_DIM)
    ia = head(IDX_DIM)
    i2 = tuple(np.concatenate([t, t], axis=1) for t in ia)
    ones, zeros = np.ones((T, IDX_DIM)), np.zeros((T, IDX_DIM))
    ik = (np.concatenate([ia[0], ones], axis=1), np.concatenate([ia[1], zeros], axis=1),
          np.concatenate([ia[2], zeros], axis=1))
    return jnp.asarray(np.concatenate(list(qa) + list(i2) + list(ik), axis=1), F32)


def _aproj_kernel(x_ref, g_ref, w_ref, tab_ref, q_ref, qi_ref, kis_ref, k_ref, v_ref, ki_ref, kb_ref, vb_ref, kib_ref,
                  *, nb, tt):
    h = _rms(x_ref[...].reshape(nb * tt, D_MODEL), g_ref[...]).astype(BF16)
    p = _dot(h, w_ref[...])

    def rope(x, kind, half):
        c = tab_ref[:, (3 * kind) * LANES:(3 * kind + 1) * LANES]
        sa = tab_ref[:, (3 * kind + 1) * LANES:(3 * kind + 2) * LANES]
        sb = tab_ref[:, (3 * kind + 2) * LANES:(3 * kind + 3) * LANES]
        return x * c + pltpu.roll(x, LANES - half, 1) * sa + pltpu.roll(x, half, 1) * sb

    def put(ref, val, *lead):
        for s in range(nb):
            ref[(s,) + lead] = val[s * tt:(s + 1) * tt]

    slab = lambda off: p[:, off:off + LANES]
    qh = ATTN_HEAD_DIM // ROPE_FRACTION // 2
    ih = IDX_DIM // ROPE_FRACTION // 2
    qscale = ATTN_HEAD_DIM ** -0.5 * math.log2(math.e)
    for hd in range(ATTN_HEADS):
        put(q_ref, (rope(slab(hd * LANES), 0, qh) * qscale).astype(BF16), hd)
    k = rope(slab(OFF_K), 0, qh)
    v = slab(OFF_V)
    put(qi_ref, jnp.concatenate([rope(slab(OFF_QI + c * LANES), 1, ih).astype(BF16)
                                 for c in range(IDX_HEADS * IDX_DIM // LANES)], axis=1))
    kis = rope(slab(OFF_KI), 2, ih)
    put(kis_ref, kis)
    put(k_ref, k)
    put(v_ref, v)
    put(ki_ref, kis[:, :IDX_DIM])
    put(kb_ref, k.astype(BF16))
    lane = lax.broadcasted_iota(jnp.int32, v.shape, 1)
    put(vb_ref, jnp.concatenate([v.astype(BF16), jnp.where(lane == 0, 1.0, 0.0).astype(BF16)], axis=1))
    put(kib_ref, kis[:, :IDX_DIM].astype(BF16))


def _attn_project(x, g, w_in_pad, tab):
    B, T, _ = x.shape
    tt = _pick_tile(T, APROJ_ROWS)
    nb = _pick_tile(B, max(1, APROJ_ROWS // tt))
    if nb > 1:
        tab = jnp.tile(tab, (nb, 1))
    QI_W = IDX_HEADS * IDX_DIM
    widths = (QI_W, LANES, ATTN_HEAD_DIM, ATTN_HEAD_DIM, IDX_DIM, ATTN_HEAD_DIM, 2 * ATTN_HEAD_DIM, IDX_DIM)
    dtypes = (BF16, F32, F32, F32, F32, BF16, BF16, BF16)
    q_spec = pl.BlockSpec((nb, ATTN_HEADS, tt, ATTN_HEAD_DIM), lambda i, b: (b, 0, i, 0))
    return pl.pallas_call(
        functools.partial(_aproj_kernel, nb=nb, tt=tt),
        grid=(T // tt, B // nb),
        in_specs=[
            pl.BlockSpec((nb, tt, D_MODEL), lambda i, b: (b, i, 0)),
            _const_spec((1, D_MODEL)),
            _const_spec((D_MODEL, ATTN_IN_PAD)),
            pl.BlockSpec((nb * tt, 9 * LANES), lambda i, b: (i, 0)),
        ],
        out_specs=[q_spec] + [pl.BlockSpec((nb, tt, w), lambda i, b: (b, i, 0)) for w in widths],
        out_shape=[SDS((B, ATTN_HEADS, T, ATTN_HEAD_DIM), BF16)] + [SDS((B, T, w), d) for w, d in zip(widths, dtypes)],
        compiler_params=_cparams(("arbitrary", "arbitrary")),
        name="attn_proj",
    )(x, g, w_in_pad, tab)


def _score_key(score):
    kb = pltpu.bitcast(score, jnp.int32)
    return jnp.where(kb >= 0, kb, kb ^ jnp.int32(0x7FFFFFFF))


KEY_BITS = 32


def _count(mask, axis):
    return jnp.sum(jnp.where(mask, 1.0, 0.0), axis=axis, keepdims=True)


def _kth_key_step(key_scr, lo_u, it, topk, axis):
    trial_u = lo_u | lax.shift_left(jnp.int32(1), KEY_BITS - 1 - it)
    c = _count(key_scr[...] >= (trial_u ^ jnp.int32(INT_MIN)), axis)
    return jnp.where(c >= topk, trial_u, lo_u)


def _select_topk(key_scr, jstar_scr, kpos, topk, axis):
    step = lambda it, lo_u: _kth_key_step(key_scr, lo_u, it, topk, axis)
    lo_u = lax.fori_loop(0, KEY_BITS, step, jnp.zeros(jstar_scr.shape, jnp.int32))
    return _finish_topk(key_scr, jstar_scr, kpos, lo_u, topk, axis)


def _finish_topk(key_scr, jstar_scr, kpos, lo_u, topk, axis):
    one = jstar_scr.shape
    n_idx = key_scr.shape[axis]
    count = functools.partial(_count, axis=axis)
    lo = lo_u ^ jnp.int32(INT_MIN)
    gt = key_scr[...] > lo
    eq = key_scr[...] == lo
    need = topk - count(gt)
    tie = (lo > jnp.int32(HALF_NEG_KEY)) & (count(eq) > need)
    jstar_scr[...] = jnp.full(one, n_idx, jnp.int32)

    @pl.when(jnp.max(jnp.where(tie, 1.0, 0.0)) > 0.0)
    def _():
        nbits = max(1, (n_idx - 1).bit_length())

        def ibody(it, lo_i):
            trial = lo_i + lax.shift_left(jnp.int32(1), nbits - 1 - it)
            c = count((kpos < trial) & (key_scr[...] == lo))
            return jnp.where(c < need, trial, lo_i)

        lo_i = lax.fori_loop(0, nbits, ibody, jnp.zeros(one, jnp.int32))
        jstar_scr[...] = jnp.where(tie, lo_i, n_idx)

    return gt | (eq & (kpos <= jstar_scr[...]))


def _attn_decode_kernel(x_ref, q_ref, qi_ref, wi_ref, kib_ref, kb_ref, vb_ref, cki_ref, ck_ref, cv_ref, wo_ref,
                        o_ref, ki_scr, k_scr, v_scr, key_scr, bias_scr, jstar_scr, o_scr, *, nb, tq, P, L, topk):
    n_keys = P + tq

    @pl.when(pl.program_id(0) == 0)
    def _():
        lane = lax.broadcasted_iota(jnp.int32, (P, ATTN_HEAD_DIM), 1)
        for s in range(nb):
            v_scr[s, :P, ATTN_HEAD_DIM:] = jnp.where(lane == 0, 1.0, 0.0).astype(BF16)
            ki_scr[s, n_keys:, :] = jnp.zeros((L - n_keys, IDX_DIM), BF16)
            k_scr[s, n_keys:, :] = jnp.zeros((L - n_keys, ATTN_HEAD_DIM), BF16)
            v_scr[s, n_keys:, :] = jnp.zeros((L - n_keys, 2 * ATTN_HEAD_DIM), BF16)

    qpos = lax.broadcasted_iota(jnp.int32, (tq, L), 0) + P
    kpos = lax.broadcasted_iota(jnp.int32, (tq, L), 1)
    adm = (kpos < n_keys) & ((kpos // CHUNK) <= (qpos // CHUNK))
    rows = lambda s: slice(s * tq, (s + 1) * tq)

    for s in range(nb):
        ki_scr[s, :P, :] = cki_ref[s].astype(BF16)
        ki_scr[s, P:n_keys, :] = kib_ref[s]
        k_scr[s, :P, :] = ck_ref[s].astype(BF16)
        k_scr[s, P:n_keys, :] = kb_ref[s]
        v_scr[s, :P, :ATTN_HEAD_DIM] = cv_ref[s].astype(BF16)
        v_scr[s, P:n_keys, :] = vb_ref[s]
        qi = qi_ref[s]
        wis = wi_ref[s] * (IDX_DIM ** -0.5 * IDX_HEADS ** -0.5)
        d_all = _dot_nt(jnp.concatenate([qi[:, h * IDX_DIM:(h + 1) * IDX_DIM] for h in range(IDX_HEADS)], axis=0),
                        ki_scr[s])
        score = jnp.zeros((tq, L), F32)
        for h in range(IDX_HEADS):
            score = score + jnp.maximum(d_all[h * tq:(h + 1) * tq], 0.0) * wis[:, IDX_DIM + h:IDX_DIM + h + 1]
        key_scr[rows(s), :] = _score_key(jnp.where(adm, score, NEG_INF))

    tile = lambda m: jnp.concatenate([m] * nb, axis=0)
    keep = _select_topk(key_scr, jstar_scr, tile(kpos), topk, axis=1)
    bias_scr[...] = jnp.where(tile(adm) & keep, 0.0, NEG_INF)

    for s in range(nb):
        def probs(lg):
            lg = lg + bias_scr[rows(s), :]
            return jnp.exp2(lg - jnp.max(lg, axis=1, keepdims=True)).astype(BF16)

        lg_all = _dot_nt(jnp.concatenate([q_ref[s, h] for h in range(ATTN_HEADS)], axis=0), k_scr[s])
        p_all = jnp.concatenate([probs(lg_all[h * tq:(h + 1) * tq]) for h in range(ATTN_HEADS)], axis=0)
        pv_all = _dot(p_all, v_scr[s])
        for h in range(ATTN_HEADS):
            pv = pv_all[h * tq:(h + 1) * tq]
            o_scr[rows(s), h * ATTN_HEAD_DIM:(h + 1) * ATTN_HEAD_DIM] = (
                pv[:, :ATTN_HEAD_DIM] / pv[:, ATTN_HEAD_DIM:ATTN_HEAD_DIM + 1]).astype(BF16)
    attn = _dot(o_scr[...], wo_ref[...])
    for s in range(nb):
        o_ref[s] = x_ref[s] + attn[rows(s)]


def _attn_decode_layer(x, q, qi, kis, kib, kb, vb, cache, wo_bf, topk):
    B, T, _ = x.shape
    ck, cv, cki, layer = cache
    P = ck.shape[2]
    nb = _pick_tile(B, DECODE_SEQS)
    past = lambda w: pl.BlockSpec((None, nb, P, w), lambda b: (layer, b, 0, 0))
    L = ((P + T + LANES - 1) // LANES) * LANES
    assert T * max(IDX_HEADS, ATTN_HEADS) <= MXU_DIM, "decode block: all heads of all new tokens in one row tile"
    assert P % 16 == 0 and T % 16 == 0, "bf16 row tiles"
    QI_W = IDX_HEADS * IDX_DIM
    seq = lambda *shape: pl.BlockSpec((nb,) + shape, lambda b: (b,) + (0,) * len(shape))
    return pl.pallas_call(
        functools.partial(_attn_decode_kernel, nb=nb, tq=T, P=P, L=L, topk=topk),
        grid=(B // nb,),
        in_specs=[
            seq(T, D_MODEL), seq(ATTN_HEADS, T, ATTN_HEAD_DIM), seq(T, QI_W), seq(T, LANES),
            seq(T, IDX_DIM), seq(T, ATTN_HEAD_DIM), seq(T, 2 * ATTN_HEAD_DIM),
            past(IDX_DIM), past(ATTN_HEAD_DIM), past(ATTN_HEAD_DIM),
            _const_spec((Q_W, D_MODEL)),
        ],
        out_specs=seq(T, D_MODEL),
        out_shape=SDS((B, T, D_MODEL), F32),
        scratch_shapes=[
            pltpu.VMEM((nb, L, IDX_DIM), BF16),
            pltpu.VMEM((nb, L, ATTN_HEAD_DIM), BF16),
            pltpu.VMEM((nb, L, 2 * ATTN_HEAD_DIM), BF16),
            pltpu.VMEM((nb * T, L), jnp.int32),
            pltpu.VMEM((nb * T, L), F32),
            pltpu.VMEM((nb * T, 1), jnp.int32),
            pltpu.VMEM((nb * T, Q_W), BF16),
        ],
        compiler_params=_cparams(("arbitrary",)),
        name="attn_decode",
    )(x, q, qi, kis, kib, kb, vb, cki, ck, cv, wo_bf)


def _attn_pipe_kernel(x_ref, q_ref, qi_ref, wi_ref, ki_ref, k_ref, v_ref, wo_ref, o_ref,
                      key_scr, bias_scr, lg_scr, jstar_scr, o_scr, *, tq, L, q_off, n_keys, topk):
    s = pl.program_id(0)
    cur = s % 2

    @pl.when(s == 0)
    def _():
        bias_scr[...] = jnp.zeros(bias_scr.shape, F32)

    qpos = lax.broadcasted_iota(jnp.int32, (tq, L), 0) + q_off
    kpos = lax.broadcasted_iota(jnp.int32, (tq, L), 1)
    adm = (kpos < n_keys) & ((kpos // CHUNK) <= (qpos // CHUNK))

    select_all = L <= topk
    if not select_all:
        qi = qi_ref[...]
        ki = ki_ref[...]
        wis = wi_ref[...] * (IDX_DIM ** -0.5 * IDX_HEADS ** -0.5)
        score = jnp.zeros((tq, L), F32)
        for h in range(IDX_HEADS):
            d = _dot_nt(qi[:, h * IDX_DIM:(h + 1) * IDX_DIM], ki)
            score = score + jnp.maximum(d, 0.0) * wis[:, IDX_DIM + h:IDX_DIM + h + 1]
        key_scr[...] = _score_key(jnp.where(adm, score, NEG_INF))

    steps_per_head = KEY_BITS // ATTN_HEADS

    lg_scr[0] = _dot_nt(q_ref[0], k_ref[...])

    def head(h, lo_u):
        if select_all:
            step = lambda i, lo: lo
        else:
            step = lambda i, lo: _kth_key_step(key_scr, lo, h * steps_per_head + i, topk, 1)
        lo_u = step(0, lo_u)
        lg = lg_scr[h % 2] + bias_scr[1 - cur]
        lg_scr[(h + 1) % 2] = _dot_nt(q_ref[jnp.minimum(h + 1, ATTN_HEADS - 1)], k_ref[...])
        lo_u = step(1, lo_u)
        p = jnp.exp2(lg - jnp.max(lg, axis=1, keepdims=True)).astype(BF16)
        lo_u = step(2, lo_u)
        pv = _dot(p, v_ref[...])
        for i in range(3, steps_per_head):
            lo_u = step(i, lo_u)
        o_scr[h] = (pv[:, :ATTN_HEAD_DIM] / pv[:, ATTN_HEAD_DIM:ATTN_HEAD_DIM + 1]).astype(BF16)
        return lo_u

    lo_u = lax.fori_loop(0, ATTN_HEADS, head, jnp.zeros((tq, 1), jnp.int32))
    attn = _dot(jnp.concatenate([o_scr[h] for h in range(ATTN_HEADS)], axis=1), wo_ref[...])
    o_ref[...] = x_ref[...] + jnp.where(s > 0, attn, 0.0)

    keep = adm if select_all else adm & _finish_topk(key_scr, jstar_scr, kpos, lo_u, topk, 1)
    bias_scr[cur] = jnp.where(keep, 0.0, NEG_INF)


def _attn_pipe_layer(x, q, qi, kis, ki_all, k_all, v_all, wo_bf, q_off, n_keys, topk, tq, j0, L):
    B, T, _ = x.shape
    QI_W = IDX_HEADS * IDX_DIM
    prev = lambda s: jnp.maximum(s - 1, 0)
    this = lambda s: jnp.minimum(s, B - 1)
    return pl.pallas_call(
        functools.partial(_attn_pipe_kernel, tq=tq, L=L, q_off=q_off + j0 * tq, n_keys=n_keys, topk=topk),
        grid=(B + 1,),
        in_specs=[
            pl.BlockSpec((None, tq, D_MODEL), lambda s: (prev(s), j0, 0)),
            pl.BlockSpec((None, ATTN_HEADS, tq, ATTN_HEAD_DIM), lambda s: (prev(s), 0, j0, 0)),
            pl.BlockSpec((None, tq, QI_W), lambda s: (this(s), j0, 0)),
            pl.BlockSpec((None, tq, LANES), lambda s: (this(s), j0, 0)),
            pl.BlockSpec((None, L, IDX_DIM), lambda s: (this(s), 0, 0)),
            pl.BlockSpec((None, L, ATTN_HEAD_DIM), lambda s: (prev(s), 0, 0)),
            pl.BlockSpec((None, L, 2 * ATTN_HEAD_DIM), lambda s: (prev(s), 0, 0)),
            _const_spec((Q_W, D_MODEL)),
        ],
        out_specs=pl.BlockSpec((None, tq, D_MODEL), lambda s: (prev(s), j0, 0)),
        out_shape=SDS((B, T, D_MODEL), F32),
        input_output_aliases={0: 0},
        scratch_shapes=[
            pltpu.VMEM((tq, L), jnp.int32),
            pltpu.VMEM((2, tq, L), F32),
            pltpu.VMEM((2, tq, L), F32),
            pltpu.VMEM((tq, 1), jnp.int32),
            pltpu.VMEM((ATTN_HEADS, tq, ATTN_HEAD_DIM), BF16),
        ],
        compiler_params=_cparams(("arbitrary",)),
        name="attn_pipe",
    )(x, q, qi, kis, ki_all, k_all, v_all, wo_bf)


def _rproj_kernel(x_ref, xp_ref, sh_ref, g_ref, mix_ref, w0_ref, w1_ref, w2_ref, a0_ref, a1_ref, a2_ref,
                  g1_ref, g2_ref, kk_ref, ka_ref, wr_ref, wk_ref, wv_ref, bd_ref,
                  r_o, lw_o, k_o, v_o, na_o, b_o, g_o, hl_o, *, nb, tt):
    i = pl.program_id(1)
    g = g_ref[...]
    row = lax.broadcasted_iota(jnp.int32, (tt, D_MODEL), 0)
    hs, xxs = [], []
    for s in range(nb):
        h_s = _rms(x_ref[s], g)
        hl_o[s] = h_s[tt - SUBLANES:, :]
        prev = jnp.where(i == 0, sh_ref[s], _rms(xp_ref[s, SUBLANES - 1:SUBLANES, :], g))
        hs.append(h_s)
        xxs.append(jnp.where(row == 0, prev, pltpu.roll(h_s, 1, 0)) - h_s)
    h = jnp.concatenate(hs, axis=0)
    xx = jnp.concatenate(xxs, axis=0)
    lerp = lambda n: (h + xx * mix_ref[n:n + 1, :]).astype(BF16)
    r = _dot(lerp(0), wr_ref[...])
    wl = w0_ref[...] + _dot(jnp.tanh(_dot(lerp(1), w1_ref[...])).astype(BF16), w2_ref[...])
    lw = -_sigmoid(wl) * math.exp(-0.5)
    k = _dot(lerp(2), wk_ref[...])
    v = _dot(lerp(3), wv_ref[...])
    a = _sigmoid(a0_ref[...] + _dot(_dot(lerp(4), a1_ref[...]).astype(BF16), a2_ref[...]))
    gate = _dot(_sigmoid(_dot(lerp(5), g1_ref[...])).astype(BF16), g2_ref[...])
    kk = k * kk_ref[...]
    kk = kk * lax.rsqrt(jnp.maximum(_head_sum(kk * kk, bd_ref), 1e-24))
    outs = ((r_o, r), (lw_o, lw), (k_o, k * (1.0 + (a - 1.0) * ka_ref[...])), (v_o, v), (na_o, -kk), (b_o, kk * a),
            (g_o, gate))
    for ref, val in outs:
        for s in range(nb):
            ref[s] = val[s * tt:(s + 1) * tt]


def _rwkv_project(x, shift_prev, g, rw):
    B, T, _ = x.shape
    tt = _pick_tile(T, RPROJ_ROWS)
    nb = _pick_tile(B, max(1, RPROJ_ROWS // tt))
    rb = tt // SUBLANES
    tok = pl.BlockSpec((nb, tt, D_MODEL), lambda b, i: (b, i, 0))
    consts = [g, rw["mix"], rw["w0"], rw["w1"], rw["w2"], rw["a0"], rw["a1"], rw["a2"], rw["g1"], rw["g2"],
              rw["k_k"], rw["k_a"], rw["w_r"], rw["w_k"], rw["w_v"], rw["bd"]]
    return pl.pallas_call(
        functools.partial(_rproj_kernel, nb=nb, tt=tt),
        grid=(B // nb, T // tt),
        in_specs=[
            tok,
            pl.BlockSpec((nb, SUBLANES, D_MODEL), lambda b, i: (b, jnp.maximum(i * rb - 1, 0), 0)),
            pl.BlockSpec((nb, 1, D_MODEL), lambda b, i: (b, 0, 0)),
        ] + [_const_spec(c.shape) for c in consts],
        out_specs=[tok] * 7 + [pl.BlockSpec((nb, SUBLANES, D_MODEL), lambda b, i: (b, 0, 0))],
        out_shape=[SDS((B, T, D_MODEL), F32)] * 7 + [SDS((B, SUBLANES, D_MODEL), F32)],
        compiler_params=_cparams(("parallel", "arbitrary")),
        name="rwkv_proj",
    )(x, x, shift_prev, *consts)


def _scan_kernel(r_ref, lw_ref, k_ref, v_ref, a_ref, b_ref, g_ref, s0_ref, rk_ref, lg_ref, lb_ref,
                 y_ref, st_ref, s_scr, *, nb, tt, C):
    N = RWKV_HEAD
    assert C == N and 2 * N == LANES

    hp = LANES // N
    zero = jnp.zeros((N, N), F32)

    @pl.when(pl.program_id(1) == 0)
    def _():
        for s in range(nb):
            for p in range(D_MODEL // LANES):
                blocks = [jnp.concatenate([s0_ref[s, hp * p + h] if g == h else zero for g in range(hp)], axis=1)
                          for h in range(hp)]
                s_scr[s, p] = jnp.concatenate(blocks, axis=0).T

    row_w = lax.broadcasted_iota(jnp.int32, (C, D_MODEL), 0)
    lane = lax.broadcasted_iota(jnp.int32, (C, LANES), 1)
    h0 = lane < N
    r1 = lax.broadcasted_iota(jnp.int32, (C, 2 * C), 0)
    c1 = lax.broadcasted_iota(jnp.int32, (C, 2 * C), 1) % C
    strict = r1 > c1
    r2 = lax.broadcasted_iota(jnp.int32, (C, 4 * C), 0)
    c2 = lax.broadcasted_iota(jnp.int32, (C, 4 * C), 1) % C
    incl = r2 >= c2
    rs = lax.broadcasted_iota(jnp.int32, (LANES, LANES), 0)
    cs = lax.broadcasted_iota(jnp.int32, (LANES, LANES), 1)
    same_head = (rs < N) == (cs < N)
    eye = rs == cs

    def split(x):
        return jnp.concatenate([jnp.where(h0, x, 0.0), jnp.where(h0, 0.0, x)], axis=0)

    def chunk(c, carry):
        sl = pl.ds(pl.multiple_of(c * C, C), C)
        n_pairs = D_MODEL // LANES
        at, rt, bt, kt, vv, wc = [], [], [], [], [], []
        for s in range(nb):
            lw = lw_ref[s, sl, :]
            cum = lw
            d = 1
            while d < C:
                cum = cum + jnp.where(row_w >= d, pltpu.roll(cum, d, 0), 0.0)
                d *= 2
            e_w = jnp.exp(cum)
            e_n = jnp.exp(-cum)
            rows = (a_ref[s, sl, :] * jnp.exp(cum - lw), r_ref[s, sl, :] * e_w, b_ref[s, sl, :] * e_n,
                    k_ref[s, sl, :] * e_n, v_ref[s, sl, :], e_w[C - 1:C, :])
            for dst, src in zip((at, rt, bt, kt, vv, wc), rows):
                dst.extend(src[:, p * LANES:(p + 1) * LANES] for p in range(n_pairs))
        pairs = range(nb * n_pairs)
        st = [s_scr[p // n_pairs, p % n_pairs] for p in pairs]
        ar = [jnp.concatenate([at[p], rt[p]], axis=0).astype(BF16) for p in pairs]
        v2 = [split(vv[p]).astype(BF16) for p in pairs]
        g = [_dot_nt(ar[p], jnp.concatenate([split(bt[p]), split(kt[p])], axis=0).astype(BF16)) for p in pairs]
        hm = [_dot(ar[p], st[p].astype(BF16)) for p in pairs]
        pw = [jnp.where(strict, g[p][:C, :2 * C], 0.0) for p in pairs]
        u = [hm[p][:C] + _dot(jnp.where(strict, g[p][:C, 2 * C:], 0.0).astype(BF16), v2[p]) for p in pairs]
        n = 1
        while n < C:
            pb = [pw[p].astype(BF16) for p in pairs]
            u = [u[p] + _dot(pb[p], split(u[p]).astype(BF16)) for p in pairs]
            n *= 2
            if n < C:
                pw = [_dot(pb[p], split(pw[p]).astype(BF16)) for p in pairs]
        ys = [hm[p][C:] + _dot(jnp.where(incl, g[p][C:, :], 0.0).astype(BF16),
                               jnp.concatenate([split(u[p]).astype(BF16), v2[p]], axis=0)) for p in pairs]
        for p in pairs:
            bk = jnp.concatenate([bt[p], kt[p]], axis=0).astype(BF16)
            uvp = jnp.concatenate([u[p], vv[p]], axis=0).astype(BF16)
            upd = jnp.where(same_head, _dot_tn(bk, uvp), 0.0)
            w_col = jnp.sum(jnp.where(eye, wc[p], 0.0), axis=1, keepdims=True)
            s_scr[p // n_pairs, p % n_pairs] = (st[p] + upd) * w_col
        inv_n = 1.0 / N

        def head_mean(z):
            s0 = jnp.sum(jnp.where(h0, z, 0.0), axis=1, keepdims=True)
            s1 = jnp.sum(jnp.where(h0, 0.0, z), axis=1, keepdims=True)
            return jnp.where(h0, s0, s1) * inv_n

        for s in range(nb):
            zs = []
            for q in range(n_pairs):
                p = s * n_pairs + q
                ps = slice(q * LANES, (q + 1) * LANES)
                yc = ys[p] - head_mean(ys[p])
                yn = yc * lax.rsqrt(head_mean(yc * yc) + LNX_EPS) * lg_ref[:, ps] + lb_ref[:, ps]
                rk = r_ref[s, sl, ps] * k_ref[s, sl, ps] * rk_ref[:, ps]
                zs.append((yn + head_mean(rk) * N * vv[p]) * g_ref[s, sl, ps])
            y_ref[s, sl, :] = jnp.concatenate(zs, axis=1).astype(BF16)
        return carry

    lax.fori_loop(0, tt // C, chunk, 0)

    @pl.when(pl.program_id(1) == pl.num_programs(1) - 1)
    def _():
        for s in range(nb):
            for p in range(D_MODEL // LANES):
                t = s_scr[s, p].T
                for h in range(hp):
                    st_ref[s, hp * p + h] = t[h * N:(h + 1) * N, h * N:(h + 1) * N]


def _rwkv_scan(r, lw, k, v, na, b, gate, wkv0, rw):
    B, T, _ = r.shape
    C = SCAN_CHUNK
    t_pad = ((T + C - 1) // C) * C
    seqs = (r, lw, k, v, na, b, gate)
    if t_pad != T:
        seqs = tuple(jnp.pad(a, ((0, 0), (0, t_pad - T), (0, 0))) for a in seqs)
    tt = _pick_tile(t_pad, 128)
    nb = _pick_tile(B, SCAN_SEQS)
    n_pairs = D_MODEL // LANES
    tok = pl.BlockSpec((nb, tt, D_MODEL), lambda b_, i: (b_, i, 0))
    st = pl.BlockSpec((nb, RWKV_HEADS, RWKV_HEAD, RWKV_HEAD), lambda b_, i: (b_, 0, 0, 0))
    consts = [rw["r_k"], rw["lnx_g"], rw["lnx_b"]]
    y, s_t = pl.pallas_call(
        functools.partial(_scan_kernel, nb=nb, tt=tt, C=C),
        grid=(B // nb, t_pad // tt),
        in_specs=[tok] * 7 + [st] + [_const_spec(c.shape) for c in consts],
        out_specs=[tok, st],
        out_shape=[SDS((B, t_pad, D_MODEL), BF16), SDS((B, RWKV_HEADS, RWKV_HEAD, RWKV_HEAD), F32)],
        scratch_shapes=[pltpu.VMEM((nb, n_pairs, LANES, LANES), F32)],
        compiler_params=_cparams(("parallel", "arbitrary")),
        name="rwkv_scan",
    )(*seqs, wkv0, *consts)
    return (y if t_pad == T else y[:, :T]), s_t


def _wo_mlp_kernel(x_ref, z_ref, wo_ref, g_ref, wu_ref, wd_ref, gf_ref, o_ref, *, final_norm):
    x = x_ref[...] + _dot(z_ref[...], wo_ref[...])
    o_ref[...] = _mlp_apply(x, g_ref, wu_ref, wd_ref, gf_ref, final_norm)


def _wo_mlp(x, z, wo, mlp_args, final_norm):
    M = x.shape[0]
    tm = _pick_tile(M, 512)
    tok = pl.BlockSpec((tm, D_MODEL), lambda i: (i, 0))
    return pl.pallas_call(
        functools.partial(_wo_mlp_kernel, final_norm=final_norm),
        grid=(M // tm,),
        in_specs=[tok, tok, _const_spec(wo.shape)] + _mlp_specs(mlp_args[4]),
        out_specs=tok,
        out_shape=SDS((M, D_MODEL), F32),
        compiler_params=_cparams(("parallel",)),
        name="wo_mlp",
    )(x, z, wo, *mlp_args[:4])


def _pool_block(x, state, n_hist, g, w_bf, scale, mlp_args=(), final_norm=False):
    B = x.shape[0]
    if state is None:
        hist = jnp.zeros((B, POOL_PAD, D_MODEL), F32)
    else:
        hist = jnp.pad(state, ((0, 0), (POOL_PAD - POOL_HIST, 0), (0, 0)))
    out, hs = _pool_layer(x, hist, n_hist, g, w_bf, scale, mlp_args, final_norm)
    return out, hs[:, POOL_PAD - POOL_HIST:]


def _attn_block(x, cache, g, w_in_pad, wo_bf):
    B, T, _ = x.shape
    past = 0 if cache is None else cache[0].shape[2]
    tab = _rope_tables(past, T)
    q, qi, kis, k_new, v_new, ki_new, kb, vb, kib = _attn_project(x, g, w_in_pad, tab)
    topk = min(TOPK_MAX, (past + T) // 4)
    if cache is not None:
        return _attn_decode_layer(x, q, qi, kis, kib, kb, vb, cache, wo_bf, topk), k_new, v_new, ki_new
    tq = _pick_tile(T, ATTN_Q_BLOCK)
    out = x
    for j0 in range(T // tq):
        l_g = (j0 + 1) * tq
        out = _attn_pipe_layer(out, q, qi, kis, kib, kb, vb, wo_bf, 0, l_g, topk, tq, j0, l_g)
    return out, k_new, v_new, ki_new


def _rwkv_block(x, shift_prev, wkv0, g, rw, mlp_args, final_norm):
    B, T, _ = x.shape
    r, lw, k, v, na, b, gate, hl = _rwkv_project(x, shift_prev, g, rw)
    z, s_t = _rwkv_scan(r, lw, k, v, na, b, gate, wkv0, rw)
    flat = lambda a: a.reshape(B * T, D_MODEL)
    out = _wo_mlp(flat(x), flat(z), rw["w_o"], mlp_args, final_norm).reshape(B, T, D_MODEL)
    return out, hl[:, SUBLANES - 1:], s_t


def kernel(x_prompt, x_sample, state_pool, cache_k, cache_v, cache_kidx, state_shift, state_wkv, ln1_g, ln2_g, w_up, w_down, ln_f_g, pool_w, pool_scale, attn_w_in, attn_w_out, rwkv_mix, rwkv_w0, rwkv_w1, rwkv_w2, rwkv_a0, rwkv_a1, rwkv_a2, rwkv_g1, rwkv_g2, rwkv_k_k, rwkv_k_a, rwkv_r_k, rwkv_w_r, rwkv_w_k, rwkv_w_v, rwkv_w_o, rwkv_lnx_g, rwkv_lnx_b):
    xp, xs = x_prompt, x_sample
    bp, sp, _ = xp.shape
    bs, ss, _ = xs.shape
    past = cache_k.shape[2]
    row = lambda a: a.reshape(1, -1)
    bf = lambda a: a.astype(BF16)
    wu_all, wd_all = bf(w_up), bf(w_down)
    head_of = jnp.arange(MXU_DIM) // RWKV_HEAD
    bd_mat = (head_of[:, None] == head_of[None, :]).astype(BF16)
    outs = {n: [] for n in ("pool_p", "pool_s", "k_p", "k_s", "v_p", "v_s", "ki_p", "ki_s",
                            "sh_p", "sh_s", "wkv_p", "wkv_s")}
    for i in range(DEPTH):
        j = i // N_MIXERS
        g1 = row(ln1_g[i])
        last = i == DEPTH - 1
        mlp_args = (row(ln2_g[i]), wu_all, wd_all, row(ln_f_g), i)
        prompt_mlp_done = sample_mlp_done = False
        if i % N_MIXERS == 0:
            w_bf = bf(pool_w[j])
            sc = row(pool_scale[j])
            prompt_mlp_done = sp >= POOL_MLP_MIN_ROWS
            xp, st_p = _pool_block(xp, None, 0, g1, w_bf, sc, mlp_args if prompt_mlp_done else (), last)
            xs, st_s = _pool_block(xs, state_pool[j], past, g1, w_bf, sc)
            outs["pool_p"].append(st_p)
            outs["pool_s"].append(st_s)
        elif i % N_MIXERS == 1:
            w_in_pad = jnp.pad(bf(attn_w_in[j]), ((0, 0), (0, ATTN_IN_PAD - ATTN_IN_W)))
            wo_bf = bf(attn_w_out[j])
            xp, kp, vp, kip = _attn_block(xp, None, g1, w_in_pad, wo_bf)
            xs, kn, vn, kin = _attn_block(xs, (cache_k, cache_v, cache_kidx, j), g1, w_in_pad, wo_bf)
            for n, a in (("k_p", kp), ("v_p", vp), ("ki_p", kip), ("k_s", kn), ("v_s", vn), ("ki_s", kin)):
                outs[n].append(a)
        else:
            rw = dict(mix=rwkv_mix[j], w0=row(rwkv_w0[j]), w1=bf(rwkv_w1[j]), w2=bf(rwkv_w2[j]),
                      a0=row(rwkv_a0[j]), a1=bf(rwkv_a1[j]), a2=bf(rwkv_a2[j]), g1=bf(rwkv_g1[j]),
                      g2=bf(rwkv_g2[j]), k_k=row(rwkv_k_k[j]), k_a=row(rwkv_k_a[j]), r_k=row(rwkv_r_k[j]),
                      w_r=bf(rwkv_w_r[j]), w_k=bf(rwkv_w_k[j]), w_v=bf(rwkv_w_v[j]), w_o=bf(rwkv_w_o[j]),
                      lnx_g=row(rwkv_lnx_g[j]), lnx_b=row(rwkv_lnx_b[j]), bd=bd_mat)
            zero_shift = jnp.zeros((bp, 1, D_MODEL), F32)
            zero_wkv = jnp.zeros((bp, RWKV_HEADS, RWKV_HEAD, RWKV_HEAD), F32)
            xp, shp, wp = _rwkv_block(xp, zero_shift, zero_wkv, g1, rw, mlp_args, last)
            xs, shs, wsn = _rwkv_block(xs, state_shift[j], state_wkv[j], g1, rw, mlp_args, last)
            prompt_mlp_done = sample_mlp_done = True
            outs["sh_p"].append(shp)
            outs["sh_s"].append(shs)
            outs["wkv_p"].append(wp)
            outs["wkv_s"].append(wsn)
        if not prompt_mlp_done:
            xp = _mlp(xp.reshape(bp * sp, D_MODEL), mlp_args, last).reshape(bp, sp, D_MODEL)
        if not sample_mlp_done:
            xs = _mlp(xs.reshape(bs * ss, D_MODEL), mlp_args, last).reshape(bs, ss, D_MODEL)
    st = lambda n: outs[n][0][None] if len(outs[n]) == 1 else jnp.stack(outs[n], 0)
    return (xp, xs, st("pool_p"), st("pool_s"), st("k_p"), st("k_s"), st("v_p"), st("v_s"),
            st("ki_p"), st("ki_s"), st("sh_p"), st("sh_s"), st("wkv_p"), st("wkv_s"))
```

```python
import functools
import math

import jax
import jax.numpy as jnp
import numpy as np
from jax import lax
from jax.experimental import pallas as pl
from jax.experimental.pallas import tpu as pltpu

F32 = jnp.float32
BF16 = jnp.bfloat16
SDS = jax.ShapeDtypeStruct

D_MODEL = 1024
DEPTH = 4
N_MIXERS = 3
CHUNK = 64
D_FF = 4 * D_MODEL
RMS_EPS = 1e-6
POOL_WINDOWS = (2, 4, 8, 16)
POOL_GROUPS = 4
POOL_GW = D_MODEL // POOL_GROUPS
POOL_HIST = max(POOL_WINDOWS) - 1
POOL_PAD = POOL_HIST + 1
ATTN_HEADS = 8
ATTN_HEAD_DIM = D_MODEL // ATTN_HEADS
IDX_HEADS = 8
IDX_DIM = 64
TOPK_MAX = 256
ROPE_THETA = 500000.0
ROPE_FRACTION = 4
NEG_INF = -1e30
Q_W = ATTN_HEADS * ATTN_HEAD_DIM
OFF_K = Q_W
OFF_V = OFF_K + ATTN_HEAD_DIM
OFF_QI = OFF_V + ATTN_HEAD_DIM
OFF_KI = OFF_QI + IDX_HEADS * IDX_DIM
OFF_WI = OFF_KI + IDX_DIM
ATTN_IN_W = OFF_WI + IDX_HEADS
RWKV_HEAD = 64
RWKV_HEADS = D_MODEL // RWKV_HEAD
LNX_EPS = 64e-5

LANES = 128
SUBLANES = 8
ATTN_IN_PAD = ((ATTN_IN_W + LANES - 1) // LANES) * LANES
MXU_DIM = 256
VMEM_LIMIT = 56 * 1024 * 1024
INT_MIN = -2 ** 31
HALF_NEG_KEY = int(np.float32(0.5 * NEG_INF).view(np.int32)) ^ 0x7FFFFFFF
POOL_MLP_MIN_ROWS = 512
POOL_MLP_SUBTILES = 2
ATTN_Q_BLOCK = 256
DECODE_SEQS = 2
APROJ_ROWS = 1024
RPROJ_ROWS = 512
SCAN_CHUNK = 64
SCAN_SEQS = 2
FF_CHUNK = 1024


def _cparams(sem):
    return pltpu.CompilerParams(dimension_semantics=sem, vmem_limit_bytes=VMEM_LIMIT)


def _const_spec(shape):
    nd = len(shape)
    return pl.BlockSpec(shape, lambda *_: (0,) * nd, pipeline_mode=pl.Buffered(1))


def _rms(x, g):
    ms = jnp.mean(x * x, axis=-1, keepdims=True)
    return x * lax.rsqrt(ms + RMS_EPS) * g


def _dot(a, b):
    return jnp.dot(a, b, preferred_element_type=F32)


def _dot_nt(a, b):
    return lax.dot_general(a, b, (((1,), (1,)), ((), ())), preferred_element_type=F32)


def _dot_tn(a, b):
    return lax.dot_general(a, b, (((0,), (0,)), ((), ())), preferred_element_type=F32)


def _head_sum(z, bd_ref):
    bd = bd_ref[...]
    hi = z.astype(BF16)
    lo = (z - hi.astype(F32)).astype(BF16)
    outs = []
    for c in range(D_MODEL // MXU_DIM):
        cs = slice(c * MXU_DIM, (c + 1) * MXU_DIM)
        outs.append(_dot(hi[:, cs], bd) + _dot(lo[:, cs], bd))
    return jnp.concatenate(outs, axis=1)


def _sigmoid(x):
    return 1.0 / (1.0 + jnp.exp(-x))


def _pick_tile(n, pref):
    t = min(n, pref)
    assert n % t == 0, (n, t)
    return t


def _mlp_apply(x, g_ref, wu_ref, wd_ref, gf_ref, final_norm, side=()):
    h = _rms(x, g_ref[...]).astype(BF16)
    acc = x
    for j in range(D_FF // FF_CHUNK):
        u = _dot(h, wu_ref[:, j * FF_CHUNK:(j + 1) * FF_CHUNK])
        u = jnp.square(jnp.maximum(u, 0.0)).astype(BF16)
        acc = acc + _dot(u, wd_ref[j * FF_CHUNK:(j + 1) * FF_CHUNK, :])
        if j < len(side):
            side[j]()
    if final_norm:
        acc = _rms(acc, gf_ref[...])
    return acc


def _mlp_specs(layer):
    pick = lambda *_: (layer, 0, 0)
    return [_const_spec((1, D_MODEL)),
            pl.BlockSpec((None, D_MODEL, D_FF), pick, pipeline_mode=pl.Buffered(1)),
            pl.BlockSpec((None, D_FF, D_MODEL), pick, pipeline_mode=pl.Buffered(1)),
            _const_spec((1, D_MODEL))]


def _mlp_kernel(x_ref, g_ref, wu_ref, wd_ref, gf_ref, o_ref, *, final_norm):
    o_ref[...] = _mlp_apply(x_ref[...], g_ref, wu_ref, wd_ref, gf_ref, final_norm)


def _mlp(x, mlp_args, final_norm):
    M = x.shape[0]
    tm = _pick_tile(M, 512)
    return pl.pallas_call(
        functools.partial(_mlp_kernel, final_norm=final_norm),
        grid=(M // tm,),
        in_specs=[pl.BlockSpec((tm, D_MODEL), lambda i: (i, 0))] + _mlp_specs(mlp_args[4]),
        out_specs=pl.BlockSpec((tm, D_MODEL), lambda i: (i, 0)),
        out_shape=SDS((M, D_MODEL), F32),
        compiler_params=_cparams(("parallel",)),
        name="mlp",
    )(x, *mlp_args[:4])


def _pool_kernel(x_ref, xp_ref, hist_ref, g_ref, w_ref, sc_ref, *rest, tt, n_hist, mlp, final_norm):
    mlp_refs, (o_ref, hs_ref) = rest[:-2], rest[-2:]
    i = pl.program_id(1)
    g = g_ref[...]
    x = x_ref[...]
    h = _rms(x, g)
    prev = jnp.where(i == 0, hist_ref[...], _rms(xp_ref[...], g))
    hs_ref[...] = h[tt - POOL_PAD:, :]
    full = jnp.concatenate([prev, h], axis=0)
    nsub = POOL_MLP_SUBTILES if mlp else 1
    rows = tt // nsub

    def group(r, gi):
        win = POOL_WINDOWS[gi]
        rs = slice(r * rows, (r + 1) * rows)
        cs = slice(gi * POOL_GW, (gi + 1) * POOL_GW)
        s = full[r * rows:(r + 1) * rows + POOL_PAD, cs]
        d = 1
        while d < win:
            s = s + pltpu.roll(s, d, 0)
            d *= 2
        t1 = lax.broadcasted_iota(jnp.int32, (POOL_PAD, POOL_GW), 0) + (i * tt + r * rows + 1 + n_hist)
        cnt = jnp.concatenate([jnp.minimum(t1, win).astype(F32), jnp.full((rows - POOL_PAD, POOL_GW), win, F32)],
                              axis=0)
        pooled = s[POOL_PAD:, :] / cnt - h[rs, cs]
        return x[rs, cs] + _dot(pooled.astype(BF16), w_ref[gi]) * sc_ref[:, cs]

    cols = [group(0, gi) for gi in range(POOL_GROUPS)]
    if not mlp:
        o_ref[...] = jnp.concatenate(cols, axis=1)
        return
    outs = []
    for r in range(nsub):
        y = jnp.concatenate(cols, axis=1)
        cols = []
        side = [functools.partial(lambda gi, rn: cols.append(group(rn, gi)), gi, r + 1)
                for gi in range(POOL_GROUPS)] if r + 1 < nsub else []
        outs.append(_mlp_apply(y, *mlp_refs, final_norm, side))
    o_ref[...] = jnp.concatenate(outs, axis=0)


def _pool_layer(x, hist, n_hist, g, w_bf, scale, mlp_args, final_norm):
    B, T, _ = x.shape
    tt = _pick_tile(T, 512)
    rb = tt // POOL_PAD
    return pl.pallas_call(
        functools.partial(_pool_kernel, tt=tt, n_hist=n_hist, mlp=bool(mlp_args), final_norm=final_norm),
        grid=(B, T // tt),
        in_specs=[
            pl.BlockSpec((None, tt, D_MODEL), lambda b, i: (b, i, 0)),
            pl.BlockSpec((None, POOL_PAD, D_MODEL), lambda b, i: (b, jnp.maximum(i * rb - 1, 0), 0)),
            pl.BlockSpec((None, POOL_PAD, D_MODEL), lambda b, i: (b, 0, 0)),
            _const_spec((1, D_MODEL)),
            _const_spec((POOL_GROUPS, POOL_GW, POOL_GW)),
            _const_spec((1, D_MODEL)),
        ] + (_mlp_specs(mlp_args[4]) if mlp_args else []),
        out_specs=[
            pl.BlockSpec((None, tt, D_MODEL), lambda b, i: (b, i, 0)),
            pl.BlockSpec((None, POOL_PAD, D_MODEL), lambda b, i: (b, 0, 0)),
        ],
        out_shape=[SDS((B, T, D_MODEL), F32), SDS((B, POOL_PAD, D_MODEL), F32)],
        compiler_params=_cparams(("parallel", "arbitrary")),
        name="pool_mlp" if mlp_args else "pool",
    )(x, x, hist, g, w_bf, scale, *mlp_args[:4])


def _rope_tables(pos0, T):
    pos = np.arange(pos0, pos0 + T, dtype=np.float64)

    def head(d):
        rd = d // ROPE_FRACTION
        half = rd // 2
        inv = ROPE_THETA ** (-np.arange(half, dtype=np.float64) / half)
        ang = pos[:, None] * inv[None, :]
        cos, sin = np.cos(ang), np.sin(ang)
        z = lambda n: np.zeros((T, n))
        c = np.concatenate([cos, cos, np.ones((T, d - rd))], axis=1)
        sa = np.concatenate([-sin, z(d - half)], axis=1)
        sb = np.concatenate([z(half), sin, z(d - rd)], axis=1)
        return c, sa, sb

    qa = head(ATTN_HEAD_DIM)
    ia = head(IDX_DIM)
    i2 = tuple(np.concatenate([t, t], axis=1) for t in ia)
    ones, zeros = np.ones((T, IDX_DIM)), np.zeros((T, IDX_DIM))
    ik = (np.concatenate([ia[0], ones], axis=1), np.concatenate([ia[1], zeros], axis=1),
          np.concatenate([ia[2], zeros], axis=1))
    return jnp.asarray(np.concatenate(list(qa) + list(i2) + list(ik), axis=1), F32)


def _aproj_kernel(x_ref, g_ref, w_ref, tab_ref, q_ref, qi_ref, kis_ref, k_ref, v_ref, ki_ref, kb_ref, vb_ref, kib_ref,
                  *, nb, tt):
    h = _rms(x_ref[...].reshape(nb * tt, D_MODEL), g_ref[...]).astype(BF16)
    p = _dot(h, w_ref[...])

    def rope(x, kind, half):
        c = tab_ref[:, (3 * kind) * LANES:(3 * kind + 1) * LANES]
        sa = tab_ref[:, (3 * kind + 1) * LANES:(3 * kind + 2) * LANES]
        sb = tab_ref[:, (3 * kind + 2) * LANES:(3 * kind + 3) * LANES]
        return x * c + pltpu.roll(x, LANES - half, 1) * sa + pltpu.roll(x, half, 1) * sb

    def put(ref, val, *lead):
        for s in range(nb):
            ref[(s,) + lead] = val[s * tt:(s + 1) * tt]

    slab = lambda off: p[:, off:off + LANES]
    qh = ATTN_HEAD_DIM // ROPE_FRACTION // 2
    ih = IDX_DIM // ROPE_FRACTION // 2
    qscale = ATTN_HEAD_DIM ** -0.5 * math.log2(math.e)
    for hd in range(ATTN_HEADS):
        put(q_ref, (rope(slab(hd * LANES), 0, qh) * qscale).astype(BF16), hd)
    k = rope(slab(OFF_K), 0, qh)
    v = slab(OFF_V)
    put(qi_ref, jnp.concatenate([rope(slab(OFF_QI + c * LANES), 1, ih).astype(BF16)
                                 for c in range(IDX_HEADS * IDX_DIM // LANES)], axis=1))
    kis = rope(slab(OFF_KI), 2, ih)
    put(kis_ref, kis)
    put(k_ref, k)
    put(v_ref, v)
    put(ki_ref, kis[:, :IDX_DIM])
    put(kb_ref, k.astype(BF16))
    lane = lax.broadcasted_iota(jnp.int32, v.shape, 1)
    put(vb_ref, jnp.concatenate([v.astype(BF16), jnp.where(lane == 0, 1.0, 0.0).astype(BF16)], axis=1))
    put(kib_ref, kis[:, :IDX_DIM].astype(BF16))


def _attn_project(x, g, w_in_pad, tab):
    B, T, _ = x.shape
    tt = _pick_tile(T, APROJ_ROWS)
    nb = _pick_tile(B, max(1, APROJ_ROWS // tt))
    if nb > 1:
        tab = jnp.tile(tab, (nb, 1))
    QI_W = IDX_HEADS * IDX_DIM
    widths = (QI_W, LANES, ATTN_HEAD_DIM, ATTN_HEAD_DIM, IDX_DIM, ATTN_HEAD_DIM, 2 * ATTN_HEAD_DIM, IDX_DIM)
    dtypes = (BF16, F32, F32, F32, F32, BF16, BF16, BF16)
    q_spec = pl.BlockSpec((nb, ATTN_HEADS, tt, ATTN_HEAD_DIM), lambda i, b: (b, 0, i, 0))
    return pl.pallas_call(
        functools.partial(_aproj_kernel, nb=nb, tt=tt),
        grid=(T // tt, B // nb),
        in_specs=[
            pl.BlockSpec((nb, tt, D_MODEL), lambda i, b: (b, i, 0)),
            _const_spec((1, D_MODEL)),
            _const_spec((D_MODEL, ATTN_IN_PAD)),
            pl.BlockSpec((nb * tt, 9 * LANES), lambda i, b: (i, 0)),
        ],
        out_specs=[q_spec] + [pl.BlockSpec((nb, tt, w), lambda i, b: (b, i, 0)) for w in widths],
        out_shape=[SDS((B, ATTN_HEADS, T, ATTN_HEAD_DIM), BF16)] + [SDS((B, T, w), d) for w, d in zip(widths, dtypes)],
        compiler_params=_cparams(("arbitrary", "arbitrary")),
        name="attn_proj",
    )(x, g, w_in_pad, tab)


def _score_key(score):
    kb = pltpu.bitcast(score, jnp.int32)
    return jnp.where(kb >= 0, kb, kb ^ jnp.int32(0x7FFFFFFF))


KEY_BITS = 32


def _count(mask, axis):
    return jnp.sum(jnp.where(mask, 1.0, 0.0), axis=axis, keepdims=True)


def _kth_key_step(key_scr, lo_u, it, topk, axis):
    trial_u = lo_u | lax.shift_left(jnp.int32(1), KEY_BITS - 1 - it)
    c = _count(key_scr[...] >= (trial_u ^ jnp.int32(INT_MIN)), axis)
    return jnp.where(c >= topk, trial_u, lo_u)


def _select_topk(key_scr, jstar_scr, kpos, topk, axis):
    step = lambda it, lo_u: _kth_key_step(key_scr, lo_u, it, topk, axis)
    lo_u = lax.fori_loop(0, KEY_BITS, step, jnp.zeros(jstar_scr.shape, jnp.int32))
    return _finish_topk(key_scr, jstar_scr, kpos, lo_u, topk, axis)


def _finish_topk(key_scr, jstar_scr, kpos, lo_u, topk, axis):
    one = jstar_scr.shape
    n_idx = key_scr.shape[axis]
    count = functools.partial(_count, axis=axis)
    lo = lo_u ^ jnp.int32(INT_MIN)
    gt = key_scr[...] > lo
    eq = key_scr[...] == lo
    need = topk - count(gt)
    tie = (lo > jnp.int32(HALF_NEG_KEY)) & (count(eq) > need)
    jstar_scr[...] = jnp.full(one, n_idx, jnp.int32)

    @pl.when(jnp.max(jnp.where(tie, 1.0, 0.0)) > 0.0)
    def _():
        nbits = max(1, (n_idx - 1).bit_length())

        def ibody(it, lo_i):
            trial = lo_i + lax.shift_left(jnp.int32(1), nbits - 1 - it)
            c = count((kpos < trial) & (key_scr[...] == lo))
            return jnp.where(c < need, trial, lo_i)

        lo_i = lax.fori_loop(0, nbits, ibody, jnp.zeros(one, jnp.int32))
        jstar_scr[...] = jnp.where(tie, lo_i, n_idx)

    return gt | (eq & (kpos <= jstar_scr[...]))


def _attn_decode_kernel(x_ref, q_ref, qi_ref, wi_ref, kib_ref, kb_ref, vb_ref, cki_ref, ck_ref, cv_ref, wo_ref,
                        o_ref, ki_scr, k_scr, v_scr, key_scr, bias_scr, jstar_scr, o_scr, *, nb, tq, P, L, topk):
    n_keys = P + tq

    @pl.when(pl.program_id(0) == 0)
    def _():
        lane = lax.broadcasted_iota(jnp.int32, (P, ATTN_HEAD_DIM), 1)
        for s in range(nb):
            v_scr[s, :P, ATTN_HEAD_DIM:] = jnp.where(lane == 0, 1.0, 0.0).astype(BF16)
            ki_scr[s, n_keys:, :] = jnp.zeros((L - n_keys, IDX_DIM), BF16)
            k_scr[s, n_keys:, :] = jnp.zeros((L - n_keys, ATTN_HEAD_DIM), BF16)
            v_scr[s, n_keys:, :] = jnp.zeros((L - n_keys, 2 * ATTN_HEAD_DIM), BF16)

    qpos = lax.broadcasted_iota(jnp.int32, (tq, L), 0) + P
    kpos = lax.broadcasted_iota(jnp.int32, (tq, L), 1)
    adm = (kpos < n_keys) & ((kpos // CHUNK) <= (qpos // CHUNK))
    rows = lambda s: slice(s * tq, (s + 1) * tq)

    for s in range(nb):
        ki_scr[s, :P, :] = cki_ref[s].astype(BF16)
        ki_scr[s, P:n_keys, :] = kib_ref[s]
        k_scr[s, :P, :] = ck_ref[s].astype(BF16)
        k_scr[s, P:n_keys, :] = kb_ref[s]
        v_scr[s, :P, :ATTN_HEAD_DIM] = cv_ref[s].astype(BF16)
        v_scr[s, P:n_keys, :] = vb_ref[s]
        qi = qi_ref[s]
        wis = wi_ref[s] * (IDX_DIM ** -0.5 * IDX_HEADS ** -0.5)
        d_all = _dot_nt(jnp.concatenate([qi[:, h * IDX_DIM:(h + 1) * IDX_DIM] for h in range(IDX_HEADS)], axis=0),
                        ki_scr[s])
        score = jnp.zeros((tq, L), F32)
        for h in range(IDX_HEADS):
            score = score + jnp.maximum(d_all[h * tq:(h + 1) * tq], 0.0) * wis[:, IDX_DIM + h:IDX_DIM + h + 1]
        key_scr[rows(s), :] = _score_key(jnp.where(adm, score, NEG_INF))

    tile = lambda m: jnp.concatenate([m] * nb, axis=0)
    keep = _select_topk(key_scr, jstar_scr, tile(kpos), topk, axis=1)
    bias_scr[...] = jnp.where(tile(adm) & keep, 0.0, NEG_INF)

    for s in range(nb):
        def probs(lg):
            lg = lg + bias_scr[rows(s), :]
            return jnp.exp2(lg - jnp.max(lg, axis=1, keepdims=True)).astype(BF16)

        lg_all = _dot_nt(jnp.concatenate([q_ref[s, h] for h in range(ATTN_HEADS)], axis=0), k_scr[s])
        p_all = jnp.concatenate([probs(lg_all[h * tq:(h + 1) * tq]) for h in range(ATTN_HEADS)], axis=0)
        pv_all = _dot(p_all, v_scr[s])
        for h in range(ATTN_HEADS):
            pv = pv_all[h * tq:(h + 1) * tq]
            o_scr[rows(s), h * ATTN_HEAD_DIM:(h + 1) * ATTN_HEAD_DIM] = (
                pv[:, :ATTN_HEAD_DIM] / pv[:, ATTN_HEAD_DIM:ATTN_HEAD_DIM + 1]).astype(BF16)
    attn = _dot(o_scr[...], wo_ref[...])
    for s in range(nb):
        o_ref[s] = x_ref[s] + attn[rows(s)]


def _attn_decode_layer(x, q, qi, kis, kib, kb, vb, cache, wo_bf, topk):
    B, T, _ = x.shape
    ck, cv, cki, layer = cache
    P = ck.shape[2]
    nb = _pick_tile(B, DECODE_SEQS)
    past = lambda w: pl.BlockSpec((None, nb, P, w), lambda b: (layer, b, 0, 0))
    L = ((P + T + LANES - 1) // LANES) * LANES
    assert T * max(IDX_HEADS, ATTN_HEADS) <= MXU_DIM, "decode block: all heads of all new tokens in one row tile"
    assert P % 16 == 0 and T % 16 == 0, "bf16 row tiles"
    QI_W = IDX_HEADS * IDX_DIM
    seq = lambda *shape: pl.BlockSpec((nb,) + shape, lambda b: (b,) + (0,) * len(shape))
    return pl.pallas_call(
        functools.partial(_attn_decode_kernel, nb=nb, tq=T, P=P, L=L, topk=topk),
        grid=(B // nb,),
        in_specs=[
            seq(T, D_MODEL), seq(ATTN_HEADS, T, ATTN_HEAD_DIM), seq(T, QI_W), seq(T, LANES),
            seq(T, IDX_DIM), seq(T, ATTN_HEAD_DIM), seq(T, 2 * ATTN_HEAD_DIM),
            past(IDX_DIM), past(ATTN_HEAD_DIM), past(ATTN_HEAD_DIM),
            _const_spec((Q_W, D_MODEL)),
        ],
        out_specs=seq(T, D_MODEL),
        out_shape=SDS((B, T, D_MODEL), F32),
        scratch_shapes=[
            pltpu.VMEM((nb, L, IDX_DIM), BF16),
            pltpu.VMEM((nb, L, ATTN_HEAD_DIM), BF16),
            pltpu.VMEM((nb, L, 2 * ATTN_HEAD_DIM), BF16),
            pltpu.VMEM((nb * T, L), jnp.int32),
            pltpu.VMEM((nb * T, L), F32),
            pltpu.VMEM((nb * T, 1), jnp.int32),
            pltpu.VMEM((nb * T, Q_W), BF16),
        ],
        compiler_params=_cparams(("arbitrary",)),
        name="attn_decode",
    )(x, q, qi, kis, kib, kb, vb, cki, ck, cv, wo_bf)


def _attn_pipe_kernel(x_ref, q_ref, qi_ref, wi_ref, ki_ref, k_ref, v_ref, wo_ref, o_ref,
                      key_scr, bias_scr, lg_scr, jstar_scr, o_scr, *, tq, L, q_off, n_keys, topk):
    s = pl.program_id(0)
    cur = s % 2

    @pl.when(s == 0)
    def _():
        bias_scr[...] = jnp.zeros(bias_scr.shape, F32)

    qpos = lax.broadcasted_iota(jnp.int32, (tq, L), 0) + q_off
    kpos = lax.broadcasted_iota(jnp.int32, (tq, L), 1)
    adm = (kpos < n_keys) & ((kpos // CHUNK) <= (qpos // CHUNK))

    select_all = L <= topk
    if not select_all:
        qi = qi_ref[...]
        ki = ki_ref[...]
        wis = wi_ref[...] * (IDX_DIM ** -0.5 * IDX_HEADS ** -0.5)
        score = jnp.zeros((tq, L), F32)
        for h in range(IDX_HEADS):
            d = _dot_nt(qi[:, h * IDX_DIM:(h + 1) * IDX_DIM], ki)
            score = score + jnp.maximum(d, 0.0) * wis[:, IDX_DIM + h:IDX_DIM + h + 1]
        key_scr[...] = _score_key(jnp.where(adm, score, NEG_INF))

    steps_per_head = KEY_BITS // ATTN_HEADS

    lg_scr[0] = _dot_nt(q_ref[0], k_ref[...])

    def head(h, lo_u):
        if select_all:
            step = lambda i, lo: lo
        else:
            step = lambda i, lo: _kth_key_step(key_scr, lo, h * steps_per_head + i, topk, 1)
        lo_u = step(0, lo_u)
        lg = lg_scr[h % 2] + bias_scr[1 - cur]
        lg_scr[(h + 1) % 2] = _dot_nt(q_ref[jnp.minimum(h + 1, ATTN_HEADS - 1)], k_ref[...])
        lo_u = step(1, lo_u)
        p = jnp.exp2(lg - jnp.max(lg, axis=1, keepdims=True)).astype(BF16)
        lo_u = step(2, lo_u)
        pv = _dot(p, v_ref[...])
        for i in range(3, steps_per_head):
            lo_u = step(i, lo_u)
        o_scr[h] = (pv[:, :ATTN_HEAD_DIM] / pv[:, ATTN_HEAD_DIM:ATTN_HEAD_DIM + 1]).astype(BF16)
        return lo_u

    lo_u = lax.fori_loop(0, ATTN_HEADS, head, jnp.zeros((tq, 1), jnp.int32))
    attn = _dot(jnp.concatenate([o_scr[h] for h in range(ATTN_HEADS)], axis=1), wo_ref[...])
    o_ref[...] = x_ref[...] + jnp.where(s > 0, attn, 0.0)

    keep = adm if select_all else adm & _finish_topk(key_scr, jstar_scr, kpos, lo_u, topk, 1)
    bias_scr[cur] = jnp.where(keep, 0.0, NEG_INF)


def _attn_pipe_layer(x, q, qi, kis, ki_all, k_all, v_all, wo_bf, q_off, n_keys, topk, tq, j0, L):
    B, T, _ = x.shape
    QI_W = IDX_HEADS * IDX_DIM
    prev = lambda s: jnp.maximum(s - 1, 0)
    this = lambda s: jnp.minimum(s, B - 1)
    return pl.pallas_call(
        functools.partial(_attn_pipe_kernel, tq=tq, L=L, q_off=q_off + j0 * tq, n_keys=n_keys, topk=topk),
        grid=(B + 1,),
        in_specs=[
            pl.BlockSpec((None, tq, D_MODEL), lambda s: (prev(s), j0, 0)),
            pl.BlockSpec((None, ATTN_HEADS, tq, ATTN_HEAD_DIM), lambda s: (prev(s), 0, j0, 0)),
            pl.BlockSpec((None, tq, QI_W), lambda s: (this(s), j0, 0)),
            pl.BlockSpec((None, tq, LANES), lambda s: (this(s), j0, 0)),
            pl.BlockSpec((None, L, IDX_DIM), lambda s: (this(s), 0, 0)),
            pl.BlockSpec((None, L, ATTN_HEAD_DIM), lambda s: (prev(s), 0, 0)),
            pl.BlockSpec((None, L, 2 * ATTN_HEAD_DIM), lambda s: (prev(s), 0, 0)),
            _const_spec((Q_W, D_MODEL)),
        ],
        out_specs=pl.BlockSpec((None, tq, D_MODEL), lambda s: (prev(s), j0, 0)),
        out_shape=SDS((B, T, D_MODEL), F32),
        input_output_aliases={0: 0},
        scratch_shapes=[
            pltpu.VMEM((tq, L), jnp.int32),
            pltpu.VMEM((2, tq, L), F32),
            pltpu.VMEM((2, tq, L), F32),
            pltpu.VMEM((tq, 1), jnp.int32),
            pltpu.VMEM((ATTN_HEADS, tq, ATTN_HEAD_DIM), BF16),
        ],
        compiler_params=_cparams(("arbitrary",)),
        name="attn_pipe",
    )(x, q, qi, kis, ki_all, k_all, v_all, wo_bf)


def _rproj_kernel(x_ref, xp_ref, sh_ref, g_ref, mix_ref, w0_ref, w1_ref, w2_ref, a0_ref, a1_ref, a2_ref,
                  g1_ref, g2_ref, kk_ref, ka_ref, wr_ref, wk_ref, wv_ref, bd_ref,
                  r_o, lw_o, k_o, v_o, na_o, b_o, g_o, hl_o, *, nb, tt):
    i = pl.program_id(1)
    g = g_ref[...]
    row = lax.broadcasted_iota(jnp.int32, (tt, D_MODEL), 0)
    hs, xxs = [], []
    for s in range(nb):
        h_s = _rms(x_ref[s], g)
        hl_o[s] = h_s[tt - SUBLANES:, :]
        prev = jnp.where(i == 0, sh_ref[s], _rms(xp_ref[s, SUBLANES - 1:SUBLANES, :], g))
        hs.append(h_s)
        xxs.append(jnp.where(row == 0, prev, pltpu.roll(h_s, 1, 0)) - h_s)
    h = jnp.concatenate(hs, axis=0)
    xx = jnp.concatenate(xxs, axis=0)
    lerp = lambda n: (h + xx * mix_ref[n:n + 1, :]).astype(BF16)
    r = _dot(lerp(0), wr_ref[...])
    wl = w0_ref[...] + _dot(jnp.tanh(_dot(lerp(1), w1_ref[...])).astype(BF16), w2_ref[...])
    lw = -_sigmoid(wl) * math.exp(-0.5)
    k = _dot(lerp(2), wk_ref[...])
    v = _dot(lerp(3), wv_ref[...])
    a = _sigmoid(a0_ref[...] + _dot(_dot(lerp(4), a1_ref[...]).astype(BF16), a2_ref[...]))
    gate = _dot(_sigmoid(_dot(lerp(5), g1_ref[...])).astype(BF16), g2_ref[...])
    kk = k * kk_ref[...]
    kk = kk * lax.rsqrt(jnp.maximum(_head_sum(kk * kk, bd_ref), 1e-24))
    outs = ((r_o, r), (lw_o, lw), (k_o, k * (1.0 + (a - 1.0) * ka_ref[...])), (v_o, v), (na_o, -kk), (b_o, kk * a),
            (g_o, gate))
    for ref, val in outs:
        for s in range(nb):
            ref[s] = val[s * tt:(s + 1) * tt]


def _rwkv_project(x, shift_prev, g, rw):
    B, T, _ = x.shape
    tt = _pick_tile(T, RPROJ_ROWS)
    nb = _pick_tile(B, max(1, RPROJ_ROWS // tt))
    rb = tt // SUBLANES
    tok = pl.BlockSpec((nb, tt, D_MODEL), lambda b, i: (b, i, 0))
    consts = [g, rw["mix"], rw["w0"], rw["w1"], rw["w2"], rw["a0"], rw["a1"], rw["a2"], rw["g1"], rw["g2"],
              rw["k_k"], rw["k_a"], rw["w_r"], rw["w_k"], rw["w_v"], rw["bd"]]
    return pl.pallas_call(
        functools.partial(_rproj_kernel, nb=nb, tt=tt),
        grid=(B // nb, T // tt),
        in_specs=[
            tok,
            pl.BlockSpec((nb, SUBLANES, D_MODEL), lambda b, i: (b, jnp.maximum(i * rb - 1, 0), 0)),
            pl.BlockSpec((nb, 1, D_MODEL), lambda b, i: (b, 0, 0)),
        ] + [_const_spec(c.shape) for c in consts],
        out_specs=[tok] * 7 + [pl.BlockSpec((nb, SUBLANES, D_MODEL), lambda b, i: (b, 0, 0))],
        out_shape=[SDS((B, T, D_MODEL), F32)] * 7 + [SDS((B, SUBLANES, D_MODEL), F32)],
        compiler_params=_cparams(("parallel", "arbitrary")),
        name="rwkv_proj",
    )(x, x, shift_prev, *consts)


def _scan_kernel(r_ref, lw_ref, k_ref, v_ref, a_ref, b_ref, g_ref, s0_ref, rk_ref, lg_ref, lb_ref,
                 y_ref, st_ref, s_scr, *, nb, tt, C):
    N = RWKV_HEAD
    assert C == N and 2 * N == LANES

    hp = LANES // N
    zero = jnp.zeros((N, N), F32)

    @pl.when(pl.program_id(1) == 0)
    def _():
        for s in range(nb):
            for p in range(D_MODEL // LANES):
                blocks = [jnp.concatenate([s0_ref[s, hp * p + h] if g == h else zero for g in range(hp)], axis=1)
                          for h in range(hp)]
                s_scr[s, p] = jnp.concatenate(blocks, axis=0).T

    row_w = lax.broadcasted_iota(jnp.int32, (C, D_MODEL), 0)
    lane = lax.broadcasted_iota(jnp.int32, (C, LANES), 1)
    h0 = lane < N
    r1 = lax.broadcasted_iota(jnp.int32, (C, 2 * C), 0)
    c1 = lax.broadcasted_iota(jnp.int32, (C, 2 * C), 1) % C
    strict = r1 > c1
    r2 = lax.broadcasted_iota(jnp.int32, (C, 4 * C), 0)
    c2 = lax.broadcasted_iota(jnp.int32, (C, 4 * C), 1) % C
    incl = r2 >= c2
    rs = lax.broadcasted_iota(jnp.int32, (LANES, LANES), 0)
    cs = lax.broadcasted_iota(jnp.int32, (LANES, LANES), 1)
    same_head = (rs < N) == (cs < N)
    eye = rs == cs

    def split(x):
        return jnp.concatenate([jnp.where(h0, x, 0.0), jnp.where(h0, 0.0, x)], axis=0)

    def chunk(c, carry):
        sl = pl.ds(pl.multiple_of(c * C, C), C)
        n_pairs = D_MODEL // LANES
        at, rt, bt, kt, vv, wc = [], [], [], [], [], []
        for s in range(nb):
            lw = lw_ref[s, sl, :]
            cum = lw
            d = 1
            while d < C:
                cum = cum + jnp.where(row_w >= d, pltpu.roll(cum, d, 0), 0.0)
                d *= 2
            e_w = jnp.exp(cum)
            e_n = jnp.exp(-cum)
            rows = (a_ref[s, sl, :] * jnp.exp(cum - lw), r_ref[s, sl, :] * e_w, b_ref[s, sl, :] * e_n,
                    k_ref[s, sl, :] * e_n, v_ref[s, sl, :], e_w[C - 1:C, :])
            for dst, src in zip((at, rt, bt, kt, vv, wc), rows):
                dst.extend(src[:, p * LANES:(p + 1) * LANES] for p in range(n_pairs))
        pairs = range(nb * n_pairs)
        st = [s_scr[p // n_pairs, p % n_pairs] for p in pairs]
        ar = [jnp.concatenate([at[p], rt[p]], axis=0).astype(BF16) for p in pairs]
        v2 = [split(vv[p]).astype(BF16) for p in pairs]
        g = [_dot_nt(ar[p], jnp.concatenate([split(bt[p]), split(kt[p])], axis=0).astype(BF16)) for p in pairs]
        hm = [_dot(ar[p], st[p].astype(BF16)) for p in pairs]
        pw = [jnp.where(strict, g[p][:C, :2 * C], 0.0) for p in pairs]
        u = [hm[p][:C] + _dot(jnp.where(strict, g[p][:C, 2 * C:], 0.0).astype(BF16), v2[p]) for p in pairs]
        n = 1
        while n < C:
            pb = [pw[p].astype(BF16) for p in pairs]
            u = [u[p] + _dot(pb[p], split(u[p]).astype(BF16)) for p in pairs]
            n *= 2
            if n < C:
                pw = [_dot(pb[p], split(pw[p]).astype(BF16)) for p in pairs]
        ys = [hm[p][C:] + _dot(jnp.where(incl, g[p][C:, :], 0.0).astype(BF16),
                               jnp.concatenate([split(u[p]).astype(BF16), v2[p]], axis=0)) for p in pairs]
        for p in pairs:
            bk = jnp.concatenate([bt[p], kt[p]], axis=0).astype(BF16)
            uvp = jnp.concatenate([u[p], vv[p]], axis=0).astype(BF16)
            upd = jnp.where(same_head, _dot_tn(bk, uvp), 0.0)
            w_col = jnp.sum(jnp.where(eye, wc[p], 0.0), axis=1, keepdims=True)
            s_scr[p // n_pairs, p % n_pairs] = (st[p] + upd) * w_col
        inv_n = 1.0 / N

        def head_mean(z):
            s0 = jnp.sum(jnp.where(h0, z, 0.0), axis=1, keepdims=True)
            s1 = jnp.sum(jnp.where(h0, 0.0, z), axis=1, keepdims=True)
            return jnp.where(h0, s0, s1) * inv_n

        for s in range(nb):
            zs = []
            for q in range(n_pairs):
                p = s * n_pairs + q
                ps = slice(q * LANES, (q + 1) * LANES)
                yc = ys[p] - head_mean(ys[p])
                yn = yc * lax.rsqrt(head_mean(yc * yc) + LNX_EPS) * lg_ref[:, ps] + lb_ref[:, ps]
                rk = r_ref[s, sl, ps] * k_ref[s, sl, ps] * rk_ref[:, ps]
                zs.append((yn + head_mean(rk) * N * vv[p]) * g_ref[s, sl, ps])
            y_ref[s, sl, :] = jnp.concatenate(zs, axis=1).astype(BF16)
        return carry

    lax.fori_loop(0, tt // C, chunk, 0)

    @pl.when(pl.program_id(1) == pl.num_programs(1) - 1)
    def _():
        for s in range(nb):
            for p in range(D_MODEL // LANES):
                t = s_scr[s, p].T
                for h in range(hp):
                    st_ref[s, hp * p + h] = t[h * N:(h + 1) * N, h * N:(h + 1) * N]


def _rwkv_scan(r, lw, k, v, na, b, gate, wkv0, rw):
    B, T, _ = r.shape
    C = SCAN_CHUNK
    t_pad = ((T + C - 1) // C) * C
    seqs = (r, lw, k, v, na, b, gate)
    if t_pad != T:
        seqs = tuple(jnp.pad(a, ((0, 0), (0, t_pad - T), (0, 0))) for a in seqs)
    tt = _pick_tile(t_pad, 128)
    nb = _pick_tile(B, SCAN_SEQS)
    n_pairs = D_MODEL // LANES
    tok = pl.BlockSpec((nb, tt, D_MODEL), lambda b_, i: (b_, i, 0))
    st = pl.BlockSpec((nb, RWKV_HEADS, RWKV_HEAD, RWKV_HEAD), lambda b_, i: (b_, 0, 0, 0))
    consts = [rw["r_k"], rw["lnx_g"], rw["lnx_b"]]
    y, s_t = pl.pallas_call(
        functools.partial(_scan_kernel, nb=nb, tt=tt, C=C),
        grid=(B // nb, t_pad // tt),
        in_specs=[tok] * 7 + [st] + [_const_spec(c.shape) for c in consts],
        out_specs=[tok, st],
        out_shape=[SDS((B, t_pad, D_MODEL), BF16), SDS((B, RWKV_HEADS, RWKV_HEAD, RWKV_HEAD), F32)],
        scratch_shapes=[pltpu.VMEM((nb, n_pairs, LANES, LANES), F32)],
        compiler_params=_cparams(("parallel", "arbitrary")),
        name="rwkv_scan",
    )(*seqs, wkv0, *consts)
    return (y if t_pad == T else y[:, :T]), s_t


def _wo_mlp_kernel(x_ref, z_ref, wo_ref, g_ref, wu_ref, wd_ref, gf_ref, o_ref, *, final_norm):
    x = x_ref[...] + _dot(z_ref[...], wo_ref[...])
    o_ref[...] = _mlp_apply(x, g_ref, wu_ref, wd_ref, gf_ref, final_norm)


def _wo_mlp(x, z, wo, mlp_args, final_norm):
    M = x.shape[0]
    tm = _pick_tile(M, 512)
    tok = pl.BlockSpec((tm, D_MODEL), lambda i: (i, 0))
    return pl.pallas_call(
        functools.partial(_wo_mlp_kernel, final_norm=final_norm),
        grid=(M // tm,),
        in_specs=[tok, tok, _const_spec(wo.shape)] + _mlp_specs(mlp_args[4]),
        out_specs=tok,
        out_shape=SDS((M, D_MODEL), F32),
        compiler_params=_cparams(("parallel",)),
        name="wo_mlp",
    )(x, z, wo, *mlp_args[:4])


def _pool_block(x, state, n_hist, g, w_bf, scale, mlp_args=(), final_norm=False):
    B = x.shape[0]
    if state is None:
        hist = jnp.zeros((B, POOL_PAD, D_MODEL), F32)
    else:
        hist = jnp.pad(state, ((0, 0), (POOL_PAD - POOL_HIST, 0), (0, 0)))
    out, hs = _pool_layer(x, hist, n_hist, g, w_bf, scale, mlp_args, final_norm)
    return out, hs[:, POOL_PAD - POOL_HIST:]


def _attn_block(x, cache, g, w_in_pad, wo_bf):
    B, T, _ = x.shape
    past = 0 if cache is None else cache[0].shape[2]
    tab = _rope_tables(past, T)
    q, qi, kis, k_new, v_new, ki_new, kb, vb, kib = _attn_project(x, g, w_in_pad, tab)
    topk = min(TOPK_MAX, (past + T) // 4)
    if cache is not None:
        return _attn_decode_layer(x, q, qi, kis, kib, kb, vb, cache, wo_bf, topk), k_new, v_new, ki_new
    tq = _pick_tile(T, ATTN_Q_BLOCK)
    out = x
    for j0 in range(T // tq):
        l_g = (j0 + 1) * tq
        out = _attn_pipe_layer(out, q, qi, kis, kib, kb, vb, wo_bf, 0, l_g, topk, tq, j0, l_g)
    return out, k_new, v_new, ki_new


def _rwkv_block(x, shift_prev, wkv0, g, rw, mlp_args, final_norm):
    B, T, _ = x.shape
    r, lw, k, v, na, b, gate, hl = _rwkv_project(x, shift_prev, g, rw)
    z, s_t = _rwkv_scan(r, lw, k, v, na, b, gate, wkv0, rw)
    flat = lambda a: a.reshape(B * T, D_MODEL)
    out = _wo_mlp(flat(x), flat(z), rw["w_o"], mlp_args, final_norm).reshape(B, T, D_MODEL)
    return out, hl[:, SUBLANES - 1:], s_t


def kernel(x_prompt, x_sample, state_pool, cache_k, cache_v, cache_kidx, state_shift, state_wkv, ln1_g, ln2_g, w_up, w_down, ln_f_g, pool_w, pool_scale, attn_w_in, attn_w_out, rwkv_mix, rwkv_w0, rwkv_w1, rwkv_w2, rwkv_a0, rwkv_a1, rwkv_a2, rwkv_g1, rwkv_g2, rwkv_k_k, rwkv_k_a, rwkv_r_k, rwkv_w_r, rwkv_w_k, rwkv_w_v, rwkv_w_o, rwkv_lnx_g, rwkv_lnx_b):
    xp, xs = x_prompt, x_sample
    bp, sp, _ = xp.shape
    bs, ss, _ = xs.shape
    past = cache_k.shape[2]
    row = lambda a: a.reshape(1, -1)
    bf = lambda a: a.astype(BF16)
    wu_all, wd_all = bf(w_up), bf(w_down)
    head_of = jnp.arange(MXU_DIM) // RWKV_HEAD
    bd_mat = (head_of[:, None] == head_of[None, :]).astype(BF16)
    outs = {n: [] for n in ("pool_p", "pool_s", "k_p", "k_s", "v_p", "v_s", "ki_p", "ki_s",
                            "sh_p", "sh_s", "wkv_p", "wkv_s")}
    for i in range(DEPTH):
        j = i // N_MIXERS
        g1 = row(ln1_g[i])
        last = i == DEPTH - 1
        mlp_args = (row(ln2_g[i]), wu_all, wd_all, row(ln_f_g), i)
        prompt_mlp_done = sample_mlp_done = False
        if i % N_MIXERS == 0:
            w_bf = bf(pool_w[j])
            sc = row(pool_scale[j])
            prompt_mlp_done = sp >= POOL_MLP_MIN_ROWS
            xp, st_p = _pool_block(xp, None, 0, g1, w_bf, sc, mlp_args if prompt_mlp_done else (), last)
            xs, st_s = _pool_block(xs, state_pool[j], past, g1, w_bf, sc)
            outs["pool_p"].append(st_p)
            outs["pool_s"].append(st_s)
        elif i % N_MIXERS == 1:
            w_in_pad = jnp.pad(bf(attn_w_in[j]), ((0, 0), (0, ATTN_IN_PAD - ATTN_IN_W)))
            wo_bf = bf(attn_w_out[j])
            xp, kp, vp, kip = _attn_block(xp, None, g1, w_in_pad, wo_bf)
            xs, kn, vn, kin = _attn_block(xs, (cache_k, cache_v, cache_kidx, j), g1, w_in_pad, wo_bf)
            for n, a in (("k_p", kp), ("v_p", vp), ("ki_p", kip), ("k_s", kn), ("v_s", vn), ("ki_s", kin)):
                outs[n].append(a)
        else:
            rw = dict(mix=rwkv_mix[j], w0=row(rwkv_w0[j]), w1=bf(rwkv_w1[j]), w2=bf(rwkv_w2[j]),
                      a0=row(rwkv_a0[j]), a1=bf(rwkv_a1[j]), a2=bf(rwkv_a2[j]), g1=bf(rwkv_g1[j]),
                      g2=bf(rwkv_g2[j]), k_k=row(rwkv_k_k[j]), k_a=row(rwkv_k_a[j]), r_k=row(rwkv_r_k[j]),
                      w_r=bf(rwkv_w_r[j]), w_k=bf(rwkv_w_k[j]), w_v=bf(rwkv_w_v[j]), w_o=bf(rwkv_w_o[j]),
                      lnx_g=row(rwkv_lnx_g[j]), lnx_b=row(rwkv_lnx_b[j]), bd=bd_mat)
            zero_shift = jnp.zeros((bp, 1, D_MODEL), F32)
            zero_wkv = jnp.zeros((bp, RWKV_HEADS, RWKV_HEAD, RWKV_HEAD), F32)
            xp, shp, wp = _rwkv_block(xp, zero_shift, zero_wkv, g1, rw, mlp_args, last)
            xs, shs, wsn = _rwkv_block(xs, state_shift[j], state_wkv[j], g1, rw, mlp_args, last)
            prompt_mlp_done = sample_mlp_done = True
            outs["sh_p"].append(shp)
            outs["sh_s"].append(shs)
            outs["wkv_p"].append(wp)
            outs["wkv_s"].append(wsn)
        if not prompt_mlp_done:
            xp = _mlp(xp.reshape(bp * sp, D_MODEL), mlp_args, last).reshape(bp, sp, D_MODEL)
        if not sample_mlp_done:
            xs = _mlp(xs.reshape(bs * ss, D_MODEL), mlp_args, last).reshape(bs, ss, D_MODEL)
    st = lambda n: outs[n][0][None] if len(outs[n]) == 1 else jnp.stack(outs[n], 0)
    return (xp, xs, st("pool_p"), st("pool_s"), st("k_p"), st("k_s"), st("v_p"), st("v_s"),
            st("ki_p"), st("ki_s"), st("sh_p"), st("sh_s"), st("wkv_p"), st("wkv_s"))
```

```python
import functools
import math

import jax
import jax.numpy as jnp
import numpy as np
from jax import lax
from jax.experimental import pallas as pl
from jax.experimental.pallas import tpu as pltpu

F32 = jnp.float32
BF16 = jnp.bfloat16
SDS = jax.ShapeDtypeStruct

D_MODEL = 1024
DEPTH = 4
N_MIXERS = 3
CHUNK = 64
D_FF = 4 * D_MODEL
RMS_EPS = 1e-6
POOL_WINDOWS = (2, 4, 8, 16)
POOL_GROUPS = 4
POOL_GW = D_MODEL // POOL_GROUPS
POOL_HIST = max(POOL_WINDOWS) - 1
POOL_PAD = POOL_HIST + 1
ATTN_HEADS = 8
ATTN_HEAD_DIM = D_MODEL // ATTN_HEADS
IDX_HEADS = 8
IDX_DIM = 64
TOPK_MAX = 256
ROPE_THETA = 500000.0
ROPE_FRACTION = 4
NEG_INF = -1e30
Q_W = ATTN_HEADS * ATTN_HEAD_DIM
OFF_K = Q_W
OFF_V = OFF_K + ATTN_HEAD_DIM
OFF_QI = OFF_V + ATTN_HEAD_DIM
OFF_KI = OFF_QI + IDX_HEADS * IDX_DIM
OFF_WI = OFF_KI + IDX_DIM
ATTN_IN_W = OFF_WI + IDX_HEADS
RWKV_HEAD = 64
RWKV_HEADS = D_MODEL // RWKV_HEAD
LNX_EPS = 64e-5

LANES = 128
SUBLANES = 8
ATTN_IN_PAD = ((ATTN_IN_W + LANES - 1) // LANES) * LANES
MXU_DIM = 256
VMEM_LIMIT = 56 * 1024 * 1024
INT_MIN = -2 ** 31
HALF_NEG_KEY = int(np.float32(0.5 * NEG_INF).view(np.int32)) ^ 0x7FFFFFFF
POOL_MLP_MIN_ROWS = 512
POOL_MLP_SUBTILES = 2
ATTN_Q_BLOCK = 256
MLP_ROWS = 1024
POOL_ROWS = 512
DECODE_SEQS = 2
APROJ_ROWS = 1024
RPROJ_ROWS = 512
SCAN_ROWS = 256
SCAN_CHUNK = 64
SCAN_SEQS = 2
FF_CHUNK = 1024


def _cparams(sem):
    return pltpu.CompilerParams(dimension_semantics=sem, vmem_limit_bytes=VMEM_LIMIT)


def _const_spec(shape):
    nd = len(shape)
    return pl.BlockSpec(shape, lambda *_: (0,) * nd, pipeline_mode=pl.Buffered(1))


def _rms(x, g):
    ms = jnp.mean(x * x, axis=-1, keepdims=True)
    return x * lax.rsqrt(ms + RMS_EPS) * g


def _dot(a, b):
    return jnp.dot(a, b, preferred_element_type=F32)


def _dot_nt(a, b):
    return lax.dot_general(a, b, (((1,), (1,)), ((), ())), preferred_element_type=F32)


def _dot_tn(a, b):
    return lax.dot_general(a, b, (((0,), (0,)), ((), ())), preferred_element_type=F32)


def _head_sum(z, bd_ref):
    bd = bd_ref[...]
    hi = z.astype(BF16)
    lo = (z - hi.astype(F32)).astype(BF16)
    outs = []
    for c in range(D_MODEL // MXU_DIM):
        cs = slice(c * MXU_DIM, (c + 1) * MXU_DIM)
        outs.append(_dot(hi[:, cs], bd) + _dot(lo[:, cs], bd))
    return jnp.concatenate(outs, axis=1)


def _sigmoid(x):
    return 1.0 / (1.0 + jnp.exp(-x))


def _pick_tile(n, pref):
    t = min(n, pref)
    assert n % t == 0, (n, t)
    return t


def _mlp_apply(x, g_ref, wu_ref, wd_ref, gf_ref, final_norm, side=()):
    h = _rms(x, g_ref[...]).astype(BF16)
    acc = x
    for j in range(D_FF // FF_CHUNK):
        u = _dot(h, wu_ref[:, j * FF_CHUNK:(j + 1) * FF_CHUNK])
        u = jnp.square(jnp.maximum(u, 0.0)).astype(BF16)
        acc = acc + _dot(u, wd_ref[j * FF_CHUNK:(j + 1) * FF_CHUNK, :])
        if j < len(side):
            side[j]()
    if final_norm:
        acc = _rms(acc, gf_ref[...])
    return acc


def _mlp_specs(layer):
    pick = lambda *_: (layer, 0, 0)
    return [_const_spec((1, D_MODEL)),
            pl.BlockSpec((None, D_MODEL, D_FF), pick, pipeline_mode=pl.Buffered(1)),
            pl.BlockSpec((None, D_FF, D_MODEL), pick, pipeline_mode=pl.Buffered(1)),
            _const_spec((1, D_MODEL))]


def _mlp_kernel(x_ref, g_ref, wu_ref, wd_ref, gf_ref, o_ref, *, final_norm):
    o_ref[...] = _mlp_apply(x_ref[...], g_ref, wu_ref, wd_ref, gf_ref, final_norm)


def _mlp(x, mlp_args, final_norm):
    M = x.shape[0]
    tm = _pick_tile(M, MLP_ROWS)
    return pl.pallas_call(
        functools.partial(_mlp_kernel, final_norm=final_norm),
        grid=(M // tm,),
        in_specs=[pl.BlockSpec((tm, D_MODEL), lambda i: (i, 0))] + _mlp_specs(mlp_args[4]),
        out_specs=pl.BlockSpec((tm, D_MODEL), lambda i: (i, 0)),
        out_shape=SDS((M, D_MODEL), F32),
        compiler_params=_cparams(("parallel",)),
        name="mlp",
    )(x, *mlp_args[:4])


def _pool_kernel(x_ref, xp_ref, hist_ref, g_ref, w_ref, sc_ref, *rest, tt, n_hist, mlp, final_norm):
    mlp_refs, (o_ref, hs_ref) = rest[:-2], rest[-2:]
    i = pl.program_id(1)
    g = g_ref[...]
    x = x_ref[...]
    h = _rms(x, g)
    prev = jnp.where(i == 0, hist_ref[...], _rms(xp_ref[...], g))
    hs_ref[...] = h[tt - POOL_PAD:, :]
    full = jnp.concatenate([prev, h], axis=0)
    nsub = POOL_MLP_SUBTILES if mlp else 1
    rows = tt // nsub

    def group(r, gi):
        win = POOL_WINDOWS[gi]
        rs = slice(r * rows, (r + 1) * rows)
        cs = slice(gi * POOL_GW, (gi + 1) * POOL_GW)
        s = full[r * rows:(r + 1) * rows + POOL_PAD, cs]
        d = 1
        while d < win:
            s = s + pltpu.roll(s, d, 0)
            d *= 2
        t1 = lax.broadcasted_iota(jnp.int32, (POOL_PAD, POOL_GW), 0) + (i * tt + r * rows + 1 + n_hist)
        cnt = jnp.concatenate([jnp.minimum(t1, win).astype(F32), jnp.full((rows - POOL_PAD, POOL_GW), win, F32)],
                              axis=0)
        pooled = s[POOL_PAD:, :] / cnt - h[rs, cs]
        return x[rs, cs] + _dot(pooled.astype(BF16), w_ref[gi]) * sc_ref[:, cs]

    cols = [group(0, gi) for gi in range(POOL_GROUPS)]
    if not mlp:
        o_ref[...] = jnp.concatenate(cols, axis=1)
        return
    outs = []
    for r in range(nsub):
        y = jnp.concatenate(cols, axis=1)
        cols = []
        side = [functools.partial(lambda gi, rn: cols.append(group(rn, gi)), gi, r + 1)
                for gi in range(POOL_GROUPS)] if r + 1 < nsub else []
        outs.append(_mlp_apply(y, *mlp_refs, final_norm, side))
    o_ref[...] = jnp.concatenate(outs, axis=0)


def _pool_layer(x, hist, n_hist, g, w_bf, scale, mlp_args, final_norm):
    B, T, _ = x.shape
    tt = _pick_tile(T, POOL_ROWS)
    rb = tt // POOL_PAD
    return pl.pallas_call(
        functools.partial(_pool_kernel, tt=tt, n_hist=n_hist, mlp=bool(mlp_args), final_norm=final_norm),
        grid=(B, T // tt),
        in_specs=[
            pl.BlockSpec((None, tt, D_MODEL), lambda b, i: (b, i, 0)),
            pl.BlockSpec((None, POOL_PAD, D_MODEL), lambda b, i: (b, jnp.maximum(i * rb - 1, 0), 0)),
            pl.BlockSpec((None, POOL_PAD, D_MODEL), lambda b, i: (b, 0, 0)),
            _const_spec((1, D_MODEL)),
            _const_spec((POOL_GROUPS, POOL_GW, POOL_GW)),
            _const_spec((1, D_MODEL)),
        ] + (_mlp_specs(mlp_args[4]) if mlp_args else []),
        out_specs=[
            pl.BlockSpec((None, tt, D_MODEL), lambda b, i: (b, i, 0)),
            pl.BlockSpec((None, POOL_PAD, D_MODEL), lambda b, i: (b, 0, 0)),
        ],
        out_shape=[SDS((B, T, D_MODEL), F32), SDS((B, POOL_PAD, D_MODEL), F32)],
        compiler_params=_cparams(("parallel", "arbitrary")),
        name="pool_mlp" if mlp_args else "pool",
    )(x, x, hist, g, w_bf, scale, *mlp_args[:4])


def _rope_tables(pos0, T):
    pos = np.arange(pos0, pos0 + T, dtype=np.float64)

    def head(d):
        rd = d // ROPE_FRACTION
        half = rd // 2
        inv = ROPE_THETA ** (-np.arange(half, dtype=np.float64) / half)
        ang = pos[:, None] * inv[None, :]
        cos, sin = np.cos(ang), np.sin(ang)
        z = lambda n: np.zeros((T, n))
        c = np.concatenate([cos, cos, np.ones((T, d - rd))], axis=1)
        sa = np.concatenate([-sin, z(d - half)], axis=1)
        sb = np.concatenate([z(half), sin, z(d - rd)], axis=1)
        return c, sa, sb

    qa = head(ATTN_HEAD_DIM)
    ia = head(IDX_DIM)
    i2 = tuple(np.concatenate([t, t], axis=1) for t in ia)
    ones, zeros = np.ones((T, IDX_DIM)), np.zeros((T, IDX_DIM))
    ik = (np.concatenate([ia[0], ones], axis=1), np.concatenate([ia[1], zeros], axis=1),
          np.concatenate([ia[2], zeros], axis=1))
    return jnp.asarray(np.concatenate(list(qa) + list(i2) + list(ik), axis=1), F32)


def _aproj_kernel(x_ref, g_ref, w_ref, tab_ref, q_ref, qi_ref, kis_ref, k_ref, v_ref, ki_ref, kb_ref, vb_ref, kib_ref,
                  *, nb, tt):
    h = _rms(x_ref[...].reshape(nb * tt, D_MODEL), g_ref[...]).astype(BF16)
    p = _dot(h, w_ref[...])

    def rope(x, kind, half):
        c = tab_ref[:, (3 * kind) * LANES:(3 * kind + 1) * LANES]
        sa = tab_ref[:, (3 * kind + 1) * LANES:(3 * kind + 2) * LANES]
        sb = tab_ref[:, (3 * kind + 2) * LANES:(3 * kind + 3) * LANES]
        return x * c + pltpu.roll(x, LANES - half, 1) * sa + pltpu.roll(x, half, 1) * sb

    def put(ref, val, *lead):
        for s in range(nb):
            ref[(s,) + lead] = val[s * tt:(s + 1) * tt]

    slab = lambda off: p[:, off:off + LANES]
    qh = ATTN_HEAD_DIM // ROPE_FRACTION // 2
    ih = IDX_DIM // ROPE_FRACTION // 2
    qscale = ATTN_HEAD_DIM ** -0.5 * math.log2(math.e)
    for hd in range(ATTN_HEADS):
        put(q_ref, (rope(slab(hd * LANES), 0, qh) * qscale).astype(BF16), hd)
    k = rope(slab(OFF_K), 0, qh)
    v = slab(OFF_V)
    put(qi_ref, jnp.concatenate([rope(slab(OFF_QI + c * LANES), 1, ih).astype(BF16)
                                 for c in range(IDX_HEADS * IDX_DIM // LANES)], axis=1))
    kis = rope(slab(OFF_KI), 2, ih)
    put(kis_ref, kis)
    put(k_ref, k)
    put(v_ref, v)
    put(ki_ref, kis[:, :IDX_DIM])
    put(kb_ref, k.astype(BF16))
    lane = lax.broadcasted_iota(jnp.int32, v.shape, 1)
    put(vb_ref, jnp.concatenate([v.astype(BF16), jnp.where(lane == 0, 1.0, 0.0).astype(BF16)], axis=1))
    put(kib_ref, kis[:, :IDX_DIM].astype(BF16))


def _attn_project(x, g, w_in_pad, tab):
    B, T, _ = x.shape
    tt = _pick_tile(T, APROJ_ROWS)
    nb = _pick_tile(B, max(1, APROJ_ROWS // tt))
    if nb > 1:
        tab = jnp.tile(tab, (nb, 1))
    QI_W = IDX_HEADS * IDX_DIM
    widths = (QI_W, LANES, ATTN_HEAD_DIM, ATTN_HEAD_DIM, IDX_DIM, ATTN_HEAD_DIM, 2 * ATTN_HEAD_DIM, IDX_DIM)
    dtypes = (BF16, F32, F32, F32, F32, BF16, BF16, BF16)
    q_spec = pl.BlockSpec((nb, ATTN_HEADS, tt, ATTN_HEAD_DIM), lambda i, b: (b, 0, i, 0))
    return pl.pallas_call(
        functools.partial(_aproj_kernel, nb=nb, tt=tt),
        grid=(T // tt, B // nb),
        in_specs=[
            pl.BlockSpec((nb, tt, D_MODEL), lambda i, b: (b, i, 0)),
            _const_spec((1, D_MODEL)),
            _const_spec((D_MODEL, ATTN_IN_PAD)),
            pl.BlockSpec((nb * tt, 9 * LANES), lambda i, b: (i, 0)),
        ],
        out_specs=[q_spec] + [pl.BlockSpec((nb, tt, w), lambda i, b: (b, i, 0)) for w in widths],
        out_shape=[SDS((B, ATTN_HEADS, T, ATTN_HEAD_DIM), BF16)] + [SDS((B, T, w), d) for w, d in zip(widths, dtypes)],
        compiler_params=_cparams(("arbitrary", "arbitrary")),
        name="attn_proj",
    )(x, g, w_in_pad, tab)


def _score_key(score):
    kb = pltpu.bitcast(score, jnp.int32)
    return jnp.where(kb >= 0, kb, kb ^ jnp.int32(0x7FFFFFFF))


KEY_BITS = 32


def _count(mask, axis):
    return jnp.sum(jnp.where(mask, 1.0, 0.0), axis=axis, keepdims=True)


def _kth_key_step(key_scr, lo_u, it, topk, axis):
    trial_u = lo_u | lax.shift_left(jnp.int32(1), KEY_BITS - 1 - it)
    c = _count(key_scr[...] >= (trial_u ^ jnp.int32(INT_MIN)), axis)
    return jnp.where(c >= topk, trial_u, lo_u)


def _select_topk(key_scr, jstar_scr, kpos, topk, axis):
    step = lambda it, lo_u: _kth_key_step(key_scr, lo_u, it, topk, axis)
    lo_u = lax.fori_loop(0, KEY_BITS, step, jnp.zeros(jstar_scr.shape, jnp.int32))
    return _finish_topk(key_scr, jstar_scr, kpos, lo_u, topk, axis)


def _finish_topk(key_scr, jstar_scr, kpos, lo_u, topk, axis):
    one = jstar_scr.shape
    n_idx = key_scr.shape[axis]
    count = functools.partial(_count, axis=axis)
    lo = lo_u ^ jnp.int32(INT_MIN)
    gt = key_scr[...] > lo
    eq = key_scr[...] == lo
    need = topk - count(gt)
    tie = (lo > jnp.int32(HALF_NEG_KEY)) & (count(eq) > need)
    jstar_scr[...] = jnp.full(one, n_idx, jnp.int32)

    @pl.when(jnp.max(jnp.where(tie, 1.0, 0.0)) > 0.0)
    def _():
        nbits = max(1, (n_idx - 1).bit_length())

        def ibody(it, lo_i):
            trial = lo_i + lax.shift_left(jnp.int32(1), nbits - 1 - it)
            c = count((kpos < trial) & (key_scr[...] == lo))
            return jnp.where(c < need, trial, lo_i)

        lo_i = lax.fori_loop(0, nbits, ibody, jnp.zeros(one, jnp.int32))
        jstar_scr[...] = jnp.where(tie, lo_i, n_idx)

    return gt | (eq & (kpos <= jstar_scr[...]))


def _attn_decode_kernel(x_ref, q_ref, qi_ref, wi_ref, kib_ref, kb_ref, vb_ref, cki_ref, ck_ref, cv_ref, wo_ref,
                        o_ref, ki_scr, k_scr, v_scr, key_scr, bias_scr, jstar_scr, o_scr, *, nb, tq, P, L, topk):
    n_keys = P + tq

    @pl.when(pl.program_id(0) == 0)
    def _():
        lane = lax.broadcasted_iota(jnp.int32, (P, ATTN_HEAD_DIM), 1)
        for s in range(nb):
            v_scr[s, :P, ATTN_HEAD_DIM:] = jnp.where(lane == 0, 1.0, 0.0).astype(BF16)
            ki_scr[s, n_keys:, :] = jnp.zeros((L - n_keys, IDX_DIM), BF16)
            k_scr[s, n_keys:, :] = jnp.zeros((L - n_keys, ATTN_HEAD_DIM), BF16)
            v_scr[s, n_keys:, :] = jnp.zeros((L - n_keys, 2 * ATTN_HEAD_DIM), BF16)

    qpos = lax.broadcasted_iota(jnp.int32, (tq, L), 0) + P
    kpos = lax.broadcasted_iota(jnp.int32, (tq, L), 1)
    adm = (kpos < n_keys) & ((kpos // CHUNK) <= (qpos // CHUNK))
    rows = lambda s: slice(s * tq, (s + 1) * tq)

    for s in range(nb):
        ki_scr[s, :P, :] = cki_ref[s].astype(BF16)
        ki_scr[s, P:n_keys, :] = kib_ref[s]
        k_scr[s, :P, :] = ck_ref[s].astype(BF16)
        k_scr[s, P:n_keys, :] = kb_ref[s]
        v_scr[s, :P, :ATTN_HEAD_DIM] = cv_ref[s].astype(BF16)
        v_scr[s, P:n_keys, :] = vb_ref[s]
        qi = qi_ref[s]
        wis = wi_ref[s] * (IDX_DIM ** -0.5 * IDX_HEADS ** -0.5)
        d_all = _dot_nt(jnp.concatenate([qi[:, h * IDX_DIM:(h + 1) * IDX_DIM] for h in range(IDX_HEADS)], axis=0),
                        ki_scr[s])
        score = jnp.zeros((tq, L), F32)
        for h in range(IDX_HEADS):
            score = score + jnp.maximum(d_all[h * tq:(h + 1) * tq], 0.0) * wis[:, IDX_DIM + h:IDX_DIM + h + 1]
        key_scr[rows(s), :] = _score_key(jnp.where(adm, score, NEG_INF))

    tile = lambda m: jnp.concatenate([m] * nb, axis=0)
    keep = _select_topk(key_scr, jstar_scr, tile(kpos), topk, axis=1)
    bias_scr[...] = jnp.where(tile(adm) & keep, 0.0, NEG_INF)

    for s in range(nb):
        def probs(lg):
            lg = lg + bias_scr[rows(s), :]
            return jnp.exp2(lg - jnp.max(lg, axis=1, keepdims=True)).astype(BF16)

        lg_all = _dot_nt(jnp.concatenate([q_ref[s, h] for h in range(ATTN_HEADS)], axis=0), k_scr[s])
        p_all = jnp.concatenate([probs(lg_all[h * tq:(h + 1) * tq]) for h in range(ATTN_HEADS)], axis=0)
        pv_all = _dot(p_all, v_scr[s])
        for h in range(ATTN_HEADS):
            pv = pv_all[h * tq:(h + 1) * tq]
            o_scr[rows(s), h * ATTN_HEAD_DIM:(h + 1) * ATTN_HEAD_DIM] = (
                pv[:, :ATTN_HEAD_DIM] / pv[:, ATTN_HEAD_DIM:ATTN_HEAD_DIM + 1]).astype(BF16)
    attn = _dot(o_scr[...], wo_ref[...])
    for s in range(nb):
        o_ref[s] = x_ref[s] + attn[rows(s)]


def _attn_decode_layer(x, q, qi, kis, kib, kb, vb, cache, wo_bf, topk):
    B, T, _ = x.shape
    ck, cv, cki, layer = cache
    P = ck.shape[2]
    nb = _pick_tile(B, DECODE_SEQS)
    past = lambda w: pl.BlockSpec((None, nb, P, w), lambda b: (layer, b, 0, 0))
    L = ((P + T + LANES - 1) // LANES) * LANES
    assert T * max(IDX_HEADS, ATTN_HEADS) <= MXU_DIM, "decode block: all heads of all new tokens in one row tile"
    assert P % 16 == 0 and T % 16 == 0, "bf16 row tiles"
    QI_W = IDX_HEADS * IDX_DIM
    seq = lambda *shape: pl.BlockSpec((nb,) + shape, lambda b: (b,) + (0,) * len(shape))
    return pl.pallas_call(
        functools.partial(_attn_decode_kernel, nb=nb, tq=T, P=P, L=L, topk=topk),
        grid=(B // nb,),
        in_specs=[
            seq(T, D_MODEL), seq(ATTN_HEADS, T, ATTN_HEAD_DIM), seq(T, QI_W), seq(T, LANES),
            seq(T, IDX_DIM), seq(T, ATTN_HEAD_DIM), seq(T, 2 * ATTN_HEAD_DIM),
            past(IDX_DIM), past(ATTN_HEAD_DIM), past(ATTN_HEAD_DIM),
            _const_spec((Q_W, D_MODEL)),
        ],
        out_specs=seq(T, D_MODEL),
        out_shape=SDS((B, T, D_MODEL), F32),
        scratch_shapes=[
            pltpu.VMEM((nb, L, IDX_DIM), BF16),
            pltpu.VMEM((nb, L, ATTN_HEAD_DIM), BF16),
            pltpu.VMEM((nb, L, 2 * ATTN_HEAD_DIM), BF16),
            pltpu.VMEM((nb * T, L), jnp.int32),
            pltpu.VMEM((nb * T, L), F32),
            pltpu.VMEM((nb * T, 1), jnp.int32),
            pltpu.VMEM((nb * T, Q_W), BF16),
        ],
        compiler_params=_cparams(("arbitrary",)),
        name="attn_decode",
    )(x, q, qi, kis, kib, kb, vb, cki, ck, cv, wo_bf)


def _attn_pipe_kernel(x_ref, q_ref, qi_ref, wi_ref, ki_ref, k_ref, v_ref, wo_ref, o_ref,
                      key_scr, bias_scr, lg_scr, jstar_scr, o_scr, *, tq, L, q_off, n_keys, topk):
    s = pl.program_id(0)
    cur = s % 2

    @pl.when(s == 0)
    def _():
        bias_scr[...] = jnp.zeros(bias_scr.shape, F32)

    qpos = lax.broadcasted_iota(jnp.int32, (tq, L), 0) + q_off
    kpos = lax.broadcasted_iota(jnp.int32, (tq, L), 1)
    adm = (kpos < n_keys) & ((kpos // CHUNK) <= (qpos // CHUNK))

    select_all = L <= topk
    if not select_all:
        qi = qi_ref[...]
        ki = ki_ref[...]
        wis = wi_ref[...] * (IDX_DIM ** -0.5 * IDX_HEADS ** -0.5)
        score = jnp.zeros((tq, L), F32)
        for h in range(IDX_HEADS):
            d = _dot_nt(qi[:, h * IDX_DIM:(h + 1) * IDX_DIM], ki)
            score = score + jnp.maximum(d, 0.0) * wis[:, IDX_DIM + h:IDX_DIM + h + 1]
        key_scr[...] = _score_key(jnp.where(adm, score, NEG_INF))

    steps_per_head = KEY_BITS // ATTN_HEADS

    lg_scr[0] = _dot_nt(q_ref[0], k_ref[...])

    def head(h, lo_u):
        if select_all:
            step = lambda i, lo: lo
        else:
            step = lambda i, lo: _kth_key_step(key_scr, lo, h * steps_per_head + i, topk, 1)
        lo_u = step(0, lo_u)
        lg = lg_scr[h % 2] + bias_scr[1 - cur]
        lg_scr[(h + 1) % 2] = _dot_nt(q_ref[jnp.minimum(h + 1, ATTN_HEADS - 1)], k_ref[...])
        lo_u = step(1, lo_u)
        p = jnp.exp2(lg - jnp.max(lg, axis=1, keepdims=True)).astype(BF16)
        lo_u = step(2, lo_u)
        pv = _dot(p, v_ref[...])
        for i in range(3, steps_per_head):
            lo_u = step(i, lo_u)
        o_scr[h] = (pv[:, :ATTN_HEAD_DIM] / pv[:, ATTN_HEAD_DIM:ATTN_HEAD_DIM + 1]).astype(BF16)
        return lo_u

    lo_u = lax.fori_loop(0, ATTN_HEADS, head, jnp.zeros((tq, 1), jnp.int32))
    attn = _dot(jnp.concatenate([o_scr[h] for h in range(ATTN_HEADS)], axis=1), wo_ref[...])
    o_ref[...] = x_ref[...] + jnp.where(s > 0, attn, 0.0)

    keep = adm if select_all else adm & _finish_topk(key_scr, jstar_scr, kpos, lo_u, topk, 1)
    bias_scr[cur] = jnp.where(keep, 0.0, NEG_INF)


def _attn_pipe_layer(x, q, qi, kis, ki_all, k_all, v_all, wo_bf, q_off, n_keys, topk, tq, j0, L):
    B, T, _ = x.shape
    QI_W = IDX_HEADS * IDX_DIM
    prev = lambda s: jnp.maximum(s - 1, 0)
    this = lambda s: jnp.minimum(s, B - 1)
    return pl.pallas_call(
        functools.partial(_attn_pipe_kernel, tq=tq, L=L, q_off=q_off + j0 * tq, n_keys=n_keys, topk=topk),
        grid=(B + 1,),
        in_specs=[
            pl.BlockSpec((None, tq, D_MODEL), lambda s: (prev(s), j0, 0)),
            pl.BlockSpec((None, ATTN_HEADS, tq, ATTN_HEAD_DIM), lambda s: (prev(s), 0, j0, 0)),
            pl.BlockSpec((None, tq, QI_W), lambda s: (this(s), j0, 0)),
            pl.BlockSpec((None, tq, LANES), lambda s: (this(s), j0, 0)),
            pl.BlockSpec((None, L, IDX_DIM), lambda s: (this(s), 0, 0)),
            pl.BlockSpec((None, L, ATTN_HEAD_DIM), lambda s: (prev(s), 0, 0)),
            pl.BlockSpec((None, L, 2 * ATTN_HEAD_DIM), lambda s: (prev(s), 0, 0)),
            _const_spec((Q_W, D_MODEL)),
        ],
        out_specs=pl.BlockSpec((None, tq, D_MODEL), lambda s: (prev(s), j0, 0)),
        out_shape=SDS((B, T, D_MODEL), F32),
        input_output_aliases={0: 0},
        scratch_shapes=[
            pltpu.VMEM((tq, L), jnp.int32),
            pltpu.VMEM((2, tq, L), F32),
            pltpu.VMEM((2, tq, L), F32),
            pltpu.VMEM((tq, 1), jnp.int32),
            pltpu.VMEM((ATTN_HEADS, tq, ATTN_HEAD_DIM), BF16),
        ],
        compiler_params=_cparams(("arbitrary",)),
        name="attn_pipe",
    )(x, q, qi, kis, ki_all, k_all, v_all, wo_bf)


def _rproj_kernel(x_ref, xp_ref, sh_ref, g_ref, mix_ref, w0_ref, w1_ref, w2_ref, a0_ref, a1_ref, a2_ref,
                  g1_ref, g2_ref, kk_ref, ka_ref, wr_ref, wk_ref, wv_ref, bd_ref,
                  r_o, lw_o, k_o, v_o, na_o, b_o, g_o, hl_o, *, nb, tt):
    i = pl.program_id(1)
    g = g_ref[...]
    row = lax.broadcasted_iota(jnp.int32, (tt, D_MODEL), 0)
    hs, xxs = [], []
    for s in range(nb):
        h_s = _rms(x_ref[s], g)
        hl_o[s] = h_s[tt - SUBLANES:, :]
        prev = jnp.where(i == 0, sh_ref[s], _rms(xp_ref[s, SUBLANES - 1:SUBLANES, :], g))
        hs.append(h_s)
        xxs.append(jnp.where(row == 0, prev, pltpu.roll(h_s, 1, 0)) - h_s)
    h = jnp.concatenate(hs, axis=0)
    xx = jnp.concatenate(xxs, axis=0)
    lerp = lambda n: (h + xx * mix_ref[n:n + 1, :]).astype(BF16)
    r = _dot(lerp(0), wr_ref[...])
    wl = w0_ref[...] + _dot(jnp.tanh(_dot(lerp(1), w1_ref[...])).astype(BF16), w2_ref[...])
    lw = -_sigmoid(wl) * math.exp(-0.5)
    k = _dot(lerp(2), wk_ref[...])
    v = _dot(lerp(3), wv_ref[...])
    a = _sigmoid(a0_ref[...] + _dot(_dot(lerp(4), a1_ref[...]).astype(BF16), a2_ref[...]))
    gate = _dot(_sigmoid(_dot(lerp(5), g1_ref[...])).astype(BF16), g2_ref[...])
    kk = k * kk_ref[...]
    kk = kk * lax.rsqrt(jnp.maximum(_head_sum(kk * kk, bd_ref), 1e-24))
    outs = ((r_o, r), (lw_o, lw), (k_o, k * (1.0 + (a - 1.0) * ka_ref[...])), (v_o, v), (na_o, -kk), (b_o, kk * a),
            (g_o, gate))
    for ref, val in outs:
        for s in range(nb):
            ref[s] = val[s * tt:(s + 1) * tt]


def _rwkv_project(x, shift_prev, g, rw):
    B, T, _ = x.shape
    tt = _pick_tile(T, RPROJ_ROWS)
    nb = _pick_tile(B, max(1, RPROJ_ROWS // tt))
    rb = tt // SUBLANES
    tok = pl.BlockSpec((nb, tt, D_MODEL), lambda b, i: (b, i, 0))
    consts = [g, rw["mix"], rw["w0"], rw["w1"], rw["w2"], rw["a0"], rw["a1"], rw["a2"], rw["g1"], rw["g2"],
              rw["k_k"], rw["k_a"], rw["w_r"], rw["w_k"], rw["w_v"], rw["bd"]]
    return pl.pallas_call(
        functools.partial(_rproj_kernel, nb=nb, tt=tt),
        grid=(B // nb, T // tt),
        in_specs=[
            tok,
            pl.BlockSpec((nb, SUBLANES, D_MODEL), lambda b, i: (b, jnp.maximum(i * rb - 1, 0), 0)),
            pl.BlockSpec((nb, 1, D_MODEL), lambda b, i: (b, 0, 0)),
        ] + [_const_spec(c.shape) for c in consts],
        out_specs=[tok] * 7 + [pl.BlockSpec((nb, SUBLANES, D_MODEL), lambda b, i: (b, 0, 0))],
        out_shape=[SDS((B, T, D_MODEL), F32)] * 7 + [SDS((B, SUBLANES, D_MODEL), F32)],
        compiler_params=_cparams(("parallel", "arbitrary")),
        name="rwkv_proj",
    )(x, x, shift_prev, *consts)


def _scan_kernel(r_ref, lw_ref, k_ref, v_ref, a_ref, b_ref, g_ref, s0_ref, rk_ref, lg_ref, lb_ref,
                 y_ref, st_ref, s_scr, *, nb, tt, C):
    N = RWKV_HEAD
    assert C == N and 2 * N == LANES

    hp = LANES // N
    zero = jnp.zeros((N, N), F32)

    @pl.when(pl.program_id(1) == 0)
    def _():
        for s in range(nb):
            for p in range(D_MODEL // LANES):
                blocks = [jnp.concatenate([s0_ref[s, hp * p + h] if g == h else zero for g in range(hp)], axis=1)
                          for h in range(hp)]
                s_scr[s, p] = jnp.concatenate(blocks, axis=0).T

    row_w = lax.broadcasted_iota(jnp.int32, (C, D_MODEL), 0)
    lane = lax.broadcasted_iota(jnp.int32, (C, LANES), 1)
    h0 = lane < N
    r1 = lax.broadcasted_iota(jnp.int32, (C, 2 * C), 0)
    c1 = lax.broadcasted_iota(jnp.int32, (C, 2 * C), 1) % C
    strict = r1 > c1
    r2 = lax.broadcasted_iota(jnp.int32, (C, 4 * C), 0)
    c2 = lax.broadcasted_iota(jnp.int32, (C, 4 * C), 1) % C
    incl = r2 >= c2
    rs = lax.broadcasted_iota(jnp.int32, (LANES, LANES), 0)
    cs = lax.broadcasted_iota(jnp.int32, (LANES, LANES), 1)
    same_head = (rs < N) == (cs < N)
    eye = rs == cs

    def split(x):
        return jnp.concatenate([jnp.where(h0, x, 0.0), jnp.where(h0, 0.0, x)], axis=0)

    def chunk(c, carry):
        sl = pl.ds(pl.multiple_of(c * C, C), C)
        n_pairs = D_MODEL // LANES
        at, rt, bt, kt, vv, wc = [], [], [], [], [], []
        for s in range(nb):
            lw = lw_ref[s, sl, :]
            cum = lw
            d = 1
            while d < C:
                cum = cum + jnp.where(row_w >= d, pltpu.roll(cum, d, 0), 0.0)
                d *= 2
            e_w = jnp.exp(cum)
            e_n = jnp.exp(-cum)
            rows = (a_ref[s, sl, :] * jnp.exp(cum - lw), r_ref[s, sl, :] * e_w, b_ref[s, sl, :] * e_n,
                    k_ref[s, sl, :] * e_n, v_ref[s, sl, :], e_w[C - 1:C, :])
            for dst, src in zip((at, rt, bt, kt, vv, wc), rows):
                dst.extend(src[:, p * LANES:(p + 1) * LANES] for p in range(n_pairs))
        pairs = range(nb * n_pairs)
        st = [s_scr[p // n_pairs, p % n_pairs] for p in pairs]
        ar = [jnp.concatenate([at[p], rt[p]], axis=0).astype(BF16) for p in pairs]
        v2 = [split(vv[p]).astype(BF16) for p in pairs]
        g = [_dot_nt(ar[p], jnp.concatenate([split(bt[p]), split(kt[p])], axis=0).astype(BF16)) for p in pairs]
        hm = [_dot(ar[p], st[p].astype(BF16)) for p in pairs]
        pw = [jnp.where(strict, g[p][:C, :2 * C], 0.0) for p in pairs]
        u = [hm[p][:C] + _dot(jnp.where(strict, g[p][:C, 2 * C:], 0.0).astype(BF16), v2[p]) for p in pairs]
        n = 1
        while n < C:
            pb = [pw[p].astype(BF16) for p in pairs]
            u = [u[p] + _dot(pb[p], split(u[p]).astype(BF16)) for p in pairs]
            n *= 2
            if n < C:
                pw = [_dot(pb[p], split(pw[p]).astype(BF16)) for p in pairs]
        ys = [hm[p][C:] + _dot(jnp.where(incl, g[p][C:, :], 0.0).astype(BF16),
                               jnp.concatenate([split(u[p]).astype(BF16), v2[p]], axis=0)) for p in pairs]
        for p in pairs:
            bk = jnp.concatenate([bt[p], kt[p]], axis=0).astype(BF16)
            uvp = jnp.concatenate([u[p], vv[p]], axis=0).astype(BF16)
            upd = jnp.where(same_head, _dot_tn(bk, uvp), 0.0)
            w_col = jnp.sum(jnp.where(eye, wc[p], 0.0), axis=1, keepdims=True)
            s_scr[p // n_pairs, p % n_pairs] = (st[p] + upd) * w_col
        inv_n = 1.0 / N

        def head_mean(z):
            s0 = jnp.sum(jnp.where(h0, z, 0.0), axis=1, keepdims=True)
            s1 = jnp.sum(jnp.where(h0, 0.0, z), axis=1, keepdims=True)
            return jnp.where(h0, s0, s1) * inv_n

        for s in range(nb):
            zs = []
            for q in range(n_pairs):
                p = s * n_pairs + q
                ps = slice(q * LANES, (q + 1) * LANES)
                yc = ys[p] - head_mean(ys[p])
                yn = yc * lax.rsqrt(head_mean(yc * yc) + LNX_EPS) * lg_ref[:, ps] + lb_ref[:, ps]
                rk = r_ref[s, sl, ps] * k_ref[s, sl, ps] * rk_ref[:, ps]
                zs.append((yn + head_mean(rk) * N * vv[p]) * g_ref[s, sl, ps])
            y_ref[s, sl, :] = jnp.concatenate(zs, axis=1).astype(BF16)
        return carry

    lax.fori_loop(0, tt // C, chunk, 0)

    @pl.when(pl.program_id(1) == pl.num_programs(1) - 1)
    def _():
        for s in range(nb):
            for p in range(D_MODEL // LANES):
                t = s_scr[s, p].T
                for h in range(hp):
                    st_ref[s, hp * p + h] = t[h * N:(h + 1) * N, h * N:(h + 1) * N]


def _rwkv_scan(r, lw, k, v, na, b, gate, wkv0, rw):
    B, T, _ = r.shape
    C = SCAN_CHUNK
    t_pad = ((T + C - 1) // C) * C
    seqs = (r, lw, k, v, na, b, gate)
    if t_pad != T:
        seqs = tuple(jnp.pad(a, ((0, 0), (0, t_pad - T), (0, 0))) for a in seqs)
    tt = _pick_tile(t_pad, SCAN_ROWS)
    nb = _pick_tile(B, SCAN_SEQS)
    n_pairs = D_MODEL // LANES
    tok = pl.BlockSpec((nb, tt, D_MODEL), lambda b_, i: (b_, i, 0))
    st = pl.BlockSpec((nb, RWKV_HEADS, RWKV_HEAD, RWKV_HEAD), lambda b_, i: (b_, 0, 0, 0))
    consts = [rw["r_k"], rw["lnx_g"], rw["lnx_b"]]
    y, s_t = pl.pallas_call(
        functools.partial(_scan_kernel, nb=nb, tt=tt, C=C),
        grid=(B // nb, t_pad // tt),
        in_specs=[tok] * 7 + [st] + [_const_spec(c.shape) for c in consts],
        out_specs=[tok, st],
        out_shape=[SDS((B, t_pad, D_MODEL), BF16), SDS((B, RWKV_HEADS, RWKV_HEAD, RWKV_HEAD), F32)],
        scratch_shapes=[pltpu.VMEM((nb, n_pairs, LANES, LANES), F32)],
        compiler_params=_cparams(("parallel", "arbitrary")),
        name="rwkv_scan",
    )(*seqs, wkv0, *consts)
    return (y if t_pad == T else y[:, :T]), s_t


def _wo_mlp_kernel(x_ref, z_ref, wo_ref, g_ref, wu_ref, wd_ref, gf_ref, o_ref, *, final_norm):
    x = x_ref[...] + _dot(z_ref[...], wo_ref[...])
    o_ref[...] = _mlp_apply(x, g_ref, wu_ref, wd_ref, gf_ref, final_norm)


def _wo_mlp(x, z, wo, mlp_args, final_norm):
    M = x.shape[0]
    tm = _pick_tile(M, MLP_ROWS)
    tok = pl.BlockSpec((tm, D_MODEL), lambda i: (i, 0))
    return pl.pallas_call(
        functools.partial(_wo_mlp_kernel, final_norm=final_norm),
        grid=(M // tm,),
        in_specs=[tok, tok, _const_spec(wo.shape)] + _mlp_specs(mlp_args[4]),
        out_specs=tok,
        out_shape=SDS((M, D_MODEL), F32),
        compiler_params=_cparams(("parallel",)),
        name="wo_mlp",
    )(x, z, wo, *mlp_args[:4])


def _pool_block(x, state, n_hist, g, w_bf, scale, mlp_args=(), final_norm=False):
    B = x.shape[0]
    if state is None:
        hist = jnp.zeros((B, POOL_PAD, D_MODEL), F32)
    else:
        hist = jnp.pad(state, ((0, 0), (POOL_PAD - POOL_HIST, 0), (0, 0)))
    out, hs = _pool_layer(x, hist, n_hist, g, w_bf, scale, mlp_args, final_norm)
    return out, hs[:, POOL_PAD - POOL_HIST:]


def _attn_block(x, cache, g, w_in_pad, wo_bf):
    B, T, _ = x.shape
    past = 0 if cache is None else cache[0].shape[2]
    tab = _rope_tables(past, T)
    q, qi, kis, k_new, v_new, ki_new, kb, vb, kib = _attn_project(x, g, w_in_pad, tab)
    topk = min(TOPK_MAX, (past + T) // 4)
    if cache is not None:
        return _attn_decode_layer(x, q, qi, kis, kib, kb, vb, cache, wo_bf, topk), k_new, v_new, ki_new
    tq = _pick_tile(T, ATTN_Q_BLOCK)
    out = x
    for j0 in range(T // tq):
        l_g = (j0 + 1) * tq
        out = _attn_pipe_layer(out, q, qi, kis, kib, kb, vb, wo_bf, 0, l_g, topk, tq, j0, l_g)
    return out, k_new, v_new, ki_new


def _rwkv_block(x, shift_prev, wkv0, g, rw, mlp_args, final_norm):
    B, T, _ = x.shape
    r, lw, k, v, na, b, gate, hl = _rwkv_project(x, shift_prev, g, rw)
    z, s_t = _rwkv_scan(r, lw, k, v, na, b, gate, wkv0, rw)
    flat = lambda a: a.reshape(B * T, D_MODEL)
    out = _wo_mlp(flat(x), flat(z), rw["w_o"], mlp_args, final_norm).reshape(B, T, D_MODEL)
    return out, hl[:, SUBLANES - 1:], s_t


def kernel(x_prompt, x_sample, state_pool, cache_k, cache_v, cache_kidx, state_shift, state_wkv, ln1_g, ln2_g, w_up, w_down, ln_f_g, pool_w, pool_scale, attn_w_in, attn_w_out, rwkv_mix, rwkv_w0, rwkv_w1, rwkv_w2, rwkv_a0, rwkv_a1, rwkv_a2, rwkv_g1, rwkv_g2, rwkv_k_k, rwkv_k_a, rwkv_r_k, rwkv_w_r, rwkv_w_k, rwkv_w_v, rwkv_w_o, rwkv_lnx_g, rwkv_lnx_b):
    xp, xs = x_prompt, x_sample
    bp, sp, _ = xp.shape
    bs, ss, _ = xs.shape
    past = cache_k.shape[2]
    row = lambda a: a.reshape(1, -1)
    bf = lambda a: a.astype(BF16)
    wu_all, wd_all = bf(w_up), bf(w_down)
    head_of = jnp.arange(MXU_DIM) // RWKV_HEAD
    bd_mat = (head_of[:, None] == head_of[None, :]).astype(BF16)
    outs = {n: [] for n in ("pool_p", "pool_s", "k_p", "k_s", "v_p", "v_s", "ki_p", "ki_s",
                            "sh_p", "sh_s", "wkv_p", "wkv_s")}
    for i in range(DEPTH):
        j = i // N_MIXERS
        g1 = row(ln1_g[i])
        last = i == DEPTH - 1
        mlp_args = (row(ln2_g[i]), wu_all, wd_all, row(ln_f_g), i)
        prompt_mlp_done = sample_mlp_done = False
        if i % N_MIXERS == 0:
            w_bf = bf(pool_w[j])
            sc = row(pool_scale[j])
            prompt_mlp_done = sp >= POOL_MLP_MIN_ROWS
            xp, st_p = _pool_block(xp, None, 0, g1, w_bf, sc, mlp_args if prompt_mlp_done else (), last)
            xs, st_s = _pool_block(xs, state_pool[j], past, g1, w_bf, sc)
            outs["pool_p"].append(st_p)
            outs["pool_s"].append(st_s)
        elif i % N_MIXERS == 1:
            w_in_pad = jnp.pad(bf(attn_w_in[j]), ((0, 0), (0, ATTN_IN_PAD - ATTN_IN_W)))
            wo_bf = bf(attn_w_out[j])
            xp, kp, vp, kip = _attn_block(xp, None, g1, w_in_pad, wo_bf)
            xs, kn, vn, kin = _attn_block(xs, (cache_k, cache_v, cache_kidx, j), g1, w_in_pad, wo_bf)
            for n, a in (("k_p", kp), ("v_p", vp), ("ki_p", kip), ("k_s", kn), ("v_s", vn), ("ki_s", kin)):
                outs[n].append(a)
        else:
            rw = dict(mix=rwkv_mix[j], w0=row(rwkv_w0[j]), w1=bf(rwkv_w1[j]), w2=bf(rwkv_w2[j]),
                      a0=row(rwkv_a0[j]), a1=bf(rwkv_a1[j]), a2=bf(rwkv_a2[j]), g1=bf(rwkv_g1[j]),
                      g2=bf(rwkv_g2[j]), k_k=row(rwkv_k_k[j]), k_a=row(rwkv_k_a[j]), r_k=row(rwkv_r_k[j]),
                      w_r=bf(rwkv_w_r[j]), w_k=bf(rwkv_w_k[j]), w_v=bf(rwkv_w_v[j]), w_o=bf(rwkv_w_o[j]),
                      lnx_g=row(rwkv_lnx_g[j]), lnx_b=row(rwkv_lnx_b[j]), bd=bd_mat)
            zero_shift = jnp.zeros((bp, 1, D_MODEL), F32)
            zero_wkv = jnp.zeros((bp, RWKV_HEADS, RWKV_HEAD, RWKV_HEAD), F32)
            xp, shp, wp = _rwkv_block(xp, zero_shift, zero_wkv, g1, rw, mlp_args, last)
            xs, shs, wsn = _rwkv_block(xs, state_shift[j], state_wkv[j], g1, rw, mlp_args, last)
            prompt_mlp_done = sample_mlp_done = True
            outs["sh_p"].append(shp)
            outs["sh_s"].append(shs)
            outs["wkv_p"].append(wp)
            outs["wkv_s"].append(wsn)
        if not prompt_mlp_done:
            xp = _mlp(xp.reshape(bp * sp, D_MODEL), mlp_args, last).reshape(bp, sp, D_MODEL)
        if not sample_mlp_done:
            xs = _mlp(xs.reshape(bs * ss, D_MODEL), mlp_args, last).reshape(bs, ss, D_MODEL)
    st = lambda n: outs[n][0][None] if len(outs[n]) == 1 else jnp.stack(outs[n], 0)
    return (xp, xs, st("pool_p"), st("pool_s"), st("k_p"), st("k_s"), st("v_p"), st("v_s"),
            st("ki_p"), st("ki_s"), st("sh_p"), st("sh_s"), st("wkv_p"), st("wkv_s"))
```

```python
import functools
import math

import jax
import jax.numpy as jnp
import numpy as np
from jax import lax
from jax.experimental import pallas as pl
from jax.experimental.pallas import tpu as pltpu

F32 = jnp.float32
BF16 = jnp.bfloat16
SDS = jax.ShapeDtypeStruct

D_MODEL = 1024
DEPTH = 4
N_MIXERS = 3
CHUNK = 64
D_FF = 4 * D_MODEL
RMS_EPS = 1e-6
POOL_WINDOWS = (2, 4, 8, 16)
POOL_GROUPS = 4
POOL_GW = D_MODEL // POOL_GROUPS
POOL_HIST = max(POOL_WINDOWS) - 1
POOL_PAD = POOL_HIST + 1
ATTN_HEADS = 8
ATTN_HEAD_DIM = D_MODEL // ATTN_HEADS
IDX_HEADS = 8
IDX_DIM = 64
TOPK_MAX = 256
ROPE_THETA = 500000.0
ROPE_FRACTION = 4
NEG_INF = -1e30
Q_W = ATTN_HEADS * ATTN_HEAD_DIM
OFF_K = Q_W
OFF_V = OFF_K + ATTN_HEAD_DIM
OFF_QI = OFF_V + ATTN_HEAD_DIM
OFF_KI = OFF_QI + IDX_HEADS * IDX_DIM
OFF_WI = OFF_KI + IDX_DIM
ATTN_IN_W = OFF_WI + IDX_HEADS
RWKV_HEAD = 64
RWKV_HEADS = D_MODEL // RWKV_HEAD
LNX_EPS = 64e-5

LANES = 128
SUBLANES = 8
ATTN_IN_PAD = ((ATTN_IN_W + LANES - 1) // LANES) * LANES
MXU_DIM = 256
VMEM_LIMIT = 56 * 1024 * 1024
INT_MIN = -2 ** 31
HALF_NEG_KEY = int(np.float32(0.5 * NEG_INF).view(np.int32)) ^ 0x7FFFFFFF
POOL_MLP_MIN_ROWS = 512
POOL_MLP_SUBTILES = 2
ATTN_Q_BLOCK = 256
MLP_ROWS = 1024
POOL_ROWS = 1024
DECODE_SEQS = 2
APROJ_ROWS = 1024
RPROJ_ROWS = 512
SCAN_ROWS = 256
SCAN_CHUNK = 64
SCAN_SEQS = 2
FF_CHUNK = 1024


def _cparams(sem):
    return pltpu.CompilerParams(dimension_semantics=sem, vmem_limit_bytes=VMEM_LIMIT)


def _const_spec(shape):
    nd = len(shape)
    return pl.BlockSpec(shape, lambda *_: (0,) * nd, pipeline_mode=pl.Buffered(1))


def _rms(x, g):
    ms = jnp.mean(x * x, axis=-1, keepdims=True)
    return x * lax.rsqrt(ms + RMS_EPS) * g


def _dot(a, b):
    return jnp.dot(a, b, preferred_element_type=F32)


def _dot_nt(a, b):
    return lax.dot_general(a, b, (((1,), (1,)), ((), ())), preferred_element_type=F32)


def _dot_tn(a, b):
    return lax.dot_general(a, b, (((0,), (0,)), ((), ())), preferred_element_type=F32)


def _head_sum(z, bd_ref):
    bd = bd_ref[...]
    hi = z.astype(BF16)
    lo = (z - hi.astype(F32)).astype(BF16)
    outs = []
    for c in range(D_MODEL // MXU_DIM):
        cs = slice(c * MXU_DIM, (c + 1) * MXU_DIM)
        outs.append(_dot(hi[:, cs], bd) + _dot(lo[:, cs], bd))
    return jnp.concatenate(outs, axis=1)


def _sigmoid(x):
    return 1.0 / (1.0 + jnp.exp(-x))


def _pick_tile(n, pref):
    t = min(n, pref)
    assert n % t == 0, (n, t)
    return t


def _mlp_apply(x, g_ref, wu_ref, wd_ref, gf_ref, final_norm, side=()):
    h = _rms(x, g_ref[...]).astype(BF16)
    acc = x
    for j in range(D_FF // FF_CHUNK):
        u = _dot(h, wu_ref[:, j * FF_CHUNK:(j + 1) * FF_CHUNK])
        u = jnp.square(jnp.maximum(u, 0.0)).astype(BF16)
        acc = acc + _dot(u, wd_ref[j * FF_CHUNK:(j + 1) * FF_CHUNK, :])
        if j < len(side):
            side[j]()
    if final_norm:
        acc = _rms(acc, gf_ref[...])
    return acc


def _mlp_specs(layer):
    pick = lambda *_: (layer, 0, 0)
    return [_const_spec((1, D_MODEL)),
            pl.BlockSpec((None, D_MODEL, D_FF), pick, pipeline_mode=pl.Buffered(1)),
            pl.BlockSpec((None, D_FF, D_MODEL), pick, pipeline_mode=pl.Buffered(1)),
            _const_spec((1, D_MODEL))]


def _mlp_kernel(x_ref, g_ref, wu_ref, wd_ref, gf_ref, o_ref, *, final_norm):
    o_ref[...] = _mlp_apply(x_ref[...], g_ref, wu_ref, wd_ref, gf_ref, final_norm)


def _mlp(x, mlp_args, final_norm):
    M = x.shape[0]
    tm = _pick_tile(M, MLP_ROWS)
    return pl.pallas_call(
        functools.partial(_mlp_kernel, final_norm=final_norm),
        grid=(M // tm,),
        in_specs=[pl.BlockSpec((tm, D_MODEL), lambda i: (i, 0))] + _mlp_specs(mlp_args[4]),
        out_specs=pl.BlockSpec((tm, D_MODEL), lambda i: (i, 0)),
        out_shape=SDS((M, D_MODEL), F32),
        compiler_params=_cparams(("parallel",)),
        name="mlp",
    )(x, *mlp_args[:4])


def _pool_kernel(x_ref, xp_ref, hist_ref, g_ref, w_ref, sc_ref, *rest, tt, n_hist, mlp, final_norm):
    mlp_refs, (o_ref, hs_ref) = rest[:-2], rest[-2:]
    i = pl.program_id(1)
    g = g_ref[...]
    x = x_ref[...]
    h = _rms(x, g)
    prev = jnp.where(i == 0, hist_ref[...], _rms(xp_ref[...], g))
    hs_ref[...] = h[tt - POOL_PAD:, :]
    full = jnp.concatenate([prev, h], axis=0)
    nsub = POOL_MLP_SUBTILES if mlp else 1
    rows = tt // nsub

    def group(r, gi):
        win = POOL_WINDOWS[gi]
        rs = slice(r * rows, (r + 1) * rows)
        cs = slice(gi * POOL_GW, (gi + 1) * POOL_GW)
        s = full[r * rows:(r + 1) * rows + POOL_PAD, cs]
        d = 1
        while d < win:
            s = s + pltpu.roll(s, d, 0)
            d *= 2
        t1 = lax.broadcasted_iota(jnp.int32, (POOL_PAD, POOL_GW), 0) + (i * tt + r * rows + 1 + n_hist)
        cnt = jnp.concatenate([jnp.minimum(t1, win).astype(F32), jnp.full((rows - POOL_PAD, POOL_GW), win, F32)],
                              axis=0)
        pooled = s[POOL_PAD:, :] / cnt - h[rs, cs]
        return x[rs, cs] + _dot(pooled.astype(BF16), w_ref[gi]) * sc_ref[:, cs]

    cols = [group(0, gi) for gi in range(POOL_GROUPS)]
    if not mlp:
        o_ref[...] = jnp.concatenate(cols, axis=1)
        return
    outs = []
    for r in range(nsub):
        y = jnp.concatenate(cols, axis=1)
        cols = []
        side = [functools.partial(lambda gi, rn: cols.append(group(rn, gi)), gi, r + 1)
                for gi in range(POOL_GROUPS)] if r + 1 < nsub else []
        outs.append(_mlp_apply(y, *mlp_refs, final_norm, side))
    o_ref[...] = jnp.concatenate(outs, axis=0)


def _pool_layer(x, hist, n_hist, g, w_bf, scale, mlp_args, final_norm):
    B, T, _ = x.shape
    tt = _pick_tile(T, POOL_ROWS)
    rb = tt // POOL_PAD
    return pl.pallas_call(
        functools.partial(_pool_kernel, tt=tt, n_hist=n_hist, mlp=bool(mlp_args), final_norm=final_norm),
        grid=(B, T // tt),
        in_specs=[
            pl.BlockSpec((None, tt, D_MODEL), lambda b, i: (b, i, 0)),
            pl.BlockSpec((None, POOL_PAD, D_MODEL), lambda b, i: (b, jnp.maximum(i * rb - 1, 0), 0)),
            pl.BlockSpec((None, POOL_PAD, D_MODEL), lambda b, i: (b, 0, 0)),
            _const_spec((1, D_MODEL)),
            _const_spec((POOL_GROUPS, POOL_GW, POOL_GW)),
            _const_spec((1, D_MODEL)),
        ] + (_mlp_specs(mlp_args[4]) if mlp_args else []),
        out_specs=[
            pl.BlockSpec((None, tt, D_MODEL), lambda b, i: (b, i, 0)),
            pl.BlockSpec((None, POOL_PAD, D_MODEL), lambda b, i: (b, 0, 0)),
        ],
        out_shape=[SDS((B, T, D_MODEL), F32), SDS((B, POOL_PAD, D_MODEL), F32)],
        compiler_params=_cparams(("parallel", "arbitrary")),
        name="pool_mlp" if mlp_args else "pool",
    )(x, x, hist, g, w_bf, scale, *mlp_args[:4])


def _rope_tables(pos0, T):
    pos = np.arange(pos0, pos0 + T, dtype=np.float64)

    def head(d):
        rd = d // ROPE_FRACTION
        half = rd // 2
        inv = ROPE_THETA ** (-np.arange(half, dtype=np.float64) / half)
        ang = pos[:, None] * inv[None, :]
        cos, sin = np.cos(ang), np.sin(ang)
        z = lambda n: np.zeros((T, n))
        c = np.concatenate([cos, cos, np.ones((T, d - rd))], axis=1)
        sa = np.concatenate([-sin, z(d - half)], axis=1)
        sb = np.concatenate([z(half), sin, z(d - rd)], axis=1)
        return c, sa, sb

    qa = head(ATTN_HEAD_DIM)
    ia = head(IDX_DIM)
    i2 = tuple(np.concatenate([t, t], axis=1) for t in ia)
    ones, zeros = np.ones((T, IDX_DIM)), np.zeros((T, IDX_DIM))
    ik = (np.concatenate([ia[0], ones], axis=1), np.concatenate([ia[1], zeros], axis=1),
          np.concatenate([ia[2], zeros], axis=1))
    return jnp.asarray(np.concatenate(list(qa) + list(i2) + list(ik), axis=1), F32)


def _aproj_kernel(x_ref, g_ref, w_ref, tab_ref, q_ref, qi_ref, kis_ref, k_ref, v_ref, ki_ref, kb_ref, vb_ref, kib_ref,
                  *, nb, tt):
    h = _rms(x_ref[...].reshape(nb * tt, D_MODEL), g_ref[...]).astype(BF16)
    p = _dot(h, w_ref[...])

    def rope(x, kind, half):
        c = tab_ref[:, (3 * kind) * LANES:(3 * kind + 1) * LANES]
        sa = tab_ref[:, (3 * kind + 1) * LANES:(3 * kind + 2) * LANES]
        sb = tab_ref[:, (3 * kind + 2) * LANES:(3 * kind + 3) * LANES]
        return x * c + pltpu.roll(x, LANES - half, 1) * sa + pltpu.roll(x, half, 1) * sb

    def put(ref, val, *lead):
        for s in range(nb):
            ref[(s,) + lead] = val[s * tt:(s + 1) * tt]

    slab = lambda off: p[:, off:off + LANES]
    qh = ATTN_HEAD_DIM // ROPE_FRACTION // 2
    ih = IDX_DIM // ROPE_FRACTION // 2
    qscale = ATTN_HEAD_DIM ** -0.5 * math.log2(math.e)
    for hd in range(ATTN_HEADS):
        put(q_ref, (rope(slab(hd * LANES), 0, qh) * qscale).astype(BF16), hd)
    k = rope(slab(OFF_K), 0, qh)
    v = slab(OFF_V)
    put(qi_ref, jnp.concatenate([rope(slab(OFF_QI + c * LANES), 1, ih).astype(BF16)
                                 for c in range(IDX_HEADS * IDX_DIM // LANES)], axis=1))
    kis = rope(slab(OFF_KI), 2, ih)
    put(kis_ref, kis)
    put(k_ref, k)
    put(v_ref, v)
    put(ki_ref, kis[:, :IDX_DIM])
    put(kb_ref, k.astype(BF16))
    lane = lax.broadcasted_iota(jnp.int32, v.shape, 1)
    put(vb_ref, jnp.concatenate([v.astype(BF16), jnp.where(lane == 0, 1.0, 0.0).astype(BF16)], axis=1))
    put(kib_ref, kis[:, :IDX_DIM].astype(BF16))


def _attn_project(x, g, w_in_pad, tab):
    B, T, _ = x.shape
    tt = _pick_tile(T, APROJ_ROWS)
    nb = _pick_tile(B, max(1, APROJ_ROWS // tt))
    if nb > 1:
        tab = jnp.tile(tab, (nb, 1))
    QI_W = IDX_HEADS * IDX_DIM
    widths = (QI_W, LANES, ATTN_HEAD_DIM, ATTN_HEAD_DIM, IDX_DIM, ATTN_HEAD_DIM, 2 * ATTN_HEAD_DIM, IDX_DIM)
    dtypes = (BF16, F32, F32, F32, F32, BF16, BF16, BF16)
    q_spec = pl.BlockSpec((nb, ATTN_HEADS, tt, ATTN_HEAD_DIM), lambda i, b: (b, 0, i, 0))
    return pl.pallas_call(
        functools.partial(_aproj_kernel, nb=nb, tt=tt),
        grid=(T // tt, B // nb),
        in_specs=[
            pl.BlockSpec((nb, tt, D_MODEL), lambda i, b: (b, i, 0)),
            _const_spec((1, D_MODEL)),
            _const_spec((D_MODEL, ATTN_IN_PAD)),
            pl.BlockSpec((nb * tt, 9 * LANES), lambda i, b: (i, 0)),
        ],
        out_specs=[q_spec] + [pl.BlockSpec((nb, tt, w), lambda i, b: (b, i, 0)) for w in widths],
        out_shape=[SDS((B, ATTN_HEADS, T, ATTN_HEAD_DIM), BF16)] + [SDS((B, T, w), d) for w, d in zip(widths, dtypes)],
        compiler_params=_cparams(("arbitrary", "arbitrary")),
        name="attn_proj",
    )(x, g, w_in_pad, tab)


def _score_key(score):
    kb = pltpu.bitcast(score, jnp.int32)
    return jnp.where(kb >= 0, kb, kb ^ jnp.int32(0x7FFFFFFF))


KEY_BITS = 32


def _count(mask, axis):
    return jnp.sum(jnp.where(mask, 1.0, 0.0), axis=axis, keepdims=True)


def _kth_key_step(key_scr, lo_u, it, topk, axis):
    trial_u = lo_u | lax.shift_left(jnp.int32(1), KEY_BITS - 1 - it)
    c = _count(key_scr[...] >= (trial_u ^ jnp.int32(INT_MIN)), axis)
    return jnp.where(c >= topk, trial_u, lo_u)


def _select_topk(key_scr, jstar_scr, kpos, topk, axis):
    step = lambda it, lo_u: _kth_key_step(key_scr, lo_u, it, topk, axis)
    lo_u = lax.fori_loop(0, KEY_BITS, step, jnp.zeros(jstar_scr.shape, jnp.int32))
    return _finish_topk(key_scr, jstar_scr, kpos, lo_u, topk, axis)


def _finish_topk(key_scr, jstar_scr, kpos, lo_u, topk, axis):
    one = jstar_scr.shape
    n_idx = key_scr.shape[axis]
    count = functools.partial(_count, axis=axis)
    lo = lo_u ^ jnp.int32(INT_MIN)
    gt = key_scr[...] > lo
    eq = key_scr[...] == lo
    need = topk - count(gt)
    tie = (lo > jnp.int32(HALF_NEG_KEY)) & (count(eq) > need)
    jstar_scr[...] = jnp.full(one, n_idx, jnp.int32)

    @pl.when(jnp.max(jnp.where(tie, 1.0, 0.0)) > 0.0)
    def _():
        nbits = max(1, (n_idx - 1).bit_length())

        def ibody(it, lo_i):
            trial = lo_i + lax.shift_left(jnp.int32(1), nbits - 1 - it)
            c = count((kpos < trial) & (key_scr[...] == lo))
            return jnp.where(c < need, trial, lo_i)

        lo_i = lax.fori_loop(0, nbits, ibody, jnp.zeros(one, jnp.int32))
        jstar_scr[...] = jnp.where(tie, lo_i, n_idx)

    return gt | (eq & (kpos <= jstar_scr[...]))


def _attn_decode_kernel(x_ref, q_ref, qi_ref, wi_ref, kib_ref, kb_ref, vb_ref, cki_ref, ck_ref, cv_ref, wo_ref,
                        o_ref, ki_scr, k_scr, v_scr, key_scr, bias_scr, jstar_scr, o_scr, *, nb, tq, P, L, topk):
    n_keys = P + tq

    @pl.when(pl.program_id(0) == 0)
    def _():
        lane = lax.broadcasted_iota(jnp.int32, (P, ATTN_HEAD_DIM), 1)
        for s in range(nb):
            v_scr[s, :P, ATTN_HEAD_DIM:] = jnp.where(lane == 0, 1.0, 0.0).astype(BF16)
            ki_scr[s, n_keys:, :] = jnp.zeros((L - n_keys, IDX_DIM), BF16)
            k_scr[s, n_keys:, :] = jnp.zeros((L - n_keys, ATTN_HEAD_DIM), BF16)
            v_scr[s, n_keys:, :] = jnp.zeros((L - n_keys, 2 * ATTN_HEAD_DIM), BF16)

    qpos = lax.broadcasted_iota(jnp.int32, (tq, L), 0) + P
    kpos = lax.broadcasted_iota(jnp.int32, (tq, L), 1)
    adm = (kpos < n_keys) & ((kpos // CHUNK) <= (qpos // CHUNK))
    rows = lambda s: slice(s * tq, (s + 1) * tq)

    for s in range(nb):
        ki_scr[s, :P, :] = cki_ref[s].astype(BF16)
        ki_scr[s, P:n_keys, :] = kib_ref[s]
        k_scr[s, :P, :] = ck_ref[s].astype(BF16)
        k_scr[s, P:n_keys, :] = kb_ref[s]
        v_scr[s, :P, :ATTN_HEAD_DIM] = cv_ref[s].astype(BF16)
        v_scr[s, P:n_keys, :] = vb_ref[s]
        qi = qi_ref[s]
        wis = wi_ref[s] * (IDX_DIM ** -0.5 * IDX_HEADS ** -0.5)
        d_all = _dot_nt(jnp.concatenate([qi[:, h * IDX_DIM:(h + 1) * IDX_DIM] for h in range(IDX_HEADS)], axis=0),
                        ki_scr[s])
        score = jnp.zeros((tq, L), F32)
        for h in range(IDX_HEADS):
            score = score + jnp.maximum(d_all[h * tq:(h + 1) * tq], 0.0) * wis[:, IDX_DIM + h:IDX_DIM + h + 1]
        key_scr[rows(s), :] = _score_key(jnp.where(adm, score, NEG_INF))

    tile = lambda m: jnp.concatenate([m] * nb, axis=0)
    keep = _select_topk(key_scr, jstar_scr, tile(kpos), topk, axis=1)
    bias_scr[...] = jnp.where(tile(adm) & keep, 0.0, NEG_INF)

    for s in range(nb):
        def probs(lg):
            lg = lg + bias_scr[rows(s), :]
            return jnp.exp2(lg - jnp.max(lg, axis=1, keepdims=True)).astype(BF16)

        lg_all = _dot_nt(jnp.concatenate([q_ref[s, h] for h in range(ATTN_HEADS)], axis=0), k_scr[s])
        p_all = jnp.concatenate([probs(lg_all[h * tq:(h + 1) * tq]) for h in range(ATTN_HEADS)], axis=0)
        pv_all = _dot(p_all, v_scr[s])
        for h in range(ATTN_HEADS):
            pv = pv_all[h * tq:(h + 1) * tq]
            o_scr[rows(s), h * ATTN_HEAD_DIM:(h + 1) * ATTN_HEAD_DIM] = (
                pv[:, :ATTN_HEAD_DIM] / pv[:, ATTN_HEAD_DIM:ATTN_HEAD_DIM + 1]).astype(BF16)
    attn = _dot(o_scr[...], wo_ref[...])
    for s in range(nb):
        o_ref[s] = x_ref[s] + attn[rows(s)]


def _attn_decode_layer(x, q, qi, kis, kib, kb, vb, cache, wo_bf, topk):
    B, T, _ = x.shape
    ck, cv, cki, layer = cache
    P = ck.shape[2]
    nb = _pick_tile(B, DECODE_SEQS)
    past = lambda w: pl.BlockSpec((None, nb, P, w), lambda b: (layer, b, 0, 0))
    L = ((P + T + LANES - 1) // LANES) * LANES
    assert T * max(IDX_HEADS, ATTN_HEADS) <= MXU_DIM, "decode block: all heads of all new tokens in one row tile"
    assert P % 16 == 0 and T % 16 == 0, "bf16 row tiles"
    QI_W = IDX_HEADS * IDX_DIM
    seq = lambda *shape: pl.BlockSpec((nb,) + shape, lambda b: (b,) + (0,) * len(shape))
    return pl.pallas_call(
        functools.partial(_attn_decode_kernel, nb=nb, tq=T, P=P, L=L, topk=topk),
        grid=(B // nb,),
        in_specs=[
            seq(T, D_MODEL), seq(ATTN_HEADS, T, ATTN_HEAD_DIM), seq(T, QI_W), seq(T, LANES),
            seq(T, IDX_DIM), seq(T, ATTN_HEAD_DIM), seq(T, 2 * ATTN_HEAD_DIM),
            past(IDX_DIM), past(ATTN_HEAD_DIM), past(ATTN_HEAD_DIM),
            _const_spec((Q_W, D_MODEL)),
        ],
        out_specs=seq(T, D_MODEL),
        out_shape=SDS((B, T, D_MODEL), F32),
        scratch_shapes=[
            pltpu.VMEM((nb, L, IDX_DIM), BF16),
            pltpu.VMEM((nb, L, ATTN_HEAD_DIM), BF16),
            pltpu.VMEM((nb, L, 2 * ATTN_HEAD_DIM), BF16),
            pltpu.VMEM((nb * T, L), jnp.int32),
            pltpu.VMEM((nb * T, L), F32),
            pltpu.VMEM((nb * T, 1), jnp.int32),
            pltpu.VMEM((nb * T, Q_W), BF16),
        ],
        compiler_params=_cparams(("arbitrary",)),
        name="attn_decode",
    )(x, q, qi, kis, kib, kb, vb, cki, ck, cv, wo_bf)


def _attn_pipe_kernel(x_ref, q_ref, qi_ref, wi_ref, ki_ref, k_ref, v_ref, wo_ref, o_ref,
                      key_scr, bias_scr, lg_scr, jstar_scr, o_scr, *, tq, L, q_off, n_keys, topk):
    s = pl.program_id(0)
    cur = s % 2

    @pl.when(s == 0)
    def _():
        bias_scr[...] = jnp.zeros(bias_scr.shape, F32)

    qpos = lax.broadcasted_iota(jnp.int32, (tq, L), 0) + q_off
    kpos = lax.broadcasted_iota(jnp.int32, (tq, L), 1)
    adm = (kpos < n_keys) & ((kpos // CHUNK) <= (qpos // CHUNK))

    select_all = L <= topk
    if not select_all:
        qi = qi_ref[...]
        ki = ki_ref[...]
        wis = wi_ref[...] * (IDX_DIM ** -0.5 * IDX_HEADS ** -0.5)
        score = jnp.zeros((tq, L), F32)
        for h in range(IDX_HEADS):
            d = _dot_nt(qi[:, h * IDX_DIM:(h + 1) * IDX_DIM], ki)
            score = score + jnp.maximum(d, 0.0) * wis[:, IDX_DIM + h:IDX_DIM + h + 1]
        key_scr[...] = _score_key(jnp.where(adm, score, NEG_INF))

    steps_per_head = KEY_BITS // ATTN_HEADS

    lg_scr[0] = _dot_nt(q_ref[0], k_ref[...])

    def head(h, lo_u):
        if select_all:
            step = lambda i, lo: lo
        else:
            step = lambda i, lo: _kth_key_step(key_scr, lo, h * steps_per_head + i, topk, 1)
        lo_u = step(0, lo_u)
        lg = lg_scr[h % 2] + bias_scr[1 - cur]
        lg_scr[(h + 1) % 2] = _dot_nt(q_ref[jnp.minimum(h + 1, ATTN_HEADS - 1)], k_ref[...])
        lo_u = step(1, lo_u)
        p = jnp.exp2(lg - jnp.max(lg, axis=1, keepdims=True)).astype(BF16)
        lo_u = step(2, lo_u)
        pv = _dot(p, v_ref[...])
        for i in range(3, steps_per_head):
            lo_u = step(i, lo_u)
        o_scr[h] = (pv[:, :ATTN_HEAD_DIM] / pv[:, ATTN_HEAD_DIM:ATTN_HEAD_DIM + 1]).astype(BF16)
        return lo_u

    lo_u = lax.fori_loop(0, ATTN_HEADS, head, jnp.zeros((tq, 1), jnp.int32))
    attn = _dot(jnp.concatenate([o_scr[h] for h in range(ATTN_HEADS)], axis=1), wo_ref[...])
    o_ref[...] = x_ref[...] + jnp.where(s > 0, attn, 0.0)

    keep = adm if select_all else adm & _finish_topk(key_scr, jstar_scr, kpos, lo_u, topk, 1)
    bias_scr[cur] = jnp.where(keep, 0.0, NEG_INF)


def _attn_pipe_layer(x, q, qi, kis, ki_all, k_all, v_all, wo_bf, q_off, n_keys, topk, tq, j0, L):
    B, T, _ = x.shape
    QI_W = IDX_HEADS * IDX_DIM
    prev = lambda s: jnp.maximum(s - 1, 0)
    this = lambda s: jnp.minimum(s, B - 1)
    return pl.pallas_call(
        functools.partial(_attn_pipe_kernel, tq=tq, L=L, q_off=q_off + j0 * tq, n_keys=n_keys, topk=topk),
        grid=(B + 1,),
        in_specs=[
            pl.BlockSpec((None, tq, D_MODEL), lambda s: (prev(s), j0, 0)),
            pl.BlockSpec((None, ATTN_HEADS, tq, ATTN_HEAD_DIM), lambda s: (prev(s), 0, j0, 0)),
            pl.BlockSpec((None, tq, QI_W), lambda s: (this(s), j0, 0)),
            pl.BlockSpec((None, tq, LANES), lambda s: (this(s), j0, 0)),
            pl.BlockSpec((None, L, IDX_DIM), lambda s: (this(s), 0, 0)),
            pl.BlockSpec((None, L, ATTN_HEAD_DIM), lambda s: (prev(s), 0, 0)),
            pl.BlockSpec((None, L, 2 * ATTN_HEAD_DIM), lambda s: (prev(s), 0, 0)),
            _const_spec((Q_W, D_MODEL)),
        ],
        out_specs=pl.BlockSpec((None, tq, D_MODEL), lambda s: (prev(s), j0, 0)),
        out_shape=SDS((B, T, D_MODEL), F32),
        input_output_aliases={0: 0},
        scratch_shapes=[
            pltpu.VMEM((tq, L), jnp.int32),
            pltpu.VMEM((2, tq, L), F32),
            pltpu.VMEM((2, tq, L), F32),
            pltpu.VMEM((tq, 1), jnp.int32),
            pltpu.VMEM((ATTN_HEADS, tq, ATTN_HEAD_DIM), BF16),
        ],
        compiler_params=_cparams(("arbitrary",)),
        name="attn_pipe",
    )(x, q, qi, kis, ki_all, k_all, v_all, wo_bf)


def _rproj_kernel(x_ref, xp_ref, sh_ref, g_ref, mix_ref, w0_ref, w1_ref, w2_ref, a0_ref, a1_ref, a2_ref,
                  g1_ref, g2_ref, kk_ref, ka_ref, wr_ref, wk_ref, wv_ref, bd_ref,
                  r_o, lw_o, k_o, v_o, na_o, b_o, g_o, hl_o, *, nb, tt):
    i = pl.program_id(1)
    g = g_ref[...]
    row = lax.broadcasted_iota(jnp.int32, (tt, D_MODEL), 0)
    hs, xxs = [], []
    for s in range(nb):
        h_s = _rms(x_ref[s], g)
        hl_o[s] = h_s[tt - SUBLANES:, :]
        prev = jnp.where(i == 0, sh_ref[s], _rms(xp_ref[s, SUBLANES - 1:SUBLANES, :], g))
        hs.append(h_s)
        xxs.append(jnp.where(row == 0, prev, pltpu.roll(h_s, 1, 0)) - h_s)
    h = jnp.concatenate(hs, axis=0)
    xx = jnp.concatenate(xxs, axis=0)
    lerp = lambda n: (h + xx * mix_ref[n:n + 1, :]).astype(BF16)
    r = _dot(lerp(0), wr_ref[...])
    wl = w0_ref[...] + _dot(jnp.tanh(_dot(lerp(1), w1_ref[...])).astype(BF16), w2_ref[...])
    lw = -_sigmoid(wl) * math.exp(-0.5)
    k = _dot(lerp(2), wk_ref[...])
    v = _dot(lerp(3), wv_ref[...])
    a = _sigmoid(a0_ref[...] + _dot(_dot(lerp(4), a1_ref[...]).astype(BF16), a2_ref[...]))
    gate = _dot(_sigmoid(_dot(lerp(5), g1_ref[...])).astype(BF16), g2_ref[...])
    kk = k * kk_ref[...]
    kk = kk * lax.rsqrt(jnp.maximum(_head_sum(kk * kk, bd_ref), 1e-24))
    outs = ((r_o, r), (lw_o, lw), (k_o, k * (1.0 + (a - 1.0) * ka_ref[...])), (v_o, v), (na_o, -kk), (b_o, kk * a),
            (g_o, gate))
    for ref, val in outs:
        for s in range(nb):
            ref[s] = val[s * tt:(s + 1) * tt]


def _rwkv_project(x, shift_prev, g, rw):
    B, T, _ = x.shape
    tt = _pick_tile(T, RPROJ_ROWS)
    nb = _pick_tile(B, max(1, RPROJ_ROWS // tt))
    rb = tt // SUBLANES
    tok = pl.BlockSpec((nb, tt, D_MODEL), lambda b, i: (b, i, 0))
    consts = [g, rw["mix"], rw["w0"], rw["w1"], rw["w2"], rw["a0"], rw["a1"], rw["a2"], rw["g1"], rw["g2"],
              rw["k_k"], rw["k_a"], rw["w_r"], rw["w_k"], rw["w_v"], rw["bd"]]
    return pl.pallas_call(
        functools.partial(_rproj_kernel, nb=nb, tt=tt),
        grid=(B // nb, T // tt),
        in_specs=[
            tok,
            pl.BlockSpec((nb, SUBLANES, D_MODEL), lambda b, i: (b, jnp.maximum(i * rb - 1, 0), 0)),
            pl.BlockSpec((nb, 1, D_MODEL), lambda b, i: (b, 0, 0)),
        ] + [_const_spec(c.shape) for c in consts],
        out_specs=[tok] * 7 + [pl.BlockSpec((nb, SUBLANES, D_MODEL), lambda b, i: (b, 0, 0))],
        out_shape=[SDS((B, T, D_MODEL), F32)] * 7 + [SDS((B, SUBLANES, D_MODEL), F32)],
        compiler_params=_cparams(("parallel", "arbitrary")),
        name="rwkv_proj",
    )(x, x, shift_prev, *consts)


def _scan_kernel(r_ref, lw_ref, k_ref, v_ref, a_ref, b_ref, g_ref, s0_ref, rk_ref, lg_ref, lb_ref,
                 y_ref, st_ref, s_scr, *, nb, tt, C):
    N = RWKV_HEAD
    assert C == N and 2 * N == LANES

    hp = LANES // N
    zero = jnp.zeros((N, N), F32)

    @pl.when(pl.program_id(1) == 0)
    def _():
        for s in range(nb):
            for p in range(D_MODEL // LANES):
                blocks = [jnp.concatenate([s0_ref[s, hp * p + h] if g == h else zero for g in range(hp)], axis=1)
                          for h in range(hp)]
                s_scr[s, p] = jnp.concatenate(blocks, axis=0).T

    row_w = lax.broadcasted_iota(jnp.int32, (C, D_MODEL), 0)
    lane = lax.broadcasted_iota(jnp.int32, (C, LANES), 1)
    h0 = lane < N
    r1 = lax.broadcasted_iota(jnp.int32, (C, 2 * C), 0)
    c1 = lax.broadcasted_iota(jnp.int32, (C, 2 * C), 1) % C
    strict = r1 > c1
    r2 = lax.broadcasted_iota(jnp.int32, (C, 4 * C), 0)
    c2 = lax.broadcasted_iota(jnp.int32, (C, 4 * C), 1) % C
    incl = r2 >= c2
    rs = lax.broadcasted_iota(jnp.int32, (LANES, LANES), 0)
    cs = lax.broadcasted_iota(jnp.int32, (LANES, LANES), 1)
    same_head = (rs < N) == (cs < N)
    eye = rs == cs

    def split(x):
        return jnp.concatenate([jnp.where(h0, x, 0.0), jnp.where(h0, 0.0, x)], axis=0)

    def chunk(c, carry):
        sl = pl.ds(pl.multiple_of(c * C, C), C)
        n_pairs = D_MODEL // LANES
        at, rt, bt, kt, vv, wc = [], [], [], [], [], []
        for s in range(nb):
            lw = lw_ref[s, sl, :]
            cum = lw
            d = 1
            while d < C:
                cum = cum + jnp.where(row_w >= d, pltpu.roll(cum, d, 0), 0.0)
                d *= 2
            e_w = jnp.exp(cum)
            e_n = jnp.exp(-cum)
            rows = (a_ref[s, sl, :] * jnp.exp(cum - lw), r_ref[s, sl, :] * e_w, b_ref[s, sl, :] * e_n,
                    k_ref[s, sl, :] * e_n, v_ref[s, sl, :], e_w[C - 1:C, :])
            for dst, src in zip((at, rt, bt, kt, vv, wc), rows):
                dst.extend(src[:, p * LANES:(p + 1) * LANES] for p in range(n_pairs))
        pairs = range(nb * n_pairs)
        st = [s_scr[p // n_pairs, p % n_pairs] for p in pairs]
        ar = [jnp.concatenate([at[p], rt[p]], axis=0).astype(BF16) for p in pairs]
        v2 = [split(vv[p]).astype(BF16) for p in pairs]
        g = [_dot_nt(ar[p], jnp.concatenate([split(bt[p]), split(kt[p])], axis=0).astype(BF16)) for p in pairs]
        hm = [_dot(ar[p], st[p].astype(BF16)) for p in pairs]
        pw = [jnp.where(strict, g[p][:C, :2 * C], 0.0) for p in pairs]
        u = [hm[p][:C] + _dot(jnp.where(strict, g[p][:C, 2 * C:], 0.0).astype(BF16), v2[p]) for p in pairs]
        n = 1
        while n < C:
            pb = [pw[p].astype(BF16) for p in pairs]
            u = [u[p] + _dot(pb[p], split(u[p]).astype(BF16)) for p in pairs]
            n *= 2
            if n < C:
                pw = [_dot(pb[p], split(pw[p]).astype(BF16)) for p in pairs]
        ys = [hm[p][C:] + _dot(jnp.where(incl, g[p][C:, :], 0.0).astype(BF16),
                               jnp.concatenate([split(u[p]).astype(BF16), v2[p]], axis=0)) for p in pairs]
        for p in pairs:
            bk = jnp.concatenate([bt[p], kt[p]], axis=0).astype(BF16)
            uvp = jnp.concatenate([u[p], vv[p]], axis=0).astype(BF16)
            upd = jnp.where(same_head, _dot_tn(bk, uvp), 0.0)
            w_col = jnp.sum(jnp.where(eye, wc[p], 0.0), axis=1, keepdims=True)
            s_scr[p // n_pairs, p % n_pairs] = (st[p] + upd) * w_col
        inv_n = 1.0 / N

        def head_mean(z):
            s0 = jnp.sum(jnp.where(h0, z, 0.0), axis=1, keepdims=True)
            s1 = jnp.sum(jnp.where(h0, 0.0, z), axis=1, keepdims=True)
            return jnp.where(h0, s0, s1) * inv_n

        for s in range(nb):
            zs = []
            for q in range(n_pairs):
                p = s * n_pairs + q
                ps = slice(q * LANES, (q + 1) * LANES)
                yc = ys[p] - head_mean(ys[p])
                yn = yc * lax.rsqrt(head_mean(yc * yc) + LNX_EPS) * lg_ref[:, ps] + lb_ref[:, ps]
                rk = r_ref[s, sl, ps] * k_ref[s, sl, ps] * rk_ref[:, ps]
                zs.append((yn + head_mean(rk) * N * vv[p]) * g_ref[s, sl, ps])
            y_ref[s, sl, :] = jnp.concatenate(zs, axis=1).astype(BF16)
        return carry

    lax.fori_loop(0, tt // C, chunk, 0)

    @pl.when(pl.program_id(1) == pl.num_programs(1) - 1)
    def _():
        for s in range(nb):
            for p in range(D_MODEL // LANES):
                t = s_scr[s, p].T
                for h in range(hp):
                    st_ref[s, hp * p + h] = t[h * N:(h + 1) * N, h * N:(h + 1) * N]


def _rwkv_scan(r, lw, k, v, na, b, gate, wkv0, rw):
    B, T, _ = r.shape
    C = SCAN_CHUNK
    t_pad = ((T + C - 1) // C) * C
    seqs = (r, lw, k, v, na, b, gate)
    if t_pad != T:
        seqs = tuple(jnp.pad(a, ((0, 0), (0, t_pad - T), (0, 0))) for a in seqs)
    tt = _pick_tile(t_pad, SCAN_ROWS)
    nb = _pick_tile(B, SCAN_SEQS)
    n_pairs = D_MODEL // LANES
    tok = pl.BlockSpec((nb, tt, D_MODEL), lambda b_, i: (b_, i, 0))
    st = pl.BlockSpec((nb, RWKV_HEADS, RWKV_HEAD, RWKV_HEAD), lambda b_, i: (b_, 0, 0, 0))
    consts = [rw["r_k"], rw["lnx_g"], rw["lnx_b"]]
    y, s_t = pl.pallas_call(
        functools.partial(_scan_kernel, nb=nb, tt=tt, C=C),
        grid=(B // nb, t_pad // tt),
        in_specs=[tok] * 7 + [st] + [_const_spec(c.shape) for c in consts],
        out_specs=[tok, st],
        out_shape=[SDS((B, t_pad, D_MODEL), BF16), SDS((B, RWKV_HEADS, RWKV_HEAD, RWKV_HEAD), F32)],
        scratch_shapes=[pltpu.VMEM((nb, n_pairs, LANES, LANES), F32)],
        compiler_params=_cparams(("parallel", "arbitrary")),
        name="rwkv_scan",
    )(*seqs, wkv0, *consts)
    return (y if t_pad == T else y[:, :T]), s_t


def _wo_mlp_kernel(x_ref, z_ref, wo_ref, g_ref, wu_ref, wd_ref, gf_ref, o_ref, *, final_norm):
    x = x_ref[...] + _dot(z_ref[...], wo_ref[...])
    o_ref[...] = _mlp_apply(x, g_ref, wu_ref, wd_ref, gf_ref, final_norm)


def _wo_mlp(x, z, wo, mlp_args, final_norm):
    M = x.shape[0]
    tm = _pick_tile(M, MLP_ROWS)
    tok = pl.BlockSpec((tm, D_MODEL), lambda i: (i, 0))
    return pl.pallas_call(
        functools.partial(_wo_mlp_kernel, final_norm=final_norm),
        grid=(M // tm,),
        in_specs=[tok, tok, _const_spec(wo.shape)] + _mlp_specs(mlp_args[4]),
        out_specs=tok,
        out_shape=SDS((M, D_MODEL), F32),
        compiler_params=_cparams(("parallel",)),
        name="wo_mlp",
    )(x, z, wo, *mlp_args[:4])


def _pool_block(x, state, n_hist, g, w_bf, scale, mlp_args=(), final_norm=False):
    B = x.shape[0]
    if state is None:
        hist = jnp.zeros((B, POOL_PAD, D_MODEL), F32)
    else:
        hist = jnp.pad(state, ((0, 0), (POOL_PAD - POOL_HIST, 0), (0, 0)))
    out, hs = _pool_layer(x, hist, n_hist, g, w_bf, scale, mlp_args, final_norm)
    return out, hs[:, POOL_PAD - POOL_HIST:]


def _attn_block(x, cache, g, w_in_pad, wo_bf):
    B, T, _ = x.shape
    past = 0 if cache is None else cache[0].shape[2]
    tab = _rope_tables(past, T)
    q, qi, kis, k_new, v_new, ki_new, kb, vb, kib = _attn_project(x, g, w_in_pad, tab)
    topk = min(TOPK_MAX, (past + T) // 4)
    if cache is not None:
        return _attn_decode_layer(x, q, qi, kis, kib, kb, vb, cache, wo_bf, topk), k_new, v_new, ki_new
    tq = _pick_tile(T, ATTN_Q_BLOCK)
    out = x
    for j0 in range(T // tq):
        l_g = (j0 + 1) * tq
        out = _attn_pipe_layer(out, q, qi, kis, kib, kb, vb, wo_bf, 0, l_g, topk, tq, j0, l_g)
    return out, k_new, v_new, ki_new


def _rwkv_block(x, shift_prev, wkv0, g, rw, mlp_args, final_norm):
    B, T, _ = x.shape
    r, lw, k, v, na, b, gate, hl = _rwkv_project(x, shift_prev, g, rw)
    z, s_t = _rwkv_scan(r, lw, k, v, na, b, gate, wkv0, rw)
    flat = lambda a: a.reshape(B * T, D_MODEL)
    out = _wo_mlp(flat(x), flat(z), rw["w_o"], mlp_args, final_norm).reshape(B, T, D_MODEL)
    return out, hl[:, SUBLANES - 1:], s_t


def kernel(x_prompt, x_sample, state_pool, cache_k, cache_v, cache_kidx, state_shift, state_wkv, ln1_g, ln2_g, w_up, w_down, ln_f_g, pool_w, pool_scale, attn_w_in, attn_w_out, rwkv_mix, rwkv_w0, rwkv_w1, rwkv_w2, rwkv_a0, rwkv_a1, rwkv_a2, rwkv_g1, rwkv_g2, rwkv_k_k, rwkv_k_a, rwkv_r_k, rwkv_w_r, rwkv_w_k, rwkv_w_v, rwkv_w_o, rwkv_lnx_g, rwkv_lnx_b):
    xp, xs = x_prompt, x_sample
    bp, sp, _ = xp.shape
    bs, ss, _ = xs.shape
    past = cache_k.shape[2]
    row = lambda a: a.reshape(1, -1)
    bf = lambda a: a.astype(BF16)
    wu_all, wd_all = bf(w_up), bf(w_down)
    head_of = jnp.arange(MXU_DIM) // RWKV_HEAD
    bd_mat = (head_of[:, None] == head_of[None, :]).astype(BF16)
    outs = {n: [] for n in ("pool_p", "pool_s", "k_p", "k_s", "v_p", "v_s", "ki_p", "ki_s",
                            "sh_p", "sh_s", "wkv_p", "wkv_s")}
    for i in range(DEPTH):
        j = i // N_MIXERS
        g1 = row(ln1_g[i])
        last = i == DEPTH - 1
        mlp_args = (row(ln2_g[i]), wu_all, wd_all, row(ln_f_g), i)
        prompt_mlp_done = sample_mlp_done = False
        if i % N_MIXERS == 0:
            w_bf = bf(pool_w[j])
            sc = row(pool_scale[j])
            prompt_mlp_done = sp >= POOL_MLP_MIN_ROWS
            xp, st_p = _pool_block(xp, None, 0, g1, w_bf, sc, mlp_args if prompt_mlp_done else (), last)
            xs, st_s = _pool_block(xs, state_pool[j], past, g1, w_bf, sc)
            outs["pool_p"].append(st_p)
            outs["pool_s"].append(st_s)
        elif i % N_MIXERS == 1:
            w_in_pad = jnp.pad(bf(attn_w_in[j]), ((0, 0), (0, ATTN_IN_PAD - ATTN_IN_W)))
            wo_bf = bf(attn_w_out[j])
            xp, kp, vp, kip = _attn_block(xp, None, g1, w_in_pad, wo_bf)
            xs, kn, vn, kin = _attn_block(xs, (cache_k, cache_v, cache_kidx, j), g1, w_in_pad, wo_bf)
            for n, a in (("k_p", kp), ("v_p", vp), ("ki_p", kip), ("k_s", kn), ("v_s", vn), ("ki_s", kin)):
                outs[n].append(a)
        else:
            rw = dict(mix=rwkv_mix[j], w0=row(rwkv_w0[j]), w1=bf(rwkv_w1[j]), w2=bf(rwkv_w2[j]),
                      a0=row(rwkv_a0[j]), a1=bf(rwkv_a1[j]), a2=bf(rwkv_a2[j]), g1=bf(rwkv_g1[j]),
                      g2=bf(rwkv_g2[j]), k_k=row(rwkv_k_k[j]), k_a=row(rwkv_k_a[j]), r_k=row(rwkv_r_k[j]),
                      w_r=bf(rwkv_w_r[j]), w_k=bf(rwkv_w_k[j]), w_v=bf(rwkv_w_v[j]), w_o=bf(rwkv_w_o[j]),
                      lnx_g=row(rwkv_lnx_g[j]), lnx_b=row(rwkv_lnx_b[j]), bd=bd_mat)
            zero_shift = jnp.zeros((bp, 1, D_MODEL), F32)
            zero_wkv = jnp.zeros((bp, RWKV_HEADS, RWKV_HEAD, RWKV_HEAD), F32)
            xp, shp, wp = _rwkv_block(xp, zero_shift, zero_wkv, g1, rw, mlp_args, last)
            xs, shs, wsn = _rwkv_block(xs, state_shift[j], state_wkv[j], g1, rw, mlp_args, last)
            prompt_mlp_done = sample_mlp_done = True
            outs["sh_p"].append(shp)
            outs["sh_s"].append(shs)
            outs["wkv_p"].append(wp)
            outs["wkv_s"].append(wsn)
        if not prompt_mlp_done:
            xp = _mlp(xp.reshape(bp * sp, D_MODEL), mlp_args, last).reshape(bp, sp, D_MODEL)
        if not sample_mlp_done:
            xs = _mlp(xs.reshape(bs * ss, D_MODEL), mlp_args, last).reshape(bs, ss, D_MODEL)
    st = lambda n: outs[n][0][None] if len(outs[n]) == 1 else jnp.stack(outs[n], 0)
    return (xp, xs, st("pool_p"), st("pool_s"), st("k_p"), st("k_s"), st("v_p"), st("v_s"),
            st("ki_p"), st("ki_s"), st("sh_p"), st("sh_s"), st("wkv_p"), st("wkv_s"))
```

```python
import functools
import math

import jax
import jax.numpy as jnp
import numpy as np
from jax import lax
from jax.experimental import pallas as pl
from jax.experimental.pallas import tpu as pltpu

F32 = jnp.float32
BF16 = jnp.bfloat16
SDS = jax.ShapeDtypeStruct

D_MODEL = 1024
DEPTH = 4
N_MIXERS = 3
CHUNK = 64
D_FF = 4 * D_MODEL
RMS_EPS = 1e-6
POOL_WINDOWS = (2, 4, 8, 16)
POOL_GROUPS = 4
POOL_GW = D_MODEL // POOL_GROUPS
POOL_HIST = max(POOL_WINDOWS) - 1
POOL_PAD = POOL_HIST + 1
ATTN_HEADS = 8
ATTN_HEAD_DIM = D_MODEL // ATTN_HEADS
IDX_HEADS = 8
IDX_DIM = 64
TOPK_MAX = 256
ROPE_THETA = 500000.0
ROPE_FRACTION = 4
NEG_INF = -1e30
Q_W = ATTN_HEADS * ATTN_HEAD_DIM
OFF_K = Q_W
OFF_V = OFF_K + ATTN_HEAD_DIM
OFF_QI = OFF_V + ATTN_HEAD_DIM
OFF_KI = OFF_QI + IDX_HEADS * IDX_DIM
OFF_WI = OFF_KI + IDX_DIM
ATTN_IN_W = OFF_WI + IDX_HEADS
RWKV_HEAD = 64
RWKV_HEADS = D_MODEL // RWKV_HEAD
LNX_EPS = 64e-5

LANES = 128
SUBLANES = 8
ATTN_IN_PAD = ((ATTN_IN_W + LANES - 1) // LANES) * LANES
MXU_DIM = 256
VMEM_LIMIT = 56 * 1024 * 1024
INT_MIN = -2 ** 31
HALF_NEG_KEY = int(np.float32(0.5 * NEG_INF).view(np.int32)) ^ 0x7FFFFFFF
POOL_MLP_MIN_ROWS = 512
POOL_MLP_SUBTILES = 2
ATTN_Q_BLOCK = 256
MLP_ROWS = 1024
POOL_ROWS = 1024
DECODE_SEQS = 2
APROJ_ROWS = 1024
RPROJ_ROWS = 512
SCAN_ROWS = 256
SCAN_CHUNK = 64
SCAN_SEQS = 2
FF_CHUNK = 1024


def _cparams(sem):
    return pltpu.CompilerParams(dimension_semantics=sem, vmem_limit_bytes=VMEM_LIMIT)


def _const_spec(shape):
    nd = len(shape)
    return pl.BlockSpec(shape, lambda *_: (0,) * nd, pipeline_mode=pl.Buffered(1))


def _rms(x, g):
    ms = jnp.mean(x * x, axis=-1, keepdims=True)
    return x * lax.rsqrt(ms + RMS_EPS) * g


def _dot(a, b):
    return jnp.dot(a, b, preferred_element_type=F32)


def _dot_nt(a, b):
    return lax.dot_general(a, b, (((1,), (1,)), ((), ())), preferred_element_type=F32)


def _dot_tn(a, b):
    return lax.dot_general(a, b, (((0,), (0,)), ((), ())), preferred_element_type=F32)


def _head_sum(z, bd_ref):
    bd = bd_ref[...]
    hi = z.astype(BF16)
    lo = (z - hi.astype(F32)).astype(BF16)
    outs = []
    for c in range(D_MODEL // MXU_DIM):
        cs = slice(c * MXU_DIM, (c + 1) * MXU_DIM)
        outs.append(_dot(hi[:, cs], bd) + _dot(lo[:, cs], bd))
    return jnp.concatenate(outs, axis=1)


def _sigmoid(x):
    return 1.0 / (1.0 + jnp.exp(-x))


def _pick_tile(n, pref):
    t = min(n, pref)
    assert n % t == 0, (n, t)
    return t


def _mlp_apply(x, g_ref, wu_ref, wd_ref, gf_ref, final_norm, side=()):
    h = _rms(x, g_ref[...]).astype(BF16)
    acc = x
    for j in range(D_FF // FF_CHUNK):
        u = _dot(h, wu_ref[:, j * FF_CHUNK:(j + 1) * FF_CHUNK])
        u = jnp.square(jnp.maximum(u, 0.0)).astype(BF16)
        acc = acc + _dot(u, wd_ref[j * FF_CHUNK:(j + 1) * FF_CHUNK, :])
        if j < len(side):
            side[j]()
    if final_norm:
        acc = _rms(acc, gf_ref[...])
    return acc


def _mlp_specs(layer):
    pick = lambda *_: (layer, 0, 0)
    return [_const_spec((1, D_MODEL)),
            pl.BlockSpec((None, D_MODEL, D_FF), pick, pipeline_mode=pl.Buffered(1)),
            pl.BlockSpec((None, D_FF, D_MODEL), pick, pipeline_mode=pl.Buffered(1)),
            _const_spec((1, D_MODEL))]


def _mlp_kernel(x_ref, g_ref, wu_ref, wd_ref, gf_ref, o_ref, *, final_norm):
    o_ref[...] = _mlp_apply(x_ref[...], g_ref, wu_ref, wd_ref, gf_ref, final_norm)


def _mlp(x, mlp_args, final_norm):
    M = x.shape[0]
    tm = _pick_tile(M, MLP_ROWS)
    return pl.pallas_call(
        functools.partial(_mlp_kernel, final_norm=final_norm),
        grid=(M // tm,),
        in_specs=[pl.BlockSpec((tm, D_MODEL), lambda i: (i, 0))] + _mlp_specs(mlp_args[4]),
        out_specs=pl.BlockSpec((tm, D_MODEL), lambda i: (i, 0)),
        out_shape=SDS((M, D_MODEL), F32),
        compiler_params=_cparams(("parallel",)),
        name="mlp",
    )(x, *mlp_args[:4])


def _pool_kernel(x_ref, xp_ref, hist_ref, g_ref, w_ref, sc_ref, *rest, tt, n_hist, mlp, final_norm):
    mlp_refs, (o_ref, hs_ref) = rest[:-2], rest[-2:]
    i = pl.program_id(1)
    g = g_ref[...]
    x = x_ref[...]
    h = _rms(x, g)
    prev = jnp.where(i == 0, hist_ref[...], _rms(xp_ref[...], g))
    hs_ref[...] = h[tt - POOL_PAD:, :]
    full = jnp.concatenate([prev, h], axis=0)
    nsub = POOL_MLP_SUBTILES if mlp else 1
    rows = tt // nsub

    def group(r, gi):
        win = POOL_WINDOWS[gi]
        rs = slice(r * rows, (r + 1) * rows)
        cs = slice(gi * POOL_GW, (gi + 1) * POOL_GW)
        s = full[r * rows:(r + 1) * rows + POOL_PAD, cs]
        d = 1
        while d < win:
            s = s + pltpu.roll(s, d, 0)
            d *= 2
        t1 = lax.broadcasted_iota(jnp.int32, (POOL_PAD, POOL_GW), 0) + (i * tt + r * rows + 1 + n_hist)
        cnt = jnp.concatenate([jnp.minimum(t1, win).astype(F32), jnp.full((rows - POOL_PAD, POOL_GW), win, F32)],
                              axis=0)
        pooled = s[POOL_PAD:, :] / cnt - h[rs, cs]
        return x[rs, cs] + _dot(pooled.astype(BF16), w_ref[gi]) * sc_ref[:, cs]

    cols = [group(0, gi) for gi in range(POOL_GROUPS)]
    if not mlp:
        o_ref[...] = jnp.concatenate(cols, axis=1)
        return
    outs = []
    for r in range(nsub):
        y = jnp.concatenate(cols, axis=1)
        cols = []
        side = [functools.partial(lambda gi, rn: cols.append(group(rn, gi)), gi, r + 1)
                for gi in range(POOL_GROUPS)] if r + 1 < nsub else []
        outs.append(_mlp_apply(y, *mlp_refs, final_norm, side))
    o_ref[...] = jnp.concatenate(outs, axis=0)


def _pool_layer(x, hist, n_hist, g, w_bf, scale, mlp_args, final_norm):
    B, T, _ = x.shape
    tt = _pick_tile(T, POOL_ROWS)
    rb = tt // POOL_PAD
    return pl.pallas_call(
        functools.partial(_pool_kernel, tt=tt, n_hist=n_hist, mlp=bool(mlp_args), final_norm=final_norm),
        grid=(B, T // tt),
        in_specs=[
            pl.BlockSpec((None, tt, D_MODEL), lambda b, i: (b, i, 0)),
            pl.BlockSpec((None, POOL_PAD, D_MODEL), lambda b, i: (b, jnp.maximum(i * rb - 1, 0), 0)),
            pl.BlockSpec((None, POOL_PAD, D_MODEL), lambda b, i: (b, 0, 0)),
            _const_spec((1, D_MODEL)),
            _const_spec((POOL_GROUPS, POOL_GW, POOL_GW)),
            _const_spec((1, D_MODEL)),
        ] + (_mlp_specs(mlp_args[4]) if mlp_args else []),
        out_specs=[
            pl.BlockSpec((None, tt, D_MODEL), lambda b, i: (b, i, 0)),
            pl.BlockSpec((None, POOL_PAD, D_MODEL), lambda b, i: (b, 0, 0)),
        ],
        out_shape=[SDS((B, T, D_MODEL), F32), SDS((B, POOL_PAD, D_MODEL), F32)],
        compiler_params=_cparams(("parallel", "arbitrary")),
        name="pool_mlp" if mlp_args else "pool",
    )(x, x, hist, g, w_bf, scale, *mlp_args[:4])


def _rope_tables(pos0, T):
    pos = np.arange(pos0, pos0 + T, dtype=np.float64)

    def head(d):
        rd = d // ROPE_FRACTION
        half = rd // 2
        inv = ROPE_THETA ** (-np.arange(half, dtype=np.float64) / half)
        ang = pos[:, None] * inv[None, :]
        cos, sin = np.cos(ang), np.sin(ang)
        z = lambda n: np.zeros((T, n))
        c = np.concatenate([cos, cos, np.ones((T, d - rd))], axis=1)
        sa = np.concatenate([-sin, z(d - half)], axis=1)
        sb = np.concatenate([z(half), sin, z(d - rd)], axis=1)
        return c, sa, sb

    qa = head(ATTN_HEAD_DIM)
    ia = head(IDX_DIM)
    i2 = tuple(np.concatenate([t, t], axis=1) for t in ia)
    ones, zeros = np.ones((T, IDX_DIM)), np.zeros((T, IDX_DIM))
    ik = (np.concatenate([ia[0], ones], axis=1), np.concatenate([ia[1], zeros], axis=1),
          np.concatenate([ia[2], zeros], axis=1))
    return jnp.asarray(np.concatenate(list(qa) + list(i2) + list(ik), axis=1), F32)


def _aproj_kernel(x_ref, g_ref, w_ref, tab_ref, q_ref, qi_ref, kis_ref, k_ref, v_ref, ki_ref, kb_ref, vb_ref, kib_ref,
                  *, nb, tt):
    h = _rms(x_ref[...].reshape(nb * tt, D_MODEL), g_ref[...]).astype(BF16)
    p = _dot(h, w_ref[...])

    def rope(x, kind, half):
        c = tab_ref[:, (3 * kind) * LANES:(3 * kind + 1) * LANES]
        sa = tab_ref[:, (3 * kind + 1) * LANES:(3 * kind + 2) * LANES]
        sb = tab_ref[:, (3 * kind + 2) * LANES:(3 * kind + 3) * LANES]
        return x * c + pltpu.roll(x, LANES - half, 1) * sa + pltpu.roll(x, half, 1) * sb

    def put(ref, val, *lead):
        for s in range(nb):
            ref[(s,) + lead] = val[s * tt:(s + 1) * tt]

    slab = lambda off: p[:, off:off + LANES]
    qh = ATTN_HEAD_DIM // ROPE_FRACTION // 2
    ih = IDX_DIM // ROPE_FRACTION // 2
    qscale = ATTN_HEAD_DIM ** -0.5 * math.log2(math.e)
    for hd in range(ATTN_HEADS):
        put(q_ref, (rope(slab(hd * LANES), 0, qh) * qscale).astype(BF16), hd)
    k = rope(slab(OFF_K), 0, qh)
    v = slab(OFF_V)
    put(qi_ref, jnp.concatenate([rope(slab(OFF_QI + c * LANES), 1, ih).astype(BF16)
                                 for c in range(IDX_HEADS * IDX_DIM // LANES)], axis=1))
    kis = rope(slab(OFF_KI), 2, ih)
    put(kis_ref, kis)
    put(k_ref, k)
    put(v_ref, v)
    put(ki_ref, kis[:, :IDX_DIM])
    put(kb_ref, k.astype(BF16))
    lane = lax.broadcasted_iota(jnp.int32, v.shape, 1)
    put(vb_ref, jnp.concatenate([v.astype(BF16), jnp.where(lane == 0, 1.0, 0.0).astype(BF16)], axis=1))
    put(kib_ref, kis[:, :IDX_DIM].astype(BF16))


def _attn_project(x, g, w_in_pad, tab):
    B, T, _ = x.shape
    tt = _pick_tile(T, APROJ_ROWS)
    nb = _pick_tile(B, max(1, APROJ_ROWS // tt))
    if nb > 1:
        tab = jnp.tile(tab, (nb, 1))
    QI_W = IDX_HEADS * IDX_DIM
    widths = (QI_W, LANES, ATTN_HEAD_DIM, ATTN_HEAD_DIM, IDX_DIM, ATTN_HEAD_DIM, 2 * ATTN_HEAD_DIM, IDX_DIM)
    dtypes = (BF16, F32, F32, F32, F32, BF16, BF16, BF16)
    q_spec = pl.BlockSpec((nb, ATTN_HEADS, tt, ATTN_HEAD_DIM), lambda i, b: (b, 0, i, 0))
    return pl.pallas_call(
        functools.partial(_aproj_kernel, nb=nb, tt=tt),
        grid=(T // tt, B // nb),
        in_specs=[
            pl.BlockSpec((nb, tt, D_MODEL), lambda i, b: (b, i, 0)),
            _const_spec((1, D_MODEL)),
            _const_spec((D_MODEL, ATTN_IN_PAD)),
            pl.BlockSpec((nb * tt, 9 * LANES), lambda i, b: (i, 0)),
        ],
        out_specs=[q_spec] + [pl.BlockSpec((nb, tt, w), lambda i, b: (b, i, 0)) for w in widths],
        out_shape=[SDS((B, ATTN_HEADS, T, ATTN_HEAD_DIM), BF16)] + [SDS((B, T, w), d) for w, d in zip(widths, dtypes)],
        compiler_params=_cparams(("arbitrary", "arbitrary")),
        name="attn_proj",
    )(x, g, w_in_pad, tab)


def _score_key(score):
    kb = pltpu.bitcast(score, jnp.int32)
    return jnp.where(kb >= 0, kb, kb ^ jnp.int32(0x7FFFFFFF))


KEY_BITS = 32


def _count(mask, axis):
    return jnp.sum(jnp.where(mask, 1.0, 0.0), axis=axis, keepdims=True)


def _kth_key_step(key_scr, lo_u, it, topk, axis):
    trial_u = lo_u | lax.shift_left(jnp.int32(1), KEY_BITS - 1 - it)
    c = _count(key_scr[...] >= (trial_u ^ jnp.int32(INT_MIN)), axis)
    return jnp.where(c >= topk, trial_u, lo_u)


def _select_topk(key_scr, jstar_scr, kpos, topk, axis):
    step = lambda it, lo_u: _kth_key_step(key_scr, lo_u, it, topk, axis)
    lo_u = lax.fori_loop(0, KEY_BITS, step, jnp.zeros(jstar_scr.shape, jnp.int32))
    return _finish_topk(key_scr, jstar_scr, kpos, lo_u, topk, axis)


def _finish_topk(key_scr, jstar_scr, kpos, lo_u, topk, axis):
    one = jstar_scr.shape
    n_idx = key_scr.shape[axis]
    count = functools.partial(_count, axis=axis)
    lo = lo_u ^ jnp.int32(INT_MIN)
    gt = key_scr[...] > lo
    eq = key_scr[...] == lo
    need = topk - count(gt)
    tie = (lo > jnp.int32(HALF_NEG_KEY)) & (count(eq) > need)
    jstar_scr[...] = jnp.full(one, n_idx, jnp.int32)

    @pl.when(jnp.max(jnp.where(tie, 1.0, 0.0)) > 0.0)
    def _():
        nbits = max(1, (n_idx - 1).bit_length())

        def ibody(it, lo_i):
            trial = lo_i + lax.shift_left(jnp.int32(1), nbits - 1 - it)
            c = count((kpos < trial) & (key_scr[...] == lo))
            return jnp.where(c < need, trial, lo_i)

        lo_i = lax.fori_loop(0, nbits, ibody, jnp.zeros(one, jnp.int32))
        jstar_scr[...] = jnp.where(tie, lo_i, n_idx)

    return gt | (eq & (kpos <= jstar_scr[...]))


def _attn_decode_kernel(x_ref, q_ref, qi_ref, wi_ref, kib_ref, kb_ref, vb_ref, cki_ref, ck_ref, cv_ref, wo_ref,
                        o_ref, ki_scr, k_scr, v_scr, key_scr, bias_scr, jstar_scr, o_scr, *, nb, tq, P, L, topk):
    n_keys = P + tq

    @pl.when(pl.program_id(0) == 0)
    def _():
        lane = lax.broadcasted_iota(jnp.int32, (P, ATTN_HEAD_DIM), 1)
        for s in range(nb):
            v_scr[s, :P, ATTN_HEAD_DIM:] = jnp.where(lane == 0, 1.0, 0.0).astype(BF16)
            ki_scr[s, n_keys:, :] = jnp.zeros((L - n_keys, IDX_DIM), BF16)
            k_scr[s, n_keys:, :] = jnp.zeros((L - n_keys, ATTN_HEAD_DIM), BF16)
            v_scr[s, n_keys:, :] = jnp.zeros((L - n_keys, 2 * ATTN_HEAD_DIM), BF16)

    qpos = lax.broadcasted_iota(jnp.int32, (tq, L), 0) + P
    kpos = lax.broadcasted_iota(jnp.int32, (tq, L), 1)
    adm = (kpos < n_keys) & ((kpos // CHUNK) <= (qpos // CHUNK))
    rows = lambda s: slice(s * tq, (s + 1) * tq)

    for s in range(nb):
        ki_scr[s, :P, :] = cki_ref[s].astype(BF16)
        ki_scr[s, P:n_keys, :] = kib_ref[s]
        k_scr[s, :P, :] = ck_ref[s].astype(BF16)
        k_scr[s, P:n_keys, :] = kb_ref[s]
        v_scr[s, :P, :ATTN_HEAD_DIM] = cv_ref[s].astype(BF16)
        v_scr[s, P:n_keys, :] = vb_ref[s]
        qi = qi_ref[s]
        wis = wi_ref[s] * (IDX_DIM ** -0.5 * IDX_HEADS ** -0.5)
        d_all = _dot_nt(jnp.concatenate([qi[:, h * IDX_DIM:(h + 1) * IDX_DIM] for h in range(IDX_HEADS)], axis=0),
                        ki_scr[s])
        score = jnp.zeros((tq, L), F32)
        for h in range(IDX_HEADS):
            score = score + jnp.maximum(d_all[h * tq:(h + 1) * tq], 0.0) * wis[:, IDX_DIM + h:IDX_DIM + h + 1]
        key_scr[rows(s), :] = _score_key(jnp.where(adm, score, NEG_INF))

    tile = lambda m: jnp.concatenate([m] * nb, axis=0)
    keep = _select_topk(key_scr, jstar_scr, tile(kpos), topk, axis=1)
    bias_scr[...] = jnp.where(tile(adm) & keep, 0.0, NEG_INF)

    for s in range(nb):
        def probs(lg):
            lg = lg + bias_scr[rows(s), :]
            return jnp.exp2(lg - jnp.max(lg, axis=1, keepdims=True)).astype(BF16)

        lg_all = _dot_nt(jnp.concatenate([q_ref[s, h] for h in range(ATTN_HEADS)], axis=0), k_scr[s])
        p_all = jnp.concatenate([probs(lg_all[h * tq:(h + 1) * tq]) for h in range(ATTN_HEADS)], axis=0)
        pv_all = _dot(p_all, v_scr[s])
        for h in range(ATTN_HEADS):
            pv = pv_all[h * tq:(h + 1) * tq]
            o_scr[rows(s), h * ATTN_HEAD_DIM:(h + 1) * ATTN_HEAD_DIM] = (
                pv[:, :ATTN_HEAD_DIM] / pv[:, ATTN_HEAD_DIM:ATTN_HEAD_DIM + 1]).astype(BF16)
    attn = _dot(o_scr[...], wo_ref[...])
    for s in range(nb):
        o_ref[s] = x_ref[s] + attn[rows(s)]


def _attn_decode_layer(x, q, qi, kis, kib, kb, vb, cache, wo_bf, topk):
    B, T, _ = x.shape
    ck, cv, cki, layer = cache
    P = ck.shape[2]
    nb = _pick_tile(B, DECODE_SEQS)
    past = lambda w: pl.BlockSpec((None, nb, P, w), lambda b: (layer, b, 0, 0))
    L = ((P + T + LANES - 1) // LANES) * LANES
    assert T * max(IDX_HEADS, ATTN_HEADS) <= MXU_DIM, "decode block: all heads of all new tokens in one row tile"
    assert P % 16 == 0 and T % 16 == 0, "bf16 row tiles"
    QI_W = IDX_HEADS * IDX_DIM
    seq = lambda *shape: pl.BlockSpec((nb,) + shape, lambda b: (b,) + (0,) * len(shape))
    return pl.pallas_call(
        functools.partial(_attn_decode_kernel, nb=nb, tq=T, P=P, L=L, topk=topk),
        grid=(B // nb,),
        in_specs=[
            seq(T, D_MODEL), seq(ATTN_HEADS, T, ATTN_HEAD_DIM), seq(T, QI_W), seq(T, LANES),
            seq(T, IDX_DIM), seq(T, ATTN_HEAD_DIM), seq(T, 2 * ATTN_HEAD_DIM),
            past(IDX_DIM), past(ATTN_HEAD_DIM), past(ATTN_HEAD_DIM),
            _const_spec((Q_W, D_MODEL)),
        ],
        out_specs=seq(T, D_MODEL),
        out_shape=SDS((B, T, D_MODEL), F32),
        scratch_shapes=[
            pltpu.VMEM((nb, L, IDX_DIM), BF16),
            pltpu.VMEM((nb, L, ATTN_HEAD_DIM), BF16),
            pltpu.VMEM((nb, L, 2 * ATTN_HEAD_DIM), BF16),
            pltpu.VMEM((nb * T, L), jnp.int32),
            pltpu.VMEM((nb * T, L), F32),
            pltpu.VMEM((nb * T, 1), jnp.int32),
            pltpu.VMEM((nb * T, Q_W), BF16),
        ],
        compiler_params=_cparams(("arbitrary",)),
        name="attn_decode",
    )(x, q, qi, kis, kib, kb, vb, cki, ck, cv, wo_bf)


def _attn_pipe_kernel(x_ref, q_ref, qi_ref, wi_ref, ki_ref, k_ref, v_ref, wo_ref, o_ref,
                      key_scr, bias_scr, lg_scr, jstar_scr, o_scr, *, tq, L, q_off, n_keys, topk):
    s = pl.program_id(0)
    cur = s % 2

    @pl.when(s == 0)
    def _():
        bias_scr[...] = jnp.zeros(bias_scr.shape, F32)

    qpos = lax.broadcasted_iota(jnp.int32, (tq, L), 0) + q_off
    kpos = lax.broadcasted_iota(jnp.int32, (tq, L), 1)
    adm = (kpos < n_keys) & ((kpos // CHUNK) <= (qpos // CHUNK))

    select_all = L <= topk
    if not select_all:
        qi = qi_ref[...]
        ki = ki_ref[...]
        wis = wi_ref[...] * (IDX_DIM ** -0.5 * IDX_HEADS ** -0.5)
        score = jnp.zeros((tq, L), F32)
        for h in range(IDX_HEADS):
            d = _dot_nt(qi[:, h * IDX_DIM:(h + 1) * IDX_DIM], ki)
            score = score + jnp.maximum(d, 0.0) * wis[:, IDX_DIM + h:IDX_DIM + h + 1]
        key_scr[...] = _score_key(jnp.where(adm, score, NEG_INF))

    steps_per_head = KEY_BITS // ATTN_HEADS

    lg_scr[0] = _dot_nt(q_ref[0], k_ref[...])

    def head(h, lo_u):
        if select_all:
            step = lambda i, lo: lo
        else:
            step = lambda i, lo: _kth_key_step(key_scr, lo, h * steps_per_head + i, topk, 1)
        lo_u = step(0, lo_u)
        lg = lg_scr[h % 2] + bias_scr[1 - cur]
        lg_scr[(h + 1) % 2] = _dot_nt(q_ref[jnp.minimum(h + 1, ATTN_HEADS - 1)], k_ref[...])
        lo_u = step(1, lo_u)
        p = jnp.exp2(lg - jnp.max(lg, axis=1, keepdims=True)).astype(BF16)
        lo_u = step(2, lo_u)
        pv = _dot(p, v_ref[...])
        for i in range(3, steps_per_head):
            lo_u = step(i, lo_u)
        o_scr[h] = (pv[:, :ATTN_HEAD_DIM] / pv[:, ATTN_HEAD_DIM:ATTN_HEAD_DIM + 1]).astype(BF16)
        return lo_u

    lo_u = lax.fori_loop(0, ATTN_HEADS, head, jnp.zeros((tq, 1), jnp.int32))
    attn = _dot(jnp.concatenate([o_scr[h] for h in range(ATTN_HEADS)], axis=1), wo_ref[...])
    o_ref[...] = x_ref[...] + jnp.where(s > 0, attn, 0.0)

    keep = adm if select_all else adm & _finish_topk(key_scr, jstar_scr, kpos, lo_u, topk, 1)
    bias_scr[cur] = jnp.where(keep, 0.0, NEG_INF)


def _attn_pipe_layer(x, q, qi, kis, ki_all, k_all, v_all, wo_bf, q_off, n_keys, topk, tq, j0, L):
    B, T, _ = x.shape
    QI_W = IDX_HEADS * IDX_DIM
    prev = lambda s: jnp.maximum(s - 1, 0)
    this = lambda s: jnp.minimum(s, B - 1)
    return pl.pallas_call(
        functools.partial(_attn_pipe_kernel, tq=tq, L=L, q_off=q_off + j0 * tq, n_keys=n_keys, topk=topk),
        grid=(B + 1,),
        in_specs=[
            pl.BlockSpec((None, tq, D_MODEL), lambda s: (prev(s), j0, 0)),
            pl.BlockSpec((None, ATTN_HEADS, tq, ATTN_HEAD_DIM), lambda s: (prev(s), 0, j0, 0)),
            pl.BlockSpec((None, tq, QI_W), lambda s: (this(s), j0, 0)),
            pl.BlockSpec((None, tq, LANES), lambda s: (this(s), j0, 0)),
            pl.BlockSpec((None, L, IDX_DIM), lambda s: (this(s), 0, 0)),
            pl.BlockSpec((None, L, ATTN_HEAD_DIM), lambda s: (prev(s), 0, 0)),
            pl.BlockSpec((None, L, 2 * ATTN_HEAD_DIM), lambda s: (prev(s), 0, 0)),
            _const_spec((Q_W, D_MODEL)),
        ],
        out_specs=pl.BlockSpec((None, tq, D_MODEL), lambda s: (prev(s), j0, 0)),
        out_shape=SDS((B, T, D_MODEL), F32),
        input_output_aliases={0: 0},
        scratch_shapes=[
            pltpu.VMEM((tq, L), jnp.int32),
            pltpu.VMEM((2, tq, L), F32),
            pltpu.VMEM((2, tq, L), F32),
            pltpu.VMEM((tq, 1), jnp.int32),
            pltpu.VMEM((ATTN_HEADS, tq, ATTN_HEAD_DIM), BF16),
        ],
        compiler_params=_cparams(("arbitrary",)),
        name="attn_pipe",
    )(x, q, qi, kis, ki_all, k_all, v_all, wo_bf)


def _rproj_kernel(x_ref, xp_ref, sh_ref, g_ref, mix_ref, w0_ref, w1_ref, w2_ref, a0_ref, a1_ref, a2_ref,
                  g1_ref, g2_ref, kk_ref, ka_ref, wr_ref, wk_ref, wv_ref, bd_ref,
                  r_o, lw_o, k_o, v_o, na_o, b_o, g_o, hl_o, *, nb, tt):
    i = pl.program_id(1)
    g = g_ref[...]
    row = lax.broadcasted_iota(jnp.int32, (tt, D_MODEL), 0)
    hs, xxs = [], []
    for s in range(nb):
        h_s = _rms(x_ref[s], g)
        hl_o[s] = h_s[tt - SUBLANES:, :]
        prev = jnp.where(i == 0, sh_ref[s], _rms(xp_ref[s, SUBLANES - 1:SUBLANES, :], g))
        hs.append(h_s)
        xxs.append(jnp.where(row == 0, prev, pltpu.roll(h_s, 1, 0)) - h_s)
    h = jnp.concatenate(hs, axis=0)
    xx = jnp.concatenate(xxs, axis=0)
    lerp = lambda n: (h + xx * mix_ref[n:n + 1, :]).astype(BF16)
    r = _dot(lerp(0), wr_ref[...])
    wl = w0_ref[...] + _dot(jnp.tanh(_dot(lerp(1), w1_ref[...])).astype(BF16), w2_ref[...])
    lw = -_sigmoid(wl) * math.exp(-0.5)
    k = _dot(lerp(2), wk_ref[...])
    v = _dot(lerp(3), wv_ref[...])
    a = _sigmoid(a0_ref[...] + _dot(_dot(lerp(4), a1_ref[...]).astype(BF16), a2_ref[...]))
    gate = _dot(_sigmoid(_dot(lerp(5), g1_ref[...])).astype(BF16), g2_ref[...])
    kk = k * kk_ref[...]
    kk = kk * lax.rsqrt(jnp.maximum(_head_sum(kk * kk, bd_ref), 1e-24))
    outs = ((r_o, r), (lw_o, lw), (k_o, k * (1.0 + (a - 1.0) * ka_ref[...])), (v_o, v), (na_o, -kk), (b_o, kk * a),
            (g_o, gate))
    for ref, val in outs:
        for s in range(nb):
            ref[s] = val[s * tt:(s + 1) * tt]


def _rwkv_project(x, shift_prev, g, rw):
    B, T, _ = x.shape
    tt = _pick_tile(T, RPROJ_ROWS)
    nb = _pick_tile(B, max(1, RPROJ_ROWS // tt))
    rb = tt // SUBLANES
    tok = pl.BlockSpec((nb, tt, D_MODEL), lambda b, i: (b, i, 0))
    consts = [g, rw["mix"], rw["w0"], rw["w1"], rw["w2"], rw["a0"], rw["a1"], rw["a2"], rw["g1"], rw["g2"],
              rw["k_k"], rw["k_a"], rw["w_r"], rw["w_k"], rw["w_v"], rw["bd"]]
    return pl.pallas_call(
        functools.partial(_rproj_kernel, nb=nb, tt=tt),
        grid=(B // nb, T // tt),
        in_specs=[
            tok,
            pl.BlockSpec((nb, SUBLANES, D_MODEL), lambda b, i: (b, jnp.maximum(i * rb - 1, 0), 0)),
            pl.BlockSpec((nb, 1, D_MODEL), lambda b, i: (b, 0, 0)),
        ] + [_const_spec(c.shape) for c in consts],
        out_specs=[tok] * 7 + [pl.BlockSpec((nb, SUBLANES, D_MODEL), lambda b, i: (b, 0, 0))],
        out_shape=[SDS((B, T, D_MODEL), F32)] * 7 + [SDS((B, SUBLANES, D_MODEL), F32)],
        compiler_params=_cparams(("parallel", "arbitrary")),
        name="rwkv_proj",
    )(x, x, shift_prev, *consts)


def _scan_kernel(r_ref, lw_ref, k_ref, v_ref, a_ref, b_ref, g_ref, s0_ref, rk_ref, lg_ref, lb_ref,
                 y_ref, st_ref, s_scr, *, nb, tt, C):
    N = RWKV_HEAD
    assert C == N and 2 * N == LANES

    hp = LANES // N
    zero = jnp.zeros((N, N), F32)

    @pl.when(pl.program_id(1) == 0)
    def _():
        for s in range(nb):
            for p in range(D_MODEL // LANES):
                blocks = [jnp.concatenate([s0_ref[s, hp * p + h] if g == h else zero for g in range(hp)], axis=1)
                          for h in range(hp)]
                s_scr[s, p] = jnp.concatenate(blocks, axis=0).T

    row_w = lax.broadcasted_iota(jnp.int32, (C, D_MODEL), 0)
    lane = lax.broadcasted_iota(jnp.int32, (C, LANES), 1)
    h0 = lane < N
    r1 = lax.broadcasted_iota(jnp.int32, (C, 2 * C), 0)
    c1 = lax.broadcasted_iota(jnp.int32, (C, 2 * C), 1) % C
    strict = r1 > c1
    r2 = lax.broadcasted_iota(jnp.int32, (C, 4 * C), 0)
    c2 = lax.broadcasted_iota(jnp.int32, (C, 4 * C), 1) % C
    incl = r2 >= c2
    rs = lax.broadcasted_iota(jnp.int32, (LANES, LANES), 0)
    cs = lax.broadcasted_iota(jnp.int32, (LANES, LANES), 1)
    same_head = (rs < N) == (cs < N)
    eye = rs == cs

    def split(x):
        return jnp.concatenate([jnp.where(h0, x, 0.0), jnp.where(h0, 0.0, x)], axis=0)

    def chunk(c, pending):
        sl = slice(c * C, (c + 1) * C)
        n_pairs = D_MODEL // LANES
        pending = list(pending)
        per_level = -(-len(pending) // 6) if pending else 0
        at, rt, bt, kt, vv, wc = [], [], [], [], [], []
        for s in range(nb):
            lw = lw_ref[s, sl, :]
            cum = lw
            d = 1
            while d < C:
                cum = cum + jnp.where(row_w >= d, pltpu.roll(cum, d, 0), 0.0)
                d *= 2
            e_w = jnp.exp(cum)
            e_n = jnp.exp(-cum)
            rows = (a_ref[s, sl, :] * jnp.exp(cum - lw), r_ref[s, sl, :] * e_w, b_ref[s, sl, :] * e_n,
                    k_ref[s, sl, :] * e_n, v_ref[s, sl, :], e_w[C - 1:C, :])
            for dst, src in zip((at, rt, bt, kt, vv, wc), rows):
                dst.extend(src[:, p * LANES:(p + 1) * LANES] for p in range(n_pairs))
        pairs = range(nb * n_pairs)
        st = [s_scr[p // n_pairs, p % n_pairs] for p in pairs]
        ar = [jnp.concatenate([at[p], rt[p]], axis=0).astype(BF16) for p in pairs]
        v2 = [split(vv[p]).astype(BF16) for p in pairs]
        g = [_dot_nt(ar[p], jnp.concatenate([split(bt[p]), split(kt[p])], axis=0).astype(BF16)) for p in pairs]
        hm = [_dot(ar[p], st[p].astype(BF16)) for p in pairs]
        pw = [jnp.where(strict, g[p][:C, :2 * C], 0.0) for p in pairs]
        u = [hm[p][:C] + _dot(jnp.where(strict, g[p][:C, 2 * C:], 0.0).astype(BF16), v2[p]) for p in pairs]
        n = 1
        while n < C:
            pb = [pw[p].astype(BF16) for p in pairs]
            u = [u[p] + _dot(pb[p], split(u[p]).astype(BF16)) for p in pairs]
            n *= 2
            if n < C:
                pw = [_dot(pb[p], split(pw[p]).astype(BF16)) for p in pairs]
            for _ in range(min(per_level, len(pending))):
                pending.pop(0)()
        ys = [hm[p][C:] + _dot(jnp.where(incl, g[p][C:, :], 0.0).astype(BF16),
                               jnp.concatenate([split(u[p]).astype(BF16), v2[p]], axis=0)) for p in pairs]
        for p in pairs:
            bk = jnp.concatenate([bt[p], kt[p]], axis=0).astype(BF16)
            uvp = jnp.concatenate([u[p], vv[p]], axis=0).astype(BF16)
            upd = jnp.where(same_head, _dot_tn(bk, uvp), 0.0)
            w_col = jnp.sum(jnp.where(eye, wc[p], 0.0), axis=1, keepdims=True)
            s_scr[p // n_pairs, p % n_pairs] = (st[p] + upd) * w_col
        inv_n = 1.0 / N

        def head_mean(z):
            s0 = jnp.sum(jnp.where(h0, z, 0.0), axis=1, keepdims=True)
            s1 = jnp.sum(jnp.where(h0, 0.0, z), axis=1, keepdims=True)
            return jnp.where(h0, s0, s1) * inv_n

        def epilogue(p):
            s, q = divmod(p, n_pairs)
            ps = slice(q * LANES, (q + 1) * LANES)
            yc = ys[p] - head_mean(ys[p])
            yn = yc * lax.rsqrt(head_mean(yc * yc) + LNX_EPS) * lg_ref[:, ps] + lb_ref[:, ps]
            rk = r_ref[s, sl, ps] * k_ref[s, sl, ps] * rk_ref[:, ps]
            y_ref[s, sl, ps] = ((yn + head_mean(rk) * N * vv[p]) * g_ref[s, sl, ps]).astype(BF16)

        for t in pending:
            t()
        return [functools.partial(epilogue, p) for p in pairs]

    pending = []
    for c in range(tt // C):
        pending = chunk(c, pending)
    for t in pending:
        t()

    @pl.when(pl.program_id(1) == pl.num_programs(1) - 1)
    def _():
        for s in range(nb):
            for p in range(D_MODEL // LANES):
                t = s_scr[s, p].T
                for h in range(hp):
                    st_ref[s, hp * p + h] = t[h * N:(h + 1) * N, h * N:(h + 1) * N]


def _rwkv_scan(r, lw, k, v, na, b, gate, wkv0, rw):
    B, T, _ = r.shape
    C = SCAN_CHUNK
    t_pad = ((T + C - 1) // C) * C
    seqs = (r, lw, k, v, na, b, gate)
    if t_pad != T:
        seqs = tuple(jnp.pad(a, ((0, 0), (0, t_pad - T), (0, 0))) for a in seqs)
    tt = _pick_tile(t_pad, SCAN_ROWS)
    nb = _pick_tile(B, SCAN_SEQS)
    n_pairs = D_MODEL // LANES
    tok = pl.BlockSpec((nb, tt, D_MODEL), lambda b_, i: (b_, i, 0))
    st = pl.BlockSpec((nb, RWKV_HEADS, RWKV_HEAD, RWKV_HEAD), lambda b_, i: (b_, 0, 0, 0))
    consts = [rw["r_k"], rw["lnx_g"], rw["lnx_b"]]
    y, s_t = pl.pallas_call(
        functools.partial(_scan_kernel, nb=nb, tt=tt, C=C),
        grid=(B // nb, t_pad // tt),
        in_specs=[tok] * 7 + [st] + [_const_spec(c.shape) for c in consts],
        out_specs=[tok, st],
        out_shape=[SDS((B, t_pad, D_MODEL), BF16), SDS((B, RWKV_HEADS, RWKV_HEAD, RWKV_HEAD), F32)],
        scratch_shapes=[pltpu.VMEM((nb, n_pairs, LANES, LANES), F32)],
        compiler_params=_cparams(("parallel", "arbitrary")),
        name="rwkv_scan",
    )(*seqs, wkv0, *consts)
    return (y if t_pad == T else y[:, :T]), s_t


def _wo_mlp_kernel(x_ref, z_ref, wo_ref, g_ref, wu_ref, wd_ref, gf_ref, o_ref, *, final_norm):
    x = x_ref[...] + _dot(z_ref[...], wo_ref[...])
    o_ref[...] = _mlp_apply(x, g_ref, wu_ref, wd_ref, gf_ref, final_norm)


def _wo_mlp(x, z, wo, mlp_args, final_norm):
    M = x.shape[0]
    tm = _pick_tile(M, MLP_ROWS)
    tok = pl.BlockSpec((tm, D_MODEL), lambda i: (i, 0))
    return pl.pallas_call(
        functools.partial(_wo_mlp_kernel, final_norm=final_norm),
        grid=(M // tm,),
        in_specs=[tok, tok, _const_spec(wo.shape)] + _mlp_specs(mlp_args[4]),
        out_specs=tok,
        out_shape=SDS((M, D_MODEL), F32),
        compiler_params=_cparams(("parallel",)),
        name="wo_mlp",
    )(x, z, wo, *mlp_args[:4])


def _pool_block(x, state, n_hist, g, w_bf, scale, mlp_args=(), final_norm=False):
    B = x.shape[0]
    if state is None:
        hist = jnp.zeros((B, POOL_PAD, D_MODEL), F32)
    else:
        hist = jnp.pad(state, ((0, 0), (POOL_PAD - POOL_HIST, 0), (0, 0)))
    out, hs = _pool_layer(x, hist, n_hist, g, w_bf, scale, mlp_args, final_norm)
    return out, hs[:, POOL_PAD - POOL_HIST:]


def _attn_block(x, cache, g, w_in_pad, wo_bf):
    B, T, _ = x.shape
    past = 0 if cache is None else cache[0].shape[2]
    tab = _rope_tables(past, T)
    q, qi, kis, k_new, v_new, ki_new, kb, vb, kib = _attn_project(x, g, w_in_pad, tab)
    topk = min(TOPK_MAX, (past + T) // 4)
    if cache is not None:
        return _attn_decode_layer(x, q, qi, kis, kib, kb, vb, cache, wo_bf, topk), k_new, v_new, ki_new
    tq = _pick_tile(T, ATTN_Q_BLOCK)
    out = x
    for j0 in range(T // tq):
        l_g = (j0 + 1) * tq
        out = _attn_pipe_layer(out, q, qi, kis, kib, kb, vb, wo_bf, 0, l_g, topk, tq, j0, l_g)
    return out, k_new, v_new, ki_new


def _rwkv_block(x, shift_prev, wkv0, g, rw, mlp_args, final_norm):
    B, T, _ = x.shape
    r, lw, k, v, na, b, gate, hl = _rwkv_project(x, shift_prev, g, rw)
    z, s_t = _rwkv_scan(r, lw, k, v, na, b, gate, wkv0, rw)
    flat = lambda a: a.reshape(B * T, D_MODEL)
    out = _wo_mlp(flat(x), flat(z), rw["w_o"], mlp_args, final_norm).reshape(B, T, D_MODEL)
    return out, hl[:, SUBLANES - 1:], s_t


def kernel(x_prompt, x_sample, state_pool, cache_k, cache_v, cache_kidx, state_shift, state_wkv, ln1_g, ln2_g, w_up, w_down, ln_f_g, pool_w, pool_scale, attn_w_in, attn_w_out, rwkv_mix, rwkv_w0, rwkv_w1, rwkv_w2, rwkv_a0, rwkv_a1, rwkv_a2, rwkv_g1, rwkv_g2, rwkv_k_k, rwkv_k_a, rwkv_r_k, rwkv_w_r, rwkv_w_k, rwkv_w_v, rwkv_w_o, rwkv_lnx_g, rwkv_lnx_b):
    xp, xs = x_prompt, x_sample
    bp, sp, _ = xp.shape
    bs, ss, _ = xs.shape
    past = cache_k.shape[2]
    row = lambda a: a.reshape(1, -1)
    bf = lambda a: a.astype(BF16)
    wu_all, wd_all = bf(w_up), bf(w_down)
    head_of = jnp.arange(MXU_DIM) // RWKV_HEAD
    bd_mat = (head_of[:, None] == head_of[None, :]).astype(BF16)
    outs = {n: [] for n in ("pool_p", "pool_s", "k_p", "k_s", "v_p", "v_s", "ki_p", "ki_s",
                            "sh_p", "sh_s", "wkv_p", "wkv_s")}
    for i in range(DEPTH):
        j = i // N_MIXERS
        g1 = row(ln1_g[i])
        last = i == DEPTH - 1
        mlp_args = (row(ln2_g[i]), wu_all, wd_all, row(ln_f_g), i)
        prompt_mlp_done = sample_mlp_done = False
        if i % N_MIXERS == 0:
            w_bf = bf(pool_w[j])
            sc = row(pool_scale[j])
            prompt_mlp_done = sp >= POOL_MLP_MIN_ROWS
            xp, st_p = _pool_block(xp, None, 0, g1, w_bf, sc, mlp_args if prompt_mlp_done else (), last)
            xs, st_s = _pool_block(xs, state_pool[j], past, g1, w_bf, sc)
            outs["pool_p"].append(st_p)
            outs["pool_s"].append(st_s)
        elif i % N_MIXERS == 1:
            w_in_pad = jnp.pad(bf(attn_w_in[j]), ((0, 0), (0, ATTN_IN_PAD - ATTN_IN_W)))
            wo_bf = bf(attn_w_out[j])
            xp, kp, vp, kip = _attn_block(xp, None, g1, w_in_pad, wo_bf)
            xs, kn, vn, kin = _attn_block(xs, (cache_k, cache_v, cache_kidx, j), g1, w_in_pad, wo_bf)
            for n, a in (("k_p", kp), ("v_p", vp), ("ki_p", kip), ("k_s", kn), ("v_s", vn), ("ki_s", kin)):
                outs[n].append(a)
        else:
            rw = dict(mix=rwkv_mix[j], w0=row(rwkv_w0[j]), w1=bf(rwkv_w1[j]), w2=bf(rwkv_w2[j]),
                      a0=row(rwkv_a0[j]), a1=bf(rwkv_a1[j]), a2=bf(rwkv_a2[j]), g1=bf(rwkv_g1[j]),
                      g2=bf(rwkv_g2[j]), k_k=row(rwkv_k_k[j]), k_a=row(rwkv_k_a[j]), r_k=row(rwkv_r_k[j]),
                      w_r=bf(rwkv_w_r[j]), w_k=bf(rwkv_w_k[j]), w_v=bf(rwkv_w_v[j]), w_o=bf(rwkv_w_o[j]),
                      lnx_g=row(rwkv_lnx_g[j]), lnx_b=row(rwkv_lnx_b[j]), bd=bd_mat)
            zero_shift = jnp.zeros((bp, 1, D_MODEL), F32)
            zero_wkv = jnp.zeros((bp, RWKV_HEADS, RWKV_HEAD, RWKV_HEAD), F32)
            xp, shp, wp = _rwkv_block(xp, zero_shift, zero_wkv, g1, rw, mlp_args, last)
            xs, shs, wsn = _rwkv_block(xs, state_shift[j], state_wkv[j], g1, rw, mlp_args, last)
            prompt_mlp_done = sample_mlp_done = True
            outs["sh_p"].append(shp)
            outs["sh_s"].append(shs)
            outs["wkv_p"].append(wp)
            outs["wkv_s"].append(wsn)
        if not prompt_mlp_done:
            xp = _mlp(xp.reshape(bp * sp, D_MODEL), mlp_args, last).reshape(bp, sp, D_MODEL)
        if not sample_mlp_done:
            xs = _mlp(xs.reshape(bs * ss, D_MODEL), mlp_args, last).reshape(bs, ss, D_MODEL)
    st = lambda n: outs[n][0][None] if len(outs[n]) == 1 else jnp.stack(outs[n], 0)
    return (xp, xs, st("pool_p"), st("pool_s"), st("k_p"), st("k_s"), st("v_p"), st("v_s"),
            st("ki_p"), st("ki_s"), st("sh_p"), st("sh_s"), st("wkv_p"), st("wkv_s"))
```
